```python
import math
import jax
import jax.numpy as jnp
from jax import lax
import numpy as np

D_MODEL = 1024
BATCH = 1
SEQ = 16384
DEPTH = 2

ATT_HEAD_DIM = D_MODEL // 16
FOX_HEADS = 4
MLSTM_HEADS = 4
MLSTM_HEAD_DIM = D_MODEL // 8
MOBA_HEADS = 4
FOX_W = FOX_HEADS * ATT_HEAD_DIM
MLSTM_W = MLSTM_HEADS * MLSTM_HEAD_DIM
MOBA_W = MOBA_HEADS * ATT_HEAD_DIM
MIX_W = FOX_W + MLSTM_W + MOBA_W
IN_SPLITS = [FOX_W, FOX_W, FOX_W, FOX_HEADS,
             MLSTM_W, MLSTM_W, MLSTM_W, MLSTM_HEADS, MLSTM_HEADS, MLSTM_W,
             MOBA_W, MOBA_W, MOBA_W]
IN_W = sum(IN_SPLITS)

ROPE_DIM = ATT_HEAD_DIM // 4
ROPE_THETA = 500000.0
Q_BLOCK = 128
MLSTM_CHUNK = 128
CONV_WIDTH = 4
MOBA_BLOCK = 256
MOBA_TOPK = 3

N_GROUPS = 4
EXPERTS_PER_GROUP = 8
N_EXPERTS = N_GROUPS * EXPERTS_PER_GROUP
EXPERT_TOPK = 2
D_EXPERT = 512
MOE_BLOCK = 128

ALPHA = (2 * DEPTH) ** 0.25
BETA = (8 * DEPTH) ** -0.25
EPS = 1e-5

kernel_name = 'hymba_fox_mlstm_moba_hmoe_deepnorm'


def layer_norm(x, g, b):
    xf = x.astype(jnp.float32)
    mu = jnp.mean(xf, -1, keepdims=True)
    var = jnp.mean(jnp.square(xf - mu), -1, keepdims=True)
    return ((xf - mu) * lax.rsqrt(var + EPS) * g + b).astype(x.dtype)


def headwise_rms(u, g, n_heads):
    B, S, W = u.shape
    uf = u.astype(jnp.float32).reshape(B, S, n_heads, W // n_heads)
    uf = uf * lax.rsqrt(jnp.mean(jnp.square(uf), -1, keepdims=True) + EPS)
    return (uf.reshape(B, S, W) * g).astype(u.dtype)


def partial_rope(u, positions):
    half = ROPE_DIM // 2
    inv = 1.0 / (ROPE_THETA ** (jnp.arange(0, ROPE_DIM, 2, dtype=jnp.float32) / ROPE_DIM))
    ang = positions.astype(jnp.float32)[..., None] * inv
    cos = jnp.cos(ang)[:, :, None, :]
    sin = jnp.sin(ang)[:, :, None, :]
    ur = u[..., :ROPE_DIM].astype(jnp.float32)
    u1, u2 = ur[..., :half], ur[..., half:]
    rot = jnp.concatenate([u1 * cos - u2 * sin, u1 * sin + u2 * cos], -1)
    return jnp.concatenate([rot.astype(u.dtype), u[..., ROPE_DIM:]], -1)


def causal_conv(u, w):
    C = u.shape[-1]
    return lax.conv_general_dilated(
        u, w[:, None, :].astype(u.dtype), window_strides=(1,),
        padding=[(CONV_WIDTH - 1, 0)], dimension_numbers=('NWC', 'WIO', 'NWC'),
        feature_group_count=C)


def fox_attention(q, k, v, log_f):
    B, S, H, D = q.shape
    qh = q.transpose(0, 2, 1, 3) * (D ** -0.5)
    kh = k.transpose(0, 2, 1, 3)
    vh = v.transpose(0, 2, 1, 3)
    cum = jnp.cumsum(log_f, axis=1).transpose(0, 2, 1)
    kpos = jnp.arange(S)

    def block(i):
        start = i * Q_BLOCK
        qb = lax.dynamic_slice_in_dim(qh, start, Q_BLOCK, axis=2)
        cb = lax.dynamic_slice_in_dim(cum, start, Q_BLOCK, axis=2)
        s = jnp.einsum('bhqd,bhkd->bhqk', qb, kh).astype(jnp.float32)
        s = s + cb[..., :, None] - cum[..., None, :]
        qpos = start + jnp.arange(Q_BLOCK)
        s = jnp.where(kpos[None, :] <= qpos[:, None], s, -jnp.inf)
        p = jax.nn.softmax(s, axis=-1)
        return jnp.einsum('bhqk,bhkd->bhqd', p.astype(vh.dtype), vh)

    out = lax.map(block, jnp.arange(S // Q_BLOCK))
    return out.transpose(1, 0, 3, 2, 4).reshape(B, S, H * D)


def mlstm(q, k, v, i_pre, f_pre):
    B, S, H, D = q.shape
    L = MLSTM_CHUNK
    NC = S // L
    to_chunks = lambda u: u.astype(jnp.float32).reshape(B, NC, L, H, D).transpose(1, 0, 3, 2, 4)
    gate_chunks = lambda g: g.reshape(B, NC, L, H).transpose(1, 0, 3, 2)
    qc, kc, vc = to_chunks(q), to_chunks(k) * (D ** -0.5), to_chunks(v)
    li = gate_chunks(i_pre)
    lf = gate_chunks(jax.nn.log_sigmoid(f_pre))
    tril = jnp.tril(jnp.ones((L, L), dtype=bool))

    def step(carry, inp):
        C, n, m = carry
        qt, kt, vt, it, ft = inp
        b = jnp.cumsum(ft, axis=-1)
        dmat = jnp.where(tril, b[..., :, None] - b[..., None, :] + it[..., None, :], -jnp.inf)
        inter = b + m[..., None]
        m_t = jnp.maximum(inter, jnp.max(dmat, -1))
        w_intra = jnp.exp(dmat - m_t[..., None])
        w_inter = jnp.exp(inter - m_t)
        a = jnp.einsum('bhtd,bhsd->bhts', qt, kt) * w_intra
        num = jnp.einsum('bhts,bhsd->bhtd', a, vt) + w_inter[..., None] * jnp.einsum('bhtd,bhde->bhte', qt, C)
        den = jnp.sum(a, -1) + w_inter * jnp.einsum('bhtd,bhd->bht', qt, n)
        h = num / jnp.maximum(jnp.abs(den), jnp.exp(-m_t))[..., None]
        b_last = b[..., -1]
        g = b_last[..., None] - b + it
        m_new = jnp.maximum(b_last + m, jnp.max(g, -1))
        decay = jnp.exp(b_last + m - m_new)
        ws = jnp.exp(g - m_new[..., None])
        C_new = decay[..., None, None] * C + jnp.einsum('bhs,bhsd,bhse->bhde', ws, kt, vt)
        n_new = decay[..., None] * n + jnp.einsum('bhs,bhsd->bhd', ws, kt)
        return (C_new, n_new, m_new), h

    init = (jnp.zeros((B, H, D, D), jnp.float32), jnp.zeros((B, H, D), jnp.float32),
            jnp.zeros((B, H), jnp.float32))
    _, hs = lax.scan(step, init, (qc, kc, vc, li, lf))
    return hs.transpose(1, 0, 3, 2, 4).reshape(B, S, H * D).astype(q.dtype)


def moba_attention(q, k, v):
    B, S, H, D = q.shape
    BS = MOBA_BLOCK
    NB = -(-S // BS)
    K_SEL = max(1, min(MOBA_TOPK, NB - 1))
    pad = NB * BS - S
    qh = q.transpose(0, 2, 1, 3) * (D ** -0.5)
    kh = jnp.pad(k.transpose(0, 2, 1, 3), ((0, 0), (0, 0), (0, pad), (0, 0)))
    vh = jnp.pad(v.transpose(0, 2, 1, 3), ((0, 0), (0, 0), (0, pad), (0, 0)))
    kb = kh.reshape(B, H, NB, BS, D)
    vb = vh.reshape(B, H, NB, BS, D)
    kmean = jnp.mean(kb, axis=3)
    bi = jnp.arange(B)[:, None, None, None]
    hi = jnp.arange(H)[None, :, None, None]

    def block(i):
        start = i * Q_BLOCK
        qb = lax.dynamic_slice_in_dim(qh, start, Q_BLOCK, axis=2)
        qpos = start + jnp.arange(Q_BLOCK)
        own = start // BS
        gate = jnp.einsum('bhqd,bhnd->bhqn', qb, kmean).astype(jnp.float32)
        gate = jnp.where(jnp.arange(NB) < own, gate, -jnp.inf)
        _, gidx = lax.top_k(gate, K_SEL)
        valid = jnp.arange(K_SEL) < own
        kg = kb[bi, hi, gidx]
        vg = vb[bi, hi, gidx]
        s_sel = jnp.einsum('bhqd,bhqnjd->bhqnj', qb, kg).astype(jnp.float32)
        s_sel = jnp.where(valid[:, None], s_sel, -jnp.inf)
        k_own = lax.dynamic_index_in_dim(kb, own, axis=2, keepdims=False)
        v_own = lax.dynamic_index_in_dim(vb, own, axis=2, keepdims=False)
        s_own = jnp.einsum('bhqd,bhjd->bhqj', qb, k_own).astype(jnp.float32)
        kpos = own * BS + jnp.arange(BS)
        s_own = jnp.where(kpos[None, :] <= qpos[:, None], s_own, -jnp.inf)
        s = jnp.concatenate([s_sel.reshape(B, H, Q_BLOCK, K_SEL * BS), s_own], -1)
        p = jax.nn.softmax(s, axis=-1).astype(vb.dtype)
        out = jnp.einsum('bhqj,bhqjd->bhqd', p[..., :K_SEL * BS],
                         vg.reshape(B, H, Q_BLOCK, K_SEL * BS, D))
        return out + jnp.einsum('bhqj,bhjd->bhqd', p[..., K_SEL * BS:], v_own)

    out = lax.map(block, jnp.arange(S // Q_BLOCK))
    return out.transpose(1, 0, 3, 2, 4).reshape(B, S, H * D)


def token_mixer(x, positions, w_in, b_fox_f, b_mlstm_i, b_mlstm_f, conv_w,
                g_fox, g_mlstm, g_moba, w_out):
    B, S, _ = x.shape
    z = jnp.einsum('bsd,dn->bsn', x, w_in)
    (fq, fk, fv, ff, mq, mk, mv, mi, mf, mo, bq, bk, bv) = jnp.split(
        z, np.cumsum(IN_SPLITS)[:-1].tolist(), axis=-1)
    heads = lambda u, h: u.reshape(B, S, h, -1)
    log_f = jax.nn.log_sigmoid(ff.astype(jnp.float32) + b_fox_f)
    y_fox = fox_attention(heads(fq, FOX_HEADS), heads(fk, FOX_HEADS), heads(fv, FOX_HEADS), log_f)
    qk = jax.nn.silu(causal_conv(jnp.concatenate([mq, mk], -1), conv_w))
    mq, mk = jnp.split(qk, 2, axis=-1)
    h = mlstm(heads(mq, MLSTM_HEADS), heads(mk, MLSTM_HEADS), heads(mv, MLSTM_HEADS),
              mi.astype(jnp.float32) + b_mlstm_i, mf.astype(jnp.float32) + b_mlstm_f)
    y_mlstm = jax.nn.sigmoid(mo) * h
    y_moba = moba_attention(partial_rope(heads(bq, MOBA_HEADS), positions),
                            partial_rope(heads(bk, MOBA_HEADS), positions),
                            heads(bv, MOBA_HEADS))
    y = jnp.concatenate([headwise_rms(y_fox, g_fox, FOX_HEADS),
                         headwise_rms(y_mlstm, g_mlstm, MLSTM_HEADS),
                         headwise_rms(y_moba, g_moba, MOBA_HEADS)], -1)
    return jnp.einsum('bsm,md->bsd', y, w_out)


def expert_ffn(xb, wg, wu, wd):
    return (jax.nn.silu(xb @ wg) * (xb @ wu)) @ wd


def hier_moe(x, w_grp, b_grp, w_exp_router, b_exp_router, w_gate, w_up, w_down):
    B, S, D = x.shape
    T = B * S
    h = x.reshape(T, D)
    grp_logits = (h @ w_grp).astype(jnp.float32) + b_grp
    g_sel = jnp.argmax(grp_logits, -1)
    p_grp = jnp.take_along_axis(jax.nn.softmax(grp_logits, -1), g_sel[:, None], 1)[:, 0]
    e_logits = jnp.einsum('td,dge->tge', h, w_exp_router).astype(jnp.float32) + b_exp_router
    e_logits = jnp.take_along_axis(e_logits, g_sel[:, None, None], 1)[:, 0]
    top_v, top_i = lax.top_k(e_logits, EXPERT_TOPK)
    gates = jax.nn.softmax(top_v, -1) * p_grp[:, None]
    expert_id = g_sel[:, None] * EXPERTS_PER_GROUP + top_i
    A = T * EXPERT_TOPK
    e_flat = expert_id.reshape(A).astype(jnp.int32)
    tok_flat = jnp.repeat(jnp.arange(T, dtype=jnp.int32), EXPERT_TOPK)
    w_flat = gates.reshape(A)
    order = jnp.argsort(e_flat)
    e_s, tok_s, w_s = e_flat[order], tok_flat[order], w_flat[order]
    counts = jnp.bincount(e_flat, length=N_EXPERTS)
    starts = jnp.cumsum(counts) - counts
    padded = ((counts + MOE_BLOCK - 1) // MOE_BLOCK) * MOE_BLOCK
    pends = jnp.cumsum(padded)
    pstarts = pends - padded
    dest = pstarts[e_s] + (jnp.arange(A, dtype=jnp.int32) - starts[e_s])
    P = A + N_EXPERTS * MOE_BLOCK
    NBLK = P // MOE_BLOCK
    buf = jnp.zeros((P, D), h.dtype).at[dest].set(h[tok_s])
    blk_e = jnp.minimum(jnp.searchsorted(pends, jnp.arange(NBLK) * MOE_BLOCK, side='right'),
                        N_EXPERTS - 1)
    ybuf = lax.map(lambda a: expert_ffn(a[0], w_gate[a[1]], w_up[a[1]], w_down[a[1]]),
                   (buf.reshape(NBLK, MOE_BLOCK, D), blk_e)).reshape(P, D)
    y_assign = (ybuf[dest] * w_s[:, None]).astype(h.dtype)
    y = jax.ops.segment_sum(y_assign, tok_s, num_segments=T)
    return y.reshape(B, S, D)


def setup_inputs(seed: int = 0) -> dict:
    key = jax.random.key(seed)
    ks = jax.random.split(key, 24)
    nrm = lambda k, shape, scale: jax.random.normal(k, shape, jnp.float32) * scale
    L, D = DEPTH, D_MODEL
    return {
        'x': nrm(ks[0], (BATCH, SEQ, D), 1.0),
        'positions': jnp.broadcast_to(jnp.arange(SEQ, dtype=jnp.int32)[None, :], (BATCH, SEQ)),
        'w_in': nrm(ks[1], (L, D, IN_W), D ** -0.5),
        'b_fox_f': 1.0 + nrm(ks[2], (L, FOX_HEADS), 0.1),
        'b_mlstm_i': nrm(ks[3], (L, MLSTM_HEADS), 0.1),
        'b_mlstm_f': 3.0 + nrm(ks[4], (L, MLSTM_HEADS), 0.5),
        'conv_w': nrm(ks[5], (L, CONV_WIDTH, 2 * MLSTM_W), CONV_WIDTH ** -0.5),
        'g_fox': 1.0 + nrm(ks[6], (L, FOX_W), 0.1),
        'g_mlstm': 1.0 + nrm(ks[7], (L, MLSTM_W), 0.1),
        'g_moba': 1.0 + nrm(ks[8], (L, MOBA_W), 0.1),
        'w_out': nrm(ks[9], (L, MIX_W, D), MIX_W ** -0.5 * BETA),
        'ln1_g': 1.0 + nrm(ks[10], (L, D), 0.1),
        'ln1_b': nrm(ks[11], (L, D), 0.02),
        'w_grp': nrm(ks[12], (L, D, N_GROUPS), D ** -0.5),
        'b_grp': nrm(ks[13], (L, N_GROUPS), 0.01),
        'w_exp_router': nrm(ks[14], (L, D, N_GROUPS, EXPERTS_PER_GROUP), D ** -0.5),
        'b_exp_router': nrm(ks[15], (L, N_GROUPS, EXPERTS_PER_GROUP), 0.01),
        'w_gate': nrm(ks[16], (L, N_EXPERTS, D, D_EXPERT), D ** -0.5),
        'w_up': nrm(ks[17], (L, N_EXPERTS, D, D_EXPERT), D ** -0.5),
        'w_down': nrm(ks[18], (L, N_EXPERTS, D_EXPERT, D), D_EXPERT ** -0.5 * BETA),
        'ln2_g': 1.0 + nrm(ks[19], (L, D), 0.1),
        'ln2_b': nrm(ks[20], (L, D), 0.02),
    }


def reference(x, positions, w_in, b_fox_f, b_mlstm_i, b_mlstm_f, conv_w, g_fox, g_mlstm,
              g_moba, w_out, ln1_g, ln1_b, w_grp, b_grp, w_exp_router, b_exp_router,
              w_gate, w_up, w_down, ln2_g, ln2_b):
    for l in range(DEPTH):
        mix = token_mixer(x, positions, w_in[l], b_fox_f[l], b_mlstm_i[l], b_mlstm_f[l],
                          conv_w[l], g_fox[l], g_mlstm[l], g_moba[l], w_out[l])
        x = layer_norm(ALPHA * x + mix, ln1_g[l], ln1_b[l])
        ffn = hier_moe(x, w_grp[l], b_grp[l], w_exp_router[l], b_exp_router[l],
                       w_gate[l], w_up[l], w_down[l])
        x = layer_norm(ALPHA * x + ffn, ln2_g[l], ln2_b[l])
    return x
```

```python
import functools

import jax
import jax.numpy as jnp
from jax import lax
from jax.experimental import pallas as pl
from jax.experimental.pallas import tpu as pltpu

D_MODEL = 1024
DEPTH = 2
HEAD_DIM = 64
N_ATT_HEADS = 4
ATT_W = N_ATT_HEADS * HEAD_DIM
ML_HEADS = 4
ML_DIM = 128
ML_W = ML_HEADS * ML_DIM
ML_CHUNK = 128
CONV_WIDTH = 4
ROPE_DIM = 16
ROPE_THETA = 500000.0
MOBA_BLOCK = 256
MOBA_TOPK = 3
N_GROUPS = 4
EXPERTS_PER_GROUP = 8
N_EXPERTS = N_GROUPS * EXPERTS_PER_GROUP
D_EXPERT = 512
ALPHA = (2 * DEPTH) ** 0.25
EPS = 1e-5

LANES = 128
NEG_BIG = -1e30
VMEM_LIMIT = 56 * 1024 * 1024

COL_MQK = 0
COL_MV = 1024
COL_MO = 1536
COL_FQ = 2048
COL_BQ = 2816
COL_GATE = 3584
Z_W = 3712

F32 = jnp.float32
BF16 = jnp.bfloat16


def _cparams(sem):
    return pltpu.CompilerParams(dimension_semantics=sem, vmem_limit_bytes=VMEM_LIMIT)


def _split3(c):
    hi = c.astype(BF16).astype(F32)
    r1 = c - hi
    mid = r1.astype(BF16).astype(F32)
    lo = (r1 - mid).astype(BF16).astype(F32)
    return hi, mid, lo


def _dot(a, b):
    return jnp.dot(a, b, preferred_element_type=F32)


def _dot_nt(a, b):
    return lax.dot_general(a, b, (((1,), (1,)), ((), ())), preferred_element_type=F32)


def _dot_exact_lhs(a_bf16, b_f32):
    hi, mid, lo = _split3(b_f32)
    return _dot(a_bf16, hi.astype(BF16)) + _dot(a_bf16, mid.astype(BF16)) + _dot(a_bf16, lo.astype(BF16))


def _inproj_kernel(x_ref, w_ref, o_ref):
    xb = x_ref[...].astype(BF16)
    n = o_ref.shape[1]
    step = 512
    for j in range(0, n, step):
        w = min(step, n - j)
        o_ref[:, j:j + w] = _dot(xb, w_ref[:, j:j + w])


def _inproj(x, w_all, tm=256):
    s = x.shape[0]
    return pl.pallas_call(
        _inproj_kernel,
        grid=(s // tm,),
        in_specs=[pl.BlockSpec((tm, D_MODEL), lambda i: (i, 0)),
                  pl.BlockSpec((D_MODEL, Z_W), lambda i: (0, 0))],
        out_specs=pl.BlockSpec((tm, Z_W), lambda i: (i, 0)),
        out_shape=jax.ShapeDtypeStruct((s, Z_W), F32),
        compiler_params=_cparams(("parallel",)),
        name="inproj",
    )(x, w_all)


def _log_sigmoid(x):
    return jnp.minimum(x, 0.0) - jnp.log(1.0 + jnp.exp(-jnp.abs(x)))


def _gate_kernel(zg_ref, bias_ref, o_ref, carry_ref):
    @pl.when(pl.program_id(0) == 0)
    def _():
        carry_ref[...] = jnp.zeros_like(carry_ref)

    g = zg_ref[...] + bias_ref[...]
    ls = _log_sigmoid(g)
    t = g.shape[0]
    r = lax.broadcasted_iota(jnp.int32, (t, t), 0)
    c = lax.broadcasted_iota(jnp.int32, (t, t), 1)
    tri = c <= r
    tri_all = jnp.where(tri, 1.0, 0.0).astype(BF16)
    tri_chunk = jnp.where(tri & ((c // ML_CHUNK) == (r // ML_CHUNK)), 1.0, 0.0).astype(BF16)
    hi, mid, lo = _split3(ls)
    parts = [p.astype(BF16) for p in (hi, mid, lo)]
    cum_all = sum(_dot(tri_all, p) for p in parts)
    cum_chunk = sum(_dot(tri_chunk, p) for p in parts)
    carry = carry_ref[...]
    lane = lax.broadcasted_iota(jnp.int32, g.shape, 1)
    o_ref[...] = jnp.where(lane < 4, cum_all + carry, jnp.where(lane < 8, g, cum_chunk))
    carry_ref[...] = carry + cum_all[t - 1:t, :]


def _gate_prep(z, bias_row, tm=512):
    s = z.shape[0]
    return pl.pallas_call(
        _gate_kernel,
        grid=(s // tm,),
        in_specs=[pl.BlockSpec((tm, LANES), lambda i: (i, COL_GATE // LANES)),
                  pl.BlockSpec((1, LANES), lambda i: (0, 0))],
        out_specs=pl.BlockSpec((tm, LANES), lambda i: (i, 0)),
        out_shape=jax.ShapeDtypeStruct((s, LANES), F32),
        scratch_shapes=[pltpu.VMEM((1, LANES), F32)],
        compiler_params=_cparams(("arbitrary",)),
        name="gate_prep",
    )(z, bias_row)


def _fox_prep_kernel(q_ref, k_ref, v_ref, g_ref, qo_ref, ko_ref, vo_ref):
    g = g_ref[...]
    t = g.shape[0]
    lane = lax.broadcasted_iota(jnp.int32, (t, HEAD_DIM), 1)
    scale = HEAD_DIM ** -0.5
    for h in range(N_ATT_HEADS):
        hi, mid, lo = _split3(g[:, h:h + 1])
        aug_q = jnp.where(lane == 0, hi, jnp.where(lane == 1, mid, jnp.where(lane == 2, lo,
                          jnp.where(lane < 6, 1.0, 0.0))))
        aug_k = jnp.where(lane < 3, 1.0, jnp.where(lane == 3, -hi, jnp.where(lane == 4, -mid,
                          jnp.where(lane == 5, -lo, 0.0))))
        sl = slice(h * HEAD_DIM, (h + 1) * HEAD_DIM)
        qo_ref[h] = jnp.concatenate([(q_ref[:, sl] * scale).astype(BF16), aug_q.astype(BF16)], axis=1)
        ko_ref[h] = jnp.concatenate([k_ref[:, sl].astype(BF16), aug_k.astype(BF16)], axis=1)
    vo_ref[...] = v_ref[...].astype(BF16)


def _fox_prep(z, gates, tm=512):
    s = z.shape[0]
    cb = COL_FQ // ATT_W
    head_spec = pl.BlockSpec((N_ATT_HEADS, tm, LANES), lambda i: (0, i, 0))
    return pl.pallas_call(
        _fox_prep_kernel,
        grid=(s // tm,),
        in_specs=[pl.BlockSpec((tm, ATT_W), lambda i: (i, cb)),
                  pl.BlockSpec((tm, ATT_W), lambda i: (i, cb + 1)),
                  pl.BlockSpec((tm, ATT_W), lambda i: (i, cb + 2)),
                  pl.BlockSpec((tm, LANES), lambda i: (i, 0))],
        out_specs=[head_spec, head_spec, pl.BlockSpec((tm, ATT_W), lambda i: (i, 0))],
        out_shape=[jax.ShapeDtypeStruct((N_ATT_HEADS, s, LANES), BF16),
                   jax.ShapeDtypeStruct((N_ATT_HEADS, s, LANES), BF16),
                   jax.ShapeDtypeStruct((s, ATT_W), BF16)],
        compiler_params=_cparams(("parallel",)),
        name="fox_prep",
    )(z, z, z, gates)


def _rope_table_kernel(pos_ref, inv_ref, sign_ref, cos_ref, sin_ref):
    ang = pos_ref[...].astype(F32) * inv_ref[...]
    cos_ref[...] = jnp.cos(ang)
    sin_ref[...] = jnp.sin(ang) * sign_ref[...]


def _rope_tables(pos_col, inv_row, sign_row, tm=512):
    s = pos_col.shape[0]
    row = pl.BlockSpec((1, ATT_W), lambda i: (0, 0))
    out = pl.BlockSpec((tm, ATT_W), lambda i: (i, 0))
    return pl.pallas_call(
        _rope_table_kernel,
        grid=(s // tm,),
        in_specs=[pl.BlockSpec((tm, 1), lambda i: (i, 0)), row, row],
        out_specs=[out, out],
        out_shape=[jax.ShapeDtypeStruct((s, ATT_W), F32)] * 2,
        compiler_params=_cparams(("parallel",)),
        name="rope_tables",
    )(pos_col, inv_row, sign_row)


def _rope(u, cos, sin_signed):
    half = ROPE_DIM // 2
    lane = lax.broadcasted_iota(jnp.int32, u.shape, 1) % HEAD_DIM
    up = pltpu.roll(u, ATT_W - half, axis=1)
    down = pltpu.roll(u, half, axis=1)
    partner = jnp.where(lane < half, up, down)
    return u * cos + partner * sin_signed


def _moba_rope_kernel(q_ref, k_ref, cos_ref, sin_ref, qo_ref, ko_ref, km_ref):
    cos = cos_ref[...]
    sin = sin_ref[...]
    qo_ref[...] = _rope(q_ref[...], cos, sin)
    kr = _rope(k_ref[...], cos, sin)
    ko_ref[...] = kr
    km_ref[0] = jnp.mean(kr, axis=0, keepdims=True)


def _moba_rope(z, cos, sin):
    s = z.shape[0]
    tm = MOBA_BLOCK
    cb = COL_BQ // ATT_W
    blk = pl.BlockSpec((tm, ATT_W), lambda i: (i, 0))
    return pl.pallas_call(
        _moba_rope_kernel,
        grid=(s // tm,),
        in_specs=[pl.BlockSpec((tm, ATT_W), lambda i: (i, cb)),
                  pl.BlockSpec((tm, ATT_W), lambda i: (i, cb + 1)), blk, blk],
        out_specs=[blk, blk, pl.BlockSpec((1, 1, ATT_W), lambda i: (i, 0, 0))],
        out_shape=[jax.ShapeDtypeStruct((s, ATT_W), F32), jax.ShapeDtypeStruct((s, ATT_W), F32),
                   jax.ShapeDtypeStruct((s // tm, 1, ATT_W), F32)],
        compiler_params=_cparams(("parallel",)),
        name="moba_rope",
    )(z, z, cos, sin)


def _moba_select_kernel(q_ref, k_ref, v_ref, km_ref, qo_ref, ko_ref, vo_ref):
    own = pl.program_id(0)
    q = q_ref[...]
    gate = jnp.dot(q, km_ref[...], preferred_element_type=F32, precision=lax.Precision.HIGHEST)
    t = q.shape[0]
    lane = lax.broadcasted_iota(jnp.int32, (t, HEAD_DIM), 1)
    scale = HEAD_DIM ** -0.5
    onehot_own = jnp.where(lane == own, 1.0, 0.0).astype(BF16)
    for h in range(N_ATT_HEADS):
        sl = slice(h * HEAD_DIM, (h + 1) * HEAD_DIM)
        g = jnp.where(lane < own, gate[:, sl], -jnp.inf)
        bias = jnp.where(lane == own, 0.0, NEG_BIG)
        for r in range(MOBA_TOPK):
            mx = jnp.max(g, axis=1, keepdims=True)
            idx = jnp.min(jnp.where(g == mx, lane, HEAD_DIM), axis=1, keepdims=True)
            hit = lane == idx
            bias = jnp.where(hit, jnp.where(r < own, 0.0, bias), bias)
            g = jnp.where(hit, -jnp.inf, g)
        qo_ref[h] = jnp.concatenate([(q[:, sl] * scale).astype(BF16), bias.astype(BF16)], axis=1)
        ko_ref[h] = jnp.concatenate([k_ref[:, sl].astype(BF16), onehot_own], axis=1)
    vo_ref[...] = v_ref[...].astype(BF16)


def _moba_select(q_rope, k_rope, z, km_mat):
    s = z.shape[0]
    tm = MOBA_BLOCK
    cb = COL_BQ // ATT_W
    blk = pl.BlockSpec((tm, ATT_W), lambda i: (i, 0))
    head_spec = pl.BlockSpec((N_ATT_HEADS, tm, LANES), lambda i: (0, i, 0))
    return pl.pallas_call(
        _moba_select_kernel,
        grid=(s // tm,),
        in_specs=[blk, blk, pl.BlockSpec((tm, ATT_W), lambda i: (i, cb + 2)),
                  pl.BlockSpec((ATT_W, ATT_W), lambda i: (0, 0))],
        out_specs=[head_spec, head_spec, blk],
        out_shape=[jax.ShapeDtypeStruct((N_ATT_HEADS, s, LANES), BF16),
                   jax.ShapeDtypeStruct((N_ATT_HEADS, s, LANES), BF16),
                   jax.ShapeDtypeStruct((s, ATT_W), BF16)],
        compiler_params=_cparams(("parallel",)),
        name="moba_select",
    )(q_rope, k_rope, z, km_mat)


def _flash_kernel(qi_tab, ki_tab, q_ref, k_ref, v_ref, g_ref, o_ref, m_ref, l_ref, acc_ref, *, tile):
    step = pl.program_id(0)
    qi = qi_tab[step]
    ki = ki_tab[step]

    @pl.when(ki == 0)
    def _():
        m_ref[...] = jnp.full_like(m_ref, -jnp.inf)
        l_ref[...] = jnp.zeros_like(l_ref)
        acc_ref[...] = jnp.zeros_like(acc_ref)

    row = lax.broadcasted_iota(jnp.int32, (tile, tile), 0) + qi * tile
    col = lax.broadcasted_iota(jnp.int32, (tile, tile), 1) + ki * tile
    causal = col <= row
    for h in range(N_ATT_HEADS):
        s = _dot_nt(q_ref[h], k_ref[h])
        s = jnp.where(causal, s, NEG_BIG)
        m_prev = m_ref[h]
        m_new = jnp.maximum(m_prev, jnp.max(s, axis=1, keepdims=True))
        alpha = jnp.exp(m_prev - m_new)
        p = jnp.exp(s - m_new[:, 0:1])
        l_ref[h] = alpha * l_ref[h] + jnp.sum(p, axis=1, keepdims=True)
        m_ref[h] = m_new
        v = v_ref[:, h * HEAD_DIM:(h + 1) * HEAD_DIM]
        acc_ref[h] = acc_ref[h] * alpha[:, 0:HEAD_DIM] + _dot(p.astype(BF16), v)

    @pl.when(ki == qi)
    def _():
        outs = []
        for h in range(N_ATT_HEADS):
            o = acc_ref[h] / l_ref[h][:, 0:HEAD_DIM]
            o = o * lax.rsqrt(jnp.mean(o * o, axis=1, keepdims=True) + EPS)
            outs.append(o)
        o_ref[...] = (jnp.concatenate(outs, axis=1) * g_ref[...]).astype(o_ref.dtype)


def _flash_attention(q_aug, k_aug, v, gain_row, tile=512):
    s = v.shape[0]
    n = s // tile
    pairs = [(qi, ki) for qi in range(n) for ki in range(qi + 1)]
    qi_tab = jnp.asarray([p[0] for p in pairs], jnp.int32)
    ki_tab = jnp.asarray([p[1] for p in pairs], jnp.int32)
    grid_spec = pltpu.PrefetchScalarGridSpec(
        num_scalar_prefetch=2,
        grid=(len(pairs),),
        in_specs=[pl.BlockSpec((N_ATT_HEADS, tile, LANES), lambda i, qt, kt: (0, qt[i], 0)),
                  pl.BlockSpec((N_ATT_HEADS, tile, LANES), lambda i, qt, kt: (0, kt[i], 0)),
                  pl.BlockSpec((tile, ATT_W), lambda i, qt, kt: (kt[i], 0)),
                  pl.BlockSpec((1, ATT_W), lambda i, qt, kt: (0, 0))],
        out_specs=pl.BlockSpec((tile, ATT_W), lambda i, qt, kt: (qt[i], 0)),
        scratch_shapes=[pltpu.VMEM((N_ATT_HEADS, tile, LANES), F32),
                        pltpu.VMEM((N_ATT_HEADS, tile, LANES), F32),
                        pltpu.VMEM((N_ATT_HEADS, tile, HEAD_DIM), F32)],
    )
    return pl.pallas_call(
        functools.partial(_flash_kernel, tile=tile),
        grid_spec=grid_spec,
        out_shape=jax.ShapeDtypeStruct((s, ATT_W), BF16),
        compiler_params=_cparams(("arbitrary",)),
        name="flash_attention",
    )(qi_tab, ki_tab, q_aug, k_aug, v, gain_row)


def _shift_rows(u, tail, s):
    rolled = pltpu.roll(u, s, axis=0)
    rolled_tail = pltpu.roll(tail, s, axis=0)
    row8 = lax.broadcasted_iota(jnp.int32, tail.shape, 0)
    top = jnp.where(row8 < s, rolled_tail, rolled[0:8])
    return jnp.concatenate([top, rolled[8:]], axis=0)


def _mlstm_kernel(qk_ref, v_ref, o_ref, g_ref, cw_ref, gain_ref, y_ref, tail_ref, c_ref, n_ref, m_ref):
    @pl.when(pl.program_id(0) == 0)
    def _():
        tail_ref[...] = jnp.zeros_like(tail_ref)
        c_ref[...] = jnp.zeros_like(c_ref)
        n_ref[...] = jnp.zeros_like(n_ref)
        m_ref[...] = jnp.zeros_like(m_ref)

    L = ML_CHUNK
    u = qk_ref[...]
    tail = tail_ref[...]
    cw = cw_ref[...]
    conv = u * cw[CONV_WIDTH - 1:CONV_WIDTH]
    for s in range(1, CONV_WIDTH):
        conv = conv + _shift_rows(u, tail, s) * cw[CONV_WIDTH - 1 - s:CONV_WIDTH - s]
    tail_ref[...] = u[L - 8:L]
    qk = conv * jax.nn.sigmoid(conv)

    g = g_ref[...]
    gt = g.T
    r = lax.broadcasted_iota(jnp.int32, (L, L), 0)
    c = lax.broadcasted_iota(jnp.int32, (L, L), 1)
    tril = c <= r
    kscale = ML_DIM ** -0.5
    outs = []
    for h in range(ML_HEADS):
        sl = slice(h * ML_DIM, (h + 1) * ML_DIM)
        qh = qk[:, sl]
        kh = qk[:, ML_W + h * ML_DIM:ML_W + (h + 1) * ML_DIM] * kscale
        vh = v_ref[:, sl]
        b_col = g[:, 8 + h:9 + h]
        i_col = g[:, 4 + h:5 + h]
        b_row = gt[8 + h:9 + h, :]
        i_row = gt[4 + h:5 + h, :]
        m_prev = m_ref[h][:, 0:1]
        dmat = jnp.where(tril, b_col - b_row + i_row, -jnp.inf)
        inter = b_col + m_prev
        m_t = jnp.maximum(inter, jnp.max(dmat, axis=1, keepdims=True))
        w_intra = jnp.exp(dmat - m_t)
        w_inter = jnp.exp(inter - m_t)
        qb = qh.astype(BF16)
        kb = kh.astype(BF16)
        vb = vh.astype(BF16)
        a = _dot_nt(qb, kb) * w_intra
        cmat = c_ref[h]
        nrow = n_ref[h]
        num = _dot(a.astype(BF16), vb) + w_inter * _dot(qb, cmat.astype(BF16))
        den = jnp.sum(a, axis=1, keepdims=True) + w_inter * jnp.sum(qh * nrow, axis=1, keepdims=True)
        hh = num / jnp.maximum(jnp.abs(den), jnp.exp(-m_t))
        b_last = b_row[:, L - 1:L]
        g_col = b_last - b_col + i_col
        m_new = jnp.maximum(b_last + m_prev, jnp.max(g_col, axis=0, keepdims=True))
        decay = jnp.exp(b_last + m_prev - m_new)
        kw = kh * jnp.exp(g_col - m_new)
        c_ref[h] = decay * cmat + _dot(kw.T.astype(BF16), vb)
        n_ref[h] = decay * nrow + jnp.sum(kw, axis=0, keepdims=True)
        m_ref[h] = jnp.broadcast_to(m_new, (1, LANES))
        y = jax.nn.sigmoid(o_ref[:, sl]) * hh
        outs.append(y * lax.rsqrt(jnp.mean(y * y, axis=1, keepdims=True) + EPS))
    y_ref[...] = (jnp.concatenate(outs, axis=1) * gain_ref[...]).astype(y_ref.dtype)


def _mlstm(z, gates, conv_w, gain_row):
    s = z.shape[0]
    L = ML_CHUNK
    return pl.pallas_call(
        _mlstm_kernel,
        grid=(s // L,),
        in_specs=[pl.BlockSpec((L, 2 * ML_W), lambda i: (i, COL_MQK // (2 * ML_W))),
                  pl.BlockSpec((L, ML_W), lambda i: (i, COL_MV // ML_W)),
                  pl.BlockSpec((L, ML_W), lambda i: (i, COL_MO // ML_W)),
                  pl.BlockSpec((L, LANES), lambda i: (i, 0)),
                  pl.BlockSpec((CONV_WIDTH, 2 * ML_W), lambda i: (0, 0)),
                  pl.BlockSpec((1, ML_W), lambda i: (0, 0))],
        out_specs=pl.BlockSpec((L, ML_W), lambda i: (i, 0)),
        out_shape=jax.ShapeDtypeStruct((s, ML_W), BF16),
        scratch_shapes=[pltpu.VMEM((8, 2 * ML_W), F32),
                        pltpu.VMEM((ML_HEADS, ML_DIM, ML_DIM), F32),
                        pltpu.VMEM((ML_HEADS, 1, ML_DIM), F32),
                        pltpu.VMEM((ML_HEADS, 1, LANES), F32)],
        compiler_params=_cparams(("arbitrary",)),
        name="mlstm",
    )(z, z, z, gates, conv_w, gain_row)


def _layer_norm(h, g, b):
    mu = jnp.mean(h, axis=1, keepdims=True)
    d = h - mu
    var = jnp.mean(d * d, axis=1, keepdims=True)
    return d * lax.rsqrt(var + EPS) * g + b


def _outproj_kernel(yf_ref, ym_ref, yb_ref, w_ref, x_ref, g_ref, b_ref, wr_ref, br_ref, x1_ref, lg_ref):
    mix = (_dot(yf_ref[...], w_ref[0:ATT_W, :]) + _dot(ym_ref[...], w_ref[ATT_W:ATT_W + ML_W, :])
           + _dot(yb_ref[...], w_ref[ATT_W + ML_W:, :]))
    x1 = _layer_norm(ALPHA * x_ref[...] + mix, g_ref[...], b_ref[...])
    x1_ref[...] = x1
    lg_ref[...] = jnp.dot(x1, wr_ref[...], preferred_element_type=F32,
                          precision=lax.Precision.HIGHEST) + br_ref[...]


def _outproj_ln_router(yf, ym, yb, w_out, x, ln_g, ln_b, w_router, b_router, tm=512):
    s = x.shape[0]
    const = lambda shape: pl.BlockSpec(shape, lambda i: (0, 0))
    rows = lambda w: pl.BlockSpec((tm, w), lambda i: (i, 0))
    return pl.pallas_call(
        _outproj_kernel,
        grid=(s // tm,),
        in_specs=[rows(ATT_W), rows(ML_W), rows(ATT_W), const((D_MODEL, D_MODEL)), rows(D_MODEL),
                  const((1, D_MODEL)), const((1, D_MODEL)), const((D_MODEL, LANES)), const((1, LANES))],
        out_specs=[rows(D_MODEL), rows(LANES)],
        out_shape=[jax.ShapeDtypeStruct((s, D_MODEL), F32), jax.ShapeDtypeStruct((s, LANES), F32)],
        compiler_params=_cparams(("parallel",)),
        name="outproj_ln_router",
    )(yf, ym, yb, w_out, x, ln_g, ln_b, w_router, b_router)


def _first_argmax(v, lane):
    mx = jnp.max(v, axis=1, keepdims=True)
    idx = jnp.min(jnp.where(v == mx, lane, LANES), axis=1, keepdims=True)
    return mx, idx


def _route_kernel(lg_ref, info_ref, cnt_ref, carry_ref):
    @pl.when(pl.program_id(0) == 0)
    def _():
        carry_ref[...] = jnp.zeros_like(carry_ref)

    lg = lg_ref[...]
    t = lg.shape[0]
    lane = lax.broadcasted_iota(jnp.int32, lg.shape, 1)
    is_grp = lane < N_GROUPS
    gmax, gsel = _first_argmax(jnp.where(is_grp, lg, -jnp.inf), lane)
    p_grp = 1.0 / jnp.sum(jnp.where(is_grp, jnp.exp(lg - gmax), 0.0), axis=1, keepdims=True)
    lo = N_GROUPS + EXPERTS_PER_GROUP * gsel
    el = jnp.where((lane >= lo) & (lane < lo + EXPERTS_PER_GROUP), lg, -jnp.inf)
    v0, i0 = _first_argmax(el, lane)
    v1, i1 = _first_argmax(jnp.where(lane == i0, -jnp.inf, el), lane)
    ex = jnp.exp(v1 - v0)
    w0 = p_grp / (1.0 + ex)
    w1 = p_grp * ex / (1.0 + ex)
    e0 = i0 - N_GROUPS
    e1 = i1 - N_GROUPS

    cnt = jnp.where((lane == e0) | (lane == e1), 1.0, 0.0)
    r = lax.broadcasted_iota(jnp.int32, (t, t), 0)
    c = lax.broadcasted_iota(jnp.int32, (t, t), 1)
    strict = jnp.where(c < r, 1.0, 0.0).astype(BF16)
    carry = carry_ref[...]
    before = _dot(strict, cnt.astype(BF16)) + carry
    rank0 = jnp.sum(jnp.where(lane == e0, before, 0.0), axis=1, keepdims=True)
    rank1 = jnp.sum(jnp.where(lane == e1, before, 0.0), axis=1, keepdims=True)
    carry = carry + jnp.sum(cnt, axis=0, keepdims=True)
    carry_ref[...] = carry
    cnt_ref[...] = carry
    vals = [e0.astype(F32), e1.astype(F32), w0, w1, rank0, rank1]
    info = jnp.zeros(lg.shape, F32)
    for j, val in enumerate(vals):
        info = jnp.where(lane == j, val, info)
    info_ref[...] = info


def _route(logits, tm=512):
    s = logits.shape[0]
    return pl.pallas_call(
        _route_kernel,
        grid=(s // tm,),
        in_specs=[pl.BlockSpec((tm, LANES), lambda i: (i, 0))],
        out_specs=[pl.BlockSpec((tm, LANES), lambda i: (i, 0)), pl.BlockSpec((1, LANES), lambda i: (0, 0))],
        out_shape=[jax.ShapeDtypeStruct((s, LANES), F32), jax.ShapeDtypeStruct((1, LANES), F32)],
        scratch_shapes=[pltpu.VMEM((1, LANES), F32)],
        compiler_params=_cparams(("arbitrary",)),
        name="route",
    )(logits)


MOE_ROWS = 256
TRASH_ROWS = 2 * MOE_ROWS


def _moe_kernel(blk_e, nused, src0_ref, srcn_ref, slot_ref, x_hbm, wg_ref, wu_ref, wd_ref, out_hbm,
                xbuf, ybuf, wgb, wub, wdb, gsem, ssem, *, nblk, trash_base):
    b = pl.program_id(0)
    nu = nused[0]
    cur = b % 2

    def row_in(tok, slot, r):
        return pltpu.make_async_copy(x_hbm.at[pl.ds(tok, 1), :], xbuf.at[slot, pl.ds(r, 1), :], gsem.at[slot])

    def row_out(dst, slot, r):
        return pltpu.make_async_copy(ybuf.at[slot, pl.ds(r, 1), :], out_hbm.at[pl.ds(dst, 1), :], ssem.at[slot])

    def start_gather(idx_ref, slot):
        def body(r, carry):
            row_in(idx_ref[0, 0, r], slot, r).start()
            return carry
        lax.fori_loop(0, MOE_ROWS, body, 0, unroll=8)

    def wait_rows(make, slot):
        def body(r, carry):
            make(0, slot, r).wait()
            return carry
        lax.fori_loop(0, MOE_ROWS, body, 0, unroll=8)

    @pl.when(b == 0)
    def _():
        ybuf[...] = jnp.zeros_like(ybuf)
        for half in range(2):
            cp = pltpu.make_async_copy(ybuf.at[half], out_hbm.at[pl.ds(trash_base + half * MOE_ROWS, MOE_ROWS), :],
                                       ssem.at[half])
            cp.start()
            cp.wait()

    @pl.when((b == 0) & (nu > 0))
    def _():
        start_gather(src0_ref, 0)

    @pl.when(b + 1 < nu)
    def _():
        start_gather(srcn_ref, 1 - cur)

    @pl.when(b < nu)
    def _():
        wait_rows(row_in, cur)

        @pl.when((b == 0) | (blk_e[b] != blk_e[jnp.maximum(b - 1, 0)]))
        def _():
            wgb[...] = wg_ref[0].astype(BF16)
            wub[...] = wu_ref[0].astype(BF16)
            wdb[...] = wd_ref[0].astype(BF16)

        xb = xbuf[cur].astype(BF16)
        gate = _dot(xb, wgb[...])
        up = _dot(xb, wub[...])
        hid = (gate * jax.nn.sigmoid(gate) * up).astype(BF16)
        y = _dot(hid, wdb[...])

        @pl.when(b >= 2)
        def _():
            wait_rows(row_out, cur)

        ybuf[cur] = y

        def body(r, carry):
            row_out(slot_ref[0, 0, r], cur, r).start()
            return carry
        lax.fori_loop(0, MOE_ROWS, body, 0, unroll=8)

    @pl.when(b == nblk - 1)
    def _():
        @pl.when(nu >= 1)
        def _():
            wait_rows(row_out, (nu - 1) % 2)

        @pl.when(nu >= 2)
        def _():
            wait_rows(row_out, nu % 2)


def _moe_ffn(x1, src_tok, out_slot, blk_e, nused, w_gate, w_up, w_down):
    s = x1.shape[0]
    nblk = src_tok.shape[0]
    idx_blk = (1, 1, MOE_ROWS)
    smem = pltpu.SMEM
    grid_spec = pltpu.PrefetchScalarGridSpec(
        num_scalar_prefetch=2,
        grid=(nblk,),
        in_specs=[pl.BlockSpec(idx_blk, lambda b, be, nu: (0, 0, 0), memory_space=smem),
                  pl.BlockSpec(idx_blk, lambda b, be, nu: (jnp.minimum(b + 1, nblk - 1), 0, 0), memory_space=smem),
                  pl.BlockSpec(idx_blk, lambda b, be, nu: (b, 0, 0), memory_space=smem),
                  pl.BlockSpec(memory_space=pl.ANY),
                  pl.BlockSpec((1, D_MODEL, D_EXPERT), lambda b, be, nu: (be[b], 0, 0)),
                  pl.BlockSpec((1, D_MODEL, D_EXPERT), lambda b, be, nu: (be[b], 0, 0)),
                  pl.BlockSpec((1, D_EXPERT, D_MODEL), lambda b, be, nu: (be[b], 0, 0))],
        out_specs=pl.BlockSpec(memory_space=pl.ANY),
        scratch_shapes=[pltpu.VMEM((2, MOE_ROWS, D_MODEL), F32),
                        pltpu.VMEM((2, MOE_ROWS, D_MODEL), F32),
                        pltpu.VMEM((D_MODEL, D_EXPERT), BF16),
                        pltpu.VMEM((D_MODEL, D_EXPERT), BF16),
                        pltpu.VMEM((D_EXPERT, D_MODEL), BF16),
                        pltpu.SemaphoreType.DMA((2,)),
                        pltpu.SemaphoreType.DMA((2,))],
    )
    return pl.pallas_call(
        functools.partial(_moe_kernel, nblk=nblk, trash_base=2 * s),
        grid_spec=grid_spec,
        out_shape=jax.ShapeDtypeStruct((2 * s + TRASH_ROWS, D_MODEL), F32),
        compiler_params=_cparams(("arbitrary",)),
        name="moe_ffn",
    )(blk_e, nused, src_tok, src_tok, out_slot, x1, w_gate, w_up, w_down)


def _moe_plan(info, counts_row, s):
    e0 = info[:, 0].astype(jnp.int32)
    e1 = info[:, 1].astype(jnp.int32)
    rank0 = info[:, 4].astype(jnp.int32)
    rank1 = info[:, 5].astype(jnp.int32)
    counts = counts_row[0, :N_EXPERTS].astype(jnp.int32)
    padded = ((counts + MOE_ROWS - 1) // MOE_ROWS) * MOE_ROWS
    pends = jnp.cumsum(padded)
    pstarts = pends - padded
    nblk = (2 * s) // MOE_ROWS + N_EXPERTS
    p = nblk * MOE_ROWS
    dest0 = pstarts[e0] + rank0
    dest1 = pstarts[e1] + rank1
    tok = jnp.arange(s, dtype=jnp.int32)
    rows = jnp.arange(p, dtype=jnp.int32)
    src_tok = jnp.zeros((p,), jnp.int32).at[dest0].set(tok).at[dest1].set(tok)
    out_slot = (2 * s + rows % TRASH_ROWS).at[dest0].set(2 * tok).at[dest1].set(2 * tok + 1)
    nused = (pends[-1] // MOE_ROWS).astype(jnp.int32)
    blk_start = jnp.minimum(jnp.arange(nblk, dtype=jnp.int32), nused - 1) * MOE_ROWS
    blk_e = jnp.minimum(jnp.searchsorted(pends, blk_start, side="right"), N_EXPERTS - 1).astype(jnp.int32)
    return (src_tok.reshape(nblk, 1, MOE_ROWS), out_slot.reshape(nblk, 1, MOE_ROWS), blk_e, nused.reshape(1))


def _combine_kernel(x1_ref, y2_ref, info_ref, g_ref, b_ref, o_ref):
    info = info_ref[...]
    ffn = info[:, 2:3] * y2_ref[:, 0:D_MODEL] + info[:, 3:4] * y2_ref[:, D_MODEL:]
    o_ref[...] = _layer_norm(ALPHA * x1_ref[...] + ffn, g_ref[...], b_ref[...])


def _combine_ln(x1, y2, info, ln_g, ln_b, tm=512):
    s = x1.shape[0]
    const = pl.BlockSpec((1, D_MODEL), lambda i: (0, 0))
    return pl.pallas_call(
        _combine_kernel,
        grid=(s // tm,),
        in_specs=[pl.BlockSpec((tm, D_MODEL), lambda i: (i, 0)),
                  pl.BlockSpec((tm, 2 * D_MODEL), lambda i: (i, 0)),
                  pl.BlockSpec((tm, LANES), lambda i: (i, 0)), const, const],
        out_specs=pl.BlockSpec((tm, D_MODEL), lambda i: (i, 0)),
        out_shape=jax.ShapeDtypeStruct((s, D_MODEL), F32),
        compiler_params=_cparams(("parallel",)),
        name="combine_ln",
    )(x1, y2, info, ln_g, ln_b)


def _pad_lanes(v, width=LANES):
    return jnp.zeros((1, width), F32).at[0, :v.shape[0]].set(v)


def _layer(x, cos, sin, w_in, b_fox_f, b_mlstm_i, b_mlstm_f, conv_w, g_fox, g_mlstm, g_moba, w_out,
           ln1_g, ln1_b, w_grp, b_grp, w_exp_router, b_exp_router, w_gate, w_up, w_down, ln2_g, ln2_b):
    s = x.shape[0]
    bounds = [0, 256, 512, 768, 772, 1284, 1796, 2308, 2312, 2316, 2828, 3084, 3340, 3596]
    fq, fk, fv, ff, mq, mk, mv, mi, mf, mo, bq, bk, bv = [w_in[:, a:b] for a, b in zip(bounds[:-1], bounds[1:])]
    gate_pad = jnp.zeros((D_MODEL, LANES - 12), F32)
    w_all = jnp.concatenate([mq, mk, mv, mo, fq, fk, fv, bq, bk, bv, ff, mi, mf, gate_pad], axis=1).astype(BF16)
    gate_bias = _pad_lanes(jnp.concatenate([b_fox_f, b_mlstm_i, b_mlstm_f]))

    z = _inproj(x, w_all)
    gates = _gate_prep(z, gate_bias)

    fq_aug, fk_aug, fv_b = _fox_prep(z, gates)
    y_fox = _flash_attention(fq_aug, fk_aug, fv_b, g_fox[None, :])

    q_rope, k_rope, kmean = _moba_rope(z, cos, sin)
    km = kmean[:, 0, :].reshape(s // MOBA_BLOCK, N_ATT_HEADS, HEAD_DIM)
    km_mat = jnp.zeros((N_ATT_HEADS, HEAD_DIM, N_ATT_HEADS, HEAD_DIM), F32)
    for h in range(N_ATT_HEADS):
        km_mat = km_mat.at[h, :, h, :s // MOBA_BLOCK].set(km[:, h, :].T)
    km_mat = km_mat.reshape(ATT_W, ATT_W)
    bq_aug, bk_aug, bv_b = _moba_select(q_rope, k_rope, z, km_mat)
    y_moba = _flash_attention(bq_aug, bk_aug, bv_b, g_moba[None, :])

    y_mlstm = _mlstm(z, gates, conv_w, g_mlstm[None, :])

    w_router = jnp.zeros((D_MODEL, LANES), F32)
    w_router = w_router.at[:, :N_GROUPS].set(w_grp)
    w_router = w_router.at[:, N_GROUPS:N_GROUPS + N_EXPERTS].set(w_exp_router.reshape(D_MODEL, N_EXPERTS))
    b_router = _pad_lanes(jnp.concatenate([b_grp, b_exp_router.reshape(N_EXPERTS)]))
    x1, logits = _outproj_ln_router(y_fox, y_mlstm, y_moba, w_out.astype(BF16), x, ln1_g[None, :],
                                    ln1_b[None, :], w_router, b_router)

    info, counts = _route(logits)
    src_tok, out_slot, blk_e, nused = _moe_plan(info, counts, s)
    y2 = _moe_ffn(x1, src_tok, out_slot, blk_e, nused, w_gate, w_up, w_down)
    y2 = y2.reshape(s + TRASH_ROWS // 2, 2 * D_MODEL)
    return _combine_ln(x1, y2, info, ln2_g[None, :], ln2_b[None, :])


def kernel(x, positions, w_in, b_fox_f, b_mlstm_i, b_mlstm_f, conv_w, g_fox, g_mlstm, g_moba, w_out, ln1_g, ln1_b, w_grp, b_grp, w_exp_router, b_exp_router, w_gate, w_up, w_down, ln2_g, ln2_b):
    assert x.shape[0] == 1
    xs = x[0]
    d = jnp.arange(ATT_W) % HEAD_DIM
    half = ROPE_DIM // 2
    inv = 1.0 / (ROPE_THETA ** (jnp.arange(0, ROPE_DIM, 2, dtype=F32) / ROPE_DIM))
    inv_row = jnp.where(d < ROPE_DIM, inv[d % half], 0.0)[None, :].astype(F32)
    sign_row = jnp.where(d < half, -1.0, 1.0)[None, :].astype(F32)
    cos, sin = _rope_tables(positions[0][:, None], inv_row, sign_row)
    for l in range(DEPTH):
        xs = _layer(xs, cos, sin, w_in[l], b_fox_f[l], b_mlstm_i[l], b_mlstm_f[l], conv_w[l], g_fox[l],
                    g_mlstm[l], g_moba[l], w_out[l], ln1_g[l], ln1_b[l], w_grp[l], b_grp[l],
                    w_exp_router[l], b_exp_router[l], w_gate[l], w_up[l], w_down[l], ln2_g[l], ln2_b[l])
    return xs[None]
```

```python
import functools

import jax
import jax.numpy as jnp
from jax import lax
from jax.experimental import pallas as pl
from jax.experimental.pallas import tpu as pltpu

D_MODEL = 1024
DEPTH = 2
HEAD_DIM = 64
N_ATT_HEADS = 4
ATT_W = N_ATT_HEADS * HEAD_DIM
ML_HEADS = 4
ML_DIM = 128
ML_W = ML_HEADS * ML_DIM
ML_CHUNK = 128
CONV_WIDTH = 4
ROPE_DIM = 16
ROPE_THETA = 500000.0
MOBA_BLOCK = 256
MOBA_TOPK = 3
N_GROUPS = 4
EXPERTS_PER_GROUP = 8
N_EXPERTS = N_GROUPS * EXPERTS_PER_GROUP
D_EXPERT = 512
ALPHA = (2 * DEPTH) ** 0.25
EPS = 1e-5

LANES = 128
NEG_BIG = -1e30
VMEM_LIMIT = 56 * 1024 * 1024

COL_MQK = 0
COL_MV = 1024
COL_MO = 1536
COL_FQ = 2048
COL_BQ = 2816
COL_GATE = 3584
Z_W = 3712

F32 = jnp.float32
BF16 = jnp.bfloat16


def _cparams(sem):
    return pltpu.CompilerParams(dimension_semantics=sem, vmem_limit_bytes=VMEM_LIMIT)


def _split3(c):
    hi = c.astype(BF16).astype(F32)
    r1 = c - hi
    mid = r1.astype(BF16).astype(F32)
    lo = (r1 - mid).astype(BF16).astype(F32)
    return hi, mid, lo


def _dot(a, b):
    return jnp.dot(a, b, preferred_element_type=F32)


def _dot_nt(a, b):
    return lax.dot_general(a, b, (((1,), (1,)), ((), ())), preferred_element_type=F32)


def _dot_exact_lhs(a_bf16, b_f32):
    hi, mid, lo = _split3(b_f32)
    return _dot(a_bf16, hi.astype(BF16)) + _dot(a_bf16, mid.astype(BF16)) + _dot(a_bf16, lo.astype(BF16))


IN_W = 3596
_W_RUNS = ((COL_MQK, 772, 2308),
           (COL_MO, 2316, 2828),
           (COL_FQ, 0, 768),
           (COL_BQ, 2828, 3596))
_W_GATE_RUNS = ((768, 772), (2308, 2316))


def _cols(w_ref, r0, r1, a, b):
    a0 = (a // LANES) * LANES
    b0 = min(-(-b // LANES) * LANES, IN_W)
    return w_ref[0, r0:r1, a0:b0][:, a - a0:b - a0]


def _inproj_kernel(x_ref, w_ref, o_ref, wb_ref):
    @pl.when(pl.program_id(0) == 0)
    def _():
        rows = 256
        for r0 in range(0, D_MODEL, rows):
            r1 = r0 + rows
            for dst, a, b in _W_RUNS:
                wb_ref[r0:r1, dst:dst + (b - a)] = _cols(w_ref, r0, r1, a, b).astype(BF16)
            gate = [_cols(w_ref, r0, r1, a, b) for a, b in _W_GATE_RUNS]
            used = sum(b - a for a, b in _W_GATE_RUNS)
            gate.append(jnp.zeros((rows, LANES - used), F32))
            wb_ref[r0:r1, COL_GATE:] = jnp.concatenate(gate, axis=1).astype(BF16)

    xb = x_ref[...].astype(BF16)
    n = o_ref.shape[1]
    step = 512
    for j in range(0, n, step):
        w = min(step, n - j)
        o_ref[:, j:j + w] = _dot(xb, wb_ref[:, j:j + w])


def _inproj(x, w_in, layer, tm=256):
    s = x.shape[0]
    return pl.pallas_call(
        _inproj_kernel,
        grid=(s // tm,),
        in_specs=[pl.BlockSpec((tm, D_MODEL), lambda i: (i, 0)),
                  pl.BlockSpec((1, D_MODEL, IN_W), lambda i: (layer, 0, 0), pipeline_mode=pl.Buffered(1))],
        out_specs=pl.BlockSpec((tm, Z_W), lambda i: (i, 0)),
        out_shape=jax.ShapeDtypeStruct((s, Z_W), F32),
        scratch_shapes=[pltpu.VMEM((D_MODEL, Z_W), BF16)],
        compiler_params=_cparams(("arbitrary",)),
        name="inproj",
    )(x, w_in)


def _log_sigmoid(x):
    return jnp.minimum(x, 0.0) - jnp.log(1.0 + jnp.exp(-jnp.abs(x)))


def _gate_kernel(zg_ref, bias_ref, o_ref, carry_ref):
    @pl.when(pl.program_id(0) == 0)
    def _():
        carry_ref[...] = jnp.zeros_like(carry_ref)

    g = zg_ref[...] + bias_ref[...]
    ls = _log_sigmoid(g)
    t = g.shape[0]
    r = lax.broadcasted_iota(jnp.int32, (t, t), 0)
    c = lax.broadcasted_iota(jnp.int32, (t, t), 1)
    tri = c <= r
    tri_all = jnp.where(tri, 1.0, 0.0).astype(BF16)
    tri_chunk = jnp.where(tri & ((c // ML_CHUNK) == (r // ML_CHUNK)), 1.0, 0.0).astype(BF16)
    hi, mid, lo = _split3(ls)
    parts = [p.astype(BF16) for p in (hi, mid, lo)]
    cum_all = sum(_dot(tri_all, p) for p in parts)
    cum_chunk = sum(_dot(tri_chunk, p) for p in parts)
    carry = carry_ref[...]
    lane = lax.broadcasted_iota(jnp.int32, g.shape, 1)
    o_ref[...] = jnp.where(lane < 4, cum_all + carry, jnp.where(lane < 8, g, cum_chunk))
    carry_ref[...] = carry + cum_all[t - 1:t, :]


def _gate_prep(z, bias_row, tm=512):
    s = z.shape[0]
    return pl.pallas_call(
        _gate_kernel,
        grid=(s // tm,),
        in_specs=[pl.BlockSpec((tm, LANES), lambda i: (i, COL_GATE // LANES)),
                  pl.BlockSpec((1, LANES), lambda i: (0, 0))],
        out_specs=pl.BlockSpec((tm, LANES), lambda i: (i, 0)),
        out_shape=jax.ShapeDtypeStruct((s, LANES), F32),
        scratch_shapes=[pltpu.VMEM((1, LANES), F32)],
        compiler_params=_cparams(("arbitrary",)),
        name="gate_prep",
    )(z, bias_row)


def _fox_prep_kernel(q_ref, k_ref, v_ref, g_ref, qo_ref, ko_ref, vo_ref):
    g = g_ref[...]
    t = g.shape[0]
    lane = lax.broadcasted_iota(jnp.int32, (t, HEAD_DIM), 1)
    scale = HEAD_DIM ** -0.5
    for h in range(N_ATT_HEADS):
        hi, mid, lo = _split3(g[:, h:h + 1])
        aug_q = jnp.where(lane == 0, hi, jnp.where(lane == 1, mid, jnp.where(lane == 2, lo,
                          jnp.where(lane < 6, 1.0, 0.0))))
        aug_k = jnp.where(lane < 3, 1.0, jnp.where(lane == 3, -hi, jnp.where(lane == 4, -mid,
                          jnp.where(lane == 5, -lo, 0.0))))
        sl = slice(h * HEAD_DIM, (h + 1) * HEAD_DIM)
        qo_ref[h] = jnp.concatenate([(q_ref[:, sl] * scale).astype(BF16), aug_q.astype(BF16)], axis=1)
        ko_ref[h] = jnp.concatenate([k_ref[:, sl].astype(BF16), aug_k.astype(BF16)], axis=1)
    vo_ref[...] = v_ref[...].T.astype(BF16)


def _fox_prep(z, gates, tm=512):
    s = z.shape[0]
    cb = COL_FQ // ATT_W
    head_spec = pl.BlockSpec((N_ATT_HEADS, tm, LANES), lambda i: (0, i, 0))
    return pl.pallas_call(
        _fox_prep_kernel,
        grid=(s // tm,),
        in_specs=[pl.BlockSpec((tm, ATT_W), lambda i: (i, cb)),
                  pl.BlockSpec((tm, ATT_W), lambda i: (i, cb + 1)),
                  pl.BlockSpec((tm, ATT_W), lambda i: (i, cb + 2)),
                  pl.BlockSpec((tm, LANES), lambda i: (i, 0))],
        out_specs=[head_spec, head_spec, pl.BlockSpec((ATT_W, tm), lambda i: (0, i))],
        out_shape=[jax.ShapeDtypeStruct((N_ATT_HEADS, s, LANES), BF16),
                   jax.ShapeDtypeStruct((N_ATT_HEADS, s, LANES), BF16),
                   jax.ShapeDtypeStruct((ATT_W, s), BF16)],
        compiler_params=_cparams(("parallel",)),
        name="fox_prep",
    )(z, z, z, gates)


def _rope_table_kernel(pos_ref, inv_ref, sign_ref, cos_ref, sin_ref):
    ang = pos_ref[...].astype(F32) * inv_ref[...]
    cos_ref[...] = jnp.cos(ang)
    sin_ref[...] = jnp.sin(ang) * sign_ref[...]


def _rope_tables(pos_col, inv_row, sign_row, tm=512):
    s = pos_col.shape[0]
    row = pl.BlockSpec((1, ATT_W), lambda i: (0, 0))
    out = pl.BlockSpec((tm, ATT_W), lambda i: (i, 0))
    return pl.pallas_call(
        _rope_table_kernel,
        grid=(s // tm,),
        in_specs=[pl.BlockSpec((tm, 1), lambda i: (i, 0)), row, row],
        out_specs=[out, out],
        out_shape=[jax.ShapeDtypeStruct((s, ATT_W), F32)] * 2,
        compiler_params=_cparams(("parallel",)),
        name="rope_tables",
    )(pos_col, inv_row, sign_row)


def _rope(u, cos, sin_signed):
    half = ROPE_DIM // 2
    lane = lax.broadcasted_iota(jnp.int32, u.shape, 1) % HEAD_DIM
    up = pltpu.roll(u, ATT_W - half, axis=1)
    down = pltpu.roll(u, half, axis=1)
    partner = jnp.where(lane < half, up, down)
    return u * cos + partner * sin_signed


def _moba_rope_kernel(q_ref, k_ref, cos_ref, sin_ref, qo_ref, ko_ref, km_ref):
    cos = cos_ref[...]
    sin = sin_ref[...]
    qo_ref[...] = _rope(q_ref[...], cos, sin)
    kr = _rope(k_ref[...], cos, sin)
    ko_ref[...] = kr
    km_ref[0] = jnp.mean(kr, axis=0, keepdims=True)


def _moba_rope(z, cos, sin):
    s = z.shape[0]
    tm = MOBA_BLOCK
    cb = COL_BQ // ATT_W
    blk = pl.BlockSpec((tm, ATT_W), lambda i: (i, 0))
    return pl.pallas_call(
        _moba_rope_kernel,
        grid=(s // tm,),
        in_specs=[pl.BlockSpec((tm, ATT_W), lambda i: (i, cb)),
                  pl.BlockSpec((tm, ATT_W), lambda i: (i, cb + 1)), blk, blk],
        out_specs=[blk, blk, pl.BlockSpec((1, 1, ATT_W), lambda i: (i, 0, 0))],
        out_shape=[jax.ShapeDtypeStruct((s, ATT_W), F32), jax.ShapeDtypeStruct((s, ATT_W), F32),
                   jax.ShapeDtypeStruct((s // tm, 1, ATT_W), F32)],
        compiler_params=_cparams(("parallel",)),
        name="moba_rope",
    )(z, z, cos, sin)


def _moba_select_kernel(q_ref, k_ref, v_ref, km_ref, qo_ref, ko_ref, vo_ref):
    own = pl.program_id(0)
    q = q_ref[...]
    gate = jnp.dot(q, km_ref[...], preferred_element_type=F32, precision=lax.Precision.HIGHEST)
    t = q.shape[0]
    lane = lax.broadcasted_iota(jnp.int32, (t, HEAD_DIM), 1)
    scale = HEAD_DIM ** -0.5
    onehot_own = jnp.where(lane == own, 1.0, 0.0).astype(BF16)
    for h in range(N_ATT_HEADS):
        sl = slice(h * HEAD_DIM, (h + 1) * HEAD_DIM)
        g = jnp.where(lane < own, gate[:, sl], -jnp.inf)
        bias = jnp.where(lane == own, 0.0, NEG_BIG)
        for r in range(MOBA_TOPK):
            mx = jnp.max(g, axis=1, keepdims=True)
            idx = jnp.min(jnp.where(g == mx, lane, HEAD_DIM), axis=1, keepdims=True)
            hit = lane == idx
            bias = jnp.where(hit, jnp.where(r < own, 0.0, bias), bias)
            g = jnp.where(hit, -jnp.inf, g)
        qo_ref[h] = jnp.concatenate([(q[:, sl] * scale).astype(BF16), bias.astype(BF16)], axis=1)
        ko_ref[h] = jnp.concatenate([k_ref[:, sl].astype(BF16), onehot_own], axis=1)
    vo_ref[...] = v_ref[...].T.astype(BF16)


def _moba_select(q_rope, k_rope, z, km_mat):
    s = z.shape[0]
    tm = MOBA_BLOCK
    cb = COL_BQ // ATT_W
    blk = pl.BlockSpec((tm, ATT_W), lambda i: (i, 0))
    head_spec = pl.BlockSpec((N_ATT_HEADS, tm, LANES), lambda i: (0, i, 0))
    return pl.pallas_call(
        _moba_select_kernel,
        grid=(s // tm,),
        in_specs=[blk, blk, pl.BlockSpec((tm, ATT_W), lambda i: (i, cb + 2)),
                  pl.BlockSpec((ATT_W, ATT_W), lambda i: (0, 0))],
        out_specs=[head_spec, head_spec, pl.BlockSpec((ATT_W, tm), lambda i: (0, i))],
        out_shape=[jax.ShapeDtypeStruct((N_ATT_HEADS, s, LANES), BF16),
                   jax.ShapeDtypeStruct((N_ATT_HEADS, s, LANES), BF16),
                   jax.ShapeDtypeStruct((ATT_W, s), BF16)],
        compiler_params=_cparams(("parallel",)),
        name="moba_select",
    )(q_rope, k_rope, z, km_mat)


def _flash_kernel(qi_tab, ki_tab, q_ref, k_ref, vt_ref, g_ref, o_ref, m_ref, l_ref, acc_ref, *, tile):
    step = pl.program_id(0)
    qi = qi_tab[step]
    ki = ki_tab[step]

    @pl.when(ki == 0)
    def _():
        m_ref[...] = jnp.full_like(m_ref, -jnp.inf)
        l_ref[...] = jnp.zeros_like(l_ref)
        acc_ref[...] = jnp.zeros_like(acc_ref)

    def update(masked):
        if masked:
            key = lax.broadcasted_iota(jnp.int32, (tile, tile), 0)
            qry = lax.broadcasted_iota(jnp.int32, (tile, tile), 1)
            causal = key <= qry
        scores = lambda h: _dot_nt(k_ref[h], q_ref[h])
        st_next = scores(0)
        for h in range(N_ATT_HEADS):
            st = st_next
            if h + 1 < N_ATT_HEADS:
                st_next = scores(h + 1)
            if masked:
                st = jnp.where(causal, st, NEG_BIG)
            m_prev = m_ref[h]
            m_new = jnp.maximum(m_prev, jnp.max(st, axis=0, keepdims=True))
            alpha = jnp.exp(m_prev - m_new)
            p = jnp.exp(st - m_new)
            l_ref[h] = alpha * l_ref[h] + jnp.sum(p, axis=0, keepdims=True)
            m_ref[h] = m_new
            vt = vt_ref[h * HEAD_DIM:(h + 1) * HEAD_DIM, :]
            acc_ref[h] = acc_ref[h] * alpha + _dot(vt, p.astype(BF16))

    @pl.when(ki < qi)
    def _():
        update(False)

    @pl.when(ki == qi)
    def _():
        update(True)
        outs = []
        for h in range(N_ATT_HEADS):
            o = acc_ref[h] / l_ref[h]
            outs.append(o * lax.rsqrt(jnp.mean(o * o, axis=0, keepdims=True) + EPS))
        o_ref[...] = (jnp.concatenate(outs, axis=0).T * g_ref[...]).astype(o_ref.dtype)


def _flash_attention(q_aug, k_aug, v_t, gain_row, tile=1024):
    v = v_t
    s = v.shape[1]
    n = s // tile
    pairs = [(qi, ki) for qi in range(n) for ki in range(qi + 1)]
    qi_tab = jnp.asarray([p[0] for p in pairs], jnp.int32)
    ki_tab = jnp.asarray([p[1] for p in pairs], jnp.int32)
    grid_spec = pltpu.PrefetchScalarGridSpec(
        num_scalar_prefetch=2,
        grid=(len(pairs),),
        in_specs=[pl.BlockSpec((N_ATT_HEADS, tile, LANES), lambda i, qt, kt: (0, qt[i], 0)),
                  pl.BlockSpec((N_ATT_HEADS, tile, LANES), lambda i, qt, kt: (0, kt[i], 0)),
                  pl.BlockSpec((ATT_W, tile), lambda i, qt, kt: (0, kt[i])),
                  pl.BlockSpec((1, ATT_W), lambda i, qt, kt: (0, 0))],
        out_specs=pl.BlockSpec((tile, ATT_W), lambda i, qt, kt: (qt[i], 0)),
        scratch_shapes=[pltpu.VMEM((N_ATT_HEADS, 1, tile), F32),
                        pltpu.VMEM((N_ATT_HEADS, 1, tile), F32),
                        pltpu.VMEM((N_ATT_HEADS, HEAD_DIM, tile), F32)],
    )
    return pl.pallas_call(
        functools.partial(_flash_kernel, tile=tile),
        grid_spec=grid_spec,
        out_shape=jax.ShapeDtypeStruct((s, ATT_W), BF16),
        compiler_params=_cparams(("arbitrary",)),
        name="flash_attention",
    )(qi_tab, ki_tab, q_aug, k_aug, v, gain_row)


def _shift_rows(u, tail, s):
    rolled = pltpu.roll(u, s, axis=0)
    rolled_tail = pltpu.roll(tail, s, axis=0)
    row8 = lax.broadcasted_iota(jnp.int32, tail.shape, 0)
    top = jnp.where(row8 < s, rolled_tail, rolled[0:8])
    return jnp.concatenate([top, rolled[8:]], axis=0)


def _mlstm_kernel(qk_ref, v_ref, o_ref, g_ref, cw_ref, gain_ref, y_ref, tail_ref, c_ref, n_ref, m_ref):
    @pl.when(pl.program_id(0) == 0)
    def _():
        tail_ref[...] = jnp.zeros_like(tail_ref)
        c_ref[...] = jnp.zeros_like(c_ref)
        n_ref[...] = jnp.zeros_like(n_ref)
        m_ref[...] = jnp.zeros_like(m_ref)

    L = ML_CHUNK
    u = qk_ref[...]
    tail = tail_ref[...]
    cw = cw_ref[...]
    conv = u * cw[CONV_WIDTH - 1:CONV_WIDTH]
    for s in range(1, CONV_WIDTH):
        conv = conv + _shift_rows(u, tail, s) * cw[CONV_WIDTH - 1 - s:CONV_WIDTH - s]
    tail_ref[...] = u[L - 8:L]
    qk = conv * jax.nn.sigmoid(conv)

    g = g_ref[...]
    gt = g.T
    r = lax.broadcasted_iota(jnp.int32, (L, L), 0)
    c = lax.broadcasted_iota(jnp.int32, (L, L), 1)
    tril = c <= r
    kscale = ML_DIM ** -0.5
    outs = []
    for h in range(ML_HEADS):
        sl = slice(h * ML_DIM, (h + 1) * ML_DIM)
        qh = qk[:, sl]
        kh = qk[:, ML_W + h * ML_DIM:ML_W + (h + 1) * ML_DIM] * kscale
        vh = v_ref[:, sl]
        b_col = g[:, 8 + h:9 + h]
        i_col = g[:, 4 + h:5 + h]
        b_row = gt[8 + h:9 + h, :]
        i_row = gt[4 + h:5 + h, :]
        m_prev = m_ref[h][:, 0:1]
        dmat = jnp.where(tril, b_col - b_row + i_row, -jnp.inf)
        inter = b_col + m_prev
        m_t = jnp.maximum(inter, jnp.max(dmat, axis=1, keepdims=True))
        w_intra = jnp.exp(dmat - m_t)
        w_inter = jnp.exp(inter - m_t)
        qb = qh.astype(BF16)
        kb = kh.astype(BF16)
        vb = vh.astype(BF16)
        a = _dot_nt(qb, kb) * w_intra
        cmat = c_ref[h]
        nrow = n_ref[h]
        num = _dot(a.astype(BF16), vb) + w_inter * _dot(qb, cmat.astype(BF16))
        den = jnp.sum(a, axis=1, keepdims=True) + w_inter * jnp.sum(qh * nrow, axis=1, keepdims=True)
        hh = num / jnp.maximum(jnp.abs(den), jnp.exp(-m_t))
        b_last = b_row[:, L - 1:L]
        g_col = b_last - b_col + i_col
        m_new = jnp.maximum(b_last + m_prev, jnp.max(g_col, axis=0, keepdims=True))
        decay = jnp.exp(b_last + m_prev - m_new)
        kw = kh * jnp.exp(g_col - m_new)
        c_ref[h] = decay * cmat + _dot(kw.T.astype(BF16), vb)
        n_ref[h] = decay * nrow + jnp.sum(kw, axis=0, keepdims=True)
        m_ref[h] = jnp.broadcast_to(m_new, (1, LANES))
        y = jax.nn.sigmoid(o_ref[:, sl]) * hh
        outs.append(y * lax.rsqrt(jnp.mean(y * y, axis=1, keepdims=True) + EPS))
    y_ref[...] = (jnp.concatenate(outs, axis=1) * gain_ref[...]).astype(y_ref.dtype)


def _mlstm(z, gates, conv_w, gain_row):
    s = z.shape[0]
    L = ML_CHUNK
    return pl.pallas_call(
        _mlstm_kernel,
        grid=(s // L,),
        in_specs=[pl.BlockSpec((L, 2 * ML_W), lambda i: (i, COL_MQK // (2 * ML_W))),
                  pl.BlockSpec((L, ML_W), lambda i: (i, COL_MV // ML_W)),
                  pl.BlockSpec((L, ML_W), lambda i: (i, COL_MO // ML_W)),
                  pl.BlockSpec((L, LANES), lambda i: (i, 0)),
                  pl.BlockSpec((CONV_WIDTH, 2 * ML_W), lambda i: (0, 0)),
                  pl.BlockSpec((1, ML_W), lambda i: (0, 0))],
        out_specs=pl.BlockSpec((L, ML_W), lambda i: (i, 0)),
        out_shape=jax.ShapeDtypeStruct((s, ML_W), BF16),
        scratch_shapes=[pltpu.VMEM((8, 2 * ML_W), F32),
                        pltpu.VMEM((ML_HEADS, ML_DIM, ML_DIM), F32),
                        pltpu.VMEM((ML_HEADS, 1, ML_DIM), F32),
                        pltpu.VMEM((ML_HEADS, 1, LANES), F32)],
        compiler_params=_cparams(("arbitrary",)),
        name="mlstm",
    )(z, z, z, gates, conv_w, gain_row)


def _layer_norm(h, g, b):
    mu = jnp.mean(h, axis=1, keepdims=True)
    d = h - mu
    var = jnp.mean(d * d, axis=1, keepdims=True)
    return d * lax.rsqrt(var + EPS) * g + b


def _outproj_kernel(yf_ref, ym_ref, yb_ref, w_ref, x_ref, g_ref, b_ref, wr_ref, br_ref, x1_ref, lg_ref):
    mix = (_dot(yf_ref[...], w_ref[0:ATT_W, :]) + _dot(ym_ref[...], w_ref[ATT_W:ATT_W + ML_W, :])
           + _dot(yb_ref[...], w_ref[ATT_W + ML_W:, :]))
    x1 = _layer_norm(ALPHA * x_ref[...] + mix, g_ref[...], b_ref[...])
    x1_ref[...] = x1
    lg_ref[...] = jnp.dot(x1, wr_ref[...], preferred_element_type=F32,
                          precision=lax.Precision.HIGHEST) + br_ref[...]


def _outproj_ln_router(yf, ym, yb, w_out, x, ln_g, ln_b, w_router, b_router, tm=512):
    s = x.shape[0]
    const = lambda shape: pl.BlockSpec(shape, lambda i: (0, 0))
    rows = lambda w: pl.BlockSpec((tm, w), lambda i: (i, 0))
    return pl.pallas_call(
        _outproj_kernel,
        grid=(s // tm,),
        in_specs=[rows(ATT_W), rows(ML_W), rows(ATT_W), const((D_MODEL, D_MODEL)), rows(D_MODEL),
                  const((1, D_MODEL)), const((1, D_MODEL)), const((D_MODEL, LANES)), const((1, LANES))],
        out_specs=[rows(D_MODEL), rows(LANES)],
        out_shape=[jax.ShapeDtypeStruct((s, D_MODEL), F32), jax.ShapeDtypeStruct((s, LANES), F32)],
        compiler_params=_cparams(("parallel",)),
        name="outproj_ln_router",
    )(yf, ym, yb, w_out, x, ln_g, ln_b, w_router, b_router)


def _first_argmax(v, lane):
    mx = jnp.max(v, axis=1, keepdims=True)
    idx = jnp.min(jnp.where(v == mx, lane, LANES), axis=1, keepdims=True)
    return mx, idx


def _route_kernel(lg_ref, info_ref, cnt_ref, carry_ref):
    @pl.when(pl.program_id(0) == 0)
    def _():
        carry_ref[...] = jnp.zeros_like(carry_ref)

    lg = lg_ref[...]
    t = lg.shape[0]
    lane = lax.broadcasted_iota(jnp.int32, lg.shape, 1)
    is_grp = lane < N_GROUPS
    gmax, gsel = _first_argmax(jnp.where(is_grp, lg, -jnp.inf), lane)
    p_grp = 1.0 / jnp.sum(jnp.where(is_grp, jnp.exp(lg - gmax), 0.0), axis=1, keepdims=True)
    lo = N_GROUPS + EXPERTS_PER_GROUP * gsel
    el = jnp.where((lane >= lo) & (lane < lo + EXPERTS_PER_GROUP), lg, -jnp.inf)
    v0, i0 = _first_argmax(el, lane)
    v1, i1 = _first_argmax(jnp.where(lane == i0, -jnp.inf, el), lane)
    ex = jnp.exp(v1 - v0)
    w0 = p_grp / (1.0 + ex)
    w1 = p_grp * ex / (1.0 + ex)
    e0 = i0 - N_GROUPS
    e1 = i1 - N_GROUPS

    cnt = jnp.where((lane == e0) | (lane == e1), 1.0, 0.0)
    r = lax.broadcasted_iota(jnp.int32, (t, t), 0)
    c = lax.broadcasted_iota(jnp.int32, (t, t), 1)
    strict = jnp.where(c < r, 1.0, 0.0).astype(BF16)
    carry = carry_ref[...]
    before = _dot(strict, cnt.astype(BF16)) + carry
    rank0 = jnp.sum(jnp.where(lane == e0, before, 0.0), axis=1, keepdims=True)
    rank1 = jnp.sum(jnp.where(lane == e1, before, 0.0), axis=1, keepdims=True)
    carry = carry + jnp.sum(cnt, axis=0, keepdims=True)
    carry_ref[...] = carry
    cnt_ref[...] = carry
    vals = [e0.astype(F32), e1.astype(F32), w0, w1, rank0, rank1]
    info = jnp.zeros(lg.shape, F32)
    for j, val in enumerate(vals):
        info = jnp.where(lane == j, val, info)
    info_ref[...] = info


def _route(logits, tm=512):
    s = logits.shape[0]
    return pl.pallas_call(
        _route_kernel,
        grid=(s // tm,),
        in_specs=[pl.BlockSpec((tm, LANES), lambda i: (i, 0))],
        out_specs=[pl.BlockSpec((tm, LANES), lambda i: (i, 0)), pl.BlockSpec((1, LANES), lambda i: (0, 0))],
        out_shape=[jax.ShapeDtypeStruct((s, LANES), F32), jax.ShapeDtypeStruct((1, LANES), F32)],
        scratch_shapes=[pltpu.VMEM((1, LANES), F32)],
        compiler_params=_cparams(("arbitrary",)),
        name="route",
    )(logits)


MOE_ROWS = 256
TRASH_ROWS = 2 * MOE_ROWS


def _moe_kernel(blk_e, nused, src0_ref, srcn_ref, slot_ref, x_hbm, wg_ref, wu_ref, wd_ref, out_hbm,
                xbuf, ybuf, wgb, wub, wdb, gsem, ssem, *, nblk, trash_base):
    b = pl.program_id(0)
    nu = nused[0]
    cur = b % 2

    def row_in(tok, slot, r):
        return pltpu.make_async_copy(x_hbm.at[pl.ds(tok, 1), :], xbuf.at[slot, pl.ds(r, 1), :], gsem.at[slot])

    def row_out(dst, slot, r):
        return pltpu.make_async_copy(ybuf.at[slot, pl.ds(r, 1), :], out_hbm.at[pl.ds(dst, 1), :], ssem.at[slot])

    def start_gather(idx_ref, slot):
        def body(r, carry):
            row_in(idx_ref[0, 0, r], slot, r).start()
            return carry
        lax.fori_loop(0, MOE_ROWS, body, 0, unroll=8)

    def wait_rows(make, slot):
        def body(r, carry):
            make(0, slot, r).wait()
            return carry
        lax.fori_loop(0, MOE_ROWS, body, 0, unroll=8)

    @pl.when(b == 0)
    def _():
        ybuf[...] = jnp.zeros_like(ybuf)
        for half in range(2):
            cp = pltpu.make_async_copy(ybuf.at[half], out_hbm.at[pl.ds(trash_base + half * MOE_ROWS, MOE_ROWS), :],
                                       ssem.at[half])
            cp.start()
            cp.wait()

    @pl.when((b == 0) & (nu > 0))
    def _():
        start_gather(src0_ref, 0)

    @pl.when(b + 1 < nu)
    def _():
        start_gather(srcn_ref, 1 - cur)

    @pl.when(b < nu)
    def _():
        wait_rows(row_in, cur)

        @pl.when((b == 0) | (blk_e[b] != blk_e[jnp.maximum(b - 1, 0)]))
        def _():
            wgb[...] = wg_ref[0, 0].astype(BF16)
            wub[...] = wu_ref[0, 0].astype(BF16)
            wdb[...] = wd_ref[0, 0].astype(BF16)

        xb = xbuf[cur].astype(BF16)
        gate = _dot(xb, wgb[...])
        up = _dot(xb, wub[...])
        hid = (gate * jax.nn.sigmoid(gate) * up).astype(BF16)
        y = _dot(hid, wdb[...])

        @pl.when(b >= 2)
        def _():
            wait_rows(row_out, cur)

        ybuf[cur] = y

        def body(r, carry):
            row_out(slot_ref[0, 0, r], cur, r).start()
            return carry
        lax.fori_loop(0, MOE_ROWS, body, 0, unroll=8)

    @pl.when(b == nblk - 1)
    def _():
        @pl.when(nu >= 1)
        def _():
            wait_rows(row_out, (nu - 1) % 2)

        @pl.when(nu >= 2)
        def _():
            wait_rows(row_out, nu % 2)


def _moe_ffn(x1, src_tok, out_slot, blk_e, nused, w_gate, w_up, w_down, layer):
    s = x1.shape[0]
    nblk = src_tok.shape[0]
    idx_blk = (1, 1, MOE_ROWS)
    smem = pltpu.SMEM
    grid_spec = pltpu.PrefetchScalarGridSpec(
        num_scalar_prefetch=2,
        grid=(nblk,),
        in_specs=[pl.BlockSpec(idx_blk, lambda b, be, nu: (0, 0, 0), memory_space=smem),
                  pl.BlockSpec(idx_blk, lambda b, be, nu: (jnp.minimum(b + 1, nblk - 1), 0, 0), memory_space=smem),
                  pl.BlockSpec(idx_blk, lambda b, be, nu: (b, 0, 0), memory_space=smem),
                  pl.BlockSpec(memory_space=pl.ANY),
                  pl.BlockSpec((1, 1, D_MODEL, D_EXPERT), lambda b, be, nu: (layer, be[b], 0, 0)),
                  pl.BlockSpec((1, 1, D_MODEL, D_EXPERT), lambda b, be, nu: (layer, be[b], 0, 0)),
                  pl.BlockSpec((1, 1, D_EXPERT, D_MODEL), lambda b, be, nu: (layer, be[b], 0, 0))],
        out_specs=pl.BlockSpec(memory_space=pl.ANY),
        scratch_shapes=[pltpu.VMEM((2, MOE_ROWS, D_MODEL), F32),
                        pltpu.VMEM((2, MOE_ROWS, D_MODEL), F32),
                        pltpu.VMEM((D_MODEL, D_EXPERT), BF16),
                        pltpu.VMEM((D_MODEL, D_EXPERT), BF16),
                        pltpu.VMEM((D_EXPERT, D_MODEL), BF16),
                        pltpu.SemaphoreType.DMA((2,)),
                        pltpu.SemaphoreType.DMA((2,))],
    )
    return pl.pallas_call(
        functools.partial(_moe_kernel, nblk=nblk, trash_base=2 * s),
        grid_spec=grid_spec,
        out_shape=jax.ShapeDtypeStruct((2 * s + TRASH_ROWS, D_MODEL), F32),
        compiler_params=_cparams(("arbitrary",)),
        name="moe_ffn",
    )(blk_e, nused, src_tok, src_tok, out_slot, x1, w_gate, w_up, w_down)


def _dest_kernel(info_ref, pstart_ref, d_ref):
    info = info_ref[...]
    tm = info.shape[0]
    lane = lax.broadcasted_iota(jnp.int32, info.shape, 1)
    ps = pstart_ref[...]
    dests = []
    for k in range(2):
        e = info[:, k:k + 1].astype(jnp.int32)
        dests.append(jnp.sum(jnp.where(lane == e, ps, 0.0), axis=1, keepdims=True) + info[:, 4 + k:5 + k])
    packed = jnp.where(lane == 0, dests[0], jnp.where(lane == 1, dests[1], 0.0))
    for j in range(tm // LANES):
        rows = packed[j * LANES:(j + 1) * LANES, :].T
        for k in range(2):
            d_ref[k, j:j + 1, :] = rows[k:k + 1, :].astype(jnp.int32)


def _dest(info, pstart_row, tm=1024):
    s = info.shape[0]
    return pl.pallas_call(
        _dest_kernel,
        grid=(s // tm,),
        in_specs=[pl.BlockSpec((tm, LANES), lambda i: (i, 0)), pl.BlockSpec((1, LANES), lambda i: (0, 0))],
        out_specs=pl.BlockSpec((2, tm // LANES, LANES), lambda i: (0, i, 0)),
        out_shape=jax.ShapeDtypeStruct((2, s // LANES, LANES), jnp.int32),
        compiler_params=_cparams(("parallel",)),
        name="moe_dest",
    )(info, pstart_row)


def _moe_plan(info, counts_row, s):
    counts = counts_row[0, :N_EXPERTS].astype(jnp.int32)
    padded = ((counts + MOE_ROWS - 1) // MOE_ROWS) * MOE_ROWS
    pends = jnp.cumsum(padded)
    pstarts = pends - padded
    nblk = (2 * s) // MOE_ROWS + N_EXPERTS
    p = nblk * MOE_ROWS
    dest = _dest(info, _pad_lanes(pstarts.astype(F32))).reshape(2 * s)
    rows = jnp.arange(p, dtype=jnp.int32)
    out_slot = (2 * s + rows % TRASH_ROWS).at[dest].set(jnp.arange(2 * s, dtype=jnp.int32))
    src_tok = jnp.where(out_slot >= 2 * s, 0, jnp.where(out_slot >= s, out_slot - s, out_slot))
    nused = (pends[-1] // MOE_ROWS).astype(jnp.int32)
    blk_start = jnp.minimum(jnp.arange(nblk, dtype=jnp.int32), nused - 1) * MOE_ROWS
    blk_e = jnp.sum((pends[None, :] <= blk_start[:, None]).astype(jnp.int32), axis=1)
    blk_e = jnp.minimum(blk_e, N_EXPERTS - 1)
    return (src_tok.reshape(nblk, 1, MOE_ROWS), out_slot.reshape(nblk, 1, MOE_ROWS), blk_e, nused.reshape(1))


def _combine_kernel(x1_ref, ya_ref, yb_ref, info_ref, g_ref, b_ref, o_ref):
    info = info_ref[...]
    ffn = info[:, 2:3] * ya_ref[...] + info[:, 3:4] * yb_ref[...]
    o_ref[...] = _layer_norm(ALPHA * x1_ref[...] + ffn, g_ref[...], b_ref[...])


def _combine_ln(x1, y2, info, ln_g, ln_b, tm=512):
    s = x1.shape[0]
    const = pl.BlockSpec((1, D_MODEL), lambda i: (0, 0))
    return pl.pallas_call(
        _combine_kernel,
        grid=(s // tm,),
        in_specs=[pl.BlockSpec((tm, D_MODEL), lambda i: (i, 0)),
                  pl.BlockSpec((tm, D_MODEL), lambda i: (i, 0)),
                  pl.BlockSpec((tm, D_MODEL), lambda i: (i + s // tm, 0)),
                  pl.BlockSpec((tm, LANES), lambda i: (i, 0)), const, const],
        out_specs=pl.BlockSpec((tm, D_MODEL), lambda i: (i, 0)),
        out_shape=jax.ShapeDtypeStruct((s, D_MODEL), F32),
        compiler_params=_cparams(("parallel",)),
        name="combine_ln",
    )(x1, y2, y2, info, ln_g, ln_b)


def _pad_lanes(v, width=LANES):
    return jnp.zeros((1, width), F32).at[0, :v.shape[0]].set(v)


def _layer(layer, x, cos, sin, w_in, b_fox_f, b_mlstm_i, b_mlstm_f, conv_w, g_fox, g_mlstm, g_moba, w_out,
           ln1_g, ln1_b, w_grp, b_grp, w_exp_router, b_exp_router, w_gate, w_up, w_down, ln2_g, ln2_b):
    s = x.shape[0]
    gate_bias = _pad_lanes(jnp.concatenate([b_fox_f, b_mlstm_i, b_mlstm_f]))

    z = _inproj(x, w_in, layer)
    gates = _gate_prep(z, gate_bias)

    fq_aug, fk_aug, fv_b = _fox_prep(z, gates)
    y_fox = _flash_attention(fq_aug, fk_aug, fv_b, g_fox[None, :])

    q_rope, k_rope, kmean = _moba_rope(z, cos, sin)
    km = kmean[:, 0, :].reshape(s // MOBA_BLOCK, N_ATT_HEADS, HEAD_DIM)
    km_mat = jnp.zeros((N_ATT_HEADS, HEAD_DIM, N_ATT_HEADS, HEAD_DIM), F32)
    for h in range(N_ATT_HEADS):
        km_mat = km_mat.at[h, :, h, :s // MOBA_BLOCK].set(km[:, h, :].T)
    km_mat = km_mat.reshape(ATT_W, ATT_W)
    bq_aug, bk_aug, bv_b = _moba_select(q_rope, k_rope, z, km_mat)
    y_moba = _flash_attention(bq_aug, bk_aug, bv_b, g_moba[None, :])

    y_mlstm = _mlstm(z, gates, conv_w, g_mlstm[None, :])

    w_router = jnp.zeros((D_MODEL, LANES), F32)
    w_router = w_router.at[:, :N_GROUPS].set(w_grp)
    w_router = w_router.at[:, N_GROUPS:N_GROUPS + N_EXPERTS].set(w_exp_router.reshape(D_MODEL, N_EXPERTS))
    b_router = _pad_lanes(jnp.concatenate([b_grp, b_exp_router.reshape(N_EXPERTS)]))
    x1, logits = _outproj_ln_router(y_fox, y_mlstm, y_moba, w_out.astype(BF16), x, ln1_g[None, :],
                                    ln1_b[None, :], w_router, b_router)

    info, counts = _route(logits)
    src_tok, out_slot, blk_e, nused = _moe_plan(info, counts, s)
    y2 = _moe_ffn(x1, src_tok, out_slot, blk_e, nused, w_gate, w_up, w_down, layer)
    return _combine_ln(x1, y2, info, ln2_g[None, :], ln2_b[None, :])


def kernel(x, positions, w_in, b_fox_f, b_mlstm_i, b_mlstm_f, conv_w, g_fox, g_mlstm, g_moba, w_out, ln1_g, ln1_b, w_grp, b_grp, w_exp_router, b_exp_router, w_gate, w_up, w_down, ln2_g, ln2_b):
    assert x.shape[0] == 1
    xs = x[0]
    d = jnp.arange(ATT_W) % HEAD_DIM
    half = ROPE_DIM // 2
    inv = 1.0 / (ROPE_THETA ** (jnp.arange(0, ROPE_DIM, 2, dtype=F32) / ROPE_DIM))
    inv_row = jnp.where(d < ROPE_DIM, inv[d % half], 0.0)[None, :].astype(F32)
    sign_row = jnp.where(d < half, -1.0, 1.0)[None, :].astype(F32)
    cos, sin = _rope_tables(positions[0][:, None], inv_row, sign_row)
    for l in range(DEPTH):
        xs = _layer(l, xs, cos, sin, w_in, b_fox_f[l], b_mlstm_i[l], b_mlstm_f[l], conv_w[l], g_fox[l],
                    g_mlstm[l], g_moba[l], w_out[l], ln1_g[l], ln1_b[l], w_grp[l], b_grp[l],
                    w_exp_router[l], b_exp_router[l], w_gate, w_up, w_down, ln2_g[l], ln2_b[l])
    return xs[None]
```

```python
import functools

import jax
import jax.numpy as jnp
from jax import lax
from jax.experimental import pallas as pl
from jax.experimental.pallas import tpu as pltpu

D_MODEL = 1024
DEPTH = 2
HEAD_DIM = 64
N_ATT_HEADS = 4
ATT_W = N_ATT_HEADS * HEAD_DIM
ML_HEADS = 4
ML_DIM = 128
ML_W = ML_HEADS * ML_DIM
ML_CHUNK = 128
CONV_WIDTH = 4
ROPE_DIM = 16
ROPE_THETA = 500000.0
MOBA_BLOCK = 256
MOBA_TOPK = 3
N_GROUPS = 4
EXPERTS_PER_GROUP = 8
N_EXPERTS = N_GROUPS * EXPERTS_PER_GROUP
D_EXPERT = 512
ALPHA = (2 * DEPTH) ** 0.25
EPS = 1e-5

LANES = 128
NEG_BIG = -1e30
VMEM_LIMIT = 56 * 1024 * 1024

COL_MQK = 0
COL_MV = 1024
COL_MO = 1536
COL_FQ = 2048
COL_BQ = 2816
COL_GATE = 3584
Z_W = 3712

F32 = jnp.float32
BF16 = jnp.bfloat16


def _cparams(sem):
    return pltpu.CompilerParams(dimension_semantics=sem, vmem_limit_bytes=VMEM_LIMIT)


def _split3(c):
    hi = c.astype(BF16).astype(F32)
    r1 = c - hi
    mid = r1.astype(BF16).astype(F32)
    lo = (r1 - mid).astype(BF16).astype(F32)
    return hi, mid, lo


def _dot(a, b):
    return jnp.dot(a, b, preferred_element_type=F32)


def _dot_nt(a, b):
    return lax.dot_general(a, b, (((1,), (1,)), ((), ())), preferred_element_type=F32)


LOG2E = 1.4426950408889634
VT_ROWS = 80


def _store_vt(v_ref, vo_ref):
    vt = v_ref[...].T
    t = vt.shape[1]
    row = lax.broadcasted_iota(jnp.int32, (VT_ROWS - HEAD_DIM, t), 0)
    tail = jnp.where(row == 0, 1.0, 0.0)
    for h in range(N_ATT_HEADS):
        vo_ref[h] = jnp.concatenate([vt[h * HEAD_DIM:(h + 1) * HEAD_DIM, :], tail], axis=0).astype(BF16)


IN_W = 3596
_W_RUNS = ((COL_MQK, 772, 2308),
           (COL_MO, 2316, 2828),
           (COL_FQ, 0, 768),
           (COL_BQ, 2828, 3596))
_W_GATE_RUNS = ((768, 772), (2308, 2316))


def _cols(w_ref, r0, r1, a, b):
    a0 = (a // LANES) * LANES
    b0 = min(-(-b // LANES) * LANES, IN_W)
    return w_ref[0, r0:r1, a0:b0][:, a - a0:b - a0]


def _inproj_kernel(x_ref, w_ref, o_ref, wb_ref):
    @pl.when(pl.program_id(0) == 0)
    def _():
        rows = 256
        for r0 in range(0, D_MODEL, rows):
            r1 = r0 + rows
            for dst, a, b in _W_RUNS:
                wb_ref[r0:r1, dst:dst + (b - a)] = _cols(w_ref, r0, r1, a, b).astype(BF16)
            gate = [_cols(w_ref, r0, r1, a, b) for a, b in _W_GATE_RUNS]
            used = sum(b - a for a, b in _W_GATE_RUNS)
            gate.append(jnp.zeros((rows, LANES - used), F32))
            wb_ref[r0:r1, COL_GATE:] = jnp.concatenate(gate, axis=1).astype(BF16)

    xb = x_ref[...].astype(BF16)
    n = o_ref.shape[1]
    step = 512
    for j in range(0, n, step):
        w = min(step, n - j)
        o_ref[:, j:j + w] = _dot(xb, wb_ref[:, j:j + w])


def _inproj(x, w_in, layer, tm=256):
    s = x.shape[0]
    return pl.pallas_call(
        _inproj_kernel,
        grid=(s // tm,),
        in_specs=[pl.BlockSpec((tm, D_MODEL), lambda i: (i, 0)),
                  pl.BlockSpec((1, D_MODEL, IN_W), lambda i: (layer, 0, 0), pipeline_mode=pl.Buffered(1))],
        out_specs=pl.BlockSpec((tm, Z_W), lambda i: (i, 0)),
        out_shape=jax.ShapeDtypeStruct((s, Z_W), F32),
        scratch_shapes=[pltpu.VMEM((D_MODEL, Z_W), BF16)],
        compiler_params=_cparams(("arbitrary",)),
        name="inproj",
    )(x, w_in)


def _log_sigmoid(x):
    return jnp.minimum(x, 0.0) - jnp.log(1.0 + jnp.exp(-jnp.abs(x)))


def _gate_kernel(zg_ref, bias_ref, o_ref, carry_ref):
    @pl.when(pl.program_id(0) == 0)
    def _():
        carry_ref[...] = jnp.zeros_like(carry_ref)

    g = zg_ref[...] + bias_ref[...]
    ls = _log_sigmoid(g)
    t = g.shape[0]
    r = lax.broadcasted_iota(jnp.int32, (t, t), 0)
    c = lax.broadcasted_iota(jnp.int32, (t, t), 1)
    tri = c <= r
    tri_all = jnp.where(tri, 1.0, 0.0).astype(BF16)
    tri_chunk = jnp.where(tri & ((c // ML_CHUNK) == (r // ML_CHUNK)), 1.0, 0.0).astype(BF16)
    hi, mid, lo = _split3(ls)
    parts = [p.astype(BF16) for p in (hi, mid, lo)]
    cum_all = sum(_dot(tri_all, p) for p in parts)
    cum_chunk = sum(_dot(tri_chunk, p) for p in parts)
    carry = carry_ref[...]
    lane = lax.broadcasted_iota(jnp.int32, g.shape, 1)
    o_ref[...] = jnp.where(lane < 4, cum_all + carry, jnp.where(lane < 8, g, cum_chunk))
    carry_ref[...] = carry + cum_all[t - 1:t, :]


def _gate_prep(z, bias_row, tm=512):
    s = z.shape[0]
    return pl.pallas_call(
        _gate_kernel,
        grid=(s // tm,),
        in_specs=[pl.BlockSpec((tm, LANES), lambda i: (i, COL_GATE // LANES)),
                  pl.BlockSpec((1, LANES), lambda i: (0, 0))],
        out_specs=pl.BlockSpec((tm, LANES), lambda i: (i, 0)),
        out_shape=jax.ShapeDtypeStruct((s, LANES), F32),
        scratch_shapes=[pltpu.VMEM((1, LANES), F32)],
        compiler_params=_cparams(("arbitrary",)),
        name="gate_prep",
    )(z, bias_row)


def _fox_prep_kernel(q_ref, k_ref, v_ref, g_ref, qo_ref, ko_ref, vo_ref):
    g = g_ref[...]
    t = g.shape[0]
    lane = lax.broadcasted_iota(jnp.int32, (t, HEAD_DIM), 1)
    scale = HEAD_DIM ** -0.5 * LOG2E
    for h in range(N_ATT_HEADS):
        hi, mid, lo = _split3(g[:, h:h + 1] * LOG2E)
        aug_q = jnp.where(lane == 0, hi, jnp.where(lane == 1, mid, jnp.where(lane == 2, lo,
                          jnp.where(lane < 6, 1.0, 0.0))))
        aug_k = jnp.where(lane < 3, 1.0, jnp.where(lane == 3, -hi, jnp.where(lane == 4, -mid,
                          jnp.where(lane == 5, -lo, 0.0))))
        sl = slice(h * HEAD_DIM, (h + 1) * HEAD_DIM)
        qo_ref[h] = jnp.concatenate([(q_ref[:, sl] * scale).astype(BF16), aug_q.astype(BF16)], axis=1)
        ko_ref[h] = jnp.concatenate([k_ref[:, sl].astype(BF16), aug_k.astype(BF16)], axis=1)
    _store_vt(v_ref, vo_ref)


def _fox_prep(z, gates, tm=512):
    s = z.shape[0]
    cb = COL_FQ // ATT_W
    head_spec = pl.BlockSpec((N_ATT_HEADS, tm, LANES), lambda i: (0, i, 0))
    return pl.pallas_call(
        _fox_prep_kernel,
        grid=(s // tm,),
        in_specs=[pl.BlockSpec((tm, ATT_W), lambda i: (i, cb)),
                  pl.BlockSpec((tm, ATT_W), lambda i: (i, cb + 1)),
                  pl.BlockSpec((tm, ATT_W), lambda i: (i, cb + 2)),
                  pl.BlockSpec((tm, LANES), lambda i: (i, 0))],
        out_specs=[head_spec, head_spec, pl.BlockSpec((N_ATT_HEADS, VT_ROWS, tm), lambda i: (0, 0, i))],
        out_shape=[jax.ShapeDtypeStruct((N_ATT_HEADS, s, LANES), BF16),
                   jax.ShapeDtypeStruct((N_ATT_HEADS, s, LANES), BF16),
                   jax.ShapeDtypeStruct((N_ATT_HEADS, VT_ROWS, s), BF16)],
        compiler_params=_cparams(("parallel",)),
        name="fox_prep",
    )(z, z, z, gates)


def _rope_table_kernel(pos_ref, inv_ref, sign_ref, cos_ref, sin_ref):
    ang = pos_ref[...].astype(F32) * inv_ref[...]
    cos_ref[...] = jnp.cos(ang)
    sin_ref[...] = jnp.sin(ang) * sign_ref[...]


def _rope_tables(pos_col, inv_row, sign_row, tm=512):
    s = pos_col.shape[0]
    row = pl.BlockSpec((1, ATT_W), lambda i: (0, 0))
    out = pl.BlockSpec((tm, ATT_W), lambda i: (i, 0))
    return pl.pallas_call(
        _rope_table_kernel,
        grid=(s // tm,),
        in_specs=[pl.BlockSpec((tm, 1), lambda i: (i, 0)), row, row],
        out_specs=[out, out],
        out_shape=[jax.ShapeDtypeStruct((s, ATT_W), F32)] * 2,
        compiler_params=_cparams(("parallel",)),
        name="rope_tables",
    )(pos_col, inv_row, sign_row)


def _rope(u, cos, sin_signed):
    half = ROPE_DIM // 2
    lane = lax.broadcasted_iota(jnp.int32, u.shape, 1) % HEAD_DIM
    up = pltpu.roll(u, ATT_W - half, axis=1)
    down = pltpu.roll(u, half, axis=1)
    partner = jnp.where(lane < half, up, down)
    return u * cos + partner * sin_signed


def _moba_rope_kernel(q_ref, k_ref, cos_ref, sin_ref, qo_ref, ko_ref, km_ref):
    cos = cos_ref[...]
    sin = sin_ref[...]
    qo_ref[...] = _rope(q_ref[...], cos, sin)
    kr = _rope(k_ref[...], cos, sin)
    ko_ref[...] = kr
    km_ref[0] = jnp.mean(kr, axis=0, keepdims=True)


def _moba_rope(z, cos, sin):
    s = z.shape[0]
    tm = MOBA_BLOCK
    cb = COL_BQ // ATT_W
    blk = pl.BlockSpec((tm, ATT_W), lambda i: (i, 0))
    return pl.pallas_call(
        _moba_rope_kernel,
        grid=(s // tm,),
        in_specs=[pl.BlockSpec((tm, ATT_W), lambda i: (i, cb)),
                  pl.BlockSpec((tm, ATT_W), lambda i: (i, cb + 1)), blk, blk],
        out_specs=[blk, blk, pl.BlockSpec((1, 1, ATT_W), lambda i: (i, 0, 0))],
        out_shape=[jax.ShapeDtypeStruct((s, ATT_W), F32), jax.ShapeDtypeStruct((s, ATT_W), F32),
                   jax.ShapeDtypeStruct((s // tm, 1, ATT_W), F32)],
        compiler_params=_cparams(("parallel",)),
        name="moba_rope",
    )(z, z, cos, sin)


def _moba_select_kernel(q_ref, k_ref, v_ref, km_ref, qo_ref, ko_ref, vo_ref):
    own = pl.program_id(0)
    q = q_ref[...]
    gate_t = lax.dot_general(km_ref[...], q, (((1,), (1,)), ((), ())), preferred_element_type=F32,
                             precision=lax.Precision.HIGHEST)
    t = q.shape[0]
    blk = lax.broadcasted_iota(jnp.int32, (HEAD_DIM, t), 0)
    biases = []
    for h in range(N_ATT_HEADS):
        g = jnp.where(blk < own, gate_t[h * HEAD_DIM:(h + 1) * HEAD_DIM, :], -jnp.inf)
        bias = jnp.where(blk == own, 0.0, NEG_BIG)
        for r in range(MOBA_TOPK):
            mx = jnp.max(g, axis=0, keepdims=True)
            idx = jnp.min(jnp.where(g == mx, blk, HEAD_DIM), axis=0, keepdims=True)
            hit = blk == idx
            bias = jnp.where(hit, jnp.where(r < own, 0.0, bias), bias)
            g = jnp.where(hit, -jnp.inf, g)
        biases.append(bias)
    bias_all = jnp.concatenate(biases, axis=0).T
    lane = lax.broadcasted_iota(jnp.int32, (t, HEAD_DIM), 1)
    scale = HEAD_DIM ** -0.5 * LOG2E
    onehot_own = jnp.where(lane == own, 1.0, 0.0).astype(BF16)
    for h in range(N_ATT_HEADS):
        sl = slice(h * HEAD_DIM, (h + 1) * HEAD_DIM)
        qo_ref[h] = jnp.concatenate([(q[:, sl] * scale).astype(BF16), bias_all[:, sl].astype(BF16)], axis=1)
        ko_ref[h] = jnp.concatenate([k_ref[:, sl].astype(BF16), onehot_own], axis=1)
    _store_vt(v_ref, vo_ref)


def _moba_select(q_rope, k_rope, z, km_mat):
    s = z.shape[0]
    tm = MOBA_BLOCK
    cb = COL_BQ // ATT_W
    blk = pl.BlockSpec((tm, ATT_W), lambda i: (i, 0))
    head_spec = pl.BlockSpec((N_ATT_HEADS, tm, LANES), lambda i: (0, i, 0))
    return pl.pallas_call(
        _moba_select_kernel,
        grid=(s // tm,),
        in_specs=[blk, blk, pl.BlockSpec((tm, ATT_W), lambda i: (i, cb + 2)),
                  pl.BlockSpec((ATT_W, ATT_W), lambda i: (0, 0))],
        out_specs=[head_spec, head_spec, pl.BlockSpec((N_ATT_HEADS, VT_ROWS, tm), lambda i: (0, 0, i))],
        out_shape=[jax.ShapeDtypeStruct((N_ATT_HEADS, s, LANES), BF16),
                   jax.ShapeDtypeStruct((N_ATT_HEADS, s, LANES), BF16),
                   jax.ShapeDtypeStruct((N_ATT_HEADS, VT_ROWS, s), BF16)],
        compiler_params=_cparams(("parallel",)),
        name="moba_select",
    )(q_rope, k_rope, z, km_mat)


FLASH_Q_SPLIT = 2


def _flash_kernel(qi_tab, ki_tab, q_ref, k_ref, vt_ref, g_ref, o_ref, m_ref, acc_ref, *, tile):
    step = pl.program_id(0)
    qi = qi_tab[step]
    ki = ki_tab[step]

    @pl.when(ki == 0)
    def _():
        m_ref[...] = jnp.full_like(m_ref, -jnp.inf)
        acc_ref[...] = jnp.zeros_like(acc_ref)

    def update(masked):
        qw = tile // FLASH_Q_SPLIT
        if masked:
            key = lax.broadcasted_iota(jnp.int32, (tile, qw), 0)
            qry = lax.broadcasted_iota(jnp.int32, (tile, qw), 1)
        units = [(h, j) for h in range(N_ATT_HEADS) for j in range(FLASH_Q_SPLIT)]
        scores = lambda h, j: _dot_nt(k_ref[h], q_ref[h, j * qw:(j + 1) * qw, :])
        st_next = scores(*units[0])
        for u, (h, j) in enumerate(units):
            st = st_next
            if u + 1 < len(units):
                st_next = scores(*units[u + 1])
            if masked:
                st = jnp.where(key <= qry + j * qw, st, NEG_BIG)
            cols = slice(j * qw, (j + 1) * qw)
            m_prev = m_ref[h, :, cols]
            m_new = jnp.maximum(m_prev, jnp.max(st, axis=0, keepdims=True))
            alpha = jnp.exp2(m_prev - m_new)
            p = jnp.exp2((st - m_new).astype(BF16))
            m_ref[h, :, cols] = m_new
            acc_ref[h, :, cols] = acc_ref[h, :, cols] * alpha + _dot(vt_ref[h], p)

    @pl.when(ki < qi)
    def _():
        update(False)

    @pl.when(ki == qi)
    def _():
        update(True)
        outs = []
        for h in range(N_ATT_HEADS):
            acc = acc_ref[h]
            o = acc[0:HEAD_DIM, :] / acc[HEAD_DIM:HEAD_DIM + 1, :]
            outs.append(o * lax.rsqrt(jnp.mean(o * o, axis=0, keepdims=True) + EPS))
        o_ref[...] = (jnp.concatenate(outs, axis=0).T * g_ref[...]).astype(o_ref.dtype)


def _flash_attention(q_aug, k_aug, v_t, gain_row, tile=1024):
    v = v_t
    s = v.shape[2]
    n = s // tile
    pairs = [(qi, ki) for qi in range(n) for ki in range(qi + 1)]
    qi_tab = jnp.asarray([p[0] for p in pairs], jnp.int32)
    ki_tab = jnp.asarray([p[1] for p in pairs], jnp.int32)
    grid_spec = pltpu.PrefetchScalarGridSpec(
        num_scalar_prefetch=2,
        grid=(len(pairs),),
        in_specs=[pl.BlockSpec((N_ATT_HEADS, tile, LANES), lambda i, qt, kt: (0, qt[i], 0)),
                  pl.BlockSpec((N_ATT_HEADS, tile, LANES), lambda i, qt, kt: (0, kt[i], 0)),
                  pl.BlockSpec((N_ATT_HEADS, VT_ROWS, tile), lambda i, qt, kt: (0, 0, kt[i])),
                  pl.BlockSpec((1, ATT_W), lambda i, qt, kt: (0, 0))],
        out_specs=pl.BlockSpec((tile, ATT_W), lambda i, qt, kt: (qt[i], 0)),
        scratch_shapes=[pltpu.VMEM((N_ATT_HEADS, 1, tile), F32),
                        pltpu.VMEM((N_ATT_HEADS, VT_ROWS, tile), F32)],
    )
    return pl.pallas_call(
        functools.partial(_flash_kernel, tile=tile),
        grid_spec=grid_spec,
        out_shape=jax.ShapeDtypeStruct((s, ATT_W), BF16),
        compiler_params=_cparams(("arbitrary",)),
        name="flash_attention",
    )(qi_tab, ki_tab, q_aug, k_aug, v, gain_row)


def _shift_rows(u, tail, s):
    rolled = pltpu.roll(u, s, axis=0)
    rolled_tail = pltpu.roll(tail, s, axis=0)
    row8 = lax.broadcasted_iota(jnp.int32, tail.shape, 0)
    top = jnp.where(row8 < s, rolled_tail, rolled[0:8])
    return jnp.concatenate([top, rolled[8:]], axis=0)


def _mlstm_kernel(qk_ref, v_ref, o_ref, g_ref, cw_ref, gain_ref, y_ref, tail_ref, c_ref, n_ref, m_ref):
    @pl.when(pl.program_id(0) == 0)
    def _():
        tail_ref[...] = jnp.zeros_like(tail_ref)
        c_ref[...] = jnp.zeros_like(c_ref)
        n_ref[...] = jnp.zeros_like(n_ref)
        m_ref[...] = jnp.zeros_like(m_ref)

    L = ML_CHUNK
    u = qk_ref[...]
    tail = tail_ref[...]
    cw = cw_ref[...]
    conv = u * cw[CONV_WIDTH - 1:CONV_WIDTH]
    for s in range(1, CONV_WIDTH):
        conv = conv + _shift_rows(u, tail, s) * cw[CONV_WIDTH - 1 - s:CONV_WIDTH - s]
    tail_ref[...] = u[L - 8:L]
    qk = conv * jax.nn.sigmoid(conv)

    g = g_ref[...]
    gt = g.T
    r = lax.broadcasted_iota(jnp.int32, (L, L), 0)
    c = lax.broadcasted_iota(jnp.int32, (L, L), 1)
    tril = c <= r
    kscale = ML_DIM ** -0.5
    heads = range(ML_HEADS)
    st = []
    for h in heads:
        sl = slice(h * ML_DIM, (h + 1) * ML_DIM)
        qh = qk[:, sl]
        kh = qk[:, ML_W + h * ML_DIM:ML_W + (h + 1) * ML_DIM] * kscale
        qb, kb, vb = qh.astype(BF16), kh.astype(BF16), v_ref[:, sl].astype(BF16)
        b_col = g[:, 8 + h:9 + h]
        i_col = g[:, 4 + h:5 + h]
        b_row = gt[8 + h:9 + h, :]
        i_row = gt[4 + h:5 + h, :]
        dmat = jnp.where(tril, b_col - b_row + i_row, -jnp.inf)
        b_last = b_row[:, L - 1:L]
        g_col = b_last - b_col + i_col
        st.append(dict(qh=qh, kh=kh, qb=qb, kb=kb, vb=vb, b_col=b_col, dmat=dmat, b_last=b_last, g_col=g_col,
                       qkt=_dot_nt(qb, kb), dmax=jnp.max(dmat, axis=1, keepdims=True),
                       gmax=jnp.max(g_col, axis=0, keepdims=True)))
    outs = []
    for h in heads:
        s_ = st[h]
        sl = slice(h * ML_DIM, (h + 1) * ML_DIM)
        m_prev = m_ref[h][:, 0:1]
        cmat = c_ref[h]
        nrow = n_ref[h]
        inter = s_["b_col"] + m_prev
        m_t = jnp.maximum(inter, s_["dmax"])
        w_intra = jnp.exp(s_["dmat"] - m_t)
        w_inter = jnp.exp(inter - m_t)
        a = s_["qkt"] * w_intra
        num = _dot(a.astype(BF16), s_["vb"]) + w_inter * _dot(s_["qb"], cmat.astype(BF16))
        den = jnp.sum(a, axis=1, keepdims=True) + w_inter * jnp.sum(s_["qh"] * nrow, axis=1, keepdims=True)
        hh = num / jnp.maximum(jnp.abs(den), jnp.exp(-m_t))
        y = jax.nn.sigmoid(o_ref[:, sl]) * hh
        outs.append(y * lax.rsqrt(jnp.mean(y * y, axis=1, keepdims=True) + EPS))
        s_.update(m_prev=m_prev, cmat=cmat, nrow=nrow)
    for h in heads:
        s_ = st[h]
        m_new = jnp.maximum(s_["b_last"] + s_["m_prev"], s_["gmax"])
        decay = jnp.exp(s_["b_last"] + s_["m_prev"] - m_new)
        kw = s_["kh"] * jnp.exp(s_["g_col"] - m_new)
        c_ref[h] = decay * s_["cmat"] + _dot(kw.T.astype(BF16), s_["vb"])
        n_ref[h] = decay * s_["nrow"] + jnp.sum(kw, axis=0, keepdims=True)
        m_ref[h] = jnp.broadcast_to(m_new, (1, LANES))
    y_ref[...] = (jnp.concatenate(outs, axis=1) * gain_ref[...]).astype(y_ref.dtype)


def _mlstm(z, gates, conv_w, gain_row):
    s = z.shape[0]
    L = ML_CHUNK
    return pl.pallas_call(
        _mlstm_kernel,
        grid=(s // L,),
        in_specs=[pl.BlockSpec((L, 2 * ML_W), lambda i: (i, COL_MQK // (2 * ML_W))),
                  pl.BlockSpec((L, ML_W), lambda i: (i, COL_MV // ML_W)),
                  pl.BlockSpec((L, ML_W), lambda i: (i, COL_MO // ML_W)),
                  pl.BlockSpec((L, LANES), lambda i: (i, 0)),
                  pl.BlockSpec((CONV_WIDTH, 2 * ML_W), lambda i: (0, 0)),
                  pl.BlockSpec((1, ML_W), lambda i: (0, 0))],
        out_specs=pl.BlockSpec((L, ML_W), lambda i: (i, 0)),
        out_shape=jax.ShapeDtypeStruct((s, ML_W), BF16),
        scratch_shapes=[pltpu.VMEM((8, 2 * ML_W), F32),
                        pltpu.VMEM((ML_HEADS, ML_DIM, ML_DIM), F32),
                        pltpu.VMEM((ML_HEADS, 1, ML_DIM), F32),
                        pltpu.VMEM((ML_HEADS, 1, LANES), F32)],
        compiler_params=_cparams(("arbitrary",)),
        name="mlstm",
    )(z, z, z, gates, conv_w, gain_row)


def _layer_norm(h, g, b):
    mu = jnp.mean(h, axis=1, keepdims=True)
    d = h - mu
    var = jnp.mean(d * d, axis=1, keepdims=True)
    return d * lax.rsqrt(var + EPS) * g + b


def _outproj_kernel(yf_ref, ym_ref, yb_ref, w_ref, x_ref, g_ref, b_ref, wr_ref, br_ref, x1_ref, lg_ref):
    mix = (_dot(yf_ref[...], w_ref[0:ATT_W, :]) + _dot(ym_ref[...], w_ref[ATT_W:ATT_W + ML_W, :])
           + _dot(yb_ref[...], w_ref[ATT_W + ML_W:, :]))
    x1 = _layer_norm(ALPHA * x_ref[...] + mix, g_ref[...], b_ref[...])
    x1_ref[...] = x1
    lg_ref[...] = jnp.dot(x1, wr_ref[...], preferred_element_type=F32,
                          precision=lax.Precision.HIGHEST) + br_ref[...]


def _outproj_ln_router(yf, ym, yb, w_out, x, ln_g, ln_b, w_router, b_router, tm=512):
    s = x.shape[0]
    const = lambda shape: pl.BlockSpec(shape, lambda i: (0, 0))
    rows = lambda w: pl.BlockSpec((tm, w), lambda i: (i, 0))
    return pl.pallas_call(
        _outproj_kernel,
        grid=(s // tm,),
        in_specs=[rows(ATT_W), rows(ML_W), rows(ATT_W), const((D_MODEL, D_MODEL)), rows(D_MODEL),
                  const((1, D_MODEL)), const((1, D_MODEL)), const((D_MODEL, LANES)), const((1, LANES))],
        out_specs=[rows(D_MODEL), rows(LANES)],
        out_shape=[jax.ShapeDtypeStruct((s, D_MODEL), F32), jax.ShapeDtypeStruct((s, LANES), F32)],
        compiler_params=_cparams(("parallel",)),
        name="outproj_ln_router",
    )(yf, ym, yb, w_out, x, ln_g, ln_b, w_router, b_router)


def _first_argmax(v, lane):
    mx = jnp.max(v, axis=1, keepdims=True)
    idx = jnp.min(jnp.where(v == mx, lane, LANES), axis=1, keepdims=True)
    return mx, idx


def _route_kernel(lg_ref, info_ref, cnt_ref, carry_ref):
    @pl.when(pl.program_id(0) == 0)
    def _():
        carry_ref[...] = jnp.zeros_like(carry_ref)

    lg = lg_ref[...]
    t = lg.shape[0]
    lane = lax.broadcasted_iota(jnp.int32, lg.shape, 1)
    is_grp = lane < N_GROUPS
    gmax, gsel = _first_argmax(jnp.where(is_grp, lg, -jnp.inf), lane)
    p_grp = 1.0 / jnp.sum(jnp.where(is_grp, jnp.exp(lg - gmax), 0.0), axis=1, keepdims=True)
    lo = N_GROUPS + EXPERTS_PER_GROUP * gsel
    el = jnp.where((lane >= lo) & (lane < lo + EXPERTS_PER_GROUP), lg, -jnp.inf)
    v0, i0 = _first_argmax(el, lane)
    v1, i1 = _first_argmax(jnp.where(lane == i0, -jnp.inf, el), lane)
    ex = jnp.exp(v1 - v0)
    w0 = p_grp / (1.0 + ex)
    w1 = p_grp * ex / (1.0 + ex)
    e0 = i0 - N_GROUPS
    e1 = i1 - N_GROUPS

    cnt = jnp.where((lane == e0) | (lane == e1), 1.0, 0.0)
    r = lax.broadcasted_iota(jnp.int32, (t, t), 0)
    c = lax.broadcasted_iota(jnp.int32, (t, t), 1)
    strict = jnp.where(c < r, 1.0, 0.0).astype(BF16)
    carry = carry_ref[...]
    before = _dot(strict, cnt.astype(BF16)) + carry
    rank0 = jnp.sum(jnp.where(lane == e0, before, 0.0), axis=1, keepdims=True)
    rank1 = jnp.sum(jnp.where(lane == e1, before, 0.0), axis=1, keepdims=True)
    carry = carry + jnp.sum(cnt, axis=0, keepdims=True)
    carry_ref[...] = carry
    cnt_ref[...] = carry
    vals = [e0.astype(F32), e1.astype(F32), w0, w1, rank0, rank1]
    info = jnp.zeros(lg.shape, F32)
    for j, val in enumerate(vals):
        info = jnp.where(lane == j, val, info)
    info_ref[...] = info


def _route(logits, tm=512):
    s = logits.shape[0]
    return pl.pallas_call(
        _route_kernel,
        grid=(s // tm,),
        in_specs=[pl.BlockSpec((tm, LANES), lambda i: (i, 0))],
        out_specs=[pl.BlockSpec((tm, LANES), lambda i: (i, 0)), pl.BlockSpec((1, LANES), lambda i: (0, 0))],
        out_shape=[jax.ShapeDtypeStruct((s, LANES), F32), jax.ShapeDtypeStruct((1, LANES), F32)],
        scratch_shapes=[pltpu.VMEM((1, LANES), F32)],
        compiler_params=_cparams(("arbitrary",)),
        name="route",
    )(logits)


MOE_ROWS = 256
TRASH_ROWS = 3 * MOE_ROWS


def _moe_kernel(blk_e, src0_ref, srcn_ref, slotp_ref, slotc_ref, x_hbm, wg_ref, wu_ref, wd_ref, out_hbm,
                xbuf0, xbuf1, ybuf0, ybuf1, wgb, wub, wdb, gsem, ssem, *, nblk, trash_base):
    b = pl.program_id(0)
    xbufs = (xbuf0, xbuf1)
    ybufs = (ybuf0, ybuf1)

    def row_in(tok, slot, r):
        return pltpu.make_async_copy(x_hbm.at[pl.ds(tok, 1), :], xbufs[slot].at[pl.ds(r, 1), :], gsem.at[slot])

    def row_out(dst, slot, r):
        return pltpu.make_async_copy(ybufs[slot].at[pl.ds(r, 1), :], out_hbm.at[pl.ds(dst, 1), :], ssem.at[slot])

    def start_rows(make, idx_ref, slot):
        for r in range(MOE_ROWS):
            make(idx_ref[0, 0, r], slot, r).start()

    def wait_rows(make, slot):
        def body(r, carry):
            make(0, slot, r).wait()
            return carry
        lax.fori_loop(0, MOE_ROWS, body, 0, unroll=8)

    @pl.when(b == 0)
    def _():
        for half in range(2):
            ybufs[half][...] = jnp.zeros_like(ybufs[half])
            cp = pltpu.make_async_copy(ybufs[half], out_hbm.at[pl.ds(trash_base + half * MOE_ROWS, MOE_ROWS), :],
                                       ssem.at[half])
            cp.start()
            cp.wait()
        start_rows(row_in, src0_ref, 0)

    @pl.when((b == 0) | (blk_e[b] != blk_e[jnp.maximum(b - 1, 0)]))
    def _():
        wgb[...] = wg_ref[0, 0].astype(BF16)
        wub[...] = wu_ref[0, 0].astype(BF16)
        wdb[...] = wd_ref[0, 0].astype(BF16)

    def step(cur):
        nxt = 1 - cur
        wait_rows(row_in, cur)
        start_rows(row_out, slotp_ref, nxt)
        start_rows(row_in, srcn_ref, nxt)

        xb = xbufs[cur][...].astype(BF16)
        gate = _dot(xb, wgb[...])
        up = _dot(xb, wub[...])
        hid = (gate * jax.nn.sigmoid(gate) * up).astype(BF16)
        y = _dot(hid, wdb[...])

        @pl.when(b > 0)
        def _():
            wait_rows(row_out, cur)

        ybufs[cur][...] = y

        @pl.when(b == nblk - 1)
        def _():
            start_rows(row_out, slotc_ref, cur)
            wait_rows(row_out, nxt)
            wait_rows(row_out, cur)
            wait_rows(row_in, nxt)

    for parity in range(2):
        pl.when(b % 2 == parity)(functools.partial(step, parity))


def _moe_ffn(x1, src_ext, slot_ext, blk_e, w_gate, w_up, w_down, layer):
    s = x1.shape[0]
    nblk = src_ext.shape[0] - 1
    idx_blk = (1, 1, MOE_ROWS)
    smem = pltpu.SMEM
    grid_spec = pltpu.PrefetchScalarGridSpec(
        num_scalar_prefetch=1,
        grid=(nblk,),
        in_specs=[pl.BlockSpec(idx_blk, lambda b, be: (0, 0, 0), memory_space=smem),
                  pl.BlockSpec(idx_blk, lambda b, be: (b + 1, 0, 0), memory_space=smem),
                  pl.BlockSpec(idx_blk, lambda b, be: (b, 0, 0), memory_space=smem),
                  pl.BlockSpec(idx_blk, lambda b, be: (b + 1, 0, 0), memory_space=smem),
                  pl.BlockSpec(memory_space=pl.ANY),
                  pl.BlockSpec((1, 1, D_MODEL, D_EXPERT), lambda b, be: (layer, be[b], 0, 0)),
                  pl.BlockSpec((1, 1, D_MODEL, D_EXPERT), lambda b, be: (layer, be[b], 0, 0)),
                  pl.BlockSpec((1, 1, D_EXPERT, D_MODEL), lambda b, be: (layer, be[b], 0, 0))],
        out_specs=pl.BlockSpec(memory_space=pl.ANY),
        scratch_shapes=[pltpu.VMEM((MOE_ROWS, D_MODEL), F32),
                        pltpu.VMEM((MOE_ROWS, D_MODEL), F32),
                        pltpu.VMEM((MOE_ROWS, D_MODEL), F32),
                        pltpu.VMEM((MOE_ROWS, D_MODEL), F32),
                        pltpu.VMEM((D_MODEL, D_EXPERT), BF16),
                        pltpu.VMEM((D_MODEL, D_EXPERT), BF16),
                        pltpu.VMEM((D_EXPERT, D_MODEL), BF16),
                        pltpu.SemaphoreType.DMA((2,)),
                        pltpu.SemaphoreType.DMA((2,))],
    )
    return pl.pallas_call(
        functools.partial(_moe_kernel, nblk=nblk, trash_base=2 * s),
        grid_spec=grid_spec,
        out_shape=jax.ShapeDtypeStruct((2 * s + TRASH_ROWS, D_MODEL), F32),
        compiler_params=_cparams(("arbitrary",)),
        name="moe_ffn",
    )(blk_e, src_ext, src_ext, slot_ext, slot_ext, x1, w_gate, w_up, w_down)


def _dest_kernel(info_ref, pstart_ref, d_ref):
    info = info_ref[...]
    tm = info.shape[0]
    lane = lax.broadcasted_iota(jnp.int32, info.shape, 1)
    ps = pstart_ref[...]
    dests = []
    for k in range(2):
        e = info[:, k:k + 1].astype(jnp.int32)
        dests.append(jnp.sum(jnp.where(lane == e, ps, 0.0), axis=1, keepdims=True) + info[:, 4 + k:5 + k])
    packed = jnp.where(lane == 0, dests[0], jnp.where(lane == 1, dests[1], 0.0))
    for j in range(tm // LANES):
        rows = packed[j * LANES:(j + 1) * LANES, :].T
        for k in range(2):
            d_ref[k, j:j + 1, :] = rows[k:k + 1, :].astype(jnp.int32)


def _dest(info, pstart_row, tm=1024):
    s = info.shape[0]
    return pl.pallas_call(
        _dest_kernel,
        grid=(s // tm,),
        in_specs=[pl.BlockSpec((tm, LANES), lambda i: (i, 0)), pl.BlockSpec((1, LANES), lambda i: (0, 0))],
        out_specs=pl.BlockSpec((2, tm // LANES, LANES), lambda i: (0, i, 0)),
        out_shape=jax.ShapeDtypeStruct((2, s // LANES, LANES), jnp.int32),
        compiler_params=_cparams(("parallel",)),
        name="moe_dest",
    )(info, pstart_row)


def _moe_plan(info, counts_row, s):
    counts = counts_row[0, :N_EXPERTS].astype(jnp.int32)
    padded = ((counts + MOE_ROWS - 1) // MOE_ROWS) * MOE_ROWS
    pends = jnp.cumsum(padded)
    pstarts = pends - padded
    nblk = (2 * s) // MOE_ROWS + N_EXPERTS
    p = nblk * MOE_ROWS
    dest = _dest(info, _pad_lanes(pstarts.astype(F32))).reshape(2 * s)
    rows = jnp.arange(p, dtype=jnp.int32)
    out_slot = (2 * s + rows % (2 * MOE_ROWS)).at[dest].set(jnp.arange(2 * s, dtype=jnp.int32))
    src_tok = jnp.where(out_slot >= 2 * s, 0, jnp.where(out_slot >= s, out_slot - s, out_slot))
    nused = (pends[-1] // MOE_ROWS).astype(jnp.int32)
    blk_start = jnp.minimum(jnp.arange(nblk, dtype=jnp.int32), nused - 1) * MOE_ROWS
    blk_e = jnp.sum((pends[None, :] <= blk_start[:, None]).astype(jnp.int32), axis=1)
    blk_e = jnp.minimum(blk_e, N_EXPERTS - 1)
    first_slots = 2 * s + 2 * MOE_ROWS + jnp.arange(MOE_ROWS, dtype=jnp.int32)
    src_ext = jnp.concatenate([src_tok, jnp.zeros((MOE_ROWS,), jnp.int32)])
    slot_ext = jnp.concatenate([first_slots, out_slot])
    return src_ext.reshape(nblk + 1, 1, MOE_ROWS), slot_ext.reshape(nblk + 1, 1, MOE_ROWS), blk_e


def _combine_kernel(x1_ref, ya_ref, yb_ref, info_ref, g_ref, b_ref, o_ref):
    info = info_ref[...]
    ffn = info[:, 2:3] * ya_ref[...] + info[:, 3:4] * yb_ref[...]
    o_ref[...] = _layer_norm(ALPHA * x1_ref[...] + ffn, g_ref[...], b_ref[...])


def _combine_ln(x1, y2, info, ln_g, ln_b, tm=512):
    s = x1.shape[0]
    const = pl.BlockSpec((1, D_MODEL), lambda i: (0, 0))
    return pl.pallas_call(
        _combine_kernel,
        grid=(s // tm,),
        in_specs=[pl.BlockSpec((tm, D_MODEL), lambda i: (i, 0)),
                  pl.BlockSpec((tm, D_MODEL), lambda i: (i, 0)),
                  pl.BlockSpec((tm, D_MODEL), lambda i: (i + s // tm, 0)),
                  pl.BlockSpec((tm, LANES), lambda i: (i, 0)), const, const],
        out_specs=pl.BlockSpec((tm, D_MODEL), lambda i: (i, 0)),
        out_shape=jax.ShapeDtypeStruct((s, D_MODEL), F32),
        compiler_params=_cparams(("parallel",)),
        name="combine_ln",
    )(x1, y2, y2, info, ln_g, ln_b)


def _pad_lanes(v, width=LANES):
    return jnp.zeros((1, width), F32).at[0, :v.shape[0]].set(v)


def _layer(layer, x, cos, sin, w_in, b_fox_f, b_mlstm_i, b_mlstm_f, conv_w, g_fox, g_mlstm, g_moba, w_out,
           ln1_g, ln1_b, w_grp, b_grp, w_exp_router, b_exp_router, w_gate, w_up, w_down, ln2_g, ln2_b):
    s = x.shape[0]
    gate_bias = _pad_lanes(jnp.concatenate([b_fox_f, b_mlstm_i, b_mlstm_f]))

    z = _inproj(x, w_in, layer)
    gates = _gate_prep(z, gate_bias)

    fq_aug, fk_aug, fv_b = _fox_prep(z, gates)
    y_fox = _flash_attention(fq_aug, fk_aug, fv_b, g_fox[None, :])

    q_rope, k_rope, kmean = _moba_rope(z, cos, sin)
    km = kmean[:, 0, :].reshape(s // MOBA_BLOCK, N_ATT_HEADS, HEAD_DIM)
    km_mat = jnp.zeros((N_ATT_HEADS, HEAD_DIM, N_ATT_HEADS, HEAD_DIM), F32)
    for h in range(N_ATT_HEADS):
        km_mat = km_mat.at[h, :s // MOBA_BLOCK, h, :].set(km[:, h, :])
    km_mat = km_mat.reshape(ATT_W, ATT_W)
    bq_aug, bk_aug, bv_b = _moba_select(q_rope, k_rope, z, km_mat)
    y_moba = _flash_attention(bq_aug, bk_aug, bv_b, g_moba[None, :])

    y_mlstm = _mlstm(z, gates, conv_w, g_mlstm[None, :])

    w_router = jnp.zeros((D_MODEL, LANES), F32)
    w_router = w_router.at[:, :N_GROUPS].set(w_grp)
    w_router = w_router.at[:, N_GROUPS:N_GROUPS + N_EXPERTS].set(w_exp_router.reshape(D_MODEL, N_EXPERTS))
    b_router = _pad_lanes(jnp.concatenate([b_grp, b_exp_router.reshape(N_EXPERTS)]))
    x1, logits = _outproj_ln_router(y_fox, y_mlstm, y_moba, w_out.astype(BF16), x, ln1_g[None, :],
                                    ln1_b[None, :], w_router, b_router)

    info, counts = _route(logits)
    src_ext, slot_ext, blk_e = _moe_plan(info, counts, s)
    y2 = _moe_ffn(x1, src_ext, slot_ext, blk_e, w_gate, w_up, w_down, layer)
    return _combine_ln(x1, y2, info, ln2_g[None, :], ln2_b[None, :])


def kernel(x, positions, w_in, b_fox_f, b_mlstm_i, b_mlstm_f, conv_w, g_fox, g_mlstm, g_moba, w_out, ln1_g, ln1_b, w_grp, b_grp, w_exp_router, b_exp_router, w_gate, w_up, w_down, ln2_g, ln2_b):
    assert x.shape[0] == 1
    xs = x[0]
    d = jnp.arange(ATT_W) % HEAD_DIM
    half = ROPE_DIM // 2
    inv = 1.0 / (ROPE_THETA ** (jnp.arange(0, ROPE_DIM, 2, dtype=F32) / ROPE_DIM))
    inv_row = jnp.where(d < ROPE_DIM, inv[d % half], 0.0)[None, :].astype(F32)
    sign_row = jnp.where(d < half, -1.0, 1.0)[None, :].astype(F32)
    cos, sin = _rope_tables(positions[0][:, None], inv_row, sign_row)
    for l in range(DEPTH):
        xs = _layer(l, xs, cos, sin, w_in, b_fox_f[l], b_mlstm_i[l], b_mlstm_f[l], conv_w[l], g_fox[l],
                    g_mlstm[l], g_moba[l], w_out[l], ln1_g[l], ln1_b[l], w_grp[l], b_grp[l],
                    w_exp_router[l], b_exp_router[l], w_gate, w_up, w_down, ln2_g[l], ln2_b[l])
    return xs[None]
```

```python
import functools

import jax
import jax.numpy as jnp
from jax import lax
from jax.experimental import pallas as pl
from jax.experimental.pallas import tpu as pltpu

D_MODEL = 1024
DEPTH = 2
HEAD_DIM = 64
N_ATT_HEADS = 4
ATT_W = N_ATT_HEADS * HEAD_DIM
ML_HEADS = 4
ML_DIM = 128
ML_W = ML_HEADS * ML_DIM
ML_CHUNK = 128
CONV_WIDTH = 4
ROPE_DIM = 16
ROPE_THETA = 500000.0
MOBA_BLOCK = 256
MOBA_TOPK = 3
N_GROUPS = 4
EXPERTS_PER_GROUP = 8
N_EXPERTS = N_GROUPS * EXPERTS_PER_GROUP
D_EXPERT = 512
ALPHA = (2 * DEPTH) ** 0.25
EPS = 1e-5

LANES = 128
SUBLANES = 8
NEG_BIG = -1e30
VMEM_LIMIT = 56 * 1024 * 1024

COL_MQK = 0
COL_MV = 1024
COL_MO = 1536
COL_FQ = 2048
COL_BQ = 2816
COL_GATE = 3584
Z_W = 3712

F32 = jnp.float32
BF16 = jnp.bfloat16


def _cparams(sem):
    return pltpu.CompilerParams(dimension_semantics=sem, vmem_limit_bytes=VMEM_LIMIT)


def _split3(c):
    hi = c.astype(BF16).astype(F32)
    r1 = c - hi
    mid = r1.astype(BF16).astype(F32)
    lo = (r1 - mid).astype(BF16).astype(F32)
    return hi, mid, lo


def _dot(a, b):
    return jnp.dot(a, b, preferred_element_type=F32)


def _dot_nt(a, b):
    return lax.dot_general(a, b, (((1,), (1,)), ((), ())), preferred_element_type=F32)


LOG2E = 1.4426950408889634
VT_ROWS = 80


def _store_vt(v_ref, vo_ref):
    vt = v_ref[...].T
    t = vt.shape[1]
    row = lax.broadcasted_iota(jnp.int32, (VT_ROWS - HEAD_DIM, t), 0)
    tail = jnp.where(row == 0, 1.0, 0.0)
    for h in range(N_ATT_HEADS):
        vo_ref[h] = jnp.concatenate([vt[h * HEAD_DIM:(h + 1) * HEAD_DIM, :], tail], axis=0).astype(BF16)


IN_W = 3596
_W_RUNS = ((COL_MQK, 772, 2308),
           (COL_MO, 2316, 2828),
           (COL_FQ, 0, 768),
           (COL_BQ, 2828, 3596))
_W_GATE_RUNS = ((768, 772), (2308, 2316))


def _cols(w_ref, r0, r1, a, b):
    a0 = (a // LANES) * LANES
    b0 = min(-(-b // LANES) * LANES, IN_W)
    return w_ref[0, r0:r1, a0:b0][:, a - a0:b - a0]


def _inproj_kernel(x_ref, w_ref, o_ref, wb_ref):
    @pl.when(pl.program_id(0) == 0)
    def _():
        rows = 256
        for r0 in range(0, D_MODEL, rows):
            r1 = r0 + rows
            for dst, a, b in _W_RUNS:
                wb_ref[r0:r1, dst:dst + (b - a)] = _cols(w_ref, r0, r1, a, b).astype(BF16)
            gate = [_cols(w_ref, r0, r1, a, b) for a, b in _W_GATE_RUNS]
            used = sum(b - a for a, b in _W_GATE_RUNS)
            gate.append(jnp.zeros((rows, LANES - used), F32))
            wb_ref[r0:r1, COL_GATE:] = jnp.concatenate(gate, axis=1).astype(BF16)

    xb = x_ref[...].astype(BF16)
    n = o_ref.shape[1]
    step = 512
    for j in range(0, n, step):
        w = min(step, n - j)
        o_ref[:, j:j + w] = _dot(xb, wb_ref[:, j:j + w])


def _inproj(x, w_in, layer, tm=256):
    s = x.shape[0]
    return pl.pallas_call(
        _inproj_kernel,
        grid=(s // tm,),
        in_specs=[pl.BlockSpec((tm, D_MODEL), lambda i: (i, 0)),
                  pl.BlockSpec((1, D_MODEL, IN_W), lambda i: (layer, 0, 0), pipeline_mode=pl.Buffered(1))],
        out_specs=pl.BlockSpec((tm, Z_W), lambda i: (i, 0)),
        out_shape=jax.ShapeDtypeStruct((s, Z_W), F32),
        scratch_shapes=[pltpu.VMEM((D_MODEL, Z_W), BF16)],
        compiler_params=_cparams(("arbitrary",)),
        name="inproj",
    )(x, w_in)


def _log_sigmoid(x):
    return jnp.minimum(x, 0.0) - jnp.log(1.0 + jnp.exp(-jnp.abs(x)))


def _gate_kernel(zg_ref, bias_ref, o_ref, carry_ref):
    @pl.when(pl.program_id(0) == 0)
    def _():
        carry_ref[...] = jnp.zeros_like(carry_ref)

    g = zg_ref[...] + bias_ref[...]
    ls = _log_sigmoid(g)
    t = g.shape[0]
    r = lax.broadcasted_iota(jnp.int32, (t, t), 0)
    c = lax.broadcasted_iota(jnp.int32, (t, t), 1)
    tri = c <= r
    tri_all = jnp.where(tri, 1.0, 0.0).astype(BF16)
    tri_chunk = jnp.where(tri & ((c // ML_CHUNK) == (r // ML_CHUNK)), 1.0, 0.0).astype(BF16)
    hi, mid, lo = _split3(ls)
    parts = [p.astype(BF16) for p in (hi, mid, lo)]
    cum_all = sum(_dot(tri_all, p) for p in parts)
    cum_chunk = sum(_dot(tri_chunk, p) for p in parts)
    carry = carry_ref[...]
    lane = lax.broadcasted_iota(jnp.int32, g.shape, 1)
    o_ref[...] = jnp.where(lane < 4, cum_all + carry, jnp.where(lane < 8, g, cum_chunk))
    carry_ref[...] = carry + cum_all[t - 1:t, :]


def _gate_prep(z, bias_row, tm=512):
    s = z.shape[0]
    return pl.pallas_call(
        _gate_kernel,
        grid=(s // tm,),
        in_specs=[pl.BlockSpec((tm, LANES), lambda i: (i, COL_GATE // LANES)),
                  pl.BlockSpec((1, LANES), lambda i: (0, 0))],
        out_specs=pl.BlockSpec((tm, LANES), lambda i: (i, 0)),
        out_shape=jax.ShapeDtypeStruct((s, LANES), F32),
        scratch_shapes=[pltpu.VMEM((1, LANES), F32)],
        compiler_params=_cparams(("arbitrary",)),
        name="gate_prep",
    )(z, bias_row)


def _fox_prep_kernel(q_ref, k_ref, v_ref, g_ref, qo_ref, ko_ref, vo_ref):
    g = g_ref[...]
    t = g.shape[0]
    lane = lax.broadcasted_iota(jnp.int32, (t, HEAD_DIM), 1)
    scale = HEAD_DIM ** -0.5 * LOG2E
    for h in range(N_ATT_HEADS):
        hi, mid, lo = _split3(g[:, h:h + 1] * LOG2E)
        aug_q = jnp.where(lane == 0, hi, jnp.where(lane == 1, mid, jnp.where(lane == 2, lo,
                          jnp.where(lane < 6, 1.0, 0.0))))
        aug_k = jnp.where(lane < 3, 1.0, jnp.where(lane == 3, -hi, jnp.where(lane == 4, -mid,
                          jnp.where(lane == 5, -lo, 0.0))))
        sl = slice(h * HEAD_DIM, (h + 1) * HEAD_DIM)
        qo_ref[h] = jnp.concatenate([(q_ref[:, sl] * scale).astype(BF16), aug_q.astype(BF16)], axis=1)
        ko_ref[h] = jnp.concatenate([k_ref[:, sl].astype(BF16), aug_k.astype(BF16)], axis=1)
    _store_vt(v_ref, vo_ref)


def _fox_prep(z, gates, tm=512):
    s = z.shape[0]
    cb = COL_FQ // ATT_W
    head_spec = pl.BlockSpec((N_ATT_HEADS, tm, LANES), lambda i: (0, i, 0))
    return pl.pallas_call(
        _fox_prep_kernel,
        grid=(s // tm,),
        in_specs=[pl.BlockSpec((tm, ATT_W), lambda i: (i, cb)),
                  pl.BlockSpec((tm, ATT_W), lambda i: (i, cb + 1)),
                  pl.BlockSpec((tm, ATT_W), lambda i: (i, cb + 2)),
                  pl.BlockSpec((tm, LANES), lambda i: (i, 0))],
        out_specs=[head_spec, head_spec, pl.BlockSpec((N_ATT_HEADS, VT_ROWS, tm), lambda i: (0, 0, i))],
        out_shape=[jax.ShapeDtypeStruct((N_ATT_HEADS, s, LANES), BF16),
                   jax.ShapeDtypeStruct((N_ATT_HEADS, s, LANES), BF16),
                   jax.ShapeDtypeStruct((N_ATT_HEADS, VT_ROWS, s), BF16)],
        compiler_params=_cparams(("parallel",)),
        name="fox_prep",
    )(z, z, z, gates)


def _rope_table_kernel(pos_ref, inv_ref, sign_ref, cos_ref, sin_ref):
    ang = pos_ref[...].astype(F32) * inv_ref[...]
    cos_ref[...] = jnp.cos(ang)
    sin_ref[...] = jnp.sin(ang) * sign_ref[...]


def _rope_tables(pos_col, inv_row, sign_row, tm=512):
    s = pos_col.shape[0]
    row = pl.BlockSpec((1, ATT_W), lambda i: (0, 0))
    out = pl.BlockSpec((tm, ATT_W), lambda i: (i, 0))
    return pl.pallas_call(
        _rope_table_kernel,
        grid=(s // tm,),
        in_specs=[pl.BlockSpec((tm, 1), lambda i: (i, 0)), row, row],
        out_specs=[out, out],
        out_shape=[jax.ShapeDtypeStruct((s, ATT_W), F32)] * 2,
        compiler_params=_cparams(("parallel",)),
        name="rope_tables",
    )(pos_col, inv_row, sign_row)


def _rope(u, cos, sin_signed):
    half = ROPE_DIM // 2
    lane = lax.broadcasted_iota(jnp.int32, u.shape, 1) % HEAD_DIM
    up = pltpu.roll(u, ATT_W - half, axis=1)
    down = pltpu.roll(u, half, axis=1)
    partner = jnp.where(lane < half, up, down)
    return u * cos + partner * sin_signed


def _moba_rope_kernel(q_ref, k_ref, cos_ref, sin_ref, qo_ref, ko_ref, km_ref):
    cos = cos_ref[...]
    sin = sin_ref[...]
    qo_ref[...] = _rope(q_ref[...], cos, sin)
    kr = _rope(k_ref[...], cos, sin)
    ko_ref[...] = kr
    km_ref[0] = jnp.mean(kr, axis=0, keepdims=True)


def _moba_rope(z, cos, sin):
    s = z.shape[0]
    tm = MOBA_BLOCK
    cb = COL_BQ // ATT_W
    blk = pl.BlockSpec((tm, ATT_W), lambda i: (i, 0))
    return pl.pallas_call(
        _moba_rope_kernel,
        grid=(s // tm,),
        in_specs=[pl.BlockSpec((tm, ATT_W), lambda i: (i, cb)),
                  pl.BlockSpec((tm, ATT_W), lambda i: (i, cb + 1)), blk, blk],
        out_specs=[blk, blk, pl.BlockSpec((1, 1, ATT_W), lambda i: (i, 0, 0))],
        out_shape=[jax.ShapeDtypeStruct((s, ATT_W), F32), jax.ShapeDtypeStruct((s, ATT_W), F32),
                   jax.ShapeDtypeStruct((s // tm, 1, ATT_W), F32)],
        compiler_params=_cparams(("parallel",)),
        name="moba_rope",
    )(z, z, cos, sin)


def _moba_select_kernel(q_ref, k_ref, v_ref, km_ref, qo_ref, ko_ref, vo_ref):
    own = pl.program_id(0)
    q = q_ref[...]
    gate_t = lax.dot_general(km_ref[...], q, (((1,), (1,)), ((), ())), preferred_element_type=F32,
                             precision=lax.Precision.HIGHEST)
    t = q.shape[0]
    blk = lax.broadcasted_iota(jnp.int32, (HEAD_DIM, t), 0)
    biases = []
    for h in range(N_ATT_HEADS):
        g = jnp.where(blk < own, gate_t[h * HEAD_DIM:(h + 1) * HEAD_DIM, :], -jnp.inf)
        bias = jnp.where(blk == own, 0.0, NEG_BIG)
        for r in range(MOBA_TOPK):
            mx = jnp.max(g, axis=0, keepdims=True)
            idx = jnp.min(jnp.where(g == mx, blk, HEAD_DIM), axis=0, keepdims=True)
            hit = blk == idx
            bias = jnp.where(hit, jnp.where(r < own, 0.0, bias), bias)
            g = jnp.where(hit, -jnp.inf, g)
        biases.append(bias)
    bias_all = jnp.concatenate(biases, axis=0).T
    lane = lax.broadcasted_iota(jnp.int32, (t, HEAD_DIM), 1)
    scale = HEAD_DIM ** -0.5 * LOG2E
    onehot_own = jnp.where(lane == own, 1.0, 0.0).astype(BF16)
    for h in range(N_ATT_HEADS):
        sl = slice(h * HEAD_DIM, (h + 1) * HEAD_DIM)
        qo_ref[h] = jnp.concatenate([(q[:, sl] * scale).astype(BF16), bias_all[:, sl].astype(BF16)], axis=1)
        ko_ref[h] = jnp.concatenate([k_ref[:, sl].astype(BF16), onehot_own], axis=1)
    _store_vt(v_ref, vo_ref)


def _moba_select(q_rope, k_rope, z, km_mat):
    s = z.shape[0]
    tm = MOBA_BLOCK
    cb = COL_BQ // ATT_W
    blk = pl.BlockSpec((tm, ATT_W), lambda i: (i, 0))
    head_spec = pl.BlockSpec((N_ATT_HEADS, tm, LANES), lambda i: (0, i, 0))
    return pl.pallas_call(
        _moba_select_kernel,
        grid=(s // tm,),
        in_specs=[blk, blk, pl.BlockSpec((tm, ATT_W), lambda i: (i, cb + 2)),
                  pl.BlockSpec((ATT_W, ATT_W), lambda i: (0, 0))],
        out_specs=[head_spec, head_spec, pl.BlockSpec((N_ATT_HEADS, VT_ROWS, tm), lambda i: (0, 0, i))],
        out_shape=[jax.ShapeDtypeStruct((N_ATT_HEADS, s, LANES), BF16),
                   jax.ShapeDtypeStruct((N_ATT_HEADS, s, LANES), BF16),
                   jax.ShapeDtypeStruct((N_ATT_HEADS, VT_ROWS, s), BF16)],
        compiler_params=_cparams(("parallel",)),
        name="moba_select",
    )(q_rope, k_rope, z, km_mat)


FLASH_Q_SPLIT = 2


def _flash_kernel(qi_tab, ki_tab, q_ref, k_ref, vt_ref, g_ref, o_ref, m_ref, acc_ref, *, tile):
    step = pl.program_id(0)
    qi = qi_tab[step]
    ki = ki_tab[step]

    @pl.when(ki == 0)
    def _():
        m_ref[...] = jnp.full_like(m_ref, -jnp.inf)
        acc_ref[...] = jnp.zeros_like(acc_ref)

    def update(masked):
        qw = tile // FLASH_Q_SPLIT
        if masked:
            key = lax.broadcasted_iota(jnp.int32, (tile, qw), 0)
            qry = lax.broadcasted_iota(jnp.int32, (tile, qw), 1)
        units = [(h, j) for h in range(N_ATT_HEADS) for j in range(FLASH_Q_SPLIT)]
        scores = lambda h, j: _dot_nt(k_ref[h], q_ref[h, j * qw:(j + 1) * qw, :])
        st_next = scores(*units[0])
        for u, (h, j) in enumerate(units):
            st = st_next
            if u + 1 < len(units):
                st_next = scores(*units[u + 1])
            if masked:
                st = jnp.where(key <= qry + j * qw, st, NEG_BIG)
            cols = slice(j * qw, (j + 1) * qw)
            m_prev = m_ref[h, :, cols]
            m_new = jnp.maximum(m_prev, jnp.max(st, axis=0, keepdims=True))
            alpha = jnp.exp2(m_prev - m_new)
            p = jnp.exp2((st - m_new).astype(BF16))
            m_ref[h, :, cols] = m_new
            acc_ref[h, :, cols] = acc_ref[h, :, cols] * alpha + _dot(vt_ref[h], p)

    @pl.when(ki < qi)
    def _():
        update(False)

    @pl.when(ki == qi)
    def _():
        update(True)
        outs = []
        for h in range(N_ATT_HEADS):
            acc = acc_ref[h]
            o = acc[0:HEAD_DIM, :] / acc[HEAD_DIM:HEAD_DIM + 1, :]
            outs.append(o * lax.rsqrt(jnp.mean(o * o, axis=0, keepdims=True) + EPS))
        o_ref[...] = (jnp.concatenate(outs, axis=0).T * g_ref[...]).astype(o_ref.dtype)


def _flash_attention(q_aug, k_aug, v_t, gain_row, tile=1024):
    v = v_t
    s = v.shape[2]
    n = s // tile
    pairs = [(qi, ki) for qi in range(n) for ki in range(qi + 1)]
    qi_tab = jnp.asarray([p[0] for p in pairs], jnp.int32)
    ki_tab = jnp.asarray([p[1] for p in pairs], jnp.int32)
    grid_spec = pltpu.PrefetchScalarGridSpec(
        num_scalar_prefetch=2,
        grid=(len(pairs),),
        in_specs=[pl.BlockSpec((N_ATT_HEADS, tile, LANES), lambda i, qt, kt: (0, qt[i], 0)),
                  pl.BlockSpec((N_ATT_HEADS, tile, LANES), lambda i, qt, kt: (0, kt[i], 0)),
                  pl.BlockSpec((N_ATT_HEADS, VT_ROWS, tile), lambda i, qt, kt: (0, 0, kt[i])),
                  pl.BlockSpec((1, ATT_W), lambda i, qt, kt: (0, 0))],
        out_specs=pl.BlockSpec((tile, ATT_W), lambda i, qt, kt: (qt[i], 0)),
        scratch_shapes=[pltpu.VMEM((N_ATT_HEADS, 1, tile), F32),
                        pltpu.VMEM((N_ATT_HEADS, VT_ROWS, tile), F32)],
    )
    return pl.pallas_call(
        functools.partial(_flash_kernel, tile=tile),
        grid_spec=grid_spec,
        out_shape=jax.ShapeDtypeStruct((s, ATT_W), BF16),
        compiler_params=_cparams(("arbitrary",)),
        name="flash_attention",
    )(qi_tab, ki_tab, q_aug, k_aug, v, gain_row)


def _shift_rows(u, tail, s):
    rolled = pltpu.roll(u, s, axis=0)
    rolled_tail = pltpu.roll(tail, s, axis=0)
    row8 = lax.broadcasted_iota(jnp.int32, tail.shape, 0)
    top = jnp.where(row8 < s, rolled_tail, rolled[0:8])
    return jnp.concatenate([top, rolled[8:]], axis=0)


def _mlstm_kernel(qk_ref, v_ref, o_ref, g_ref, cw_ref, gain_ref, y_ref, tail_ref, c_ref, n_ref, m_ref):
    @pl.when(pl.program_id(0) == 0)
    def _():
        tail_ref[...] = jnp.zeros_like(tail_ref)
        c_ref[...] = jnp.zeros_like(c_ref)
        n_ref[...] = jnp.zeros_like(n_ref)
        m_ref[...] = jnp.zeros_like(m_ref)

    L = ML_CHUNK
    u = qk_ref[...]
    tail = tail_ref[...]
    cw = cw_ref[...]
    conv = u * cw[CONV_WIDTH - 1:CONV_WIDTH]
    for s in range(1, CONV_WIDTH):
        conv = conv + _shift_rows(u, tail, s) * cw[CONV_WIDTH - 1 - s:CONV_WIDTH - s]
    tail_ref[...] = u[L - 8:L]
    qk = conv * jax.nn.sigmoid(conv)

    g = g_ref[...]
    gt = g.T
    r = lax.broadcasted_iota(jnp.int32, (L, L), 0)
    c = lax.broadcasted_iota(jnp.int32, (L, L), 1)
    tril = c <= r
    kscale = ML_DIM ** -0.5
    heads = range(ML_HEADS)
    st = []
    for h in heads:
        sl = slice(h * ML_DIM, (h + 1) * ML_DIM)
        qh = qk[:, sl]
        kh = qk[:, ML_W + h * ML_DIM:ML_W + (h + 1) * ML_DIM] * kscale
        qb, kb, vb = qh.astype(BF16), kh.astype(BF16), v_ref[:, sl].astype(BF16)
        b_col = g[:, 8 + h:9 + h]
        i_col = g[:, 4 + h:5 + h]
        b_row = gt[8 + h:9 + h, :]
        i_row = gt[4 + h:5 + h, :]
        dmat = jnp.where(tril, b_col - b_row + i_row, -jnp.inf)
        b_last = b_row[:, L - 1:L]
        g_col = b_last - b_col + i_col
        st.append(dict(qh=qh, kh=kh, qb=qb, kb=kb, vb=vb, b_col=b_col, dmat=dmat, b_last=b_last, g_col=g_col,
                       qkt=_dot_nt(qb, kb), dmax=jnp.max(dmat, axis=1, keepdims=True),
                       gmax=jnp.max(g_col, axis=0, keepdims=True)))
    outs = []
    for h in heads:
        s_ = st[h]
        sl = slice(h * ML_DIM, (h + 1) * ML_DIM)
        m_prev = m_ref[h][:, 0:1]
        cmat = c_ref[h]
        nrow = n_ref[h]
        inter = s_["b_col"] + m_prev
        m_t = jnp.maximum(inter, s_["dmax"])
        w_intra = jnp.exp(s_["dmat"] - m_t)
        w_inter = jnp.exp(inter - m_t)
        a = s_["qkt"] * w_intra
        num = _dot(a.astype(BF16), s_["vb"]) + w_inter * _dot(s_["qb"], cmat.astype(BF16))
        den = jnp.sum(a, axis=1, keepdims=True) + w_inter * jnp.sum(s_["qh"] * nrow, axis=1, keepdims=True)
        hh = num / jnp.maximum(jnp.abs(den), jnp.exp(-m_t))
        y = jax.nn.sigmoid(o_ref[:, sl]) * hh
        outs.append(y * lax.rsqrt(jnp.mean(y * y, axis=1, keepdims=True) + EPS))
        s_.update(m_prev=m_prev, cmat=cmat, nrow=nrow)
    for h in heads:
        s_ = st[h]
        m_new = jnp.maximum(s_["b_last"] + s_["m_prev"], s_["gmax"])
        decay = jnp.exp(s_["b_last"] + s_["m_prev"] - m_new)
        kw = s_["kh"] * jnp.exp(s_["g_col"] - m_new)
        c_ref[h] = decay * s_["cmat"] + _dot(kw.T.astype(BF16), s_["vb"])
        n_ref[h] = decay * s_["nrow"] + jnp.sum(kw, axis=0, keepdims=True)
        m_ref[h] = jnp.broadcast_to(m_new, (1, LANES))
    y_ref[...] = (jnp.concatenate(outs, axis=1) * gain_ref[...]).astype(y_ref.dtype)


def _mlstm(z, gates, conv_w, gain_row):
    s = z.shape[0]
    L = ML_CHUNK
    return pl.pallas_call(
        _mlstm_kernel,
        grid=(s // L,),
        in_specs=[pl.BlockSpec((L, 2 * ML_W), lambda i: (i, COL_MQK // (2 * ML_W))),
                  pl.BlockSpec((L, ML_W), lambda i: (i, COL_MV // ML_W)),
                  pl.BlockSpec((L, ML_W), lambda i: (i, COL_MO // ML_W)),
                  pl.BlockSpec((L, LANES), lambda i: (i, 0)),
                  pl.BlockSpec((CONV_WIDTH, 2 * ML_W), lambda i: (0, 0)),
                  pl.BlockSpec((1, ML_W), lambda i: (0, 0))],
        out_specs=pl.BlockSpec((L, ML_W), lambda i: (i, 0)),
        out_shape=jax.ShapeDtypeStruct((s, ML_W), BF16),
        scratch_shapes=[pltpu.VMEM((8, 2 * ML_W), F32),
                        pltpu.VMEM((ML_HEADS, ML_DIM, ML_DIM), F32),
                        pltpu.VMEM((ML_HEADS, 1, ML_DIM), F32),
                        pltpu.VMEM((ML_HEADS, 1, LANES), F32)],
        compiler_params=_cparams(("arbitrary",)),
        name="mlstm",
    )(z, z, z, gates, conv_w, gain_row)


def _layer_norm(h, g, b):
    mu = jnp.mean(h, axis=1, keepdims=True)
    d = h - mu
    var = jnp.mean(d * d, axis=1, keepdims=True)
    return d * lax.rsqrt(var + EPS) * g + b


def _outproj_kernel(yf_ref, ym_ref, yb_ref, w_ref, x_ref, g_ref, b_ref, wr_ref, br_ref, x1_ref, lg_ref):
    mix = (_dot(yf_ref[...], w_ref[0:ATT_W, :]) + _dot(ym_ref[...], w_ref[ATT_W:ATT_W + ML_W, :])
           + _dot(yb_ref[...], w_ref[ATT_W + ML_W:, :]))
    x1 = _layer_norm(ALPHA * x_ref[...] + mix, g_ref[...], b_ref[...])
    x1_ref[...] = x1
    lg_ref[...] = jnp.dot(x1, wr_ref[...], preferred_element_type=F32,
                          precision=lax.Precision.HIGHEST) + br_ref[...]


def _outproj_ln_router(yf, ym, yb, w_out, x, ln_g, ln_b, w_router, b_router, tm=512):
    s = x.shape[0]
    const = lambda shape: pl.BlockSpec(shape, lambda i: (0, 0))
    rows = lambda w: pl.BlockSpec((tm, w), lambda i: (i, 0))
    return pl.pallas_call(
        _outproj_kernel,
        grid=(s // tm,),
        in_specs=[rows(ATT_W), rows(ML_W), rows(ATT_W), const((D_MODEL, D_MODEL)), rows(D_MODEL),
                  const((1, D_MODEL)), const((1, D_MODEL)), const((D_MODEL, LANES)), const((1, LANES))],
        out_specs=[rows(D_MODEL), rows(LANES)],
        out_shape=[jax.ShapeDtypeStruct((s, D_MODEL), F32), jax.ShapeDtypeStruct((s, LANES), F32)],
        compiler_params=_cparams(("parallel",)),
        name="outproj_ln_router",
    )(yf, ym, yb, w_out, x, ln_g, ln_b, w_router, b_router)


def _first_argmax(v, lane):
    mx = jnp.max(v, axis=1, keepdims=True)
    idx = jnp.min(jnp.where(v == mx, lane, LANES), axis=1, keepdims=True)
    return mx, idx


def _route_kernel(lg_ref, info_ref, cnt_ref, carry_ref):
    @pl.when(pl.program_id(0) == 0)
    def _():
        carry_ref[...] = jnp.zeros_like(carry_ref)

    lg = lg_ref[...]
    t = lg.shape[0]
    lane = lax.broadcasted_iota(jnp.int32, lg.shape, 1)
    is_grp = lane < N_GROUPS
    gmax, gsel = _first_argmax(jnp.where(is_grp, lg, -jnp.inf), lane)
    p_grp = 1.0 / jnp.sum(jnp.where(is_grp, jnp.exp(lg - gmax), 0.0), axis=1, keepdims=True)
    lo = N_GROUPS + EXPERTS_PER_GROUP * gsel
    el = jnp.where((lane >= lo) & (lane < lo + EXPERTS_PER_GROUP), lg, -jnp.inf)
    v0, i0 = _first_argmax(el, lane)
    v1, i1 = _first_argmax(jnp.where(lane == i0, -jnp.inf, el), lane)
    ex = jnp.exp(v1 - v0)
    w0 = p_grp / (1.0 + ex)
    w1 = p_grp * ex / (1.0 + ex)
    e0 = i0 - N_GROUPS
    e1 = i1 - N_GROUPS

    cnt = jnp.where((lane == e0) | (lane == e1), 1.0, 0.0)
    r = lax.broadcasted_iota(jnp.int32, (t, t), 0)
    c = lax.broadcasted_iota(jnp.int32, (t, t), 1)
    strict = jnp.where(c < r, 1.0, 0.0).astype(BF16)
    carry = carry_ref[...]
    before = _dot(strict, cnt.astype(BF16)) + carry
    rank0 = jnp.sum(jnp.where(lane == e0, before, 0.0), axis=1, keepdims=True)
    rank1 = jnp.sum(jnp.where(lane == e1, before, 0.0), axis=1, keepdims=True)
    carry = carry + jnp.sum(cnt, axis=0, keepdims=True)
    carry_ref[...] = carry
    cnt_ref[...] = carry
    vals = [e0.astype(F32), e1.astype(F32), w0, w1, rank0, rank1]
    info = jnp.zeros(lg.shape, F32)
    for j, val in enumerate(vals):
        info = jnp.where(lane == j, val, info)
    info_ref[...] = info


def _route(logits, tm=512):
    s = logits.shape[0]
    return pl.pallas_call(
        _route_kernel,
        grid=(s // tm,),
        in_specs=[pl.BlockSpec((tm, LANES), lambda i: (i, 0))],
        out_specs=[pl.BlockSpec((tm, LANES), lambda i: (i, 0)), pl.BlockSpec((1, LANES), lambda i: (0, 0))],
        out_shape=[jax.ShapeDtypeStruct((s, LANES), F32), jax.ShapeDtypeStruct((1, LANES), F32)],
        scratch_shapes=[pltpu.VMEM((1, LANES), F32)],
        compiler_params=_cparams(("arbitrary",)),
        name="route",
    )(logits)


MOE_ROWS = 256
TRASH_ROWS = 3 * MOE_ROWS


def _moe_kernel(blk_e, src0_ref, srcn_ref, slotp_ref, slotc_ref, x_hbm, wg_ref, wu_ref, wd_ref, out_hbm,
                xbuf0, xbuf1, ybuf0, ybuf1, wgb, wub, wdb, gsem, ssem, *, nblk, trash_base):
    b = pl.program_id(0)
    xbufs = (xbuf0, xbuf1)
    ybufs = (ybuf0, ybuf1)

    def row_in(tok8, slot, r):
        return pltpu.make_async_copy(x_hbm.at[pl.ds(pl.multiple_of(tok8, SUBLANES), SUBLANES), :],
                                     xbufs[slot].at[pl.ds(SUBLANES * r, SUBLANES), :], gsem.at[slot])

    def row_out(dst8, slot, r):
        return pltpu.make_async_copy(ybufs[slot].at[pl.ds(SUBLANES * r, SUBLANES), :],
                                     out_hbm.at[pl.ds(pl.multiple_of(dst8, SUBLANES), SUBLANES), :], ssem.at[slot])

    def start_rows(make, idx_ref, slot):
        for r in range(MOE_ROWS):
            make(idx_ref[0, 0, r], slot, r).start(priority=r % 2)

    def wait_rows(make, slot):
        def body(r, carry):
            make(0, slot, r).wait()
            return carry
        lax.fori_loop(0, MOE_ROWS, body, 0, unroll=8)

    @pl.when(b == 0)
    def _():
        for half in range(2):
            ybufs[half][...] = jnp.zeros_like(ybufs[half])
            rows = SUBLANES * MOE_ROWS
            cp = pltpu.make_async_copy(ybufs[half], out_hbm.at[pl.ds(SUBLANES * trash_base + half * rows, rows), :],
                                       ssem.at[half])
            cp.start()
            cp.wait()
        start_rows(row_in, src0_ref, 0)

    @pl.when((b == 0) | (blk_e[b] != blk_e[jnp.maximum(b - 1, 0)]))
    def _():
        wgb[...] = wg_ref[0, 0].astype(BF16)
        wub[...] = wu_ref[0, 0].astype(BF16)
        wdb[...] = wd_ref[0, 0].astype(BF16)

    def step(cur):
        nxt = 1 - cur
        wait_rows(row_in, cur)
        start_rows(row_out, slotp_ref, nxt)
        start_rows(row_in, srcn_ref, nxt)

        nseg = D_MODEL // LANES
        seg = lambda j: pl.ds(j, MOE_ROWS, stride=SUBLANES)
        xb = jnp.concatenate([xbufs[cur][seg(j), :] for j in range(nseg)], axis=1).astype(BF16)
        gate = _dot(xb, wgb[...])
        up = _dot(xb, wub[...])
        hid = (gate * jax.nn.sigmoid(gate) * up).astype(BF16)
        y = _dot(hid, wdb[...])

        @pl.when(b > 0)
        def _():
            wait_rows(row_out, cur)

        for j in range(nseg):
            ybufs[cur][seg(j), :] = y[:, j * LANES:(j + 1) * LANES]

        @pl.when(b == nblk - 1)
        def _():
            start_rows(row_out, slotc_ref, cur)
            wait_rows(row_out, nxt)
            wait_rows(row_out, cur)
            wait_rows(row_in, nxt)

    for parity in range(2):
        pl.when(b % 2 == parity)(functools.partial(step, parity))


def _moe_ffn(x1, src_ext, slot_ext, blk_e, w_gate, w_up, w_down, layer):
    s = x1.shape[0] // SUBLANES
    nblk = src_ext.shape[0] - 1
    stage = (SUBLANES * MOE_ROWS, LANES)
    idx_blk = (1, 1, MOE_ROWS)
    smem = pltpu.SMEM
    grid_spec = pltpu.PrefetchScalarGridSpec(
        num_scalar_prefetch=1,
        grid=(nblk,),
        in_specs=[pl.BlockSpec(idx_blk, lambda b, be: (0, 0, 0), memory_space=smem),
                  pl.BlockSpec(idx_blk, lambda b, be: (b + 1, 0, 0), memory_space=smem),
                  pl.BlockSpec(idx_blk, lambda b, be: (b, 0, 0), memory_space=smem),
                  pl.BlockSpec(idx_blk, lambda b, be: (b + 1, 0, 0), memory_space=smem),
                  pl.BlockSpec(memory_space=pl.ANY),
                  pl.BlockSpec((1, 1, D_MODEL, D_EXPERT), lambda b, be: (layer, be[b], 0, 0)),
                  pl.BlockSpec((1, 1, D_MODEL, D_EXPERT), lambda b, be: (layer, be[b], 0, 0)),
                  pl.BlockSpec((1, 1, D_EXPERT, D_MODEL), lambda b, be: (layer, be[b], 0, 0))],
        out_specs=pl.BlockSpec(memory_space=pl.ANY),
        scratch_shapes=[pltpu.VMEM(stage, F32),
                        pltpu.VMEM(stage, F32),
                        pltpu.VMEM(stage, F32),
                        pltpu.VMEM(stage, F32),
                        pltpu.VMEM((D_MODEL, D_EXPERT), BF16),
                        pltpu.VMEM((D_MODEL, D_EXPERT), BF16),
                        pltpu.VMEM((D_EXPERT, D_MODEL), BF16),
                        pltpu.SemaphoreType.DMA((2,)),
                        pltpu.SemaphoreType.DMA((2,))],
    )
    return pl.pallas_call(
        functools.partial(_moe_kernel, nblk=nblk, trash_base=2 * s),
        grid_spec=grid_spec,
        out_shape=jax.ShapeDtypeStruct((SUBLANES * (2 * s + TRASH_ROWS), LANES), F32),
        compiler_params=_cparams(("arbitrary",)),
        name="moe_ffn",
    )(blk_e, src_ext, src_ext, slot_ext, slot_ext, x1, w_gate, w_up, w_down)


def _dest_kernel(info_ref, pstart_ref, d_ref):
    info = info_ref[...]
    tm = info.shape[0]
    lane = lax.broadcasted_iota(jnp.int32, info.shape, 1)
    ps = pstart_ref[...]
    dests = []
    for k in range(2):
        e = info[:, k:k + 1].astype(jnp.int32)
        dests.append(jnp.sum(jnp.where(lane == e, ps, 0.0), axis=1, keepdims=True) + info[:, 4 + k:5 + k])
    packed = jnp.where(lane == 0, dests[0], jnp.where(lane == 1, dests[1], 0.0))
    for j in range(tm // LANES):
        rows = packed[j * LANES:(j + 1) * LANES, :].T
        for k in range(2):
            d_ref[k, j:j + 1, :] = rows[k:k + 1, :].astype(jnp.int32)


def _dest(info, pstart_row, tm=1024):
    s = info.shape[0]
    return pl.pallas_call(
        _dest_kernel,
        grid=(s // tm,),
        in_specs=[pl.BlockSpec((tm, LANES), lambda i: (i, 0)), pl.BlockSpec((1, LANES), lambda i: (0, 0))],
        out_specs=pl.BlockSpec((2, tm // LANES, LANES), lambda i: (0, i, 0)),
        out_shape=jax.ShapeDtypeStruct((2, s // LANES, LANES), jnp.int32),
        compiler_params=_cparams(("parallel",)),
        name="moe_dest",
    )(info, pstart_row)


def _moe_plan(info, counts_row, s):
    counts = counts_row[0, :N_EXPERTS].astype(jnp.int32)
    padded = ((counts + MOE_ROWS - 1) // MOE_ROWS) * MOE_ROWS
    pends = jnp.cumsum(padded)
    pstarts = pends - padded
    nblk = (2 * s) // MOE_ROWS + N_EXPERTS
    p = nblk * MOE_ROWS
    dest = _dest(info, _pad_lanes(pstarts.astype(F32))).reshape(2 * s)
    rows = jnp.arange(p, dtype=jnp.int32)
    out_slot = (2 * s + rows % (2 * MOE_ROWS)).at[dest].set(jnp.arange(2 * s, dtype=jnp.int32))
    src_tok = jnp.where(out_slot >= 2 * s, 0, jnp.where(out_slot >= s, out_slot - s, out_slot))
    nused = (pends[-1] // MOE_ROWS).astype(jnp.int32)
    blk_start = jnp.minimum(jnp.arange(nblk, dtype=jnp.int32), nused - 1) * MOE_ROWS
    blk_e = jnp.sum((pends[None, :] <= blk_start[:, None]).astype(jnp.int32), axis=1)
    blk_e = jnp.minimum(blk_e, N_EXPERTS - 1)
    first_slots = 2 * s + 2 * MOE_ROWS + jnp.arange(MOE_ROWS, dtype=jnp.int32)
    src_ext = jnp.concatenate([src_tok, jnp.zeros((MOE_ROWS,), jnp.int32)]) * SUBLANES
    slot_ext = jnp.concatenate([first_slots, out_slot]) * SUBLANES
    return src_ext.reshape(nblk + 1, 1, MOE_ROWS), slot_ext.reshape(nblk + 1, 1, MOE_ROWS), blk_e


def _combine_kernel(x1_ref, ya_ref, yb_ref, info_ref, g_ref, b_ref, o_ref):
    info = info_ref[...]
    ffn = info[:, 2:3] * ya_ref[...] + info[:, 3:4] * yb_ref[...]
    o_ref[...] = _layer_norm(ALPHA * x1_ref[...] + ffn, g_ref[...], b_ref[...])


def _combine_ln(x1, y2, info, ln_g, ln_b, tm=512):
    s = x1.shape[0]
    const = pl.BlockSpec((1, D_MODEL), lambda i: (0, 0))
    return pl.pallas_call(
        _combine_kernel,
        grid=(s // tm,),
        in_specs=[pl.BlockSpec((tm, D_MODEL), lambda i: (i, 0)),
                  pl.BlockSpec((tm, D_MODEL), lambda i: (i, 0)),
                  pl.BlockSpec((tm, D_MODEL), lambda i: (i + s // tm, 0)),
                  pl.BlockSpec((tm, LANES), lambda i: (i, 0)), const, const],
        out_specs=pl.BlockSpec((tm, D_MODEL), lambda i: (i, 0)),
        out_shape=jax.ShapeDtypeStruct((s, D_MODEL), F32),
        compiler_params=_cparams(("parallel",)),
        name="combine_ln",
    )(x1, y2, y2, info, ln_g, ln_b)


def _pad_lanes(v, width=LANES):
    return jnp.zeros((1, width), F32).at[0, :v.shape[0]].set(v)


def _layer(layer, x, cos, sin, w_in, b_fox_f, b_mlstm_i, b_mlstm_f, conv_w, g_fox, g_mlstm, g_moba, w_out,
           ln1_g, ln1_b, w_grp, b_grp, w_exp_router, b_exp_router, w_gate, w_up, w_down, ln2_g, ln2_b):
    s = x.shape[0]
    gate_bias = _pad_lanes(jnp.concatenate([b_fox_f, b_mlstm_i, b_mlstm_f]))

    z = _inproj(x, w_in, layer)
    gates = _gate_prep(z, gate_bias)

    fq_aug, fk_aug, fv_b = _fox_prep(z, gates)
    y_fox = _flash_attention(fq_aug, fk_aug, fv_b, g_fox[None, :])

    q_rope, k_rope, kmean = _moba_rope(z, cos, sin)
    km = kmean[:, 0, :].reshape(s // MOBA_BLOCK, N_ATT_HEADS, HEAD_DIM)
    km_mat = jnp.zeros((N_ATT_HEADS, HEAD_DIM, N_ATT_HEADS, HEAD_DIM), F32)
    for h in range(N_ATT_HEADS):
        km_mat = km_mat.at[h, :s // MOBA_BLOCK, h, :].set(km[:, h, :])
    km_mat = km_mat.reshape(ATT_W, ATT_W)
    bq_aug, bk_aug, bv_b = _moba_select(q_rope, k_rope, z, km_mat)
    y_moba = _flash_attention(bq_aug, bk_aug, bv_b, g_moba[None, :])

    y_mlstm = _mlstm(z, gates, conv_w, g_mlstm[None, :])

    w_router = jnp.zeros((D_MODEL, LANES), F32)
    w_router = w_router.at[:, :N_GROUPS].set(w_grp)
    w_router = w_router.at[:, N_GROUPS:N_GROUPS + N_EXPERTS].set(w_exp_router.reshape(D_MODEL, N_EXPERTS))
    b_router = _pad_lanes(jnp.concatenate([b_grp, b_exp_router.reshape(N_EXPERTS)]))
    x1, logits = _outproj_ln_router(y_fox, y_mlstm, y_moba, w_out.astype(BF16), x, ln1_g[None, :],
                                    ln1_b[None, :], w_router, b_router)

    info, counts = _route(logits)
    src_ext, slot_ext, blk_e = _moe_plan(info, counts, s)
    y2 = _moe_ffn(x1.reshape(SUBLANES * s, LANES), src_ext, slot_ext, blk_e, w_gate, w_up, w_down, layer)
    y2 = y2.reshape(2 * s + TRASH_ROWS, D_MODEL)
    return _combine_ln(x1, y2, info, ln2_g[None, :], ln2_b[None, :])


def kernel(x, positions, w_in, b_fox_f, b_mlstm_i, b_mlstm_f, conv_w, g_fox, g_mlstm, g_moba, w_out, ln1_g, ln1_b, w_grp, b_grp, w_exp_router, b_exp_router, w_gate, w_up, w_down, ln2_g, ln2_b):
    assert x.shape[0] == 1
    xs = x[0]
    d = jnp.arange(ATT_W) % HEAD_DIM
    half = ROPE_DIM // 2
    inv = 1.0 / (ROPE_THETA ** (jnp.arange(0, ROPE_DIM, 2, dtype=F32) / ROPE_DIM))
    inv_row = jnp.where(d < ROPE_DIM, inv[d % half], 0.0)[None, :].astype(F32)
    sign_row = jnp.where(d < half, -1.0, 1.0)[None, :].astype(F32)
    cos, sin = _rope_tables(positions[0][:, None], inv_row, sign_row)
    for l in range(DEPTH):
        xs = _layer(l, xs, cos, sin, w_in, b_fox_f[l], b_mlstm_i[l], b_mlstm_f[l], conv_w[l], g_fox[l],
                    g_mlstm[l], g_moba[l], w_out[l], ln1_g[l], ln1_b[l], w_grp[l], b_grp[l],
                    w_exp_router[l], b_exp_router[l], w_gate, w_up, w_down, ln2_g[l], ln2_b[l])
    return xs[None]
```

```python
import functools

import jax
import jax.numpy as jnp
from jax import lax
from jax.experimental import pallas as pl
from jax.experimental.pallas import tpu as pltpu

D_MODEL = 1024
DEPTH = 2
HEAD_DIM = 64
N_ATT_HEADS = 4
ATT_W = N_ATT_HEADS * HEAD_DIM
ML_HEADS = 4
ML_DIM = 128
ML_W = ML_HEADS * ML_DIM
ML_CHUNK = 128
CONV_WIDTH = 4
ROPE_DIM = 16
ROPE_THETA = 500000.0
MOBA_BLOCK = 256
MOBA_TOPK = 3
N_GROUPS = 4
EXPERTS_PER_GROUP = 8
N_EXPERTS = N_GROUPS * EXPERTS_PER_GROUP
D_EXPERT = 512
ALPHA = (2 * DEPTH) ** 0.25
EPS = 1e-5

LANES = 128
SUBLANES = 8
NEG_BIG = -1e30
VMEM_LIMIT = 56 * 1024 * 1024

COL_MQK = 0
COL_MV = 1024
COL_MO = 1536
COL_FQ = 2048
COL_BQ = 2816
COL_GATE = 3584
Z_W = 3712

F32 = jnp.float32
BF16 = jnp.bfloat16


def _cparams(sem):
    return pltpu.CompilerParams(dimension_semantics=sem, vmem_limit_bytes=VMEM_LIMIT)


def _split3(c):
    hi = c.astype(BF16).astype(F32)
    r1 = c - hi
    mid = r1.astype(BF16).astype(F32)
    lo = (r1 - mid).astype(BF16).astype(F32)
    return hi, mid, lo


def _dot(a, b):
    return jnp.dot(a, b, preferred_element_type=F32)


def _dot_nt(a, b):
    return lax.dot_general(a, b, (((1,), (1,)), ((), ())), preferred_element_type=F32)


LOG2E = 1.4426950408889634
VT_ROWS = 80


def _store_vt(v_ref, vo_ref):
    vt = v_ref[...].T
    t = vt.shape[1]
    row = lax.broadcasted_iota(jnp.int32, (VT_ROWS - HEAD_DIM, t), 0)
    tail = jnp.where(row == 0, 1.0, 0.0)
    for h in range(N_ATT_HEADS):
        vo_ref[h] = jnp.concatenate([vt[h * HEAD_DIM:(h + 1) * HEAD_DIM, :], tail], axis=0).astype(BF16)


IN_W = 3596
_W_RUNS = ((COL_MQK, 772, 2308),
           (COL_MO, 2316, 2828),
           (COL_FQ, 0, 768),
           (COL_BQ, 2828, 3596))
_W_GATE_RUNS = ((768, 772), (2308, 2316))


def _cols(w_ref, r0, r1, a, b):
    a0 = (a // LANES) * LANES
    b0 = min(-(-b // LANES) * LANES, IN_W)
    return w_ref[0, r0:r1, a0:b0][:, a - a0:b - a0]


def _inproj_kernel(x_ref, w_ref, o_ref, wb_ref):
    @pl.when(pl.program_id(0) == 0)
    def _():
        rows = 256
        for r0 in range(0, D_MODEL, rows):
            r1 = r0 + rows
            for dst, a, b in _W_RUNS:
                wb_ref[r0:r1, dst:dst + (b - a)] = _cols(w_ref, r0, r1, a, b).astype(BF16)
            gate = [_cols(w_ref, r0, r1, a, b) for a, b in _W_GATE_RUNS]
            used = sum(b - a for a, b in _W_GATE_RUNS)
            gate.append(jnp.zeros((rows, LANES - used), F32))
            wb_ref[r0:r1, COL_GATE:] = jnp.concatenate(gate, axis=1).astype(BF16)

    xb = x_ref[...].astype(BF16)
    n = o_ref.shape[1]
    step = 512
    for j in range(0, n, step):
        w = min(step, n - j)
        o_ref[:, j:j + w] = _dot(xb, wb_ref[:, j:j + w])


def _inproj(x, w_in, layer, tm=256):
    s = x.shape[0]
    return pl.pallas_call(
        _inproj_kernel,
        grid=(s // tm,),
        in_specs=[pl.BlockSpec((tm, D_MODEL), lambda i: (i, 0)),
                  pl.BlockSpec((1, D_MODEL, IN_W), lambda i: (layer, 0, 0), pipeline_mode=pl.Buffered(1))],
        out_specs=pl.BlockSpec((tm, Z_W), lambda i: (i, 0)),
        out_shape=jax.ShapeDtypeStruct((s, Z_W), F32),
        scratch_shapes=[pltpu.VMEM((D_MODEL, Z_W), BF16)],
        compiler_params=_cparams(("arbitrary",)),
        name="inproj",
    )(x, w_in)


def _log_sigmoid(x):
    return jnp.minimum(x, 0.0) - jnp.log(1.0 + jnp.exp(-jnp.abs(x)))


def _gate_kernel(zg_ref, bias_ref, o_ref, carry_ref):
    @pl.when(pl.program_id(0) == 0)
    def _():
        carry_ref[...] = jnp.zeros_like(carry_ref)

    g = zg_ref[...] + bias_ref[...]
    ls = _log_sigmoid(g)
    t = g.shape[0]
    r = lax.broadcasted_iota(jnp.int32, (t, t), 0)
    c = lax.broadcasted_iota(jnp.int32, (t, t), 1)
    tri = c <= r
    tri_all = jnp.where(tri, 1.0, 0.0).astype(BF16)
    tri_chunk = jnp.where(tri & ((c // ML_CHUNK) == (r // ML_CHUNK)), 1.0, 0.0).astype(BF16)
    hi, mid, lo = _split3(ls)
    parts = [p.astype(BF16) for p in (hi, mid, lo)]
    cum_all = sum(_dot(tri_all, p) for p in parts)
    cum_chunk = sum(_dot(tri_chunk, p) for p in parts)
    carry = carry_ref[...]
    lane = lax.broadcasted_iota(jnp.int32, g.shape, 1)
    o_ref[...] = jnp.where(lane < 4, cum_all + carry, jnp.where(lane < 8, g, cum_chunk))
    carry_ref[...] = carry + cum_all[t - 1:t, :]


def _gate_prep(z, bias_row, tm=512):
    s = z.shape[0]
    return pl.pallas_call(
        _gate_kernel,
        grid=(s // tm,),
        in_specs=[pl.BlockSpec((tm, LANES), lambda i: (i, COL_GATE // LANES)),
                  pl.BlockSpec((1, LANES), lambda i: (0, 0))],
        out_specs=pl.BlockSpec((tm, LANES), lambda i: (i, 0)),
        out_shape=jax.ShapeDtypeStruct((s, LANES), F32),
        scratch_shapes=[pltpu.VMEM((1, LANES), F32)],
        compiler_params=_cparams(("arbitrary",)),
        name="gate_prep",
    )(z, bias_row)


def _fox_prep_kernel(q_ref, k_ref, v_ref, g_ref, qo_ref, ko_ref, vo_ref):
    g = g_ref[...]
    t = g.shape[0]
    lane = lax.broadcasted_iota(jnp.int32, (t, HEAD_DIM), 1)
    scale = HEAD_DIM ** -0.5 * LOG2E
    for h in range(N_ATT_HEADS):
        hi, mid, lo = _split3(g[:, h:h + 1] * LOG2E)
        aug_q = jnp.where(lane == 0, hi, jnp.where(lane == 1, mid, jnp.where(lane == 2, lo,
                          jnp.where(lane < 6, 1.0, 0.0))))
        aug_k = jnp.where(lane < 3, 1.0, jnp.where(lane == 3, -hi, jnp.where(lane == 4, -mid,
                          jnp.where(lane == 5, -lo, 0.0))))
        sl = slice(h * HEAD_DIM, (h + 1) * HEAD_DIM)
        qo_ref[h] = jnp.concatenate([(q_ref[:, sl] * scale).astype(BF16), aug_q.astype(BF16)], axis=1)
        ko_ref[h] = jnp.concatenate([k_ref[:, sl].astype(BF16), aug_k.astype(BF16)], axis=1)
    _store_vt(v_ref, vo_ref)


def _fox_prep(z, gates, tm=512):
    s = z.shape[0]
    cb = COL_FQ // ATT_W
    head_spec = pl.BlockSpec((N_ATT_HEADS, tm, LANES), lambda i: (0, i, 0))
    return pl.pallas_call(
        _fox_prep_kernel,
        grid=(s // tm,),
        in_specs=[pl.BlockSpec((tm, ATT_W), lambda i: (i, cb)),
                  pl.BlockSpec((tm, ATT_W), lambda i: (i, cb + 1)),
                  pl.BlockSpec((tm, ATT_W), lambda i: (i, cb + 2)),
                  pl.BlockSpec((tm, LANES), lambda i: (i, 0))],
        out_specs=[head_spec, head_spec, pl.BlockSpec((N_ATT_HEADS, VT_ROWS, tm), lambda i: (0, 0, i))],
        out_shape=[jax.ShapeDtypeStruct((N_ATT_HEADS, s, LANES), BF16),
                   jax.ShapeDtypeStruct((N_ATT_HEADS, s, LANES), BF16),
                   jax.ShapeDtypeStruct((N_ATT_HEADS, VT_ROWS, s), BF16)],
        compiler_params=_cparams(("parallel",)),
        name="fox_prep",
    )(z, z, z, gates)


def _rope_table_kernel(pos_ref, inv_ref, sign_ref, cos_ref, sin_ref):
    ang = pos_ref[...].astype(F32) * inv_ref[...]
    cos_ref[...] = jnp.cos(ang)
    sin_ref[...] = jnp.sin(ang) * sign_ref[...]


def _rope_tables(pos_col, inv_row, sign_row, tm=512):
    s = pos_col.shape[0]
    row = pl.BlockSpec((1, ATT_W), lambda i: (0, 0))
    out = pl.BlockSpec((tm, ATT_W), lambda i: (i, 0))
    return pl.pallas_call(
        _rope_table_kernel,
        grid=(s // tm,),
        in_specs=[pl.BlockSpec((tm, 1), lambda i: (i, 0)), row, row],
        out_specs=[out, out],
        out_shape=[jax.ShapeDtypeStruct((s, ATT_W), F32)] * 2,
        compiler_params=_cparams(("parallel",)),
        name="rope_tables",
    )(pos_col, inv_row, sign_row)


def _rope(u, cos, sin_signed):
    half = ROPE_DIM // 2
    lane = lax.broadcasted_iota(jnp.int32, u.shape, 1) % HEAD_DIM
    up = pltpu.roll(u, ATT_W - half, axis=1)
    down = pltpu.roll(u, half, axis=1)
    partner = jnp.where(lane < half, up, down)
    return u * cos + partner * sin_signed


def _moba_rope_kernel(q_ref, k_ref, cos_ref, sin_ref, qo_ref, ko_ref, km_ref):
    cos = cos_ref[...]
    sin = sin_ref[...]
    qo_ref[...] = _rope(q_ref[...], cos, sin)
    kr = _rope(k_ref[...], cos, sin)
    ko_ref[...] = kr
    km_ref[0] = jnp.mean(kr, axis=0, keepdims=True)


def _moba_rope(z, cos, sin):
    s = z.shape[0]
    tm = MOBA_BLOCK
    cb = COL_BQ // ATT_W
    blk = pl.BlockSpec((tm, ATT_W), lambda i: (i, 0))
    return pl.pallas_call(
        _moba_rope_kernel,
        grid=(s // tm,),
        in_specs=[pl.BlockSpec((tm, ATT_W), lambda i: (i, cb)),
                  pl.BlockSpec((tm, ATT_W), lambda i: (i, cb + 1)), blk, blk],
        out_specs=[blk, blk, pl.BlockSpec((1, 1, ATT_W), lambda i: (i, 0, 0))],
        out_shape=[jax.ShapeDtypeStruct((s, ATT_W), F32), jax.ShapeDtypeStruct((s, ATT_W), F32),
                   jax.ShapeDtypeStruct((s // tm, 1, ATT_W), F32)],
        compiler_params=_cparams(("parallel",)),
        name="moba_rope",
    )(z, z, cos, sin)


def _moba_select_kernel(q_ref, k_ref, v_ref, km_ref, qo_ref, ko_ref, vo_ref):
    own = pl.program_id(0)
    q = q_ref[...]
    gate_t = lax.dot_general(km_ref[...], q, (((1,), (1,)), ((), ())), preferred_element_type=F32,
                             precision=lax.Precision.HIGHEST)
    t = q.shape[0]
    blk = lax.broadcasted_iota(jnp.int32, (HEAD_DIM, t), 0)
    biases = []
    for h in range(N_ATT_HEADS):
        g = jnp.where(blk < own, gate_t[h * HEAD_DIM:(h + 1) * HEAD_DIM, :], -jnp.inf)
        bias = jnp.where(blk == own, 0.0, NEG_BIG)
        for r in range(MOBA_TOPK):
            mx = jnp.max(g, axis=0, keepdims=True)
            idx = jnp.min(jnp.where(g == mx, blk, HEAD_DIM), axis=0, keepdims=True)
            hit = blk == idx
            bias = jnp.where(hit, jnp.where(r < own, 0.0, bias), bias)
            g = jnp.where(hit, -jnp.inf, g)
        biases.append(bias)
    bias_all = jnp.concatenate(biases, axis=0).T
    lane = lax.broadcasted_iota(jnp.int32, (t, HEAD_DIM), 1)
    scale = HEAD_DIM ** -0.5 * LOG2E
    onehot_own = jnp.where(lane == own, 1.0, 0.0).astype(BF16)
    for h in range(N_ATT_HEADS):
        sl = slice(h * HEAD_DIM, (h + 1) * HEAD_DIM)
        qo_ref[h] = jnp.concatenate([(q[:, sl] * scale).astype(BF16), bias_all[:, sl].astype(BF16)], axis=1)
        ko_ref[h] = jnp.concatenate([k_ref[:, sl].astype(BF16), onehot_own], axis=1)
    _store_vt(v_ref, vo_ref)


def _moba_select(q_rope, k_rope, z, km_mat):
    s = z.shape[0]
    tm = MOBA_BLOCK
    cb = COL_BQ // ATT_W
    blk = pl.BlockSpec((tm, ATT_W), lambda i: (i, 0))
    head_spec = pl.BlockSpec((N_ATT_HEADS, tm, LANES), lambda i: (0, i, 0))
    return pl.pallas_call(
        _moba_select_kernel,
        grid=(s // tm,),
        in_specs=[blk, blk, pl.BlockSpec((tm, ATT_W), lambda i: (i, cb + 2)),
                  pl.BlockSpec((ATT_W, ATT_W), lambda i: (0, 0))],
        out_specs=[head_spec, head_spec, pl.BlockSpec((N_ATT_HEADS, VT_ROWS, tm), lambda i: (0, 0, i))],
        out_shape=[jax.ShapeDtypeStruct((N_ATT_HEADS, s, LANES), BF16),
                   jax.ShapeDtypeStruct((N_ATT_HEADS, s, LANES), BF16),
                   jax.ShapeDtypeStruct((N_ATT_HEADS, VT_ROWS, s), BF16)],
        compiler_params=_cparams(("parallel",)),
        name="moba_select",
    )(q_rope, k_rope, z, km_mat)


FLASH_Q_SPLIT = 2


def _flash_kernel(qi_tab, ki_tab, q_ref, k_ref, vt_ref, g_ref, o_ref, m_ref, acc_ref, *, tile):
    step = pl.program_id(0)
    qi = qi_tab[step]
    ki = ki_tab[step]

    @pl.when(ki == 0)
    def _():
        m_ref[...] = jnp.full_like(m_ref, -jnp.inf)
        acc_ref[...] = jnp.zeros_like(acc_ref)

    def update(masked):
        qw = tile // FLASH_Q_SPLIT
        if masked:
            key = lax.broadcasted_iota(jnp.int32, (tile, qw), 0)
            qry = lax.broadcasted_iota(jnp.int32, (tile, qw), 1)
        units = [(h, j) for h in range(N_ATT_HEADS) for j in range(FLASH_Q_SPLIT)]
        scores = lambda h, j: _dot_nt(k_ref[h], q_ref[h, j * qw:(j + 1) * qw, :])
        st_next = scores(*units[0])
        for u, (h, j) in enumerate(units):
            st = st_next
            if u + 1 < len(units):
                st_next = scores(*units[u + 1])
            if masked:
                st = jnp.where(key <= qry + j * qw, st, NEG_BIG)
            cols = slice(j * qw, (j + 1) * qw)
            m_prev = m_ref[h, :, cols]
            m_new = jnp.maximum(m_prev, jnp.max(st, axis=0, keepdims=True))
            alpha = jnp.exp2(m_prev - m_new)
            p = jnp.exp2((st - m_new).astype(BF16))
            m_ref[h, :, cols] = m_new
            acc_ref[h, :, cols] = acc_ref[h, :, cols] * alpha + _dot(vt_ref[h], p)

    @pl.when(ki < qi)
    def _():
        update(False)

    @pl.when(ki == qi)
    def _():
        update(True)
        outs = []
        for h in range(N_ATT_HEADS):
            acc = acc_ref[h]
            o = acc[0:HEAD_DIM, :] / acc[HEAD_DIM:HEAD_DIM + 1, :]
            outs.append(o * lax.rsqrt(jnp.mean(o * o, axis=0, keepdims=True) + EPS))
        o_ref[...] = (jnp.concatenate(outs, axis=0).T * g_ref[...]).astype(o_ref.dtype)


def _flash_attention(q_aug, k_aug, v_t, gain_row, tile=1024):
    v = v_t
    s = v.shape[2]
    n = s // tile
    pairs = [(qi, ki) for qi in range(n) for ki in range(qi + 1)]
    qi_tab = jnp.asarray([p[0] for p in pairs], jnp.int32)
    ki_tab = jnp.asarray([p[1] for p in pairs], jnp.int32)
    grid_spec = pltpu.PrefetchScalarGridSpec(
        num_scalar_prefetch=2,
        grid=(len(pairs),),
        in_specs=[pl.BlockSpec((N_ATT_HEADS, tile, LANES), lambda i, qt, kt: (0, qt[i], 0)),
                  pl.BlockSpec((N_ATT_HEADS, tile, LANES), lambda i, qt, kt: (0, kt[i], 0)),
                  pl.BlockSpec((N_ATT_HEADS, VT_ROWS, tile), lambda i, qt, kt: (0, 0, kt[i])),
                  pl.BlockSpec((1, ATT_W), lambda i, qt, kt: (0, 0))],
        out_specs=pl.BlockSpec((tile, ATT_W), lambda i, qt, kt: (qt[i], 0)),
        scratch_shapes=[pltpu.VMEM((N_ATT_HEADS, 1, tile), F32),
                        pltpu.VMEM((N_ATT_HEADS, VT_ROWS, tile), F32)],
    )
    return pl.pallas_call(
        functools.partial(_flash_kernel, tile=tile),
        grid_spec=grid_spec,
        out_shape=jax.ShapeDtypeStruct((s, ATT_W), BF16),
        compiler_params=_cparams(("arbitrary",)),
        name="flash_attention",
    )(qi_tab, ki_tab, q_aug, k_aug, v, gain_row)


def _shift_rows(u, tail, s):
    rolled = pltpu.roll(u, s, axis=0)
    rolled_tail = pltpu.roll(tail, s, axis=0)
    row8 = lax.broadcasted_iota(jnp.int32, tail.shape, 0)
    top = jnp.where(row8 < s, rolled_tail, rolled[0:8])
    return jnp.concatenate([top, rolled[8:]], axis=0)


def _mlstm_kernel(qk_ref, v_ref, o_ref, g_ref, cw_ref, gain_ref, y_ref, tail_ref, c_ref, n_ref, m_ref):
    @pl.when(pl.program_id(0) == 0)
    def _():
        tail_ref[...] = jnp.zeros_like(tail_ref)
        c_ref[...] = jnp.zeros_like(c_ref)
        n_ref[...] = jnp.zeros_like(n_ref)
        m_ref[...] = jnp.zeros_like(m_ref)

    L = ML_CHUNK
    u = qk_ref[...]
    tail = tail_ref[...]
    cw = cw_ref[...]
    conv = u * cw[CONV_WIDTH - 1:CONV_WIDTH]
    for s in range(1, CONV_WIDTH):
        conv = conv + _shift_rows(u, tail, s) * cw[CONV_WIDTH - 1 - s:CONV_WIDTH - s]
    tail_ref[...] = u[L - 8:L]
    qk = conv * jax.nn.sigmoid(conv)

    g = g_ref[...]
    gt = g.T
    r = lax.broadcasted_iota(jnp.int32, (L, L), 0)
    c = lax.broadcasted_iota(jnp.int32, (L, L), 1)
    tril = c <= r
    kscale = ML_DIM ** -0.5
    heads = range(ML_HEADS)
    st = []
    for h in heads:
        sl = slice(h * ML_DIM, (h + 1) * ML_DIM)
        qh = qk[:, sl]
        kh = qk[:, ML_W + h * ML_DIM:ML_W + (h + 1) * ML_DIM] * kscale
        qb, kb, vb = qh.astype(BF16), kh.astype(BF16), v_ref[:, sl].astype(BF16)
        b_col = g[:, 8 + h:9 + h]
        i_col = g[:, 4 + h:5 + h]
        b_row = gt[8 + h:9 + h, :]
        i_row = gt[4 + h:5 + h, :]
        dmat = jnp.where(tril, b_col - b_row + i_row, -jnp.inf)
        b_last = b_row[:, L - 1:L]
        g_col = b_last - b_col + i_col
        st.append(dict(qh=qh, kh=kh, qb=qb, kb=kb, vb=vb, b_col=b_col, dmat=dmat, b_last=b_last, g_col=g_col,
                       qkt=_dot_nt(qb, kb), dmax=jnp.max(dmat, axis=1, keepdims=True),
                       gmax=jnp.max(g_col, axis=0, keepdims=True)))
    outs = []
    for h in heads:
        s_ = st[h]
        sl = slice(h * ML_DIM, (h + 1) * ML_DIM)
        m_prev = m_ref[h][:, 0:1]
        cmat = c_ref[h]
        nrow = n_ref[h]
        inter = s_["b_col"] + m_prev
        m_t = jnp.maximum(inter, s_["dmax"])
        w_intra = jnp.exp(s_["dmat"] - m_t)
        w_inter = jnp.exp(inter - m_t)
        a = s_["qkt"] * w_intra
        num = _dot(a.astype(BF16), s_["vb"]) + w_inter * _dot(s_["qb"], cmat.astype(BF16))
        den = jnp.sum(a, axis=1, keepdims=True) + w_inter * jnp.sum(s_["qh"] * nrow, axis=1, keepdims=True)
        hh = num / jnp.maximum(jnp.abs(den), jnp.exp(-m_t))
        y = jax.nn.sigmoid(o_ref[:, sl]) * hh
        outs.append(y * lax.rsqrt(jnp.mean(y * y, axis=1, keepdims=True) + EPS))
        s_.update(m_prev=m_prev, cmat=cmat, nrow=nrow)
    for h in heads:
        s_ = st[h]
        m_new = jnp.maximum(s_["b_last"] + s_["m_prev"], s_["gmax"])
        decay = jnp.exp(s_["b_last"] + s_["m_prev"] - m_new)
        kw = s_["kh"] * jnp.exp(s_["g_col"] - m_new)
        c_ref[h] = decay * s_["cmat"] + _dot(kw.T.astype(BF16), s_["vb"])
        n_ref[h] = decay * s_["nrow"] + jnp.sum(kw, axis=0, keepdims=True)
        m_ref[h] = jnp.broadcast_to(m_new, (1, LANES))
    y_ref[...] = (jnp.concatenate(outs, axis=1) * gain_ref[...]).astype(y_ref.dtype)


def _mlstm(z, gates, conv_w, gain_row):
    s = z.shape[0]
    L = ML_CHUNK
    return pl.pallas_call(
        _mlstm_kernel,
        grid=(s // L,),
        in_specs=[pl.BlockSpec((L, 2 * ML_W), lambda i: (i, COL_MQK // (2 * ML_W))),
                  pl.BlockSpec((L, ML_W), lambda i: (i, COL_MV // ML_W)),
                  pl.BlockSpec((L, ML_W), lambda i: (i, COL_MO // ML_W)),
                  pl.BlockSpec((L, LANES), lambda i: (i, 0)),
                  pl.BlockSpec((CONV_WIDTH, 2 * ML_W), lambda i: (0, 0)),
                  pl.BlockSpec((1, ML_W), lambda i: (0, 0))],
        out_specs=pl.BlockSpec((L, ML_W), lambda i: (i, 0)),
        out_shape=jax.ShapeDtypeStruct((s, ML_W), BF16),
        scratch_shapes=[pltpu.VMEM((8, 2 * ML_W), F32),
                        pltpu.VMEM((ML_HEADS, ML_DIM, ML_DIM), F32),
                        pltpu.VMEM((ML_HEADS, 1, ML_DIM), F32),
                        pltpu.VMEM((ML_HEADS, 1, LANES), F32)],
        compiler_params=_cparams(("arbitrary",)),
        name="mlstm",
    )(z, z, z, gates, conv_w, gain_row)


def _layer_norm(h, g, b):
    mu = jnp.mean(h, axis=1, keepdims=True)
    d = h - mu
    var = jnp.mean(d * d, axis=1, keepdims=True)
    return d * lax.rsqrt(var + EPS) * g + b


def _row_seg(j, rows):
    return pl.ds(j, rows, stride=SUBLANES)


def _outproj_kernel(yf_ref, ym_ref, yb_ref, w_ref, x_ref, g_ref, b_ref, wr_ref, br_ref, x1_ref, x1t_ref, lg_ref):
    mix = (_dot(yf_ref[...], w_ref[0:ATT_W, :]) + _dot(ym_ref[...], w_ref[ATT_W:ATT_W + ML_W, :])
           + _dot(yb_ref[...], w_ref[ATT_W + ML_W:, :]))
    x1 = _layer_norm(ALPHA * x_ref[...] + mix, g_ref[...], b_ref[...])
    x1_ref[...] = x1
    tm = x1.shape[0]
    for j in range(D_MODEL // LANES):
        x1t_ref[_row_seg(j, tm), :] = x1[:, j * LANES:(j + 1) * LANES]
    lg_ref[...] = jnp.dot(x1, wr_ref[...], preferred_element_type=F32,
                          precision=lax.Precision.HIGHEST) + br_ref[...]


def _outproj_ln_router(yf, ym, yb, w_out, x, ln_g, ln_b, w_router, b_router, tm=512):
    s = x.shape[0]
    const = lambda shape: pl.BlockSpec(shape, lambda i: (0, 0))
    rows = lambda w: pl.BlockSpec((tm, w), lambda i: (i, 0))
    return pl.pallas_call(
        _outproj_kernel,
        grid=(s // tm,),
        in_specs=[rows(ATT_W), rows(ML_W), rows(ATT_W), const((D_MODEL, D_MODEL)), rows(D_MODEL),
                  const((1, D_MODEL)), const((1, D_MODEL)), const((D_MODEL, LANES)), const((1, LANES))],
        out_specs=[rows(D_MODEL), pl.BlockSpec((SUBLANES * tm, LANES), lambda i: (i, 0)), rows(LANES)],
        out_shape=[jax.ShapeDtypeStruct((s, D_MODEL), F32), jax.ShapeDtypeStruct((SUBLANES * s, LANES), F32),
                   jax.ShapeDtypeStruct((s, LANES), F32)],
        compiler_params=_cparams(("parallel",)),
        name="outproj_ln_router",
    )(yf, ym, yb, w_out, x, ln_g, ln_b, w_router, b_router)


def _first_argmax(v, lane):
    mx = jnp.max(v, axis=1, keepdims=True)
    idx = jnp.min(jnp.where(v == mx, lane, LANES), axis=1, keepdims=True)
    return mx, idx


def _route_kernel(lg_ref, info_ref, cnt_ref, carry_ref):
    @pl.when(pl.program_id(0) == 0)
    def _():
        carry_ref[...] = jnp.zeros_like(carry_ref)

    lg = lg_ref[...]
    t = lg.shape[0]
    lane = lax.broadcasted_iota(jnp.int32, lg.shape, 1)
    is_grp = lane < N_GROUPS
    gmax, gsel = _first_argmax(jnp.where(is_grp, lg, -jnp.inf), lane)
    p_grp = 1.0 / jnp.sum(jnp.where(is_grp, jnp.exp(lg - gmax), 0.0), axis=1, keepdims=True)
    lo = N_GROUPS + EXPERTS_PER_GROUP * gsel
    el = jnp.where((lane >= lo) & (lane < lo + EXPERTS_PER_GROUP), lg, -jnp.inf)
    v0, i0 = _first_argmax(el, lane)
    v1, i1 = _first_argmax(jnp.where(lane == i0, -jnp.inf, el), lane)
    ex = jnp.exp(v1 - v0)
    w0 = p_grp / (1.0 + ex)
    w1 = p_grp * ex / (1.0 + ex)
    e0 = i0 - N_GROUPS
    e1 = i1 - N_GROUPS

    cnt = jnp.where((lane == e0) | (lane == e1), 1.0, 0.0)
    r = lax.broadcasted_iota(jnp.int32, (t, t), 0)
    c = lax.broadcasted_iota(jnp.int32, (t, t), 1)
    strict = jnp.where(c < r, 1.0, 0.0).astype(BF16)
    carry = carry_ref[...]
    before = _dot(strict, cnt.astype(BF16)) + carry
    rank0 = jnp.sum(jnp.where(lane == e0, before, 0.0), axis=1, keepdims=True)
    rank1 = jnp.sum(jnp.where(lane == e1, before, 0.0), axis=1, keepdims=True)
    carry = carry + jnp.sum(cnt, axis=0, keepdims=True)
    carry_ref[...] = carry
    cnt_ref[...] = carry
    vals = [e0.astype(F32), e1.astype(F32), w0, w1, rank0, rank1]
    info = jnp.zeros(lg.shape, F32)
    for j, val in enumerate(vals):
        info = jnp.where(lane == j, val, info)
    info_ref[...] = info


def _route(logits, tm=512):
    s = logits.shape[0]
    return pl.pallas_call(
        _route_kernel,
        grid=(s // tm,),
        in_specs=[pl.BlockSpec((tm, LANES), lambda i: (i, 0))],
        out_specs=[pl.BlockSpec((tm, LANES), lambda i: (i, 0)), pl.BlockSpec((1, LANES), lambda i: (0, 0))],
        out_shape=[jax.ShapeDtypeStruct((s, LANES), F32), jax.ShapeDtypeStruct((1, LANES), F32)],
        scratch_shapes=[pltpu.VMEM((1, LANES), F32)],
        compiler_params=_cparams(("arbitrary",)),
        name="route",
    )(logits)


MOE_ROWS = 256
TRASH_ROWS = 3 * MOE_ROWS


def _moe_kernel(blk_e, src0_ref, srcn_ref, slotp_ref, slotc_ref, x_hbm, wg_ref, wu_ref, wd_ref, out_hbm,
                xbuf0, xbuf1, ybuf0, ybuf1, wgb, wub, wdb, gsem, ssem, *, nblk, trash_base):
    b = pl.program_id(0)
    xbufs = (xbuf0, xbuf1)
    ybufs = (ybuf0, ybuf1)

    def row_in(tok8, slot, r):
        return pltpu.make_async_copy(x_hbm.at[pl.ds(pl.multiple_of(tok8, SUBLANES), SUBLANES), :],
                                     xbufs[slot].at[pl.ds(SUBLANES * r, SUBLANES), :], gsem.at[slot])

    def row_out(dst8, slot, r):
        return pltpu.make_async_copy(ybufs[slot].at[pl.ds(SUBLANES * r, SUBLANES), :],
                                     out_hbm.at[pl.ds(pl.multiple_of(dst8, SUBLANES), SUBLANES), :], ssem.at[slot])

    def start_rows(make, idx_ref, slot):
        for r in range(MOE_ROWS):
            make(idx_ref[0, 0, r], slot, r).start(priority=r % 2)

    def wait_rows(make, slot):
        rows = SUBLANES * MOE_ROWS
        if make is row_in:
            pltpu.make_async_copy(x_hbm.at[pl.ds(0, rows), :], xbufs[slot], gsem.at[slot]).wait()
        else:
            pltpu.make_async_copy(ybufs[slot], out_hbm.at[pl.ds(0, rows), :], ssem.at[slot]).wait()

    @pl.when(b == 0)
    def _():
        for half in range(2):
            ybufs[half][...] = jnp.zeros_like(ybufs[half])
            rows = SUBLANES * MOE_ROWS
            cp = pltpu.make_async_copy(ybufs[half], out_hbm.at[pl.ds(SUBLANES * trash_base + half * rows, rows), :],
                                       ssem.at[half])
            cp.start()
            cp.wait()
        start_rows(row_in, src0_ref, 0)

    @pl.when((b == 0) | (blk_e[b] != blk_e[jnp.maximum(b - 1, 0)]))
    def _():
        wgb[...] = wg_ref[0, 0].astype(BF16)
        wub[...] = wu_ref[0, 0].astype(BF16)
        wdb[...] = wd_ref[0, 0].astype(BF16)

    def step(cur):
        nxt = 1 - cur
        wait_rows(row_in, cur)
        start_rows(row_out, slotp_ref, nxt)
        start_rows(row_in, srcn_ref, nxt)

        nseg = D_MODEL // LANES
        seg = lambda j: _row_seg(j, MOE_ROWS)
        xb = jnp.concatenate([xbufs[cur][seg(j), :] for j in range(nseg)], axis=1).astype(BF16)
        gate = _dot(xb, wgb[...])
        up = _dot(xb, wub[...])
        hid = (gate * jax.nn.sigmoid(gate) * up).astype(BF16)
        y = _dot(hid, wdb[...])

        @pl.when(b > 0)
        def _():
            wait_rows(row_out, cur)

        for j in range(nseg):
            ybufs[cur][seg(j), :] = y[:, j * LANES:(j + 1) * LANES]

        @pl.when(b == nblk - 1)
        def _():
            start_rows(row_out, slotc_ref, cur)
            wait_rows(row_out, nxt)
            wait_rows(row_out, cur)
            wait_rows(row_in, nxt)

    for parity in range(2):
        pl.when(b % 2 == parity)(functools.partial(step, parity))


def _moe_ffn(x1, src_ext, slot_ext, blk_e, w_gate, w_up, w_down, layer):
    s = x1.shape[0] // SUBLANES
    nblk = src_ext.shape[0] - 1
    stage = (SUBLANES * MOE_ROWS, LANES)
    idx_blk = (1, 1, MOE_ROWS)
    smem = pltpu.SMEM
    grid_spec = pltpu.PrefetchScalarGridSpec(
        num_scalar_prefetch=1,
        grid=(nblk,),
        in_specs=[pl.BlockSpec(idx_blk, lambda b, be: (0, 0, 0), memory_space=smem),
                  pl.BlockSpec(idx_blk, lambda b, be: (b + 1, 0, 0), memory_space=smem),
                  pl.BlockSpec(idx_blk, lambda b, be: (b, 0, 0), memory_space=smem),
                  pl.BlockSpec(idx_blk, lambda b, be: (b + 1, 0, 0), memory_space=smem),
                  pl.BlockSpec(memory_space=pl.ANY),
                  pl.BlockSpec((1, 1, D_MODEL, D_EXPERT), lambda b, be: (layer, be[b], 0, 0)),
                  pl.BlockSpec((1, 1, D_MODEL, D_EXPERT), lambda b, be: (layer, be[b], 0, 0)),
                  pl.BlockSpec((1, 1, D_EXPERT, D_MODEL), lambda b, be: (layer, be[b], 0, 0))],
        out_specs=pl.BlockSpec(memory_space=pl.ANY),
        scratch_shapes=[pltpu.VMEM(stage, F32),
                        pltpu.VMEM(stage, F32),
                        pltpu.VMEM(stage, F32),
                        pltpu.VMEM(stage, F32),
                        pltpu.VMEM((D_MODEL, D_EXPERT), BF16),
                        pltpu.VMEM((D_MODEL, D_EXPERT), BF16),
                        pltpu.VMEM((D_EXPERT, D_MODEL), BF16),
                        pltpu.SemaphoreType.DMA((2,)),
                        pltpu.SemaphoreType.DMA((2,))],
    )
    return pl.pallas_call(
        functools.partial(_moe_kernel, nblk=nblk, trash_base=2 * s),
        grid_spec=grid_spec,
        out_shape=jax.ShapeDtypeStruct((SUBLANES * (2 * s + TRASH_ROWS), LANES), F32),
        compiler_params=_cparams(("arbitrary",)),
        name="moe_ffn",
    )(blk_e, src_ext, src_ext, slot_ext, slot_ext, x1, w_gate, w_up, w_down)


def _dest_kernel(info_ref, pstart_ref, d_ref):
    info = info_ref[...]
    tm = info.shape[0]
    lane = lax.broadcasted_iota(jnp.int32, info.shape, 1)
    ps = pstart_ref[...]
    dests = []
    for k in range(2):
        e = info[:, k:k + 1].astype(jnp.int32)
        dests.append(jnp.sum(jnp.where(lane == e, ps, 0.0), axis=1, keepdims=True) + info[:, 4 + k:5 + k])
    packed = jnp.where(lane == 0, dests[0], jnp.where(lane == 1, dests[1], 0.0))
    for j in range(tm // LANES):
        rows = packed[j * LANES:(j + 1) * LANES, :].T
        for k in range(2):
            d_ref[k, j:j + 1, :] = rows[k:k + 1, :].astype(jnp.int32)


def _dest(info, pstart_row, tm=1024):
    s = info.shape[0]
    return pl.pallas_call(
        _dest_kernel,
        grid=(s // tm,),
        in_specs=[pl.BlockSpec((tm, LANES), lambda i: (i, 0)), pl.BlockSpec((1, LANES), lambda i: (0, 0))],
        out_specs=pl.BlockSpec((2, tm // LANES, LANES), lambda i: (0, i, 0)),
        out_shape=jax.ShapeDtypeStruct((2, s // LANES, LANES), jnp.int32),
        compiler_params=_cparams(("parallel",)),
        name="moe_dest",
    )(info, pstart_row)


def _moe_plan(info, counts_row, s):
    counts = counts_row[0, :N_EXPERTS].astype(jnp.int32)
    padded = ((counts + MOE_ROWS - 1) // MOE_ROWS) * MOE_ROWS
    pends = jnp.cumsum(padded)
    pstarts = pends - padded
    nblk = (2 * s) // MOE_ROWS + N_EXPERTS
    p = nblk * MOE_ROWS
    dest = _dest(info, _pad_lanes(pstarts.astype(F32))).reshape(2 * s)
    rows = jnp.arange(p, dtype=jnp.int32)
    out_slot = (2 * s + rows % (2 * MOE_ROWS)).at[dest].set(jnp.arange(2 * s, dtype=jnp.int32))
    src_tok = jnp.where(out_slot >= 2 * s, 0, jnp.where(out_slot >= s, out_slot - s, out_slot))
    nused = (pends[-1] // MOE_ROWS).astype(jnp.int32)
    blk_start = jnp.minimum(jnp.arange(nblk, dtype=jnp.int32), nused - 1) * MOE_ROWS
    blk_e = jnp.sum((pends[None, :] <= blk_start[:, None]).astype(jnp.int32), axis=1)
    blk_e = jnp.minimum(blk_e, N_EXPERTS - 1)
    first_slots = 2 * s + 2 * MOE_ROWS + jnp.arange(MOE_ROWS, dtype=jnp.int32)
    src_ext = jnp.concatenate([src_tok, jnp.zeros((MOE_ROWS,), jnp.int32)]) * SUBLANES
    slot_ext = jnp.concatenate([first_slots, out_slot]) * SUBLANES
    return src_ext.reshape(nblk + 1, 1, MOE_ROWS), slot_ext.reshape(nblk + 1, 1, MOE_ROWS), blk_e


def _combine_kernel(x1_ref, ya_ref, yb_ref, info_ref, g_ref, b_ref, o_ref):
    info = info_ref[...]
    tm = info.shape[0]
    wide = lambda ref: jnp.concatenate([ref[_row_seg(j, tm), :] for j in range(D_MODEL // LANES)], axis=1)
    ffn = info[:, 2:3] * wide(ya_ref) + info[:, 3:4] * wide(yb_ref)
    o_ref[...] = _layer_norm(ALPHA * x1_ref[...] + ffn, g_ref[...], b_ref[...])


def _combine_ln(x1, y2, info, ln_g, ln_b, tm=512):
    s = x1.shape[0]
    const = pl.BlockSpec((1, D_MODEL), lambda i: (0, 0))
    return pl.pallas_call(
        _combine_kernel,
        grid=(s // tm,),
        in_specs=[pl.BlockSpec((tm, D_MODEL), lambda i: (i, 0)),
                  pl.BlockSpec((SUBLANES * tm, LANES), lambda i: (i, 0)),
                  pl.BlockSpec((SUBLANES * tm, LANES), lambda i: (i + s // tm, 0)),
                  pl.BlockSpec((tm, LANES), lambda i: (i, 0)), const, const],
        out_specs=pl.BlockSpec((tm, D_MODEL), lambda i: (i, 0)),
        out_shape=jax.ShapeDtypeStruct((s, D_MODEL), F32),
        compiler_params=_cparams(("parallel",)),
        name="combine_ln",
    )(x1, y2, y2, info, ln_g, ln_b)


def _pad_lanes(v, width=LANES):
    return jnp.zeros((1, width), F32).at[0, :v.shape[0]].set(v)


def _layer(layer, x, cos, sin, w_in, b_fox_f, b_mlstm_i, b_mlstm_f, conv_w, g_fox, g_mlstm, g_moba, w_out,
           ln1_g, ln1_b, w_grp, b_grp, w_exp_router, b_exp_router, w_gate, w_up, w_down, ln2_g, ln2_b):
    s = x.shape[0]
    gate_bias = _pad_lanes(jnp.concatenate([b_fox_f, b_mlstm_i, b_mlstm_f]))

    z = _inproj(x, w_in, layer)
    gates = _gate_prep(z, gate_bias)

    fq_aug, fk_aug, fv_b = _fox_prep(z, gates)
    y_fox = _flash_attention(fq_aug, fk_aug, fv_b, g_fox[None, :])

    q_rope, k_rope, kmean = _moba_rope(z, cos, sin)
    km = kmean[:, 0, :].reshape(s // MOBA_BLOCK, N_ATT_HEADS, HEAD_DIM)
    km_mat = jnp.zeros((N_ATT_HEADS, HEAD_DIM, N_ATT_HEADS, HEAD_DIM), F32)
    for h in range(N_ATT_HEADS):
        km_mat = km_mat.at[h, :s // MOBA_BLOCK, h, :].set(km[:, h, :])
    km_mat = km_mat.reshape(ATT_W, ATT_W)
    bq_aug, bk_aug, bv_b = _moba_select(q_rope, k_rope, z, km_mat)
    y_moba = _flash_attention(bq_aug, bk_aug, bv_b, g_moba[None, :])

    y_mlstm = _mlstm(z, gates, conv_w, g_mlstm[None, :])

    w_router = jnp.zeros((D_MODEL, LANES), F32)
    w_router = w_router.at[:, :N_GROUPS].set(w_grp)
    w_router = w_router.at[:, N_GROUPS:N_GROUPS + N_EXPERTS].set(w_exp_router.reshape(D_MODEL, N_EXPERTS))
    b_router = _pad_lanes(jnp.concatenate([b_grp, b_exp_router.reshape(N_EXPERTS)]))
    x1, x1_tiles, logits = _outproj_ln_router(y_fox, y_mlstm, y_moba, w_out.astype(BF16), x, ln1_g[None, :],
                                              ln1_b[None, :], w_router, b_router)

    info, counts = _route(logits)
    src_ext, slot_ext, blk_e = _moe_plan(info, counts, s)
    y2 = _moe_ffn(x1_tiles, src_ext, slot_ext, blk_e, w_gate, w_up, w_down, layer)
    return _combine_ln(x1, y2, info, ln2_g[None, :], ln2_b[None, :])


def kernel(x, positions, w_in, b_fox_f, b_mlstm_i, b_mlstm_f, conv_w, g_fox, g_mlstm, g_moba, w_out, ln1_g, ln1_b, w_grp, b_grp, w_exp_router, b_exp_router, w_gate, w_up, w_down, ln2_g, ln2_b):
    assert x.shape[0] == 1
    xs = x[0]
    d = jnp.arange(ATT_W) % HEAD_DIM
    half = ROPE_DIM // 2
    inv = 1.0 / (ROPE_THETA ** (jnp.arange(0, ROPE_DIM, 2, dtype=F32) / ROPE_DIM))
    inv_row = jnp.where(d < ROPE_DIM, inv[d % half], 0.0)[None, :].astype(F32)
    sign_row = jnp.where(d < half, -1.0, 1.0)[None, :].astype(F32)
    cos, sin = _rope_tables(positions[0][:, None], inv_row, sign_row)
    for l in range(DEPTH):
        xs = _layer(l, xs, cos, sin, w_in, b_fox_f[l], b_mlstm_i[l], b_mlstm_f[l], conv_w[l], g_fox[l],
                    g_mlstm[l], g_moba[l], w_out[l], ln1_g[l], ln1_b[l], w_grp[l], b_grp[l],
                    w_exp_router[l], b_exp_router[l], w_gate, w_up, w_down, ln2_g[l], ln2_b[l])
    return xs[None]
```

```python
import functools

import jax
import jax.numpy as jnp
from jax import lax
from jax.experimental import pallas as pl
from jax.experimental.pallas import tpu as pltpu

D_MODEL = 1024
DEPTH = 2
HEAD_DIM = 64
N_ATT_HEADS = 4
ATT_W = N_ATT_HEADS * HEAD_DIM
ML_HEADS = 4
ML_DIM = 128
ML_W = ML_HEADS * ML_DIM
ML_CHUNK = 128
CONV_WIDTH = 4
ROPE_DIM = 16
ROPE_THETA = 500000.0
MOBA_BLOCK = 256
MOBA_TOPK = 3
N_GROUPS = 4
EXPERTS_PER_GROUP = 8
N_EXPERTS = N_GROUPS * EXPERTS_PER_GROUP
D_EXPERT = 512
ALPHA = (2 * DEPTH) ** 0.25
EPS = 1e-5

LANES = 128
SUBLANES = 8
NEG_BIG = -1e30
VMEM_LIMIT = 56 * 1024 * 1024

COL_MQK = 0
COL_MV = 1024
COL_MO = 1536
COL_FQ = 2048
COL_BQ = 2816
COL_GATE = 3584
Z_W = 3712

F32 = jnp.float32
BF16 = jnp.bfloat16


def _cparams(sem):
    return pltpu.CompilerParams(dimension_semantics=sem, vmem_limit_bytes=VMEM_LIMIT)


def _split3(c):
    hi = c.astype(BF16).astype(F32)
    r1 = c - hi
    mid = r1.astype(BF16).astype(F32)
    lo = (r1 - mid).astype(BF16).astype(F32)
    return hi, mid, lo


def _dot(a, b):
    return jnp.dot(a, b, preferred_element_type=F32)


def _dot_nt(a, b):
    return lax.dot_general(a, b, (((1,), (1,)), ((), ())), preferred_element_type=F32)


LOG2E = 1.4426950408889634
VT_ROWS = 80


def _store_vt(v_ref, vo_ref):
    vt = v_ref[...].T
    t = vt.shape[1]
    row = lax.broadcasted_iota(jnp.int32, (VT_ROWS - HEAD_DIM, t), 0)
    tail = jnp.where(row == 0, 1.0, 0.0)
    for h in range(N_ATT_HEADS):
        vo_ref[h] = jnp.concatenate([vt[h * HEAD_DIM:(h + 1) * HEAD_DIM, :], tail], axis=0).astype(BF16)


IN_W = 3596
_W_RUNS = ((COL_MQK, 772, 2308),
           (COL_MO, 2316, 2828),
           (COL_FQ, 0, 768),
           (COL_BQ, 2828, 3596))
_W_GATE_RUNS = ((768, 772), (2308, 2316))


def _cols(w_ref, r0, r1, a, b):
    a0 = (a // LANES) * LANES
    b0 = min(-(-b // LANES) * LANES, IN_W)
    return w_ref[0, r0:r1, a0:b0][:, a - a0:b - a0]


def _inproj_kernel(x_ref, w_ref, o_ref, wb_ref):
    @pl.when(pl.program_id(0) == 0)
    def _():
        rows = 256
        for r0 in range(0, D_MODEL, rows):
            r1 = r0 + rows
            for dst, a, b in _W_RUNS:
                wb_ref[r0:r1, dst:dst + (b - a)] = _cols(w_ref, r0, r1, a, b).astype(BF16)
            gate = [_cols(w_ref, r0, r1, a, b) for a, b in _W_GATE_RUNS]
            used = sum(b - a for a, b in _W_GATE_RUNS)
            gate.append(jnp.zeros((rows, LANES - used), F32))
            wb_ref[r0:r1, COL_GATE:] = jnp.concatenate(gate, axis=1).astype(BF16)

    xb = x_ref[...].astype(BF16)
    n = o_ref.shape[1]
    step = 512
    for j in range(0, n, step):
        w = min(step, n - j)
        o_ref[:, j:j + w] = _dot(xb, wb_ref[:, j:j + w])


def _inproj(x, w_in, layer, tm=256):
    s = x.shape[0]
    return pl.pallas_call(
        _inproj_kernel,
        grid=(s // tm,),
        in_specs=[pl.BlockSpec((tm, D_MODEL), lambda i: (i, 0)),
                  pl.BlockSpec((1, D_MODEL, IN_W), lambda i: (layer, 0, 0), pipeline_mode=pl.Buffered(1))],
        out_specs=pl.BlockSpec((tm, Z_W), lambda i: (i, 0)),
        out_shape=jax.ShapeDtypeStruct((s, Z_W), F32),
        scratch_shapes=[pltpu.VMEM((D_MODEL, Z_W), BF16)],
        compiler_params=_cparams(("arbitrary",)),
        name="inproj",
    )(x, w_in)


def _log_sigmoid(x):
    return jnp.minimum(x, 0.0) - jnp.log(1.0 + jnp.exp(-jnp.abs(x)))


def _gate_kernel(zg_ref, bias_ref, o_ref, carry_ref):
    @pl.when(pl.program_id(0) == 0)
    def _():
        carry_ref[...] = jnp.zeros_like(carry_ref)

    g = zg_ref[...] + bias_ref[...]
    ls = _log_sigmoid(g)
    t = g.shape[0]
    r = lax.broadcasted_iota(jnp.int32, (t, t), 0)
    c = lax.broadcasted_iota(jnp.int32, (t, t), 1)
    tri = c <= r
    tri_all = jnp.where(tri, 1.0, 0.0).astype(BF16)
    tri_chunk = jnp.where(tri & ((c // ML_CHUNK) == (r // ML_CHUNK)), 1.0, 0.0).astype(BF16)
    hi, mid, lo = _split3(ls)
    parts = [p.astype(BF16) for p in (hi, mid, lo)]
    cum_all = sum(_dot(tri_all, p) for p in parts)
    cum_chunk = sum(_dot(tri_chunk, p) for p in parts)
    carry = carry_ref[...]
    lane = lax.broadcasted_iota(jnp.int32, g.shape, 1)
    o_ref[...] = jnp.where(lane < 4, cum_all + carry, jnp.where(lane < 8, g, cum_chunk))
    carry_ref[...] = carry + cum_all[t - 1:t, :]


def _gate_prep(z, bias_row, tm=512):
    s = z.shape[0]
    return pl.pallas_call(
        _gate_kernel,
        grid=(s // tm,),
        in_specs=[pl.BlockSpec((tm, LANES), lambda i: (i, COL_GATE // LANES)),
                  pl.BlockSpec((1, LANES), lambda i: (0, 0))],
        out_specs=pl.BlockSpec((tm, LANES), lambda i: (i, 0)),
        out_shape=jax.ShapeDtypeStruct((s, LANES), F32),
        scratch_shapes=[pltpu.VMEM((1, LANES), F32)],
        compiler_params=_cparams(("arbitrary",)),
        name="gate_prep",
    )(z, bias_row)


def _fox_prep_kernel(q_ref, k_ref, v_ref, g_ref, qo_ref, ko_ref, vo_ref):
    g = g_ref[...]
    t = g.shape[0]
    lane = lax.broadcasted_iota(jnp.int32, (t, HEAD_DIM), 1)
    scale = HEAD_DIM ** -0.5 * LOG2E
    for h in range(N_ATT_HEADS):
        hi, mid, lo = _split3(g[:, h:h + 1] * LOG2E)
        aug_q = jnp.where(lane == 0, hi, jnp.where(lane == 1, mid, jnp.where(lane == 2, lo,
                          jnp.where(lane < 6, 1.0, 0.0))))
        aug_k = jnp.where(lane < 3, 1.0, jnp.where(lane == 3, -hi, jnp.where(lane == 4, -mid,
                          jnp.where(lane == 5, -lo, 0.0))))
        sl = slice(h * HEAD_DIM, (h + 1) * HEAD_DIM)
        qo_ref[h] = jnp.concatenate([(q_ref[:, sl] * scale).astype(BF16), aug_q.astype(BF16)], axis=1)
        ko_ref[h] = jnp.concatenate([k_ref[:, sl].astype(BF16), aug_k.astype(BF16)], axis=1)
    _store_vt(v_ref, vo_ref)


def _fox_prep(z, gates, tm=512):
    s = z.shape[0]
    cb = COL_FQ // ATT_W
    head_spec = pl.BlockSpec((N_ATT_HEADS, tm, LANES), lambda i: (0, i, 0))
    return pl.pallas_call(
        _fox_prep_kernel,
        grid=(s // tm,),
        in_specs=[pl.BlockSpec((tm, ATT_W), lambda i: (i, cb)),
                  pl.BlockSpec((tm, ATT_W), lambda i: (i, cb + 1)),
                  pl.BlockSpec((tm, ATT_W), lambda i: (i, cb + 2)),
                  pl.BlockSpec((tm, LANES), lambda i: (i, 0))],
        out_specs=[head_spec, head_spec, pl.BlockSpec((N_ATT_HEADS, VT_ROWS, tm), lambda i: (0, 0, i))],
        out_shape=[jax.ShapeDtypeStruct((N_ATT_HEADS, s, LANES), BF16),
                   jax.ShapeDtypeStruct((N_ATT_HEADS, s, LANES), BF16),
                   jax.ShapeDtypeStruct((N_ATT_HEADS, VT_ROWS, s), BF16)],
        compiler_params=_cparams(("parallel",)),
        name="fox_prep",
    )(z, z, z, gates)


def _rope_table_kernel(pos_ref, inv_ref, sign_ref, cos_ref, sin_ref):
    ang = pos_ref[...].astype(F32) * inv_ref[...]
    cos_ref[...] = jnp.cos(ang)
    sin_ref[...] = jnp.sin(ang) * sign_ref[...]


def _rope_tables(pos_col, inv_row, sign_row, tm=512):
    s = pos_col.shape[0]
    row = pl.BlockSpec((1, ATT_W), lambda i: (0, 0))
    out = pl.BlockSpec((tm, ATT_W), lambda i: (i, 0))
    return pl.pallas_call(
        _rope_table_kernel,
        grid=(s // tm,),
        in_specs=[pl.BlockSpec((tm, 1), lambda i: (i, 0)), row, row],
        out_specs=[out, out],
        out_shape=[jax.ShapeDtypeStruct((s, ATT_W), F32)] * 2,
        compiler_params=_cparams(("parallel",)),
        name="rope_tables",
    )(pos_col, inv_row, sign_row)


def _rope(u, cos, sin_signed):
    half = ROPE_DIM // 2
    lane = lax.broadcasted_iota(jnp.int32, u.shape, 1) % HEAD_DIM
    up = pltpu.roll(u, ATT_W - half, axis=1)
    down = pltpu.roll(u, half, axis=1)
    partner = jnp.where(lane < half, up, down)
    return u * cos + partner * sin_signed


def _moba_rope_kernel(q_ref, k_ref, cos_ref, sin_ref, qo_ref, ko_ref, km_ref):
    cos = cos_ref[...]
    sin = sin_ref[...]
    qo_ref[...] = _rope(q_ref[...], cos, sin)
    kr = _rope(k_ref[...], cos, sin)
    ko_ref[...] = kr
    km_ref[0] = jnp.mean(kr, axis=0, keepdims=True)


def _moba_rope(z, cos, sin):
    s = z.shape[0]
    tm = MOBA_BLOCK
    cb = COL_BQ // ATT_W
    blk = pl.BlockSpec((tm, ATT_W), lambda i: (i, 0))
    return pl.pallas_call(
        _moba_rope_kernel,
        grid=(s // tm,),
        in_specs=[pl.BlockSpec((tm, ATT_W), lambda i: (i, cb)),
                  pl.BlockSpec((tm, ATT_W), lambda i: (i, cb + 1)), blk, blk],
        out_specs=[blk, blk, pl.BlockSpec((1, 1, ATT_W), lambda i: (i, 0, 0))],
        out_shape=[jax.ShapeDtypeStruct((s, ATT_W), F32), jax.ShapeDtypeStruct((s, ATT_W), F32),
                   jax.ShapeDtypeStruct((s // tm, 1, ATT_W), F32)],
        compiler_params=_cparams(("parallel",)),
        name="moba_rope",
    )(z, z, cos, sin)


def _moba_select_kernel(q_ref, k_ref, v_ref, km_ref, qo_ref, ko_ref, vo_ref):
    own = pl.program_id(0)
    q = q_ref[...]
    gate_t = lax.dot_general(km_ref[...], q, (((1,), (1,)), ((), ())), preferred_element_type=F32,
                             precision=lax.Precision.HIGHEST)
    t = q.shape[0]
    blk = lax.broadcasted_iota(jnp.int32, (HEAD_DIM, t), 0)
    biases = []
    for h in range(N_ATT_HEADS):
        g = jnp.where(blk < own, gate_t[h * HEAD_DIM:(h + 1) * HEAD_DIM, :], -jnp.inf)
        bias = jnp.where(blk == own, 0.0, NEG_BIG)
        for r in range(MOBA_TOPK):
            mx = jnp.max(g, axis=0, keepdims=True)
            idx = jnp.min(jnp.where(g == mx, blk, HEAD_DIM), axis=0, keepdims=True)
            hit = blk == idx
            bias = jnp.where(hit, jnp.where(r < own, 0.0, bias), bias)
            g = jnp.where(hit, -jnp.inf, g)
        biases.append(bias)
    bias_all = jnp.concatenate(biases, axis=0).T
    lane = lax.broadcasted_iota(jnp.int32, (t, HEAD_DIM), 1)
    scale = HEAD_DIM ** -0.5 * LOG2E
    onehot_own = jnp.where(lane == own, 1.0, 0.0).astype(BF16)
    for h in range(N_ATT_HEADS):
        sl = slice(h * HEAD_DIM, (h + 1) * HEAD_DIM)
        qo_ref[h] = jnp.concatenate([(q[:, sl] * scale).astype(BF16), bias_all[:, sl].astype(BF16)], axis=1)
        ko_ref[h] = jnp.concatenate([k_ref[:, sl].astype(BF16), onehot_own], axis=1)
    _store_vt(v_ref, vo_ref)


def _moba_select(q_rope, k_rope, z, km_mat):
    s = z.shape[0]
    tm = MOBA_BLOCK
    cb = COL_BQ // ATT_W
    blk = pl.BlockSpec((tm, ATT_W), lambda i: (i, 0))
    head_spec = pl.BlockSpec((N_ATT_HEADS, tm, LANES), lambda i: (0, i, 0))
    return pl.pallas_call(
        _moba_select_kernel,
        grid=(s // tm,),
        in_specs=[blk, blk, pl.BlockSpec((tm, ATT_W), lambda i: (i, cb + 2)),
                  pl.BlockSpec((ATT_W, ATT_W), lambda i: (0, 0))],
        out_specs=[head_spec, head_spec, pl.BlockSpec((N_ATT_HEADS, VT_ROWS, tm), lambda i: (0, 0, i))],
        out_shape=[jax.ShapeDtypeStruct((N_ATT_HEADS, s, LANES), BF16),
                   jax.ShapeDtypeStruct((N_ATT_HEADS, s, LANES), BF16),
                   jax.ShapeDtypeStruct((N_ATT_HEADS, VT_ROWS, s), BF16)],
        compiler_params=_cparams(("parallel",)),
        name="moba_select",
    )(q_rope, k_rope, z, km_mat)


FLASH_Q_SPLIT = 1


def _flash_kernel(qi_tab, ki_tab, q_ref, k_ref, vt_ref, g_ref, o_ref, m_ref, acc_ref, *, tile):
    step = pl.program_id(0)
    qi = qi_tab[step]
    ki = ki_tab[step]

    @pl.when(ki == 0)
    def _():
        m_ref[...] = jnp.full_like(m_ref, -jnp.inf)
        acc_ref[...] = jnp.zeros_like(acc_ref)

    def update(masked):
        qw = tile // FLASH_Q_SPLIT
        if masked:
            key = lax.broadcasted_iota(jnp.int32, (tile, qw), 0)
            qry = lax.broadcasted_iota(jnp.int32, (tile, qw), 1)
        units = [(h, j) for h in range(N_ATT_HEADS) for j in range(FLASH_Q_SPLIT)]
        scores = lambda h, j: _dot_nt(k_ref[h], q_ref[h, j * qw:(j + 1) * qw, :])
        st_next = scores(*units[0])
        for u, (h, j) in enumerate(units):
            st = st_next
            if u + 1 < len(units):
                st_next = scores(*units[u + 1])
            if masked:
                st = jnp.where(key <= qry + j * qw, st, NEG_BIG)
            cols = slice(j * qw, (j + 1) * qw)
            m_prev = m_ref[h, :, cols]
            m_new = jnp.maximum(m_prev, jnp.max(st, axis=0, keepdims=True))
            alpha = jnp.exp2(m_prev - m_new)
            p = jnp.exp2(st - m_new).astype(BF16)
            m_ref[h, :, cols] = m_new
            acc_ref[h, :, cols] = acc_ref[h, :, cols] * alpha + _dot(vt_ref[h], p)

    @pl.when(ki < qi)
    def _():
        update(False)

    @pl.when(ki == qi)
    def _():
        update(True)
        outs = []
        for h in range(N_ATT_HEADS):
            acc = acc_ref[h]
            o = acc[0:HEAD_DIM, :] / acc[HEAD_DIM:HEAD_DIM + 1, :]
            outs.append(o * lax.rsqrt(jnp.mean(o * o, axis=0, keepdims=True) + EPS))
        o_ref[...] = (jnp.concatenate(outs, axis=0).T * g_ref[...]).astype(o_ref.dtype)


def _flash_attention(q_aug, k_aug, v_t, gain_row, tile=1024):
    v = v_t
    s = v.shape[2]
    n = s // tile
    pairs = [(qi, ki) for qi in range(n) for ki in range(qi + 1)]
    qi_tab = jnp.asarray([p[0] for p in pairs], jnp.int32)
    ki_tab = jnp.asarray([p[1] for p in pairs], jnp.int32)
    grid_spec = pltpu.PrefetchScalarGridSpec(
        num_scalar_prefetch=2,
        grid=(len(pairs),),
        in_specs=[pl.BlockSpec((N_ATT_HEADS, tile, LANES), lambda i, qt, kt: (0, qt[i], 0)),
                  pl.BlockSpec((N_ATT_HEADS, tile, LANES), lambda i, qt, kt: (0, kt[i], 0)),
                  pl.BlockSpec((N_ATT_HEADS, VT_ROWS, tile), lambda i, qt, kt: (0, 0, kt[i])),
                  pl.BlockSpec((1, ATT_W), lambda i, qt, kt: (0, 0))],
        out_specs=pl.BlockSpec((tile, ATT_W), lambda i, qt, kt: (qt[i], 0)),
        scratch_shapes=[pltpu.VMEM((N_ATT_HEADS, 1, tile), F32),
                        pltpu.VMEM((N_ATT_HEADS, VT_ROWS, tile), F32)],
    )
    return pl.pallas_call(
        functools.partial(_flash_kernel, tile=tile),
        grid_spec=grid_spec,
        out_shape=jax.ShapeDtypeStruct((s, ATT_W), BF16),
        compiler_params=_cparams(("arbitrary",)),
        name="flash_attention",
    )(qi_tab, ki_tab, q_aug, k_aug, v, gain_row)


def _shift_rows(u, tail, s):
    rolled = pltpu.roll(u, s, axis=0)
    rolled_tail = pltpu.roll(tail, s, axis=0)
    row8 = lax.broadcasted_iota(jnp.int32, tail.shape, 0)
    top = jnp.where(row8 < s, rolled_tail, rolled[0:8])
    return jnp.concatenate([top, rolled[8:]], axis=0)


def _mlstm_kernel(qk_ref, v_ref, o_ref, g_ref, cw_ref, gain_ref, y_ref, tail_ref, c_ref, n_ref, m_ref):
    @pl.when(pl.program_id(0) == 0)
    def _():
        tail_ref[...] = jnp.zeros_like(tail_ref)
        c_ref[...] = jnp.zeros_like(c_ref)
        n_ref[...] = jnp.zeros_like(n_ref)
        m_ref[...] = jnp.zeros_like(m_ref)

    L = ML_CHUNK
    u = qk_ref[...]
    tail = tail_ref[...]
    cw = cw_ref[...]
    conv = u * cw[CONV_WIDTH - 1:CONV_WIDTH]
    for s in range(1, CONV_WIDTH):
        conv = conv + _shift_rows(u, tail, s) * cw[CONV_WIDTH - 1 - s:CONV_WIDTH - s]
    tail_ref[...] = u[L - 8:L]
    qk = conv * jax.nn.sigmoid(conv)

    g = g_ref[...]
    gt = g.T
    r = lax.broadcasted_iota(jnp.int32, (L, L), 0)
    c = lax.broadcasted_iota(jnp.int32, (L, L), 1)
    tril = c <= r
    kscale = ML_DIM ** -0.5
    heads = range(ML_HEADS)
    st = []
    for h in heads:
        sl = slice(h * ML_DIM, (h + 1) * ML_DIM)
        qh = qk[:, sl]
        kh = qk[:, ML_W + h * ML_DIM:ML_W + (h + 1) * ML_DIM] * kscale
        qb, kb, vb = qh.astype(BF16), kh.astype(BF16), v_ref[:, sl].astype(BF16)
        b_col = g[:, 8 + h:9 + h]
        i_col = g[:, 4 + h:5 + h]
        b_row = gt[8 + h:9 + h, :]
        i_row = gt[4 + h:5 + h, :]
        dmat = jnp.where(tril, b_col - b_row + i_row, -jnp.inf)
        b_last = b_row[:, L - 1:L]
        g_col = b_last - b_col + i_col
        st.append(dict(qh=qh, kh=kh, qb=qb, kb=kb, vb=vb, b_col=b_col, dmat=dmat, b_last=b_last, g_col=g_col,
                       qkt=_dot_nt(qb, kb), dmax=jnp.max(dmat, axis=1, keepdims=True),
                       gmax=jnp.max(g_col, axis=0, keepdims=True)))
    outs = []
    for h in heads:
        s_ = st[h]
        sl = slice(h * ML_DIM, (h + 1) * ML_DIM)
        m_prev = m_ref[h][:, 0:1]
        cmat = c_ref[h]
        nrow = n_ref[h]
        inter = s_["b_col"] + m_prev
        m_t = jnp.maximum(inter, s_["dmax"])
        w_intra = jnp.exp(s_["dmat"] - m_t)
        w_inter = jnp.exp(inter - m_t)
        a = s_["qkt"] * w_intra
        num = _dot(a.astype(BF16), s_["vb"]) + w_inter * _dot(s_["qb"], cmat.astype(BF16))
        den = jnp.sum(a, axis=1, keepdims=True) + w_inter * jnp.sum(s_["qh"] * nrow, axis=1, keepdims=True)
        hh = num / jnp.maximum(jnp.abs(den), jnp.exp(-m_t))
        y = jax.nn.sigmoid(o_ref[:, sl]) * hh
        outs.append(y * lax.rsqrt(jnp.mean(y * y, axis=1, keepdims=True) + EPS))
        s_.update(m_prev=m_prev, cmat=cmat, nrow=nrow)
    for h in heads:
        s_ = st[h]
        m_new = jnp.maximum(s_["b_last"] + s_["m_prev"], s_["gmax"])
        decay = jnp.exp(s_["b_last"] + s_["m_prev"] - m_new)
        kw = s_["kh"] * jnp.exp(s_["g_col"] - m_new)
        c_ref[h] = decay * s_["cmat"] + _dot(kw.T.astype(BF16), s_["vb"])
        n_ref[h] = decay * s_["nrow"] + jnp.sum(kw, axis=0, keepdims=True)
        m_ref[h] = jnp.broadcast_to(m_new, (1, LANES))
    y_ref[...] = (jnp.concatenate(outs, axis=1) * gain_ref[...]).astype(y_ref.dtype)


def _mlstm(z, gates, conv_w, gain_row):
    s = z.shape[0]
    L = ML_CHUNK
    return pl.pallas_call(
        _mlstm_kernel,
        grid=(s // L,),
        in_specs=[pl.BlockSpec((L, 2 * ML_W), lambda i: (i, COL_MQK // (2 * ML_W))),
                  pl.BlockSpec((L, ML_W), lambda i: (i, COL_MV // ML_W)),
                  pl.BlockSpec((L, ML_W), lambda i: (i, COL_MO // ML_W)),
                  pl.BlockSpec((L, LANES), lambda i: (i, 0)),
                  pl.BlockSpec((CONV_WIDTH, 2 * ML_W), lambda i: (0, 0)),
                  pl.BlockSpec((1, ML_W), lambda i: (0, 0))],
        out_specs=pl.BlockSpec((L, ML_W), lambda i: (i, 0)),
        out_shape=jax.ShapeDtypeStruct((s, ML_W), BF16),
        scratch_shapes=[pltpu.VMEM((8, 2 * ML_W), F32),
                        pltpu.VMEM((ML_HEADS, ML_DIM, ML_DIM), F32),
                        pltpu.VMEM((ML_HEADS, 1, ML_DIM), F32),
                        pltpu.VMEM((ML_HEADS, 1, LANES), F32)],
        compiler_params=_cparams(("arbitrary",)),
        name="mlstm",
    )(z, z, z, gates, conv_w, gain_row)


def _layer_norm(h, g, b):
    mu = jnp.mean(h, axis=1, keepdims=True)
    d = h - mu
    var = jnp.mean(d * d, axis=1, keepdims=True)
    return d * lax.rsqrt(var + EPS) * g + b


def _row_seg(j, rows):
    return pl.ds(j, rows, stride=SUBLANES)


def _outproj_kernel(yf_ref, ym_ref, yb_ref, w_ref, x_ref, g_ref, b_ref, wr_ref, br_ref, x1_ref, x1t_ref, lg_ref):
    mix = (_dot(yf_ref[...], w_ref[0:ATT_W, :]) + _dot(ym_ref[...], w_ref[ATT_W:ATT_W + ML_W, :])
           + _dot(yb_ref[...], w_ref[ATT_W + ML_W:, :]))
    x1 = _layer_norm(ALPHA * x_ref[...] + mix, g_ref[...], b_ref[...])
    x1_ref[...] = x1
    tm = x1.shape[0]
    for j in range(D_MODEL // LANES):
        x1t_ref[_row_seg(j, tm), :] = x1[:, j * LANES:(j + 1) * LANES]
    lg_ref[...] = jnp.dot(x1, wr_ref[...], preferred_element_type=F32,
                          precision=lax.Precision.HIGHEST) + br_ref[...]


def _outproj_ln_router(yf, ym, yb, w_out, x, ln_g, ln_b, w_router, b_router, tm=512):
    s = x.shape[0]
    const = lambda shape: pl.BlockSpec(shape, lambda i: (0, 0))
    rows = lambda w: pl.BlockSpec((tm, w), lambda i: (i, 0))
    return pl.pallas_call(
        _outproj_kernel,
        grid=(s // tm,),
        in_specs=[rows(ATT_W), rows(ML_W), rows(ATT_W), const((D_MODEL, D_MODEL)), rows(D_MODEL),
                  const((1, D_MODEL)), const((1, D_MODEL)), const((D_MODEL, LANES)), const((1, LANES))],
        out_specs=[rows(D_MODEL), pl.BlockSpec((SUBLANES * tm, LANES), lambda i: (i, 0)), rows(LANES)],
        out_shape=[jax.ShapeDtypeStruct((s, D_MODEL), F32), jax.ShapeDtypeStruct((SUBLANES * s, LANES), F32),
                   jax.ShapeDtypeStruct((s, LANES), F32)],
        compiler_params=_cparams(("parallel",)),
        name="outproj_ln_router",
    )(yf, ym, yb, w_out, x, ln_g, ln_b, w_router, b_router)


def _first_argmax(v, lane):
    mx = jnp.max(v, axis=1, keepdims=True)
    idx = jnp.min(jnp.where(v == mx, lane, LANES), axis=1, keepdims=True)
    return mx, idx


def _route_kernel(lg_ref, info_ref, cnt_ref, carry_ref):
    @pl.when(pl.program_id(0) == 0)
    def _():
        carry_ref[...] = jnp.zeros_like(carry_ref)

    lg = lg_ref[...]
    t = lg.shape[0]
    lane = lax.broadcasted_iota(jnp.int32, lg.shape, 1)
    is_grp = lane < N_GROUPS
    gmax, gsel = _first_argmax(jnp.where(is_grp, lg, -jnp.inf), lane)
    p_grp = 1.0 / jnp.sum(jnp.where(is_grp, jnp.exp(lg - gmax), 0.0), axis=1, keepdims=True)
    lo = N_GROUPS + EXPERTS_PER_GROUP * gsel
    el = jnp.where((lane >= lo) & (lane < lo + EXPERTS_PER_GROUP), lg, -jnp.inf)
    v0, i0 = _first_argmax(el, lane)
    v1, i1 = _first_argmax(jnp.where(lane == i0, -jnp.inf, el), lane)
    ex = jnp.exp(v1 - v0)
    w0 = p_grp / (1.0 + ex)
    w1 = p_grp * ex / (1.0 + ex)
    e0 = i0 - N_GROUPS
    e1 = i1 - N_GROUPS

    cnt = jnp.where((lane == e0) | (lane == e1), 1.0, 0.0)
    r = lax.broadcasted_iota(jnp.int32, (t, t), 0)
    c = lax.broadcasted_iota(jnp.int32, (t, t), 1)
    strict = jnp.where(c < r, 1.0, 0.0).astype(BF16)
    carry = carry_ref[...]
    before = _dot(strict, cnt.astype(BF16)) + carry
    rank0 = jnp.sum(jnp.where(lane == e0, before, 0.0), axis=1, keepdims=True)
    rank1 = jnp.sum(jnp.where(lane == e1, before, 0.0), axis=1, keepdims=True)
    carry = carry + jnp.sum(cnt, axis=0, keepdims=True)
    carry_ref[...] = carry
    cnt_ref[...] = carry
    vals = [e0.astype(F32), e1.astype(F32), w0, w1, rank0, rank1]
    info = jnp.zeros(lg.shape, F32)
    for j, val in enumerate(vals):
        info = jnp.where(lane == j, val, info)
    info_ref[...] = info


def _route(logits, tm=512):
    s = logits.shape[0]
    return pl.pallas_call(
        _route_kernel,
        grid=(s // tm,),
        in_specs=[pl.BlockSpec((tm, LANES), lambda i: (i, 0))],
        out_specs=[pl.BlockSpec((tm, LANES), lambda i: (i, 0)), pl.BlockSpec((1, LANES), lambda i: (0, 0))],
        out_shape=[jax.ShapeDtypeStruct((s, LANES), F32), jax.ShapeDtypeStruct((1, LANES), F32)],
        scratch_shapes=[pltpu.VMEM((1, LANES), F32)],
        compiler_params=_cparams(("arbitrary",)),
        name="route",
    )(logits)


MOE_ROWS = 256


def _moe_kernel(blk_e, nv, src0_ref, srcn_ref, slotp_ref, slotc_ref, x_hbm, wg_ref, wu_ref, wd_ref, out_hbm,
                xbuf0, xbuf1, ybuf0, ybuf1, wgb, wub, wdb, gsem, ssem, *, nblk):
    b = pl.program_id(0)
    xbufs = (xbuf0, xbuf1)
    ybufs = (ybuf0, ybuf1)
    nv_prev2, nv_prev, nv_cur, nv_next = nv[b], nv[b + 1], nv[b + 2], nv[b + 3]

    def row_in(tok8, slot, r):
        return pltpu.make_async_copy(x_hbm.at[pl.ds(pl.multiple_of(tok8, SUBLANES), SUBLANES), :],
                                     xbufs[slot].at[pl.ds(SUBLANES * r, SUBLANES), :], gsem.at[slot])

    def row_out(dst8, slot, r):
        return pltpu.make_async_copy(ybufs[slot].at[pl.ds(SUBLANES * r, SUBLANES), :],
                                     out_hbm.at[pl.ds(pl.multiple_of(dst8, SUBLANES), SUBLANES), :], ssem.at[slot])

    def start_rows(make, idx_ref, slot, n):
        for r in range(MOE_ROWS):
            @pl.when(r < n)
            def _():
                make(idx_ref[0, 0, r], slot, r).start(priority=r % 2)

    def wait_rows(make, slot, n):
        @pl.when(n > 0)
        def _():
            rows = pl.multiple_of(n * SUBLANES, SUBLANES)
            if make is row_in:
                pltpu.make_async_copy(x_hbm.at[pl.ds(0, rows), :], xbufs[slot].at[pl.ds(0, rows), :],
                                      gsem.at[slot]).wait()
            else:
                pltpu.make_async_copy(ybufs[slot].at[pl.ds(0, rows), :], out_hbm.at[pl.ds(0, rows), :],
                                      ssem.at[slot]).wait()

    @pl.when(b == 0)
    def _():
        for half in range(2):
            xbufs[half][...] = jnp.zeros_like(xbufs[half])
        start_rows(row_in, src0_ref, 0, nv_cur)

    @pl.when((b == 0) | (blk_e[b] != blk_e[jnp.maximum(b - 1, 0)]))
    def _():
        wgb[...] = wg_ref[0, 0].astype(BF16)
        wub[...] = wu_ref[0, 0].astype(BF16)
        wdb[...] = wd_ref[0, 0].astype(BF16)

    def step(cur):
        nxt = 1 - cur
        wait_rows(row_in, cur, nv_cur)
        start_rows(row_out, slotp_ref, nxt, nv_prev)
        start_rows(row_in, srcn_ref, nxt, nv_next)

        nseg = D_MODEL // LANES
        seg = lambda j: _row_seg(j, MOE_ROWS)
        xb = jnp.concatenate([xbufs[cur][seg(j), :] for j in range(nseg)], axis=1).astype(BF16)
        gate = _dot(xb, wgb[...])
        up = _dot(xb, wub[...])
        hid = (gate * jax.nn.sigmoid(gate) * up).astype(BF16)
        y = _dot(hid, wdb[...])

        wait_rows(row_out, cur, nv_prev2)

        for j in range(nseg):
            ybufs[cur][seg(j), :] = y[:, j * LANES:(j + 1) * LANES]

        @pl.when(b == nblk - 1)
        def _():
            start_rows(row_out, slotc_ref, cur, nv_cur)
            wait_rows(row_out, nxt, nv_prev)
            wait_rows(row_out, cur, nv_cur)

    for parity in range(2):
        pl.when(b % 2 == parity)(functools.partial(step, parity))


def _moe_ffn(x1, src_ext, slot_ext, blk_e, nv_ext, w_gate, w_up, w_down, layer):
    s = x1.shape[0] // SUBLANES
    nblk = src_ext.shape[0] - 1
    stage = (SUBLANES * MOE_ROWS, LANES)
    idx_blk = (1, 1, MOE_ROWS)
    smem = pltpu.SMEM
    grid_spec = pltpu.PrefetchScalarGridSpec(
        num_scalar_prefetch=2,
        grid=(nblk,),
        in_specs=[pl.BlockSpec(idx_blk, lambda b, be, nv: (0, 0, 0), memory_space=smem),
                  pl.BlockSpec(idx_blk, lambda b, be, nv: (b + 1, 0, 0), memory_space=smem),
                  pl.BlockSpec(idx_blk, lambda b, be, nv: (b, 0, 0), memory_space=smem),
                  pl.BlockSpec(idx_blk, lambda b, be, nv: (b + 1, 0, 0), memory_space=smem),
                  pl.BlockSpec(memory_space=pl.ANY),
                  pl.BlockSpec((1, 1, D_MODEL, D_EXPERT), lambda b, be, nv: (layer, be[b], 0, 0)),
                  pl.BlockSpec((1, 1, D_MODEL, D_EXPERT), lambda b, be, nv: (layer, be[b], 0, 0)),
                  pl.BlockSpec((1, 1, D_EXPERT, D_MODEL), lambda b, be, nv: (layer, be[b], 0, 0))],
        out_specs=pl.BlockSpec(memory_space=pl.ANY),
        scratch_shapes=[pltpu.VMEM(stage, F32),
                        pltpu.VMEM(stage, F32),
                        pltpu.VMEM(stage, F32),
                        pltpu.VMEM(stage, F32),
                        pltpu.VMEM((D_MODEL, D_EXPERT), BF16),
                        pltpu.VMEM((D_MODEL, D_EXPERT), BF16),
                        pltpu.VMEM((D_EXPERT, D_MODEL), BF16),
                        pltpu.SemaphoreType.DMA((2,)),
                        pltpu.SemaphoreType.DMA((2,))],
    )
    return pl.pallas_call(
        functools.partial(_moe_kernel, nblk=nblk),
        grid_spec=grid_spec,
        out_shape=jax.ShapeDtypeStruct((SUBLANES * 2 * s, LANES), F32),
        compiler_params=_cparams(("arbitrary",)),
        name="moe_ffn",
    )(blk_e, nv_ext, src_ext, src_ext, slot_ext, slot_ext, x1, w_gate, w_up, w_down)


def _dest_kernel(info_ref, pstart_ref, d_ref):
    info = info_ref[...]
    tm = info.shape[0]
    lane = lax.broadcasted_iota(jnp.int32, info.shape, 1)
    ps = pstart_ref[...]
    dests = []
    for k in range(2):
        e = info[:, k:k + 1].astype(jnp.int32)
        dests.append(jnp.sum(jnp.where(lane == e, ps, 0.0), axis=1, keepdims=True) + info[:, 4 + k:5 + k])
    packed = jnp.where(lane == 0, dests[0], jnp.where(lane == 1, dests[1], 0.0))
    for j in range(tm // LANES):
        rows = packed[j * LANES:(j + 1) * LANES, :].T
        for k in range(2):
            d_ref[k, j:j + 1, :] = rows[k:k + 1, :].astype(jnp.int32)


def _dest(info, pstart_row, tm=1024):
    s = info.shape[0]
    return pl.pallas_call(
        _dest_kernel,
        grid=(s // tm,),
        in_specs=[pl.BlockSpec((tm, LANES), lambda i: (i, 0)), pl.BlockSpec((1, LANES), lambda i: (0, 0))],
        out_specs=pl.BlockSpec((2, tm // LANES, LANES), lambda i: (0, i, 0)),
        out_shape=jax.ShapeDtypeStruct((2, s // LANES, LANES), jnp.int32),
        compiler_params=_cparams(("parallel",)),
        name="moe_dest",
    )(info, pstart_row)


def _moe_plan(info, counts_row, s):
    counts = counts_row[0, :N_EXPERTS].astype(jnp.int32)
    padded = ((counts + MOE_ROWS - 1) // MOE_ROWS) * MOE_ROWS
    pends = jnp.cumsum(padded)
    pstarts = pends - padded
    nblk = (2 * s) // MOE_ROWS + N_EXPERTS
    p = nblk * MOE_ROWS
    dest = _dest(info, _pad_lanes(pstarts.astype(F32))).reshape(2 * s)
    out_slot = jnp.zeros((p,), jnp.int32).at[dest].set(jnp.arange(2 * s, dtype=jnp.int32))
    src_tok = jnp.where(out_slot >= s, out_slot - s, out_slot)
    nused = (pends[-1] // MOE_ROWS).astype(jnp.int32)
    blk = jnp.arange(nblk, dtype=jnp.int32)
    blk_start = jnp.minimum(blk, nused - 1) * MOE_ROWS
    blk_e = jnp.minimum(jnp.sum((pends[None, :] <= blk_start[:, None]).astype(jnp.int32), axis=1), N_EXPERTS - 1)
    is_e = jnp.arange(N_EXPERTS)[None, :] == blk_e[:, None]
    seg_start = jnp.sum(jnp.where(is_e, pstarts[None, :], 0), axis=1)
    seg_count = jnp.sum(jnp.where(is_e, counts[None, :], 0), axis=1)
    nvalid = jnp.where(blk < nused, jnp.clip(seg_count - (blk * MOE_ROWS - seg_start), 0, MOE_ROWS), 0)
    nv_ext = jnp.concatenate([jnp.zeros((2,), jnp.int32), nvalid.astype(jnp.int32), jnp.zeros((2,), jnp.int32)])
    pad_blk = jnp.zeros((MOE_ROWS,), jnp.int32)
    src_ext = jnp.concatenate([src_tok, pad_blk]) * SUBLANES
    slot_ext = jnp.concatenate([pad_blk, out_slot]) * SUBLANES
    return src_ext.reshape(nblk + 1, 1, MOE_ROWS), slot_ext.reshape(nblk + 1, 1, MOE_ROWS), blk_e, nv_ext


def _combine_kernel(x1_ref, ya_ref, yb_ref, info_ref, g_ref, b_ref, o_ref):
    info = info_ref[...]
    tm = info.shape[0]
    wide = lambda ref: jnp.concatenate([ref[_row_seg(j, tm), :] for j in range(D_MODEL // LANES)], axis=1)
    ffn = info[:, 2:3] * wide(ya_ref) + info[:, 3:4] * wide(yb_ref)
    o_ref[...] = _layer_norm(ALPHA * x1_ref[...] + ffn, g_ref[...], b_ref[...])


def _combine_ln(x1, y2, info, ln_g, ln_b, tm=512):
    s = x1.shape[0]
    const = pl.BlockSpec((1, D_MODEL), lambda i: (0, 0))
    return pl.pallas_call(
        _combine_kernel,
        grid=(s // tm,),
        in_specs=[pl.BlockSpec((tm, D_MODEL), lambda i: (i, 0)),
                  pl.BlockSpec((SUBLANES * tm, LANES), lambda i: (i, 0)),
                  pl.BlockSpec((SUBLANES * tm, LANES), lambda i: (i + s // tm, 0)),
                  pl.BlockSpec((tm, LANES), lambda i: (i, 0)), const, const],
        out_specs=pl.BlockSpec((tm, D_MODEL), lambda i: (i, 0)),
        out_shape=jax.ShapeDtypeStruct((s, D_MODEL), F32),
        compiler_params=_cparams(("parallel",)),
        name="combine_ln",
    )(x1, y2, y2, info, ln_g, ln_b)


def _pad_lanes(v, width=LANES):
    return jnp.zeros((1, width), F32).at[0, :v.shape[0]].set(v)


def _layer(layer, x, cos, sin, w_in, b_fox_f, b_mlstm_i, b_mlstm_f, conv_w, g_fox, g_mlstm, g_moba, w_out,
           ln1_g, ln1_b, w_grp, b_grp, w_exp_router, b_exp_router, w_gate, w_up, w_down, ln2_g, ln2_b):
    s = x.shape[0]
    gate_bias = _pad_lanes(jnp.concatenate([b_fox_f, b_mlstm_i, b_mlstm_f]))

    z = _inproj(x, w_in, layer)
    gates = _gate_prep(z, gate_bias)

    fq_aug, fk_aug, fv_b = _fox_prep(z, gates)
    y_fox = _flash_attention(fq_aug, fk_aug, fv_b, g_fox[None, :])

    q_rope, k_rope, kmean = _moba_rope(z, cos, sin)
    km = kmean[:, 0, :].reshape(s // MOBA_BLOCK, N_ATT_HEADS, HEAD_DIM)
    km_mat = jnp.zeros((N_ATT_HEADS, HEAD_DIM, N_ATT_HEADS, HEAD_DIM), F32)
    for h in range(N_ATT_HEADS):
        km_mat = km_mat.at[h, :s // MOBA_BLOCK, h, :].set(km[:, h, :])
    km_mat = km_mat.reshape(ATT_W, ATT_W)
    bq_aug, bk_aug, bv_b = _moba_select(q_rope, k_rope, z, km_mat)
    y_moba = _flash_attention(bq_aug, bk_aug, bv_b, g_moba[None, :])

    y_mlstm = _mlstm(z, gates, conv_w, g_mlstm[None, :])

    w_router = jnp.zeros((D_MODEL, LANES), F32)
    w_router = w_router.at[:, :N_GROUPS].set(w_grp)
    w_router = w_router.at[:, N_GROUPS:N_GROUPS + N_EXPERTS].set(w_exp_router.reshape(D_MODEL, N_EXPERTS))
    b_router = _pad_lanes(jnp.concatenate([b_grp, b_exp_router.reshape(N_EXPERTS)]))
    x1, x1_tiles, logits = _outproj_ln_router(y_fox, y_mlstm, y_moba, w_out.astype(BF16), x, ln1_g[None, :],
                                              ln1_b[None, :], w_router, b_router)

    info, counts = _route(logits)
    src_ext, slot_ext, blk_e, nv_ext = _moe_plan(info, counts, s)
    y2 = _moe_ffn(x1_tiles, src_ext, slot_ext, blk_e, nv_ext, w_gate, w_up, w_down, layer)
    return _combine_ln(x1, y2, info, ln2_g[None, :], ln2_b[None, :])


def kernel(x, positions, w_in, b_fox_f, b_mlstm_i, b_mlstm_f, conv_w, g_fox, g_mlstm, g_moba, w_out, ln1_g, ln1_b, w_grp, b_grp, w_exp_router, b_exp_router, w_gate, w_up, w_down, ln2_g, ln2_b):
    assert x.shape[0] == 1
    xs = x[0]
    d = jnp.arange(ATT_W) % HEAD_DIM
    half = ROPE_DIM // 2
    inv = 1.0 / (ROPE_THETA ** (jnp.arange(0, ROPE_DIM, 2, dtype=F32) / ROPE_DIM))
    inv_row = jnp.where(d < ROPE_DIM, inv[d % half], 0.0)[None, :].astype(F32)
    sign_row = jnp.where(d < half, -1.0, 1.0)[None, :].astype(F32)
    cos, sin = _rope_tables(positions[0][:, None], inv_row, sign_row)
    for l in range(DEPTH):
        xs = _layer(l, xs, cos, sin, w_in, b_fox_f[l], b_mlstm_i[l], b_mlstm_f[l], conv_w[l], g_fox[l],
                    g_mlstm[l], g_moba[l], w_out[l], ln1_g[l], ln1_b[l], w_grp[l], b_grp[l],
                    w_exp_router[l], b_exp_router[l], w_gate, w_up, w_down, ln2_g[l], ln2_b[l])
    return xs[None]
```

```python
import functools

import jax
import jax.numpy as jnp
from jax import lax
from jax.experimental import pallas as pl
from jax.experimental.pallas import tpu as pltpu

D_MODEL = 1024
DEPTH = 2
HEAD_DIM = 64
N_ATT_HEADS = 4
ATT_W = N_ATT_HEADS * HEAD_DIM
ML_HEADS = 4
ML_DIM = 128
ML_W = ML_HEADS * ML_DIM
ML_CHUNK = 128
CONV_WIDTH = 4
ROPE_DIM = 16
ROPE_THETA = 500000.0
MOBA_BLOCK = 256
MOBA_TOPK = 3
N_GROUPS = 4
EXPERTS_PER_GROUP = 8
N_EXPERTS = N_GROUPS * EXPERTS_PER_GROUP
D_EXPERT = 512
ALPHA = (2 * DEPTH) ** 0.25
EPS = 1e-5

LANES = 128
SUBLANES = 8
NEG_BIG = -1e30
VMEM_LIMIT = 56 * 1024 * 1024

COL_MQK = 0
COL_MV = 1024
COL_MO = 1536
COL_FQ = 2048
COL_BQ = 2816
COL_GATE = 3584
Z_W = 3712

F32 = jnp.float32
BF16 = jnp.bfloat16


def _cparams(sem):
    return pltpu.CompilerParams(dimension_semantics=sem, vmem_limit_bytes=VMEM_LIMIT)


def _split3(c):
    hi = c.astype(BF16).astype(F32)
    r1 = c - hi
    mid = r1.astype(BF16).astype(F32)
    lo = (r1 - mid).astype(BF16).astype(F32)
    return hi, mid, lo


def _dot(a, b):
    return jnp.dot(a, b, preferred_element_type=F32)


def _dot_nt(a, b):
    return lax.dot_general(a, b, (((1,), (1,)), ((), ())), preferred_element_type=F32)


LOG2E = 1.4426950408889634
VT_ROWS = 80


def _store_vt(v_ref, vo_ref):
    vt = v_ref[...].T
    t = vt.shape[1]
    row = lax.broadcasted_iota(jnp.int32, (VT_ROWS - HEAD_DIM, t), 0)
    tail = jnp.where(row == 0, 1.0, 0.0)
    for h in range(N_ATT_HEADS):
        vo_ref[h] = jnp.concatenate([vt[h * HEAD_DIM:(h + 1) * HEAD_DIM, :], tail], axis=0).astype(BF16)


IN_W = 3596
_W_RUNS = ((COL_MQK, 772, 2308),
           (COL_MO, 2316, 2828),
           (COL_FQ, 0, 768),
           (COL_BQ, 2828, 3596))
_W_GATE_RUNS = ((768, 772), (2308, 2316))


def _cols(w_ref, r0, r1, a, b):
    a0 = (a // LANES) * LANES
    b0 = min(-(-b // LANES) * LANES, IN_W)
    return w_ref[0, r0:r1, a0:b0][:, a - a0:b - a0]


def _inproj_kernel(x_ref, w_ref, o_ref, wb_ref):
    @pl.when(pl.program_id(0) == 0)
    def _():
        rows = 256
        for r0 in range(0, D_MODEL, rows):
            r1 = r0 + rows
            for dst, a, b in _W_RUNS:
                wb_ref[r0:r1, dst:dst + (b - a)] = _cols(w_ref, r0, r1, a, b).astype(BF16)
            gate = [_cols(w_ref, r0, r1, a, b) for a, b in _W_GATE_RUNS]
            used = sum(b - a for a, b in _W_GATE_RUNS)
            gate.append(jnp.zeros((rows, LANES - used), F32))
            wb_ref[r0:r1, COL_GATE:] = jnp.concatenate(gate, axis=1).astype(BF16)

    xb = x_ref[...].astype(BF16)
    n = o_ref.shape[1]
    step = 512
    for j in range(0, n, step):
        w = min(step, n - j)
        o_ref[:, j:j + w] = _dot(xb, wb_ref[:, j:j + w])


def _inproj(x, w_in, layer, tm=256):
    s = x.shape[0]
    return pl.pallas_call(
        _inproj_kernel,
        grid=(s // tm,),
        in_specs=[pl.BlockSpec((tm, D_MODEL), lambda i: (i, 0)),
                  pl.BlockSpec((1, D_MODEL, IN_W), lambda i: (layer, 0, 0), pipeline_mode=pl.Buffered(1))],
        out_specs=pl.BlockSpec((tm, Z_W), lambda i: (i, 0)),
        out_shape=jax.ShapeDtypeStruct((s, Z_W), F32),
        scratch_shapes=[pltpu.VMEM((D_MODEL, Z_W), BF16)],
        compiler_params=_cparams(("arbitrary",)),
        name="inproj",
    )(x, w_in)


def _log_sigmoid(x):
    return jnp.minimum(x, 0.0) - jnp.log(1.0 + jnp.exp(-jnp.abs(x)))


def _gate_kernel(zg_ref, bias_ref, o_ref, carry_ref):
    @pl.when(pl.program_id(0) == 0)
    def _():
        carry_ref[...] = jnp.zeros_like(carry_ref)

    g = zg_ref[...] + bias_ref[...]
    ls = _log_sigmoid(g)
    t = g.shape[0]
    r = lax.broadcasted_iota(jnp.int32, (t, t), 0)
    c = lax.broadcasted_iota(jnp.int32, (t, t), 1)
    tri = c <= r
    tri_all = jnp.where(tri, 1.0, 0.0).astype(BF16)
    tri_chunk = jnp.where(tri & ((c // ML_CHUNK) == (r // ML_CHUNK)), 1.0, 0.0).astype(BF16)
    hi, mid, lo = _split3(ls)
    parts = [p.astype(BF16) for p in (hi, mid, lo)]
    cum_all = sum(_dot(tri_all, p) for p in parts)
    cum_chunk = sum(_dot(tri_chunk, p) for p in parts)
    carry = carry_ref[...]
    lane = lax.broadcasted_iota(jnp.int32, g.shape, 1)
    o_ref[...] = jnp.where(lane < 4, cum_all + carry, jnp.where(lane < 8, g, cum_chunk))
    carry_ref[...] = carry + cum_all[t - 1:t, :]


def _gate_prep(z, bias_row, tm=512):
    s = z.shape[0]
    return pl.pallas_call(
        _gate_kernel,
        grid=(s // tm,),
        in_specs=[pl.BlockSpec((tm, LANES), lambda i: (i, COL_GATE // LANES)),
                  pl.BlockSpec((1, LANES), lambda i: (0, 0))],
        out_specs=pl.BlockSpec((tm, LANES), lambda i: (i, 0)),
        out_shape=jax.ShapeDtypeStruct((s, LANES), F32),
        scratch_shapes=[pltpu.VMEM((1, LANES), F32)],
        compiler_params=_cparams(("arbitrary",)),
        name="gate_prep",
    )(z, bias_row)


def _fox_prep_kernel(q_ref, k_ref, v_ref, g_ref, qo_ref, ko_ref, vo_ref):
    g = g_ref[...]
    t = g.shape[0]
    lane = lax.broadcasted_iota(jnp.int32, (t, HEAD_DIM), 1)
    scale = HEAD_DIM ** -0.5 * LOG2E
    for h in range(N_ATT_HEADS):
        hi, mid, lo = _split3(g[:, h:h + 1] * LOG2E)
        aug_q = jnp.where(lane == 0, hi, jnp.where(lane == 1, mid, jnp.where(lane == 2, lo,
                          jnp.where(lane < 6, 1.0, 0.0))))
        aug_k = jnp.where(lane < 3, 1.0, jnp.where(lane == 3, -hi, jnp.where(lane == 4, -mid,
                          jnp.where(lane == 5, -lo, 0.0))))
        sl = slice(h * HEAD_DIM, (h + 1) * HEAD_DIM)
        qo_ref[h] = jnp.concatenate([(q_ref[:, sl] * scale).astype(BF16), aug_q.astype(BF16)], axis=1)
        ko_ref[h] = jnp.concatenate([k_ref[:, sl].astype(BF16), aug_k.astype(BF16)], axis=1)
    _store_vt(v_ref, vo_ref)


def _fox_prep(z, gates, tm=512):
    s = z.shape[0]
    cb = COL_FQ // ATT_W
    head_spec = pl.BlockSpec((N_ATT_HEADS, tm, LANES), lambda i: (0, i, 0))
    return pl.pallas_call(
        _fox_prep_kernel,
        grid=(s // tm,),
        in_specs=[pl.BlockSpec((tm, ATT_W), lambda i: (i, cb)),
                  pl.BlockSpec((tm, ATT_W), lambda i: (i, cb + 1)),
                  pl.BlockSpec((tm, ATT_W), lambda i: (i, cb + 2)),
                  pl.BlockSpec((tm, LANES), lambda i: (i, 0))],
        out_specs=[head_spec, head_spec, pl.BlockSpec((N_ATT_HEADS, VT_ROWS, tm), lambda i: (0, 0, i))],
        out_shape=[jax.ShapeDtypeStruct((N_ATT_HEADS, s, LANES), BF16),
                   jax.ShapeDtypeStruct((N_ATT_HEADS, s, LANES), BF16),
                   jax.ShapeDtypeStruct((N_ATT_HEADS, VT_ROWS, s), BF16)],
        compiler_params=_cparams(("parallel",)),
        name="fox_prep",
    )(z, z, z, gates)


def _rope_table_kernel(pos_ref, inv_ref, sign_ref, cos_ref, sin_ref):
    ang = pos_ref[...].astype(F32) * inv_ref[...]
    cos_ref[...] = jnp.cos(ang)
    sin_ref[...] = jnp.sin(ang) * sign_ref[...]


def _rope_tables(pos_col, inv_row, sign_row, tm=512):
    s = pos_col.shape[0]
    row = pl.BlockSpec((1, ATT_W), lambda i: (0, 0))
    out = pl.BlockSpec((tm, ATT_W), lambda i: (i, 0))
    return pl.pallas_call(
        _rope_table_kernel,
        grid=(s // tm,),
        in_specs=[pl.BlockSpec((tm, 1), lambda i: (i, 0)), row, row],
        out_specs=[out, out],
        out_shape=[jax.ShapeDtypeStruct((s, ATT_W), F32)] * 2,
        compiler_params=_cparams(("parallel",)),
        name="rope_tables",
    )(pos_col, inv_row, sign_row)


def _rope(u, cos, sin_signed):
    half = ROPE_DIM // 2
    lane = lax.broadcasted_iota(jnp.int32, u.shape, 1) % HEAD_DIM
    up = pltpu.roll(u, ATT_W - half, axis=1)
    down = pltpu.roll(u, half, axis=1)
    partner = jnp.where(lane < half, up, down)
    return u * cos + partner * sin_signed


def _moba_rope_kernel(q_ref, k_ref, cos_ref, sin_ref, qo_ref, ko_ref, km_ref):
    cos = cos_ref[...]
    sin = sin_ref[...]
    qo_ref[...] = _rope(q_ref[...], cos, sin)
    kr = _rope(k_ref[...], cos, sin)
    ko_ref[...] = kr
    km_ref[0] = jnp.mean(kr, axis=0, keepdims=True)


def _moba_rope(z, cos, sin):
    s = z.shape[0]
    tm = MOBA_BLOCK
    cb = COL_BQ // ATT_W
    blk = pl.BlockSpec((tm, ATT_W), lambda i: (i, 0))
    return pl.pallas_call(
        _moba_rope_kernel,
        grid=(s // tm,),
        in_specs=[pl.BlockSpec((tm, ATT_W), lambda i: (i, cb)),
                  pl.BlockSpec((tm, ATT_W), lambda i: (i, cb + 1)), blk, blk],
        out_specs=[blk, blk, pl.BlockSpec((1, 1, ATT_W), lambda i: (i, 0, 0))],
        out_shape=[jax.ShapeDtypeStruct((s, ATT_W), F32), jax.ShapeDtypeStruct((s, ATT_W), F32),
                   jax.ShapeDtypeStruct((s // tm, 1, ATT_W), F32)],
        compiler_params=_cparams(("parallel",)),
        name="moba_rope",
    )(z, z, cos, sin)


def _moba_select_kernel(q_ref, k_ref, v_ref, km_ref, qo_ref, ko_ref, vo_ref):
    own = pl.program_id(0)
    q = q_ref[...]
    gate_t = lax.dot_general(km_ref[...], q, (((1,), (1,)), ((), ())), preferred_element_type=F32,
                             precision=lax.Precision.HIGHEST)
    t = q.shape[0]
    blk = lax.broadcasted_iota(jnp.int32, (HEAD_DIM, t), 0)
    biases = []
    for h in range(N_ATT_HEADS):
        g = jnp.where(blk < own, gate_t[h * HEAD_DIM:(h + 1) * HEAD_DIM, :], -jnp.inf)
        bias = jnp.where(blk == own, 0.0, NEG_BIG)
        for r in range(MOBA_TOPK):
            mx = jnp.max(g, axis=0, keepdims=True)
            idx = jnp.min(jnp.where(g == mx, blk, HEAD_DIM), axis=0, keepdims=True)
            hit = blk == idx
            bias = jnp.where(hit, jnp.where(r < own, 0.0, bias), bias)
            g = jnp.where(hit, -jnp.inf, g)
        biases.append(bias)
    bias_all = jnp.concatenate(biases, axis=0).T
    lane = lax.broadcasted_iota(jnp.int32, (t, HEAD_DIM), 1)
    scale = HEAD_DIM ** -0.5 * LOG2E
    onehot_own = jnp.where(lane == own, 1.0, 0.0).astype(BF16)
    for h in range(N_ATT_HEADS):
        sl = slice(h * HEAD_DIM, (h + 1) * HEAD_DIM)
        qo_ref[h] = jnp.concatenate([(q[:, sl] * scale).astype(BF16), bias_all[:, sl].astype(BF16)], axis=1)
        ko_ref[h] = jnp.concatenate([k_ref[:, sl].astype(BF16), onehot_own], axis=1)
    _store_vt(v_ref, vo_ref)


def _moba_select(q_rope, k_rope, z, km_mat):
    s = z.shape[0]
    tm = MOBA_BLOCK
    cb = COL_BQ // ATT_W
    blk = pl.BlockSpec((tm, ATT_W), lambda i: (i, 0))
    head_spec = pl.BlockSpec((N_ATT_HEADS, tm, LANES), lambda i: (0, i, 0))
    return pl.pallas_call(
        _moba_select_kernel,
        grid=(s // tm,),
        in_specs=[blk, blk, pl.BlockSpec((tm, ATT_W), lambda i: (i, cb + 2)),
                  pl.BlockSpec((ATT_W, ATT_W), lambda i: (0, 0))],
        out_specs=[head_spec, head_spec, pl.BlockSpec((N_ATT_HEADS, VT_ROWS, tm), lambda i: (0, 0, i))],
        out_shape=[jax.ShapeDtypeStruct((N_ATT_HEADS, s, LANES), BF16),
                   jax.ShapeDtypeStruct((N_ATT_HEADS, s, LANES), BF16),
                   jax.ShapeDtypeStruct((N_ATT_HEADS, VT_ROWS, s), BF16)],
        compiler_params=_cparams(("parallel",)),
        name="moba_select",
    )(q_rope, k_rope, z, km_mat)


def _flash_kernel(qi_tab, ki_tab, *refs, tile, ngroups):
    groups = [refs[3 * g:3 * g + 3] for g in range(ngroups)]
    g_ref, o_ref, m_ref, acc_ref = refs[3 * ngroups:]
    step = pl.program_id(0)
    qi = qi_tab[step]
    ki = ki_tab[step]

    @pl.when(ki == 0)
    def _():
        m_ref[...] = jnp.full_like(m_ref, -jnp.inf)
        acc_ref[...] = jnp.zeros_like(acc_ref)

    units = [(g, h) for g in range(ngroups) for h in range(N_ATT_HEADS)]

    def update(masked):
        if masked:
            key = lax.broadcasted_iota(jnp.int32, (tile, tile), 0)
            qry = lax.broadcasted_iota(jnp.int32, (tile, tile), 1)
            causal = key <= qry
        scores = lambda g, h: _dot_nt(groups[g][1][h], groups[g][0][h])
        st_next = scores(*units[0])
        for u, (g, h) in enumerate(units):
            st = st_next
            if u + 1 < len(units):
                st_next = scores(*units[u + 1])
            if masked:
                st = jnp.where(causal, st, NEG_BIG)
            m_prev = m_ref[u]
            m_new = jnp.maximum(m_prev, jnp.max(st, axis=0, keepdims=True))
            alpha = jnp.exp2(m_prev - m_new)
            p = jnp.exp2(st - m_new).astype(BF16)
            m_ref[u] = m_new
            acc_ref[u] = acc_ref[u] * alpha + _dot(groups[g][2][h], p)

    @pl.when(ki < qi)
    def _():
        update(False)

    @pl.when(ki == qi)
    def _():
        update(True)
        outs = []
        for u in range(len(units)):
            acc = acc_ref[u]
            o = acc[0:HEAD_DIM, :] / acc[HEAD_DIM:HEAD_DIM + 1, :]
            outs.append(o * lax.rsqrt(jnp.mean(o * o, axis=0, keepdims=True) + EPS))
        o_ref[...] = (jnp.concatenate(outs, axis=0).T * g_ref[...]).astype(o_ref.dtype)


def _flash_attention(groups, gain_row, tile=1024):
    ngroups = len(groups)
    width = ATT_W * ngroups
    s = groups[0][2].shape[2]
    n = s // tile
    pairs = [(qi, ki) for qi in range(n) for ki in range(qi + 1)]
    qi_tab = jnp.asarray([p[0] for p in pairs], jnp.int32)
    ki_tab = jnp.asarray([p[1] for p in pairs], jnp.int32)
    group_specs = [pl.BlockSpec((N_ATT_HEADS, tile, LANES), lambda i, qt, kt: (0, qt[i], 0)),
                   pl.BlockSpec((N_ATT_HEADS, tile, LANES), lambda i, qt, kt: (0, kt[i], 0)),
                   pl.BlockSpec((N_ATT_HEADS, VT_ROWS, tile), lambda i, qt, kt: (0, 0, kt[i]))]
    grid_spec = pltpu.PrefetchScalarGridSpec(
        num_scalar_prefetch=2,
        grid=(len(pairs),),
        in_specs=group_specs * ngroups + [pl.BlockSpec((1, width), lambda i, qt, kt: (0, 0))],
        out_specs=pl.BlockSpec((tile, width), lambda i, qt, kt: (qt[i], 0)),
        scratch_shapes=[pltpu.VMEM((N_ATT_HEADS * ngroups, 1, tile), F32),
                        pltpu.VMEM((N_ATT_HEADS * ngroups, VT_ROWS, tile), F32)],
    )
    operands = [a for grp in groups for a in grp]
    return pl.pallas_call(
        functools.partial(_flash_kernel, tile=tile, ngroups=ngroups),
        grid_spec=grid_spec,
        out_shape=jax.ShapeDtypeStruct((s, width), BF16),
        compiler_params=_cparams(("arbitrary",)),
        name="flash_attention",
    )(qi_tab, ki_tab, *operands, gain_row)


def _shift_rows(u, tail, s):
    rolled = pltpu.roll(u, s, axis=0)
    rolled_tail = pltpu.roll(tail, s, axis=0)
    row8 = lax.broadcasted_iota(jnp.int32, tail.shape, 0)
    top = jnp.where(row8 < s, rolled_tail, rolled[0:8])
    return jnp.concatenate([top, rolled[8:]], axis=0)


def _mlstm_kernel(qk_ref, v_ref, o_ref, g_ref, cw_ref, gain_ref, y_ref, tail_ref, c_ref, n_ref, m_ref):
    @pl.when(pl.program_id(0) == 0)
    def _():
        tail_ref[...] = jnp.zeros_like(tail_ref)
        c_ref[...] = jnp.zeros_like(c_ref)
        n_ref[...] = jnp.zeros_like(n_ref)
        m_ref[...] = jnp.zeros_like(m_ref)

    L = ML_CHUNK
    rows = qk_ref.shape[0]
    nch = rows // L
    u = qk_ref[...]
    tail = tail_ref[...]
    cw = cw_ref[...]
    conv = u * cw[CONV_WIDTH - 1:CONV_WIDTH]
    for s in range(1, CONV_WIDTH):
        conv = conv + _shift_rows(u, tail, s) * cw[CONV_WIDTH - 1 - s:CONV_WIDTH - s]
    tail_ref[...] = u[rows - 8:rows]
    qk = conv * jax.nn.sigmoid(conv)

    r = lax.broadcasted_iota(jnp.int32, (L, L), 0)
    c = lax.broadcasted_iota(jnp.int32, (L, L), 1)
    tril = c <= r
    kscale = ML_DIM ** -0.5
    heads = range(ML_HEADS)
    st = []
    for ci in range(nch):
        rs = slice(ci * L, (ci + 1) * L)
        g = g_ref[rs, :]
        gt = g.T
        per_head = []
        for h in heads:
            sl = slice(h * ML_DIM, (h + 1) * ML_DIM)
            qh = qk[rs, sl]
            kh = qk[rs, ML_W + h * ML_DIM:ML_W + (h + 1) * ML_DIM] * kscale
            qb, kb, vb = qh.astype(BF16), kh.astype(BF16), v_ref[rs, sl].astype(BF16)
            b_col = g[:, 8 + h:9 + h]
            i_col = g[:, 4 + h:5 + h]
            b_row = gt[8 + h:9 + h, :]
            i_row = gt[4 + h:5 + h, :]
            dmat = jnp.where(tril, b_col - b_row + i_row, -jnp.inf)
            dmax = jnp.max(dmat, axis=1, keepdims=True)
            a1 = _dot_nt(qb, kb) * jnp.exp(dmat - dmax)
            b_last = b_row[:, L - 1:L]
            g_col = b_last - b_col + i_col
            gmax = jnp.max(g_col, axis=0, keepdims=True)
            kw1 = kh * jnp.exp(g_col - gmax)
            per_head.append(dict(qh=qh, qb=qb, b_col=b_col, dmax=dmax, b_last=b_last, gmax=gmax,
                                 av=_dot(a1.astype(BF16), vb), asum=jnp.sum(a1, axis=1, keepdims=True),
                                 u1=_dot(kw1.T.astype(BF16), vb), ksum=jnp.sum(kw1, axis=0, keepdims=True)))
        st.append(per_head)
    state = [(c_ref[h], n_ref[h], m_ref[h][:, 0:1]) for h in heads]
    for ci in range(nch):
        rs = slice(ci * L, (ci + 1) * L)
        outs = []
        for h in heads:
            s_ = st[ci][h]
            sl = slice(h * ML_DIM, (h + 1) * ML_DIM)
            cmat, nrow, m_prev = state[h]
            inter = s_["b_col"] + m_prev
            m_t = jnp.maximum(inter, s_["dmax"])
            s_intra = jnp.exp(s_["dmax"] - m_t)
            w_inter = jnp.exp(inter - m_t)
            num = s_intra * s_["av"] + w_inter * _dot(s_["qb"], cmat.astype(BF16))
            den = s_intra * s_["asum"] + w_inter * jnp.sum(s_["qh"] * nrow, axis=1, keepdims=True)
            hh = num / jnp.maximum(jnp.abs(den), jnp.exp(-m_t))
            y = jax.nn.sigmoid(o_ref[rs, sl]) * hh
            outs.append(y * lax.rsqrt(jnp.mean(y * y, axis=1, keepdims=True) + EPS))
        y_ref[rs, :] = (jnp.concatenate(outs, axis=1) * gain_ref[...]).astype(y_ref.dtype)
        for h in heads:
            s_ = st[ci][h]
            cmat, nrow, m_prev = state[h]
            m_new = jnp.maximum(s_["b_last"] + m_prev, s_["gmax"])
            decay = jnp.exp(s_["b_last"] + m_prev - m_new)
            scale = jnp.exp(s_["gmax"] - m_new)
            state[h] = (decay * cmat + scale * s_["u1"], decay * nrow + scale * s_["ksum"], m_new)
    for h in heads:
        c_ref[h], n_ref[h] = state[h][0], state[h][1]
        m_ref[h] = jnp.broadcast_to(state[h][2], (1, LANES))


ML_CHUNKS_PER_STEP = 4


def _mlstm(z, gates, conv_w, gain_row):
    s = z.shape[0]
    L = ML_CHUNK * ML_CHUNKS_PER_STEP
    return pl.pallas_call(
        _mlstm_kernel,
        grid=(s // L,),
        in_specs=[pl.BlockSpec((L, 2 * ML_W), lambda i: (i, COL_MQK // (2 * ML_W))),
                  pl.BlockSpec((L, ML_W), lambda i: (i, COL_MV // ML_W)),
                  pl.BlockSpec((L, ML_W), lambda i: (i, COL_MO // ML_W)),
                  pl.BlockSpec((L, LANES), lambda i: (i, 0)),
                  pl.BlockSpec((CONV_WIDTH, 2 * ML_W), lambda i: (0, 0)),
                  pl.BlockSpec((1, ML_W), lambda i: (0, 0))],
        out_specs=pl.BlockSpec((L, ML_W), lambda i: (i, 0)),
        out_shape=jax.ShapeDtypeStruct((s, ML_W), BF16),
        scratch_shapes=[pltpu.VMEM((8, 2 * ML_W), F32),
                        pltpu.VMEM((ML_HEADS, ML_DIM, ML_DIM), F32),
                        pltpu.VMEM((ML_HEADS, 1, ML_DIM), F32),
                        pltpu.VMEM((ML_HEADS, 1, LANES), F32)],
        compiler_params=_cparams(("arbitrary",)),
        name="mlstm",
    )(z, z, z, gates, conv_w, gain_row)


def _layer_norm(h, g, b):
    mu = jnp.mean(h, axis=1, keepdims=True)
    d = h - mu
    var = jnp.mean(d * d, axis=1, keepdims=True)
    return d * lax.rsqrt(var + EPS) * g + b


def _row_seg(j, rows):
    return pl.ds(j, rows, stride=SUBLANES)


def _outproj_kernel(yf_ref, ym_ref, yb_ref, w_ref, x_ref, g_ref, b_ref, wr_ref, br_ref, x1_ref, x1t_ref, lg_ref):
    mix = (_dot(yf_ref[...], w_ref[0:ATT_W, :]) + _dot(ym_ref[...], w_ref[ATT_W:ATT_W + ML_W, :])
           + _dot(yb_ref[...], w_ref[ATT_W + ML_W:, :]))
    x1 = _layer_norm(ALPHA * x_ref[...] + mix, g_ref[...], b_ref[...])
    x1_ref[...] = x1
    tm = x1.shape[0]
    for j in range(D_MODEL // LANES):
        x1t_ref[_row_seg(j, tm), :] = x1[:, j * LANES:(j + 1) * LANES]
    lg_ref[...] = jnp.dot(x1, wr_ref[...], preferred_element_type=F32,
                          precision=lax.Precision.HIGHEST) + br_ref[...]


def _outproj_ln_router(y_att, ym, w_out, x, ln_g, ln_b, w_router, b_router, tm=512):
    s = x.shape[0]
    const = lambda shape: pl.BlockSpec(shape, lambda i: (0, 0))
    rows = lambda w: pl.BlockSpec((tm, w), lambda i: (i, 0))
    yf, yb = y_att, y_att
    return pl.pallas_call(
        _outproj_kernel,
        grid=(s // tm,),
        in_specs=[rows(ATT_W), rows(ML_W), pl.BlockSpec((tm, ATT_W), lambda i: (i, 1)),
                  const((D_MODEL, D_MODEL)), rows(D_MODEL),
                  const((1, D_MODEL)), const((1, D_MODEL)), const((D_MODEL, LANES)), const((1, LANES))],
        out_specs=[rows(D_MODEL), pl.BlockSpec((SUBLANES * tm, LANES), lambda i: (i, 0)), rows(LANES)],
        out_shape=[jax.ShapeDtypeStruct((s, D_MODEL), F32), jax.ShapeDtypeStruct((SUBLANES * s, LANES), F32),
                   jax.ShapeDtypeStruct((s, LANES), F32)],
        compiler_params=_cparams(("parallel",)),
        name="outproj_ln_router",
    )(yf, ym, yb, w_out, x, ln_g, ln_b, w_router, b_router)


def _first_argmax(v, lane):
    mx = jnp.max(v, axis=1, keepdims=True)
    idx = jnp.min(jnp.where(v == mx, lane, LANES), axis=1, keepdims=True)
    return mx, idx


def _route_kernel(lg_ref, info_ref, cnt_ref, carry_ref):
    @pl.when(pl.program_id(0) == 0)
    def _():
        carry_ref[...] = jnp.zeros_like(carry_ref)

    lg = lg_ref[...]
    t = lg.shape[0]
    lane = lax.broadcasted_iota(jnp.int32, lg.shape, 1)
    is_grp = lane < N_GROUPS
    gmax, gsel = _first_argmax(jnp.where(is_grp, lg, -jnp.inf), lane)
    p_grp = 1.0 / jnp.sum(jnp.where(is_grp, jnp.exp(lg - gmax), 0.0), axis=1, keepdims=True)
    lo = N_GROUPS + EXPERTS_PER_GROUP * gsel
    el = jnp.where((lane >= lo) & (lane < lo + EXPERTS_PER_GROUP), lg, -jnp.inf)
    v0, i0 = _first_argmax(el, lane)
    v1, i1 = _first_argmax(jnp.where(lane == i0, -jnp.inf, el), lane)
    ex = jnp.exp(v1 - v0)
    w0 = p_grp / (1.0 + ex)
    w1 = p_grp * ex / (1.0 + ex)
    e0 = i0 - N_GROUPS
    e1 = i1 - N_GROUPS

    cnt = jnp.where((lane == e0) | (lane == e1), 1.0, 0.0)
    r = lax.broadcasted_iota(jnp.int32, (t, t), 0)
    c = lax.broadcasted_iota(jnp.int32, (t, t), 1)
    strict = jnp.where(c < r, 1.0, 0.0).astype(BF16)
    carry = carry_ref[...]
    before = _dot(strict, cnt.astype(BF16)) + carry
    rank0 = jnp.sum(jnp.where(lane == e0, before, 0.0), axis=1, keepdims=True)
    rank1 = jnp.sum(jnp.where(lane == e1, before, 0.0), axis=1, keepdims=True)
    carry = carry + jnp.sum(cnt, axis=0, keepdims=True)
    carry_ref[...] = carry
    cnt_ref[...] = carry
    vals = [e0.astype(F32), e1.astype(F32), w0, w1, rank0, rank1]
    info = jnp.zeros(lg.shape, F32)
    for j, val in enumerate(vals):
        info = jnp.where(lane == j, val, info)
    info_ref[...] = info


def _route(logits, tm=512):
    s = logits.shape[0]
    return pl.pallas_call(
        _route_kernel,
        grid=(s // tm,),
        in_specs=[pl.BlockSpec((tm, LANES), lambda i: (i, 0))],
        out_specs=[pl.BlockSpec((tm, LANES), lambda i: (i, 0)), pl.BlockSpec((1, LANES), lambda i: (0, 0))],
        out_shape=[jax.ShapeDtypeStruct((s, LANES), F32), jax.ShapeDtypeStruct((1, LANES), F32)],
        scratch_shapes=[pltpu.VMEM((1, LANES), F32)],
        compiler_params=_cparams(("arbitrary",)),
        name="route",
    )(logits)


MOE_ROWS = 256


def _moe_kernel(blk_e, nv, src0_ref, srcn_ref, slotp_ref, slotc_ref, x_hbm, wg_ref, wu_ref, wd_ref, out_hbm,
                xbuf0, xbuf1, ybuf0, ybuf1, wgb, wub, wdb, gsem, ssem, *, nblk):
    b = pl.program_id(0)
    xbufs = (xbuf0, xbuf1)
    ybufs = (ybuf0, ybuf1)
    nv_prev2, nv_prev, nv_cur, nv_next = nv[b], nv[b + 1], nv[b + 2], nv[b + 3]

    def row_in(tok8, slot, r):
        return pltpu.make_async_copy(x_hbm.at[pl.ds(pl.multiple_of(tok8, SUBLANES), SUBLANES), :],
                                     xbufs[slot].at[pl.ds(SUBLANES * r, SUBLANES), :], gsem.at[slot])

    def row_out(dst8, slot, r):
        return pltpu.make_async_copy(ybufs[slot].at[pl.ds(SUBLANES * r, SUBLANES), :],
                                     out_hbm.at[pl.ds(pl.multiple_of(dst8, SUBLANES), SUBLANES), :], ssem.at[slot])

    def start_rows(make, idx_ref, slot, n):
        for r in range(MOE_ROWS):
            @pl.when(r < n)
            def _():
                make(idx_ref[0, 0, r], slot, r).start(priority=r % 2)

    def wait_rows(make, slot, n):
        @pl.when(n > 0)
        def _():
            rows = pl.multiple_of(n * SUBLANES, SUBLANES)
            if make is row_in:
                pltpu.make_async_copy(x_hbm.at[pl.ds(0, rows), :], xbufs[slot].at[pl.ds(0, rows), :],
                                      gsem.at[slot]).wait()
            else:
                pltpu.make_async_copy(ybufs[slot].at[pl.ds(0, rows), :], out_hbm.at[pl.ds(0, rows), :],
                                      ssem.at[slot]).wait()

    @pl.when(b == 0)
    def _():
        for half in range(2):
            xbufs[half][...] = jnp.zeros_like(xbufs[half])
        start_rows(row_in, src0_ref, 0, nv_cur)

    @pl.when((b == 0) | (blk_e[b] != blk_e[jnp.maximum(b - 1, 0)]))
    def _():
        wgb[...] = wg_ref[0, 0].astype(BF16)
        wub[...] = wu_ref[0, 0].astype(BF16)
        wdb[...] = wd_ref[0, 0].astype(BF16)

    def step(cur):
        nxt = 1 - cur
        wait_rows(row_in, cur, nv_cur)
        start_rows(row_out, slotp_ref, nxt, nv_prev)
        start_rows(row_in, srcn_ref, nxt, nv_next)

        nseg = D_MODEL // LANES
        seg = lambda j: _row_seg(j, MOE_ROWS)
        xb = jnp.concatenate([xbufs[cur][seg(j), :] for j in range(nseg)], axis=1).astype(BF16)
        gate = _dot(xb, wgb[...])
        up = _dot(xb, wub[...])
        hid = (gate * jax.nn.sigmoid(gate) * up).astype(BF16)
        y = _dot(hid, wdb[...])

        wait_rows(row_out, cur, nv_prev2)

        for j in range(nseg):
            ybufs[cur][seg(j), :] = y[:, j * LANES:(j + 1) * LANES]

        @pl.when(b == nblk - 1)
        def _():
            start_rows(row_out, slotc_ref, cur, nv_cur)
            wait_rows(row_out, nxt, nv_prev)
            wait_rows(row_out, cur, nv_cur)

    for parity in range(2):
        pl.when(b % 2 == parity)(functools.partial(step, parity))


def _moe_ffn(x1, src_ext, slot_ext, blk_e, nv_ext, w_gate, w_up, w_down, layer):
    s = x1.shape[0] // SUBLANES
    nblk = src_ext.shape[0] - 1
    stage = (SUBLANES * MOE_ROWS, LANES)
    idx_blk = (1, 1, MOE_ROWS)
    smem = pltpu.SMEM
    grid_spec = pltpu.PrefetchScalarGridSpec(
        num_scalar_prefetch=2,
        grid=(nblk,),
        in_specs=[pl.BlockSpec(idx_blk, lambda b, be, nv: (0, 0, 0), memory_space=smem),
                  pl.BlockSpec(idx_blk, lambda b, be, nv: (b + 1, 0, 0), memory_space=smem),
                  pl.BlockSpec(idx_blk, lambda b, be, nv: (b, 0, 0), memory_space=smem),
                  pl.BlockSpec(idx_blk, lambda b, be, nv: (b + 1, 0, 0), memory_space=smem),
                  pl.BlockSpec(memory_space=pl.ANY),
                  pl.BlockSpec((1, 1, D_MODEL, D_EXPERT), lambda b, be, nv: (layer, be[b], 0, 0)),
                  pl.BlockSpec((1, 1, D_MODEL, D_EXPERT), lambda b, be, nv: (layer, be[b], 0, 0)),
                  pl.BlockSpec((1, 1, D_EXPERT, D_MODEL), lambda b, be, nv: (layer, be[b], 0, 0))],
        out_specs=pl.BlockSpec(memory_space=pl.ANY),
        scratch_shapes=[pltpu.VMEM(stage, F32),
                        pltpu.VMEM(stage, F32),
                        pltpu.VMEM(stage, F32),
                        pltpu.VMEM(stage, F32),
                        pltpu.VMEM((D_MODEL, D_EXPERT), BF16),
                        pltpu.VMEM((D_MODEL, D_EXPERT), BF16),
                        pltpu.VMEM((D_EXPERT, D_MODEL), BF16),
                        pltpu.SemaphoreType.DMA((2,)),
                        pltpu.SemaphoreType.DMA((2,))],
    )
    return pl.pallas_call(
        functools.partial(_moe_kernel, nblk=nblk),
        grid_spec=grid_spec,
        out_shape=jax.ShapeDtypeStruct((SUBLANES * 2 * s, LANES), F32),
        compiler_params=_cparams(("arbitrary",)),
        name="moe_ffn",
    )(blk_e, nv_ext, src_ext, src_ext, slot_ext, slot_ext, x1, w_gate, w_up, w_down)


def _dest_kernel(info_ref, pstart_ref, d_ref):
    info = info_ref[...]
    tm = info.shape[0]
    lane = lax.broadcasted_iota(jnp.int32, info.shape, 1)
    ps = pstart_ref[...]
    dests = []
    for k in range(2):
        e = info[:, k:k + 1].astype(jnp.int32)
        dests.append(jnp.sum(jnp.where(lane == e, ps, 0.0), axis=1, keepdims=True) + info[:, 4 + k:5 + k])
    packed = jnp.where(lane == 0, dests[0], jnp.where(lane == 1, dests[1], 0.0))
    for j in range(tm // LANES):
        rows = packed[j * LANES:(j + 1) * LANES, :].T
        for k in range(2):
            d_ref[k, j:j + 1, :] = rows[k:k + 1, :].astype(jnp.int32)


def _dest(info, pstart_row, tm=1024):
    s = info.shape[0]
    return pl.pallas_call(
        _dest_kernel,
        grid=(s // tm,),
        in_specs=[pl.BlockSpec((tm, LANES), lambda i: (i, 0)), pl.BlockSpec((1, LANES), lambda i: (0, 0))],
        out_specs=pl.BlockSpec((2, tm // LANES, LANES), lambda i: (0, i, 0)),
        out_shape=jax.ShapeDtypeStruct((2, s // LANES, LANES), jnp.int32),
        compiler_params=_cparams(("parallel",)),
        name="moe_dest",
    )(info, pstart_row)


def _moe_plan(info, counts_row, s):
    counts = counts_row[0, :N_EXPERTS].astype(jnp.int32)
    padded = ((counts + MOE_ROWS - 1) // MOE_ROWS) * MOE_ROWS
    pends = jnp.cumsum(padded)
    pstarts = pends - padded
    nblk = (2 * s) // MOE_ROWS + N_EXPERTS
    p = nblk * MOE_ROWS
    dest = _dest(info, _pad_lanes(pstarts.astype(F32))).reshape(2 * s)
    out_slot = jnp.zeros((p,), jnp.int32).at[dest].set(jnp.arange(2 * s, dtype=jnp.int32))
    src_tok = jnp.where(out_slot >= s, out_slot - s, out_slot)
    nused = (pends[-1] // MOE_ROWS).astype(jnp.int32)
    blk = jnp.arange(nblk, dtype=jnp.int32)
    blk_start = jnp.minimum(blk, nused - 1) * MOE_ROWS
    blk_e = jnp.minimum(jnp.sum((pends[None, :] <= blk_start[:, None]).astype(jnp.int32), axis=1), N_EXPERTS - 1)
    is_e = jnp.arange(N_EXPERTS)[None, :] == blk_e[:, None]
    seg_start = jnp.sum(jnp.where(is_e, pstarts[None, :], 0), axis=1)
    seg_count = jnp.sum(jnp.where(is_e, counts[None, :], 0), axis=1)
    nvalid = jnp.where(blk < nused, jnp.clip(seg_count - (blk * MOE_ROWS - seg_start), 0, MOE_ROWS), 0)
    nv_ext = jnp.concatenate([jnp.zeros((2,), jnp.int32), nvalid.astype(jnp.int32), jnp.zeros((2,), jnp.int32)])
    pad_blk = jnp.zeros((MOE_ROWS,), jnp.int32)
    src_ext = jnp.concatenate([src_tok, pad_blk]) * SUBLANES
    slot_ext = jnp.concatenate([pad_blk, out_slot]) * SUBLANES
    return src_ext.reshape(nblk + 1, 1, MOE_ROWS), slot_ext.reshape(nblk + 1, 1, MOE_ROWS), blk_e, nv_ext


def _combine_kernel(x1_ref, ya_ref, yb_ref, info_ref, g_ref, b_ref, o_ref):
    info = info_ref[...]
    tm = info.shape[0]
    wide = lambda ref: jnp.concatenate([ref[_row_seg(j, tm), :] for j in range(D_MODEL // LANES)], axis=1)
    ffn = info[:, 2:3] * wide(ya_ref) + info[:, 3:4] * wide(yb_ref)
    o_ref[...] = _layer_norm(ALPHA * x1_ref[...] + ffn, g_ref[...], b_ref[...])


def _combine_ln(x1, y2, info, ln_g, ln_b, tm=512):
    s = x1.shape[0]
    const = pl.BlockSpec((1, D_MODEL), lambda i: (0, 0))
    return pl.pallas_call(
        _combine_kernel,
        grid=(s // tm,),
        in_specs=[pl.BlockSpec((tm, D_MODEL), lambda i: (i, 0)),
                  pl.BlockSpec((SUBLANES * tm, LANES), lambda i: (i, 0)),
                  pl.BlockSpec((SUBLANES * tm, LANES), lambda i: (i + s // tm, 0)),
                  pl.BlockSpec((tm, LANES), lambda i: (i, 0)), const, const],
        out_specs=pl.BlockSpec((tm, D_MODEL), lambda i: (i, 0)),
        out_shape=jax.ShapeDtypeStruct((s, D_MODEL), F32),
        compiler_params=_cparams(("parallel",)),
        name="combine_ln",
    )(x1, y2, y2, info, ln_g, ln_b)


def _pad_lanes(v, width=LANES):
    return jnp.zeros((1, width), F32).at[0, :v.shape[0]].set(v)


def _layer(layer, x, cos, sin, w_in, b_fox_f, b_mlstm_i, b_mlstm_f, conv_w, g_fox, g_mlstm, g_moba, w_out,
           ln1_g, ln1_b, w_grp, b_grp, w_exp_router, b_exp_router, w_gate, w_up, w_down, ln2_g, ln2_b):
    s = x.shape[0]
    gate_bias = _pad_lanes(jnp.concatenate([b_fox_f, b_mlstm_i, b_mlstm_f]))

    z = _inproj(x, w_in, layer)
    gates = _gate_prep(z, gate_bias)

    fox_ops = _fox_prep(z, gates)

    q_rope, k_rope, kmean = _moba_rope(z, cos, sin)
    km = kmean[:, 0, :].reshape(s // MOBA_BLOCK, N_ATT_HEADS, HEAD_DIM)
    km_mat = jnp.zeros((N_ATT_HEADS, HEAD_DIM, N_ATT_HEADS, HEAD_DIM), F32)
    for h in range(N_ATT_HEADS):
        km_mat = km_mat.at[h, :s // MOBA_BLOCK, h, :].set(km[:, h, :])
    km_mat = km_mat.reshape(ATT_W, ATT_W)
    moba_ops = _moba_select(q_rope, k_rope, z, km_mat)
    y_att = _flash_attention([fox_ops, moba_ops], jnp.concatenate([g_fox, g_moba])[None, :])

    y_mlstm = _mlstm(z, gates, conv_w, g_mlstm[None, :])

    w_router = jnp.zeros((D_MODEL, LANES), F32)
    w_router = w_router.at[:, :N_GROUPS].set(w_grp)
    w_router = w_router.at[:, N_GROUPS:N_GROUPS + N_EXPERTS].set(w_exp_router.reshape(D_MODEL, N_EXPERTS))
    b_router = _pad_lanes(jnp.concatenate([b_grp, b_exp_router.reshape(N_EXPERTS)]))
    x1, x1_tiles, logits = _outproj_ln_router(y_att, y_mlstm, w_out.astype(BF16), x, ln1_g[None, :],
                                              ln1_b[None, :], w_router, b_router)

    info, counts = _route(logits)
    src_ext, slot_ext, blk_e, nv_ext = _moe_plan(info, counts, s)
    y2 = _moe_ffn(x1_tiles, src_ext, slot_ext, blk_e, nv_ext, w_gate, w_up, w_down, layer)
    return _combine_ln(x1, y2, info, ln2_g[None, :], ln2_b[None, :])


def kernel(x, positions, w_in, b_fox_f, b_mlstm_i, b_mlstm_f, conv_w, g_fox, g_mlstm, g_moba, w_out, ln1_g, ln1_b, w_grp, b_grp, w_exp_router, b_exp_router, w_gate, w_up, w_down, ln2_g, ln2_b):
    assert x.shape[0] == 1
    xs = x[0]
    d = jnp.arange(ATT_W) % HEAD_DIM
    half = ROPE_DIM // 2
    inv = 1.0 / (ROPE_THETA ** (jnp.arange(0, ROPE_DIM, 2, dtype=F32) / ROPE_DIM))
    inv_row = jnp.where(d < ROPE_DIM, inv[d % half], 0.0)[None, :].astype(F32)
    sign_row = jnp.where(d < half, -1.0, 1.0)[None, :].astype(F32)
    cos, sin = _rope_tables(positions[0][:, None], inv_row, sign_row)
    for l in range(DEPTH):
        xs = _layer(l, xs, cos, sin, w_in, b_fox_f[l], b_mlstm_i[l], b_mlstm_f[l], conv_w[l], g_fox[l],
                    g_mlstm[l], g_moba[l], w_out[l], ln1_g[l], ln1_b[l], w_grp[l], b_grp[l],
                    w_exp_router[l], b_exp_router[l], w_gate, w_up, w_down, ln2_g[l], ln2_b[l])
    return xs[None]
```

```python
import functools

import jax
import jax.numpy as jnp
from jax import lax
from jax.experimental import pallas as pl
from jax.experimental.pallas import tpu as pltpu

D_MODEL = 1024
DEPTH = 2
HEAD_DIM = 64
N_ATT_HEADS = 4
ATT_W = N_ATT_HEADS * HEAD_DIM
ML_HEADS = 4
ML_DIM = 128
ML_W = ML_HEADS * ML_DIM
ML_CHUNK = 128
CONV_WIDTH = 4
ROPE_DIM = 16
ROPE_THETA = 500000.0
MOBA_BLOCK = 256
MOBA_TOPK = 3
N_GROUPS = 4
EXPERTS_PER_GROUP = 8
N_EXPERTS = N_GROUPS * EXPERTS_PER_GROUP
D_EXPERT = 512
ALPHA = (2 * DEPTH) ** 0.25
EPS = 1e-5

LANES = 128
SUBLANES = 8
NEG_BIG = -1e30
VMEM_LIMIT = 56 * 1024 * 1024

COL_MQK = 0
COL_MV = 1024
COL_MO = 1536
COL_FQ = 2048
COL_BQ = 2816
COL_GATE = 3584
Z_W = 3712

F32 = jnp.float32
BF16 = jnp.bfloat16


def _cparams(sem):
    return pltpu.CompilerParams(dimension_semantics=sem, vmem_limit_bytes=VMEM_LIMIT)


def _split3(c):
    hi = c.astype(BF16).astype(F32)
    r1 = c - hi
    mid = r1.astype(BF16).astype(F32)
    lo = (r1 - mid).astype(BF16).astype(F32)
    return hi, mid, lo


def _dot(a, b):
    return jnp.dot(a, b, preferred_element_type=F32)


def _dot_nt(a, b):
    return lax.dot_general(a, b, (((1,), (1,)), ((), ())), preferred_element_type=F32)


LOG2E = 1.4426950408889634
VT_ROWS = 80


def _store_vt(v_ref, vo_ref):
    vt = v_ref[...].T
    t = vt.shape[1]
    row = lax.broadcasted_iota(jnp.int32, (VT_ROWS - HEAD_DIM, t), 0)
    tail = jnp.where(row == 0, 1.0, 0.0)
    for h in range(N_ATT_HEADS):
        vo_ref[h] = jnp.concatenate([vt[h * HEAD_DIM:(h + 1) * HEAD_DIM, :], tail], axis=0).astype(BF16)


IN_W = 3596
_W_RUNS = ((COL_MQK, 772, 2308),
           (COL_MO, 2316, 2828),
           (COL_FQ, 0, 768),
           (COL_BQ, 2828, 3596))
_W_GATE_RUNS = ((768, 772), (2308, 2316))


def _cols(w_ref, r0, r1, a, b):
    a0 = (a // LANES) * LANES
    b0 = min(-(-b // LANES) * LANES, IN_W)
    return w_ref[r0:r1, a0:b0][:, a - a0:b - a0]


def _inproj_kernel(x_ref, w_ref, o_ref, wb_ref):
    @pl.when(pl.program_id(0) == 0)
    def _():
        rows = 256
        for r0 in range(0, D_MODEL, rows):
            r1 = r0 + rows
            for dst, a, b in _W_RUNS:
                wb_ref[r0:r1, dst:dst + (b - a)] = _cols(w_ref, r0, r1, a, b).astype(BF16)
            gate = [_cols(w_ref, r0, r1, a, b) for a, b in _W_GATE_RUNS]
            used = sum(b - a for a, b in _W_GATE_RUNS)
            gate.append(jnp.zeros((rows, LANES - used), F32))
            wb_ref[r0:r1, COL_GATE:] = jnp.concatenate(gate, axis=1).astype(BF16)

    xb = x_ref[...].astype(BF16)
    n = o_ref.shape[1]
    step = 512
    for j in range(0, n, step):
        w = min(step, n - j)
        o_ref[:, j:j + w] = _dot(xb, wb_ref[:, j:j + w])


def _inproj(x, w_in, layer, tm=256):
    s = x.shape[0]
    return pl.pallas_call(
        _inproj_kernel,
        grid=(s // tm,),
        in_specs=[pl.BlockSpec((tm, D_MODEL), lambda i: (i, 0)),
                  pl.BlockSpec((D_MODEL, IN_W), lambda i: (layer, 0), pipeline_mode=pl.Buffered(1))],
        out_specs=pl.BlockSpec((tm, Z_W), lambda i: (i, 0)),
        out_shape=jax.ShapeDtypeStruct((s, Z_W), F32),
        scratch_shapes=[pltpu.VMEM((D_MODEL, Z_W), BF16)],
        compiler_params=_cparams(("arbitrary",)),
        name="inproj",
    )(x, w_in.reshape(-1, IN_W))


def _log_sigmoid(x):
    return jnp.minimum(x, 0.0) - jnp.log(1.0 + jnp.exp(-jnp.abs(x)))


def _gate_kernel(zg_ref, bias_ref, o_ref, carry_ref):
    @pl.when(pl.program_id(0) == 0)
    def _():
        carry_ref[...] = jnp.zeros_like(carry_ref)

    g = zg_ref[...] + bias_ref[...]
    ls = _log_sigmoid(g)
    t = g.shape[0]
    r = lax.broadcasted_iota(jnp.int32, (t, t), 0)
    c = lax.broadcasted_iota(jnp.int32, (t, t), 1)
    tri = c <= r
    tri_all = jnp.where(tri, 1.0, 0.0).astype(BF16)
    tri_chunk = jnp.where(tri & ((c // ML_CHUNK) == (r // ML_CHUNK)), 1.0, 0.0).astype(BF16)
    hi, mid, lo = _split3(ls)
    parts = [p.astype(BF16) for p in (hi, mid, lo)]
    cum_all = sum(_dot(tri_all, p) for p in parts)
    cum_chunk = sum(_dot(tri_chunk, p) for p in parts)
    carry = carry_ref[...]
    lane = lax.broadcasted_iota(jnp.int32, g.shape, 1)
    o_ref[...] = jnp.where(lane < 4, cum_all + carry, jnp.where(lane < 8, g, cum_chunk))
    carry_ref[...] = carry + cum_all[t - 1:t, :]


def _gate_prep(z, bias_row, tm=512):
    s = z.shape[0]
    return pl.pallas_call(
        _gate_kernel,
        grid=(s // tm,),
        in_specs=[pl.BlockSpec((tm, LANES), lambda i: (i, COL_GATE // LANES)),
                  pl.BlockSpec((1, LANES), lambda i: (0, 0))],
        out_specs=pl.BlockSpec((tm, LANES), lambda i: (i, 0)),
        out_shape=jax.ShapeDtypeStruct((s, LANES), F32),
        scratch_shapes=[pltpu.VMEM((1, LANES), F32)],
        compiler_params=_cparams(("arbitrary",)),
        name="gate_prep",
    )(z, bias_row)


def _fox_prep_kernel(q_ref, k_ref, v_ref, g_ref, qo_ref, ko_ref, vo_ref):
    g = g_ref[...]
    t = g.shape[0]
    lane = lax.broadcasted_iota(jnp.int32, (t, HEAD_DIM), 1)
    scale = HEAD_DIM ** -0.5 * LOG2E
    for h in range(N_ATT_HEADS):
        hi, mid, lo = _split3(g[:, h:h + 1] * LOG2E)
        aug_q = jnp.where(lane == 0, hi, jnp.where(lane == 1, mid, jnp.where(lane == 2, lo,
                          jnp.where(lane < 6, 1.0, 0.0))))
        aug_k = jnp.where(lane < 3, 1.0, jnp.where(lane == 3, -hi, jnp.where(lane == 4, -mid,
                          jnp.where(lane == 5, -lo, 0.0))))
        sl = slice(h * HEAD_DIM, (h + 1) * HEAD_DIM)
        qo_ref[h] = jnp.concatenate([(q_ref[:, sl] * scale).astype(BF16), aug_q.astype(BF16)], axis=1)
        ko_ref[h] = jnp.concatenate([k_ref[:, sl].astype(BF16), aug_k.astype(BF16)], axis=1)
    _store_vt(v_ref, vo_ref)


def _fox_prep(z, gates, tm=512):
    s = z.shape[0]
    cb = COL_FQ // ATT_W
    head_spec = pl.BlockSpec((N_ATT_HEADS, tm, LANES), lambda i: (0, i, 0))
    return pl.pallas_call(
        _fox_prep_kernel,
        grid=(s // tm,),
        in_specs=[pl.BlockSpec((tm, ATT_W), lambda i: (i, cb)),
                  pl.BlockSpec((tm, ATT_W), lambda i: (i, cb + 1)),
                  pl.BlockSpec((tm, ATT_W), lambda i: (i, cb + 2)),
                  pl.BlockSpec((tm, LANES), lambda i: (i, 0))],
        out_specs=[head_spec, head_spec, pl.BlockSpec((N_ATT_HEADS, VT_ROWS, tm), lambda i: (0, 0, i))],
        out_shape=[jax.ShapeDtypeStruct((N_ATT_HEADS, s, LANES), BF16),
                   jax.ShapeDtypeStruct((N_ATT_HEADS, s, LANES), BF16),
                   jax.ShapeDtypeStruct((N_ATT_HEADS, VT_ROWS, s), BF16)],
        compiler_params=_cparams(("parallel",)),
        name="fox_prep",
    )(z, z, z, gates)


def _rope_table_kernel(pos_ref, inv_ref, sign_ref, cos_ref, sin_ref):
    ang = pos_ref[...].astype(F32) * inv_ref[...]
    cos_ref[...] = jnp.cos(ang)
    sin_ref[...] = jnp.sin(ang) * sign_ref[...]


def _rope_tables(pos_col, inv_row, sign_row, tm=512):
    s = pos_col.shape[0]
    row = pl.BlockSpec((1, ATT_W), lambda i: (0, 0))
    out = pl.BlockSpec((tm, ATT_W), lambda i: (i, 0))
    return pl.pallas_call(
        _rope_table_kernel,
        grid=(s // tm,),
        in_specs=[pl.BlockSpec((tm, 1), lambda i: (i, 0)), row, row],
        out_specs=[out, out],
        out_shape=[jax.ShapeDtypeStruct((s, ATT_W), F32)] * 2,
        compiler_params=_cparams(("parallel",)),
        name="rope_tables",
    )(pos_col, inv_row, sign_row)


def _rope(u, cos, sin_signed):
    half = ROPE_DIM // 2
    lane = lax.broadcasted_iota(jnp.int32, u.shape, 1) % HEAD_DIM
    up = pltpu.roll(u, ATT_W - half, axis=1)
    down = pltpu.roll(u, half, axis=1)
    partner = jnp.where(lane < half, up, down)
    return u * cos + partner * sin_signed


def _moba_rope_kernel(q_ref, k_ref, cos_ref, sin_ref, qo_ref, ko_ref, km_ref):
    cos = cos_ref[...]
    sin = sin_ref[...]
    qo_ref[...] = _rope(q_ref[...], cos, sin)
    kr = _rope(k_ref[...], cos, sin)
    ko_ref[...] = kr
    km_ref[0] = jnp.mean(kr, axis=0, keepdims=True)


def _moba_rope(z, cos, sin):
    s = z.shape[0]
    tm = MOBA_BLOCK
    cb = COL_BQ // ATT_W
    blk = pl.BlockSpec((tm, ATT_W), lambda i: (i, 0))
    return pl.pallas_call(
        _moba_rope_kernel,
        grid=(s // tm,),
        in_specs=[pl.BlockSpec((tm, ATT_W), lambda i: (i, cb)),
                  pl.BlockSpec((tm, ATT_W), lambda i: (i, cb + 1)), blk, blk],
        out_specs=[blk, blk, pl.BlockSpec((1, 1, ATT_W), lambda i: (i, 0, 0))],
        out_shape=[jax.ShapeDtypeStruct((s, ATT_W), F32), jax.ShapeDtypeStruct((s, ATT_W), F32),
                   jax.ShapeDtypeStruct((s // tm, 1, ATT_W), F32)],
        compiler_params=_cparams(("parallel",)),
        name="moba_rope",
    )(z, z, cos, sin)


def _moba_select_kernel(q_ref, k_ref, v_ref, km_ref, qo_ref, ko_ref, vo_ref):
    own = pl.program_id(0)
    q = q_ref[...]
    gate_t = lax.dot_general(km_ref[...], q, (((1,), (1,)), ((), ())), preferred_element_type=F32,
                             precision=lax.Precision.HIGHEST)
    t = q.shape[0]
    blk = lax.broadcasted_iota(jnp.int32, (HEAD_DIM, t), 0)
    biases = []
    for h in range(N_ATT_HEADS):
        g = jnp.where(blk < own, gate_t[h * HEAD_DIM:(h + 1) * HEAD_DIM, :], -jnp.inf)
        bias = jnp.where(blk == own, 0.0, NEG_BIG)
        for r in range(MOBA_TOPK):
            mx = jnp.max(g, axis=0, keepdims=True)
            idx = jnp.min(jnp.where(g == mx, blk, HEAD_DIM), axis=0, keepdims=True)
            hit = blk == idx
            bias = jnp.where(hit, jnp.where(r < own, 0.0, bias), bias)
            g = jnp.where(hit, -jnp.inf, g)
        biases.append(bias)
    bias_all = jnp.concatenate(biases, axis=0).T
    lane = lax.broadcasted_iota(jnp.int32, (t, HEAD_DIM), 1)
    scale = HEAD_DIM ** -0.5 * LOG2E
    onehot_own = jnp.where(lane == own, 1.0, 0.0).astype(BF16)
    for h in range(N_ATT_HEADS):
        sl = slice(h * HEAD_DIM, (h + 1) * HEAD_DIM)
        qo_ref[h] = jnp.concatenate([(q[:, sl] * scale).astype(BF16), bias_all[:, sl].astype(BF16)], axis=1)
        ko_ref[h] = jnp.concatenate([k_ref[:, sl].astype(BF16), onehot_own], axis=1)
    _store_vt(v_ref, vo_ref)


def _moba_select(q_rope, k_rope, z, km_mat):
    s = z.shape[0]
    tm = MOBA_BLOCK
    cb = COL_BQ // ATT_W
    blk = pl.BlockSpec((tm, ATT_W), lambda i: (i, 0))
    head_spec = pl.BlockSpec((N_ATT_HEADS, tm, LANES), lambda i: (0, i, 0))
    return pl.pallas_call(
        _moba_select_kernel,
        grid=(s // tm,),
        in_specs=[blk, blk, pl.BlockSpec((tm, ATT_W), lambda i: (i, cb + 2)),
                  pl.BlockSpec((ATT_W, ATT_W), lambda i: (0, 0))],
        out_specs=[head_spec, head_spec, pl.BlockSpec((N_ATT_HEADS, VT_ROWS, tm), lambda i: (0, 0, i))],
        out_shape=[jax.ShapeDtypeStruct((N_ATT_HEADS, s, LANES), BF16),
                   jax.ShapeDtypeStruct((N_ATT_HEADS, s, LANES), BF16),
                   jax.ShapeDtypeStruct((N_ATT_HEADS, VT_ROWS, s), BF16)],
        compiler_params=_cparams(("parallel",)),
        name="moba_select",
    )(q_rope, k_rope, z, km_mat)


def _flash_kernel(qi_tab, ki_tab, *refs, tile, ngroups):
    groups = [refs[3 * g:3 * g + 3] for g in range(ngroups)]
    g_ref, o_ref, m_ref, acc_ref = refs[3 * ngroups:]
    step = pl.program_id(0)
    qi = qi_tab[step]
    ki = ki_tab[step]

    @pl.when(ki == 0)
    def _():
        m_ref[...] = jnp.full_like(m_ref, -jnp.inf)
        acc_ref[...] = jnp.zeros_like(acc_ref)

    units = [(g, h) for g in range(ngroups) for h in range(N_ATT_HEADS)]

    def update(masked):
        if masked:
            key = lax.broadcasted_iota(jnp.int32, (tile, tile), 0)
            qry = lax.broadcasted_iota(jnp.int32, (tile, tile), 1)
            causal = key <= qry
        scores = lambda g, h: _dot_nt(groups[g][1][h], groups[g][0][h])
        st_next = scores(*units[0])
        for u, (g, h) in enumerate(units):
            st = st_next
            if u + 1 < len(units):
                st_next = scores(*units[u + 1])
            if masked:
                st = jnp.where(causal, st, NEG_BIG)
            m_prev = m_ref[u]
            m_new = jnp.maximum(m_prev, jnp.max(st, axis=0, keepdims=True))
            alpha = jnp.exp2(m_prev - m_new)
            p = jnp.exp2(st - m_new).astype(BF16)
            m_ref[u] = m_new
            acc_ref[u] = acc_ref[u] * alpha + _dot(groups[g][2][h], p)

    @pl.when(ki < qi)
    def _():
        update(False)

    @pl.when(ki == qi)
    def _():
        update(True)
        outs = []
        for u in range(len(units)):
            acc = acc_ref[u]
            o = acc[0:HEAD_DIM, :] / acc[HEAD_DIM:HEAD_DIM + 1, :]
            outs.append(o * lax.rsqrt(jnp.mean(o * o, axis=0, keepdims=True) + EPS))
        o_ref[...] = (jnp.concatenate(outs, axis=0).T * g_ref[...]).astype(o_ref.dtype)


def _flash_attention(groups, gain_row, tile=1024):
    ngroups = len(groups)
    width = ATT_W * ngroups
    s = groups[0][2].shape[2]
    n = s // tile
    pairs = [(qi, ki) for qi in range(n) for ki in range(qi + 1)]
    qi_tab = jnp.asarray([p[0] for p in pairs], jnp.int32)
    ki_tab = jnp.asarray([p[1] for p in pairs], jnp.int32)
    group_specs = [pl.BlockSpec((N_ATT_HEADS, tile, LANES), lambda i, qt, kt: (0, qt[i], 0)),
                   pl.BlockSpec((N_ATT_HEADS, tile, LANES), lambda i, qt, kt: (0, kt[i], 0)),
                   pl.BlockSpec((N_ATT_HEADS, VT_ROWS, tile), lambda i, qt, kt: (0, 0, kt[i]))]
    grid_spec = pltpu.PrefetchScalarGridSpec(
        num_scalar_prefetch=2,
        grid=(len(pairs),),
        in_specs=group_specs * ngroups + [pl.BlockSpec((1, width), lambda i, qt, kt: (0, 0))],
        out_specs=pl.BlockSpec((tile, width), lambda i, qt, kt: (qt[i], 0)),
        scratch_shapes=[pltpu.VMEM((N_ATT_HEADS * ngroups, 1, tile), F32),
                        pltpu.VMEM((N_ATT_HEADS * ngroups, VT_ROWS, tile), F32)],
    )
    operands = [a for grp in groups for a in grp]
    return pl.pallas_call(
        functools.partial(_flash_kernel, tile=tile, ngroups=ngroups),
        grid_spec=grid_spec,
        out_shape=jax.ShapeDtypeStruct((s, width), BF16),
        compiler_params=_cparams(("arbitrary",)),
        name="flash_attention",
    )(qi_tab, ki_tab, *operands, gain_row)


def _shift_rows(u, tail, s):
    rolled = pltpu.roll(u, s, axis=0)
    rolled_tail = pltpu.roll(tail, s, axis=0)
    row8 = lax.broadcasted_iota(jnp.int32, tail.shape, 0)
    top = jnp.where(row8 < s, rolled_tail, rolled[0:8])
    return jnp.concatenate([top, rolled[8:]], axis=0)


def _mlstm_kernel(qk_ref, v_ref, o_ref, g_ref, cw_ref, gain_ref, y_ref, tail_ref, c_ref, n_ref, m_ref):
    @pl.when(pl.program_id(0) == 0)
    def _():
        tail_ref[...] = jnp.zeros_like(tail_ref)
        c_ref[...] = jnp.zeros_like(c_ref)
        n_ref[...] = jnp.zeros_like(n_ref)
        m_ref[...] = jnp.zeros_like(m_ref)

    L = ML_CHUNK
    rows = qk_ref.shape[0]
    nch = rows // L
    u = qk_ref[...]
    tail = tail_ref[...]
    cw = cw_ref[...]
    conv = u * cw[CONV_WIDTH - 1:CONV_WIDTH]
    for s in range(1, CONV_WIDTH):
        conv = conv + _shift_rows(u, tail, s) * cw[CONV_WIDTH - 1 - s:CONV_WIDTH - s]
    tail_ref[...] = u[rows - 8:rows]
    qk = conv * jax.nn.sigmoid(conv)

    r = lax.broadcasted_iota(jnp.int32, (L, L), 0)
    c = lax.broadcasted_iota(jnp.int32, (L, L), 1)
    tril = c <= r
    kscale = ML_DIM ** -0.5
    heads = range(ML_HEADS)
    st = []
    for ci in range(nch):
        rs = slice(ci * L, (ci + 1) * L)
        g = g_ref[rs, :]
        gt = g.T
        per_head = []
        for h in heads:
            sl = slice(h * ML_DIM, (h + 1) * ML_DIM)
            qh = qk[rs, sl]
            kh = qk[rs, ML_W + h * ML_DIM:ML_W + (h + 1) * ML_DIM] * kscale
            qb, kb, vb = qh.astype(BF16), kh.astype(BF16), v_ref[rs, sl].astype(BF16)
            b_col = g[:, 8 + h:9 + h]
            i_col = g[:, 4 + h:5 + h]
            b_row = gt[8 + h:9 + h, :]
            i_row = gt[4 + h:5 + h, :]
            dmat = jnp.where(tril, b_col - b_row + i_row, -jnp.inf)
            dmax = jnp.max(dmat, axis=1, keepdims=True)
            a1 = _dot_nt(qb, kb) * jnp.exp(dmat - dmax)
            b_last = b_row[:, L - 1:L]
            g_col = b_last - b_col + i_col
            gmax = jnp.max(g_col, axis=0, keepdims=True)
            kw1 = kh * jnp.exp(g_col - gmax)
            per_head.append(dict(qh=qh, qb=qb, b_col=b_col, dmax=dmax, b_last=b_last, gmax=gmax,
                                 av=_dot(a1.astype(BF16), vb), asum=jnp.sum(a1, axis=1, keepdims=True),
                                 u1=_dot(kw1.T.astype(BF16), vb), ksum=jnp.sum(kw1, axis=0, keepdims=True)))
        st.append(per_head)
    state = [(c_ref[h], n_ref[h], m_ref[h][:, 0:1]) for h in heads]
    for ci in range(nch):
        rs = slice(ci * L, (ci + 1) * L)
        outs = []
        for h in heads:
            s_ = st[ci][h]
            sl = slice(h * ML_DIM, (h + 1) * ML_DIM)
            cmat, nrow, m_prev = state[h]
            inter = s_["b_col"] + m_prev
            m_t = jnp.maximum(inter, s_["dmax"])
            s_intra = jnp.exp(s_["dmax"] - m_t)
            w_inter = jnp.exp(inter - m_t)
            num = s_intra * s_["av"] + w_inter * _dot(s_["qb"], cmat.astype(BF16))
            den = s_intra * s_["asum"] + w_inter * jnp.sum(s_["qh"] * nrow, axis=1, keepdims=True)
            hh = num / jnp.maximum(jnp.abs(den), jnp.exp(-m_t))
            y = jax.nn.sigmoid(o_ref[rs, sl]) * hh
            outs.append(y * lax.rsqrt(jnp.mean(y * y, axis=1, keepdims=True) + EPS))
        y_ref[rs, :] = (jnp.concatenate(outs, axis=1) * gain_ref[...]).astype(y_ref.dtype)
        for h in heads:
            s_ = st[ci][h]
            cmat, nrow, m_prev = state[h]
            m_new = jnp.maximum(s_["b_last"] + m_prev, s_["gmax"])
            decay = jnp.exp(s_["b_last"] + m_prev - m_new)
            scale = jnp.exp(s_["gmax"] - m_new)
            state[h] = (decay * cmat + scale * s_["u1"], decay * nrow + scale * s_["ksum"], m_new)
    for h in heads:
        c_ref[h], n_ref[h] = state[h][0], state[h][1]
        m_ref[h] = jnp.broadcast_to(state[h][2], (1, LANES))


ML_CHUNKS_PER_STEP = 4


def _mlstm(z, gates, conv_w, gain_row):
    s = z.shape[0]
    L = ML_CHUNK * ML_CHUNKS_PER_STEP
    return pl.pallas_call(
        _mlstm_kernel,
        grid=(s // L,),
        in_specs=[pl.BlockSpec((L, 2 * ML_W), lambda i: (i, COL_MQK // (2 * ML_W))),
                  pl.BlockSpec((L, ML_W), lambda i: (i, COL_MV // ML_W)),
                  pl.BlockSpec((L, ML_W), lambda i: (i, COL_MO // ML_W)),
                  pl.BlockSpec((L, LANES), lambda i: (i, 0)),
                  pl.BlockSpec((CONV_WIDTH, 2 * ML_W), lambda i: (0, 0)),
                  pl.BlockSpec((1, ML_W), lambda i: (0, 0))],
        out_specs=pl.BlockSpec((L, ML_W), lambda i: (i, 0)),
        out_shape=jax.ShapeDtypeStruct((s, ML_W), BF16),
        scratch_shapes=[pltpu.VMEM((8, 2 * ML_W), F32),
                        pltpu.VMEM((ML_HEADS, ML_DIM, ML_DIM), F32),
                        pltpu.VMEM((ML_HEADS, 1, ML_DIM), F32),
                        pltpu.VMEM((ML_HEADS, 1, LANES), F32)],
        compiler_params=_cparams(("arbitrary",)),
        name="mlstm",
    )(z, z, z, gates, conv_w, gain_row)


def _layer_norm(h, g, b):
    mu = jnp.mean(h, axis=1, keepdims=True)
    d = h - mu
    var = jnp.mean(d * d, axis=1, keepdims=True)
    return d * lax.rsqrt(var + EPS) * g + b


def _row_seg(j, rows):
    return pl.ds(j, rows, stride=SUBLANES)


def _outproj_kernel(yf_ref, ym_ref, yb_ref, w_ref, x_ref, g_ref, b_ref, wr_ref, br_ref, x1t_ref, lg_ref):
    mix = (_dot(yf_ref[...], w_ref[0:ATT_W, :]) + _dot(ym_ref[...], w_ref[ATT_W:ATT_W + ML_W, :])
           + _dot(yb_ref[...], w_ref[ATT_W + ML_W:, :]))
    x1 = _layer_norm(ALPHA * x_ref[...] + mix, g_ref[...], b_ref[...])
    tm = x1.shape[0]
    for j in range(D_MODEL // LANES):
        x1t_ref[_row_seg(j, tm), :] = x1[:, j * LANES:(j + 1) * LANES]
    lg_ref[...] = jnp.dot(x1, wr_ref[...], preferred_element_type=F32,
                          precision=lax.Precision.HIGHEST) + br_ref[...]


def _outproj_ln_router(y_att, ym, w_out, x, ln_g, ln_b, w_router, b_router, tm=512):
    s = x.shape[0]
    const = lambda shape: pl.BlockSpec(shape, lambda i: (0, 0))
    rows = lambda w: pl.BlockSpec((tm, w), lambda i: (i, 0))
    yf, yb = y_att, y_att
    return pl.pallas_call(
        _outproj_kernel,
        grid=(s // tm,),
        in_specs=[rows(ATT_W), rows(ML_W), pl.BlockSpec((tm, ATT_W), lambda i: (i, 1)),
                  const((D_MODEL, D_MODEL)), rows(D_MODEL),
                  const((1, D_MODEL)), const((1, D_MODEL)), const((D_MODEL, LANES)), const((1, LANES))],
        out_specs=[pl.BlockSpec((SUBLANES * tm, LANES), lambda i: (i, 0)), rows(LANES)],
        out_shape=[jax.ShapeDtypeStruct((SUBLANES * s, LANES), F32), jax.ShapeDtypeStruct((s, LANES), F32)],
        compiler_params=_cparams(("parallel",)),
        name="outproj_ln_router",
    )(yf, ym, yb, w_out, x, ln_g, ln_b, w_router, b_router)


def _first_argmax(v, lane):
    mx = jnp.max(v, axis=1, keepdims=True)
    idx = jnp.min(jnp.where(v == mx, lane, LANES), axis=1, keepdims=True)
    return mx, idx


def _route_kernel(lg_ref, info_ref, cnt_ref, carry_ref):
    @pl.when(pl.program_id(0) == 0)
    def _():
        carry_ref[...] = jnp.zeros_like(carry_ref)

    lg = lg_ref[...]
    t = lg.shape[0]
    lane = lax.broadcasted_iota(jnp.int32, lg.shape, 1)
    is_grp = lane < N_GROUPS
    gmax, gsel = _first_argmax(jnp.where(is_grp, lg, -jnp.inf), lane)
    p_grp = 1.0 / jnp.sum(jnp.where(is_grp, jnp.exp(lg - gmax), 0.0), axis=1, keepdims=True)
    lo = N_GROUPS + EXPERTS_PER_GROUP * gsel
    el = jnp.where((lane >= lo) & (lane < lo + EXPERTS_PER_GROUP), lg, -jnp.inf)
    v0, i0 = _first_argmax(el, lane)
    v1, i1 = _first_argmax(jnp.where(lane == i0, -jnp.inf, el), lane)
    ex = jnp.exp(v1 - v0)
    w0 = p_grp / (1.0 + ex)
    w1 = p_grp * ex / (1.0 + ex)
    e0 = i0 - N_GROUPS
    e1 = i1 - N_GROUPS

    cnt = jnp.where((lane == e0) | (lane == e1), 1.0, 0.0)
    r = lax.broadcasted_iota(jnp.int32, (t, t), 0)
    c = lax.broadcasted_iota(jnp.int32, (t, t), 1)
    strict = jnp.where(c < r, 1.0, 0.0).astype(BF16)
    carry = carry_ref[...]
    before = _dot(strict, cnt.astype(BF16)) + carry
    rank0 = jnp.sum(jnp.where(lane == e0, before, 0.0), axis=1, keepdims=True)
    rank1 = jnp.sum(jnp.where(lane == e1, before, 0.0), axis=1, keepdims=True)
    carry = carry + jnp.sum(cnt, axis=0, keepdims=True)
    carry_ref[...] = carry
    cnt_ref[...] = carry
    vals = [e0.astype(F32), e1.astype(F32), w0, w1, rank0, rank1]
    info = jnp.zeros(lg.shape, F32)
    for j, val in enumerate(vals):
        info = jnp.where(lane == j, val, info)
    info_ref[...] = info


def _route(logits, tm=512):
    s = logits.shape[0]
    return pl.pallas_call(
        _route_kernel,
        grid=(s // tm,),
        in_specs=[pl.BlockSpec((tm, LANES), lambda i: (i, 0))],
        out_specs=[pl.BlockSpec((tm, LANES), lambda i: (i, 0)), pl.BlockSpec((1, LANES), lambda i: (0, 0))],
        out_shape=[jax.ShapeDtypeStruct((s, LANES), F32), jax.ShapeDtypeStruct((1, LANES), F32)],
        scratch_shapes=[pltpu.VMEM((1, LANES), F32)],
        compiler_params=_cparams(("arbitrary",)),
        name="route",
    )(logits)


MOE_ROWS = 256


def _moe_kernel(blk_e, nv, src0_ref, srcn_ref, slotp_ref, slotc_ref, x_hbm, wg_ref, wu_ref, wd_ref, out_hbm,
                xbuf0, xbuf1, ybuf0, ybuf1, wgb, wub, wdb, gsem, ssem, *, nblk):
    b = pl.program_id(0)
    xbufs = (xbuf0, xbuf1)
    ybufs = (ybuf0, ybuf1)
    nv_prev2, nv_prev, nv_cur, nv_next = nv[b], nv[b + 1], nv[b + 2], nv[b + 3]

    def row_in(tok8, slot, r):
        return pltpu.make_async_copy(x_hbm.at[pl.ds(pl.multiple_of(tok8, SUBLANES), SUBLANES), :],
                                     xbufs[slot].at[pl.ds(SUBLANES * r, SUBLANES), :], gsem.at[slot])

    def row_out(dst8, slot, r):
        return pltpu.make_async_copy(ybufs[slot].at[pl.ds(SUBLANES * r, SUBLANES), :],
                                     out_hbm.at[pl.ds(pl.multiple_of(dst8, SUBLANES), SUBLANES), :], ssem.at[slot])

    def start_rows(make, idx_ref, slot, n):
        for r in range(MOE_ROWS):
            @pl.when(r < n)
            def _():
                make(idx_ref[0, 0, r], slot, r).start(priority=r % 2)

    def wait_rows(make, slot, n):
        @pl.when(n > 0)
        def _():
            rows = pl.multiple_of(n * SUBLANES, SUBLANES)
            if make is row_in:
                pltpu.make_async_copy(x_hbm.at[pl.ds(0, rows), :], xbufs[slot].at[pl.ds(0, rows), :],
                                      gsem.at[slot]).wait()
            else:
                pltpu.make_async_copy(ybufs[slot].at[pl.ds(0, rows), :], out_hbm.at[pl.ds(0, rows), :],
                                      ssem.at[slot]).wait()

    @pl.when(b == 0)
    def _():
        for half in range(2):
            xbufs[half][...] = jnp.zeros_like(xbufs[half])
        start_rows(row_in, src0_ref, 0, nv_cur)

    @pl.when((b == 0) | (blk_e[b] != blk_e[jnp.maximum(b - 1, 0)]))
    def _():
        wgb[...] = wg_ref[0, 0].astype(BF16)
        wub[...] = wu_ref[0, 0].astype(BF16)
        wdb[...] = wd_ref[0, 0].astype(BF16)

    def step(cur):
        nxt = 1 - cur
        wait_rows(row_in, cur, nv_cur)
        start_rows(row_out, slotp_ref, nxt, nv_prev)
        start_rows(row_in, srcn_ref, nxt, nv_next)

        nseg = D_MODEL // LANES
        seg = lambda j: _row_seg(j, MOE_ROWS)
        xb = jnp.concatenate([xbufs[cur][seg(j), :] for j in range(nseg)], axis=1).astype(BF16)
        gate = _dot(xb, wgb[...])
        up = _dot(xb, wub[...])
        hid = (gate * jax.nn.sigmoid(gate) * up).astype(BF16)
        y = _dot(hid, wdb[...])

        wait_rows(row_out, cur, nv_prev2)

        for j in range(nseg):
            ybufs[cur][seg(j), :] = y[:, j * LANES:(j + 1) * LANES]

        @pl.when(b == nblk - 1)
        def _():
            start_rows(row_out, slotc_ref, cur, nv_cur)
            wait_rows(row_out, nxt, nv_prev)
            wait_rows(row_out, cur, nv_cur)

    def drain_step(cur):
        nxt = 1 - cur
        start_rows(row_out, slotp_ref, nxt, nv_prev)
        wait_rows(row_out, cur, nv_prev2)

        @pl.when(b == nblk - 1)
        def _():
            wait_rows(row_out, nxt, nv_prev)

    for parity in range(2):
        pl.when((b % 2 == parity) & (nv_cur > 0))(functools.partial(step, parity))
        pl.when((b % 2 == parity) & (nv_cur == 0))(functools.partial(drain_step, parity))


def _moe_ffn(x1, src_ext, slot_ext, blk_e, nv_ext, w_gate, w_up, w_down, layer):
    s = x1.shape[0] // SUBLANES
    nblk = src_ext.shape[0] - 1
    stage = (SUBLANES * MOE_ROWS, LANES)
    idx_blk = (1, 1, MOE_ROWS)
    smem = pltpu.SMEM
    grid_spec = pltpu.PrefetchScalarGridSpec(
        num_scalar_prefetch=2,
        grid=(nblk,),
        in_specs=[pl.BlockSpec(idx_blk, lambda b, be, nv: (0, 0, 0), memory_space=smem),
                  pl.BlockSpec(idx_blk, lambda b, be, nv: (b + 1, 0, 0), memory_space=smem),
                  pl.BlockSpec(idx_blk, lambda b, be, nv: (b, 0, 0), memory_space=smem),
                  pl.BlockSpec(idx_blk, lambda b, be, nv: (b + 1, 0, 0), memory_space=smem),
                  pl.BlockSpec(memory_space=pl.ANY),
                  pl.BlockSpec((1, 1, D_MODEL, D_EXPERT), lambda b, be, nv: (layer, be[b], 0, 0)),
                  pl.BlockSpec((1, 1, D_MODEL, D_EXPERT), lambda b, be, nv: (layer, be[b], 0, 0)),
                  pl.BlockSpec((1, 1, D_EXPERT, D_MODEL), lambda b, be, nv: (layer, be[b], 0, 0))],
        out_specs=pl.BlockSpec(memory_space=pl.ANY),
        scratch_shapes=[pltpu.VMEM(stage, F32),
                        pltpu.VMEM(stage, F32),
                        pltpu.VMEM(stage, F32),
                        pltpu.VMEM(stage, F32),
                        pltpu.VMEM((D_MODEL, D_EXPERT), BF16),
                        pltpu.VMEM((D_MODEL, D_EXPERT), BF16),
                        pltpu.VMEM((D_EXPERT, D_MODEL), BF16),
                        pltpu.SemaphoreType.DMA((2,)),
                        pltpu.SemaphoreType.DMA((2,))],
    )
    return pl.pallas_call(
        functools.partial(_moe_kernel, nblk=nblk),
        grid_spec=grid_spec,
        out_shape=jax.ShapeDtypeStruct((SUBLANES * 2 * s, LANES), F32),
        compiler_params=_cparams(("arbitrary",)),
        name="moe_ffn",
    )(blk_e, nv_ext, src_ext, src_ext, slot_ext, slot_ext, x1, w_gate, w_up, w_down)


def _dest_kernel(info_ref, pstart_ref, d_ref):
    info = info_ref[...]
    tm = info.shape[0]
    lane = lax.broadcasted_iota(jnp.int32, info.shape, 1)
    ps = pstart_ref[...]
    dests = []
    for k in range(2):
        e = info[:, k:k + 1].astype(jnp.int32)
        dests.append(jnp.sum(jnp.where(lane == e, ps, 0.0), axis=1, keepdims=True) + info[:, 4 + k:5 + k])
    packed = jnp.where(lane == 0, dests[0], jnp.where(lane == 1, dests[1], 0.0))
    for j in range(tm // LANES):
        rows = packed[j * LANES:(j + 1) * LANES, :].T
        for k in range(2):
            d_ref[k, j:j + 1, :] = rows[k:k + 1, :].astype(jnp.int32)


def _dest(info, pstart_row, tm=1024):
    s = info.shape[0]
    return pl.pallas_call(
        _dest_kernel,
        grid=(s // tm,),
        in_specs=[pl.BlockSpec((tm, LANES), lambda i: (i, 0)), pl.BlockSpec((1, LANES), lambda i: (0, 0))],
        out_specs=pl.BlockSpec((2, tm // LANES, LANES), lambda i: (0, i, 0)),
        out_shape=jax.ShapeDtypeStruct((2, s // LANES, LANES), jnp.int32),
        compiler_params=_cparams(("parallel",)),
        name="moe_dest",
    )(info, pstart_row)


def _moe_plan(info, counts_row, s):
    counts = counts_row[0, :N_EXPERTS].astype(jnp.int32)
    padded = ((counts + MOE_ROWS - 1) // MOE_ROWS) * MOE_ROWS
    pends = jnp.cumsum(padded)
    pstarts = pends - padded
    nblk = (2 * s) // MOE_ROWS + N_EXPERTS
    p = nblk * MOE_ROWS
    dest = _dest(info, _pad_lanes(pstarts.astype(F32))).reshape(2 * s)
    out_slot = jnp.zeros((p,), jnp.int32).at[dest].set(jnp.arange(2 * s, dtype=jnp.int32))
    src_tok = jnp.where(out_slot >= s, out_slot - s, out_slot)
    nused = (pends[-1] // MOE_ROWS).astype(jnp.int32)
    blk = jnp.arange(nblk, dtype=jnp.int32)
    blk_start = jnp.minimum(blk, nused - 1) * MOE_ROWS
    blk_e = jnp.minimum(jnp.sum((pends[None, :] <= blk_start[:, None]).astype(jnp.int32), axis=1), N_EXPERTS - 1)
    is_e = jnp.arange(N_EXPERTS)[None, :] == blk_e[:, None]
    seg_start = jnp.sum(jnp.where(is_e, pstarts[None, :], 0), axis=1)
    seg_count = jnp.sum(jnp.where(is_e, counts[None, :], 0), axis=1)
    nvalid = jnp.where(blk < nused, jnp.clip(seg_count - (blk * MOE_ROWS - seg_start), 0, MOE_ROWS), 0)
    nv_ext = jnp.concatenate([jnp.zeros((2,), jnp.int32), nvalid.astype(jnp.int32), jnp.zeros((2,), jnp.int32)])
    pad_blk = jnp.zeros((MOE_ROWS,), jnp.int32)
    src_ext = jnp.concatenate([src_tok, pad_blk]) * SUBLANES
    slot_ext = jnp.concatenate([pad_blk, out_slot]) * SUBLANES
    return src_ext.reshape(nblk + 1, 1, MOE_ROWS), slot_ext.reshape(nblk + 1, 1, MOE_ROWS), blk_e, nv_ext


def _combine_kernel(x1_ref, ya_ref, yb_ref, info_ref, g_ref, b_ref, o_ref):
    info = info_ref[...]
    tm = info.shape[0]
    wide = lambda ref: jnp.concatenate([ref[_row_seg(j, tm), :] for j in range(D_MODEL // LANES)], axis=1)
    ffn = info[:, 2:3] * wide(ya_ref) + info[:, 3:4] * wide(yb_ref)
    o_ref[...] = _layer_norm(ALPHA * wide(x1_ref) + ffn, g_ref[...], b_ref[...])


def _combine_ln(x1_tiles, y2, info, ln_g, ln_b, tm=512):
    s = x1_tiles.shape[0] // SUBLANES
    const = pl.BlockSpec((1, D_MODEL), lambda i: (0, 0))
    return pl.pallas_call(
        _combine_kernel,
        grid=(s // tm,),
        in_specs=[pl.BlockSpec((SUBLANES * tm, LANES), lambda i: (i, 0)),
                  pl.BlockSpec((SUBLANES * tm, LANES), lambda i: (i, 0)),
                  pl.BlockSpec((SUBLANES * tm, LANES), lambda i: (i + s // tm, 0)),
                  pl.BlockSpec((tm, LANES), lambda i: (i, 0)), const, const],
        out_specs=pl.BlockSpec((tm, D_MODEL), lambda i: (i, 0)),
        out_shape=jax.ShapeDtypeStruct((s, D_MODEL), F32),
        compiler_params=_cparams(("parallel",)),
        name="combine_ln",
    )(x1_tiles, y2, y2, info, ln_g, ln_b)


def _pad_lanes(v, width=LANES):
    return jnp.zeros((1, width), F32).at[0, :v.shape[0]].set(v)


def _layer(layer, x, cos, sin, w_in, b_fox_f, b_mlstm_i, b_mlstm_f, conv_w, g_fox, g_mlstm, g_moba, w_out,
           ln1_g, ln1_b, w_grp, b_grp, w_exp_router, b_exp_router, w_gate, w_up, w_down, ln2_g, ln2_b):
    s = x.shape[0]
    gate_bias = _pad_lanes(jnp.concatenate([b_fox_f, b_mlstm_i, b_mlstm_f]))

    z = _inproj(x, w_in, layer)
    gates = _gate_prep(z, gate_bias)

    fox_ops = _fox_prep(z, gates)

    q_rope, k_rope, kmean = _moba_rope(z, cos, sin)
    km = kmean[:, 0, :].reshape(s // MOBA_BLOCK, N_ATT_HEADS, HEAD_DIM)
    km_mat = jnp.zeros((N_ATT_HEADS, HEAD_DIM, N_ATT_HEADS, HEAD_DIM), F32)
    for h in range(N_ATT_HEADS):
        km_mat = km_mat.at[h, :s // MOBA_BLOCK, h, :].set(km[:, h, :])
    km_mat = km_mat.reshape(ATT_W, ATT_W)
    moba_ops = _moba_select(q_rope, k_rope, z, km_mat)
    y_att = _flash_attention([fox_ops, moba_ops], jnp.concatenate([g_fox, g_moba])[None, :])

    y_mlstm = _mlstm(z, gates, conv_w, g_mlstm[None, :])

    w_router = jnp.zeros((D_MODEL, LANES), F32)
    w_router = w_router.at[:, :N_GROUPS].set(w_grp)
    w_router = w_router.at[:, N_GROUPS:N_GROUPS + N_EXPERTS].set(w_exp_router.reshape(D_MODEL, N_EXPERTS))
    b_router = _pad_lanes(jnp.concatenate([b_grp, b_exp_router.reshape(N_EXPERTS)]))
    x1_tiles, logits = _outproj_ln_router(y_att, y_mlstm, w_out.astype(BF16), x, ln1_g[None, :],
                                          ln1_b[None, :], w_router, b_router)

    info, counts = _route(logits)
    src_ext, slot_ext, blk_e, nv_ext = _moe_plan(info, counts, s)
    y2 = _moe_ffn(x1_tiles, src_ext, slot_ext, blk_e, nv_ext, w_gate, w_up, w_down, layer)
    return _combine_ln(x1_tiles, y2, info, ln2_g[None, :], ln2_b[None, :])


def kernel(x, positions, w_in, b_fox_f, b_mlstm_i, b_mlstm_f, conv_w, g_fox, g_mlstm, g_moba, w_out, ln1_g, ln1_b, w_grp, b_grp, w_exp_router, b_exp_router, w_gate, w_up, w_down, ln2_g, ln2_b):
    assert x.shape[0] == 1
    xs = x[0]
    d = jnp.arange(ATT_W) % HEAD_DIM
    half = ROPE_DIM // 2
    inv = 1.0 / (ROPE_THETA ** (jnp.arange(0, ROPE_DIM, 2, dtype=F32) / ROPE_DIM))
    inv_row = jnp.where(d < ROPE_DIM, inv[d % half], 0.0)[None, :].astype(F32)
    sign_row = jnp.where(d < half, -1.0, 1.0)[None, :].astype(F32)
    cos, sin = _rope_tables(positions[0][:, None], inv_row, sign_row)
    for l in range(DEPTH):
        xs = _layer(l, xs, cos, sin, w_in, b_fox_f[l], b_mlstm_i[l], b_mlstm_f[l], conv_w[l], g_fox[l],
                    g_mlstm[l], g_moba[l], w_out[l], ln1_g[l], ln1_b[l], w_grp[l], b_grp[l],
                    w_exp_router[l], b_exp_router[l], w_gate, w_up, w_down, ln2_g[l], ln2_b[l])
    return xs[None]
```

```python
import functools

import jax
import jax.numpy as jnp
from jax import lax
from jax.experimental import pallas as pl
from jax.experimental.pallas import tpu as pltpu

D_MODEL = 1024
DEPTH = 2
HEAD_DIM = 64
N_ATT_HEADS = 4
ATT_W = N_ATT_HEADS * HEAD_DIM
ML_HEADS = 4
ML_DIM = 128
ML_W = ML_HEADS * ML_DIM
ML_CHUNK = 128
CONV_WIDTH = 4
ROPE_DIM = 16
ROPE_THETA = 500000.0
MOBA_BLOCK = 256
MOBA_TOPK = 3
N_GROUPS = 4
EXPERTS_PER_GROUP = 8
N_EXPERTS = N_GROUPS * EXPERTS_PER_GROUP
D_EXPERT = 512
ALPHA = (2 * DEPTH) ** 0.25
EPS = 1e-5

LANES = 128
SUBLANES = 8
NEG_BIG = -1e30
VMEM_LIMIT = 56 * 1024 * 1024

COL_MQK = 0
COL_MV = 1024
COL_MO = 1536
COL_FQ = 2048
COL_BQ = 2816
COL_GATE = 3584
Z_W = 3712

F32 = jnp.float32
BF16 = jnp.bfloat16


def _cparams(sem):
    return pltpu.CompilerParams(dimension_semantics=sem, vmem_limit_bytes=VMEM_LIMIT)


def _split3(c):
    hi = c.astype(BF16).astype(F32)
    r1 = c - hi
    mid = r1.astype(BF16).astype(F32)
    lo = (r1 - mid).astype(BF16).astype(F32)
    return hi, mid, lo


def _dot(a, b):
    return jnp.dot(a, b, preferred_element_type=F32)


def _dot_nt(a, b):
    return lax.dot_general(a, b, (((1,), (1,)), ((), ())), preferred_element_type=F32)


LOG2E = 1.4426950408889634
VT_ROWS = 80


def _store_vt(v_ref, vo_ref):
    vt = v_ref[...].T
    t = vt.shape[1]
    row = lax.broadcasted_iota(jnp.int32, (VT_ROWS - HEAD_DIM, t), 0)
    tail = jnp.where(row == 0, 1.0, 0.0)
    for h in range(N_ATT_HEADS):
        vo_ref[h] = jnp.concatenate([vt[h * HEAD_DIM:(h + 1) * HEAD_DIM, :], tail], axis=0).astype(BF16)


IN_W = 3596
_W_RUNS = ((COL_MQK, 772, 2308),
           (COL_MO, 2316, 2828),
           (COL_FQ, 0, 768),
           (COL_BQ, 2828, 3596))
_W_GATE_RUNS = ((768, 772), (2308, 2316))


def _cols(w_ref, r0, r1, a, b):
    a0 = (a // LANES) * LANES
    b0 = min(-(-b // LANES) * LANES, IN_W)
    return w_ref[r0:r1, a0:b0][:, a - a0:b - a0]


def _inproj_kernel(x_ref, w_ref, o_ref, wb_ref):
    @pl.when(pl.program_id(0) == 0)
    def _():
        rows = 256
        for r0 in range(0, D_MODEL, rows):
            r1 = r0 + rows
            for dst, a, b in _W_RUNS:
                wb_ref[r0:r1, dst:dst + (b - a)] = _cols(w_ref, r0, r1, a, b).astype(BF16)
            gate = [_cols(w_ref, r0, r1, a, b) for a, b in _W_GATE_RUNS]
            used = sum(b - a for a, b in _W_GATE_RUNS)
            gate.append(jnp.zeros((rows, LANES - used), F32))
            wb_ref[r0:r1, COL_GATE:] = jnp.concatenate(gate, axis=1).astype(BF16)

    xb = x_ref[...].astype(BF16)
    n = o_ref.shape[1]
    step = 512
    for j in range(0, n, step):
        w = min(step, n - j)
        o_ref[:, j:j + w] = _dot(xb, wb_ref[:, j:j + w])


def _inproj(x, w_in, layer, tm=256):
    s = x.shape[0]
    return pl.pallas_call(
        _inproj_kernel,
        grid=(s // tm,),
        in_specs=[pl.BlockSpec((tm, D_MODEL), lambda i: (i, 0)),
                  pl.BlockSpec((D_MODEL, IN_W), lambda i: (layer, 0), pipeline_mode=pl.Buffered(1))],
        out_specs=pl.BlockSpec((tm, Z_W), lambda i: (i, 0)),
        out_shape=jax.ShapeDtypeStruct((s, Z_W), F32),
        scratch_shapes=[pltpu.VMEM((D_MODEL, Z_W), BF16)],
        compiler_params=_cparams(("arbitrary",)),
        name="inproj",
    )(x, w_in.reshape(-1, IN_W))


def _log_sigmoid(x):
    return jnp.minimum(x, 0.0) - jnp.log(1.0 + jnp.exp(-jnp.abs(x)))


def _gate_kernel(zg_ref, bias_ref, o_ref, carry_ref):
    @pl.when(pl.program_id(0) == 0)
    def _():
        carry_ref[...] = jnp.zeros_like(carry_ref)

    g = zg_ref[...] + bias_ref[...]
    ls = _log_sigmoid(g)
    t = g.shape[0]
    r = lax.broadcasted_iota(jnp.int32, (t, t), 0)
    c = lax.broadcasted_iota(jnp.int32, (t, t), 1)
    tri = c <= r
    tri_all = jnp.where(tri, 1.0, 0.0).astype(BF16)
    tri_chunk = jnp.where(tri & ((c // ML_CHUNK) == (r // ML_CHUNK)), 1.0, 0.0).astype(BF16)
    hi, mid, lo = _split3(ls)
    parts = [p.astype(BF16) for p in (hi, mid, lo)]
    cum_all = sum(_dot(tri_all, p) for p in parts)
    cum_chunk = sum(_dot(tri_chunk, p) for p in parts)
    carry = carry_ref[...]
    lane = lax.broadcasted_iota(jnp.int32, g.shape, 1)
    o_ref[...] = jnp.where(lane < 4, cum_all + carry, jnp.where(lane < 8, g, cum_chunk))
    carry_ref[...] = carry + cum_all[t - 1:t, :]


def _gate_prep(z, bias_row, tm=512):
    s = z.shape[0]
    return pl.pallas_call(
        _gate_kernel,
        grid=(s // tm,),
        in_specs=[pl.BlockSpec((tm, LANES), lambda i: (i, COL_GATE // LANES)),
                  pl.BlockSpec((1, LANES), lambda i: (0, 0))],
        out_specs=pl.BlockSpec((tm, LANES), lambda i: (i, 0)),
        out_shape=jax.ShapeDtypeStruct((s, LANES), F32),
        scratch_shapes=[pltpu.VMEM((1, LANES), F32)],
        compiler_params=_cparams(("arbitrary",)),
        name="gate_prep",
    )(z, bias_row)


def _fox_prep_kernel(q_ref, k_ref, v_ref, g_ref, qo_ref, ko_ref, vo_ref):
    g = g_ref[...]
    t = g.shape[0]
    lane = lax.broadcasted_iota(jnp.int32, (t, HEAD_DIM), 1)
    scale = HEAD_DIM ** -0.5 * LOG2E
    for h in range(N_ATT_HEADS):
        hi, mid, lo = _split3(g[:, h:h + 1] * LOG2E)
        aug_q = jnp.where(lane == 0, hi, jnp.where(lane == 1, mid, jnp.where(lane == 2, lo,
                          jnp.where(lane < 6, 1.0, 0.0))))
        aug_k = jnp.where(lane < 3, 1.0, jnp.where(lane == 3, -hi, jnp.where(lane == 4, -mid,
                          jnp.where(lane == 5, -lo, 0.0))))
        sl = slice(h * HEAD_DIM, (h + 1) * HEAD_DIM)
        qo_ref[h] = jnp.concatenate([(q_ref[:, sl] * scale).astype(BF16), aug_q.astype(BF16)], axis=1)
        ko_ref[h] = jnp.concatenate([k_ref[:, sl].astype(BF16), aug_k.astype(BF16)], axis=1)
    _store_vt(v_ref, vo_ref)


def _fox_prep(z, gates, tm=512):
    s = z.shape[0]
    cb = COL_FQ // ATT_W
    head_spec = pl.BlockSpec((N_ATT_HEADS, tm, LANES), lambda i: (0, i, 0))
    return pl.pallas_call(
        _fox_prep_kernel,
        grid=(s // tm,),
        in_specs=[pl.BlockSpec((tm, ATT_W), lambda i: (i, cb)),
                  pl.BlockSpec((tm, ATT_W), lambda i: (i, cb + 1)),
                  pl.BlockSpec((tm, ATT_W), lambda i: (i, cb + 2)),
                  pl.BlockSpec((tm, LANES), lambda i: (i, 0))],
        out_specs=[head_spec, head_spec, pl.BlockSpec((N_ATT_HEADS, VT_ROWS, tm), lambda i: (0, 0, i))],
        out_shape=[jax.ShapeDtypeStruct((N_ATT_HEADS, s, LANES), BF16),
                   jax.ShapeDtypeStruct((N_ATT_HEADS, s, LANES), BF16),
                   jax.ShapeDtypeStruct((N_ATT_HEADS, VT_ROWS, s), BF16)],
        compiler_params=_cparams(("parallel",)),
        name="fox_prep",
    )(z, z, z, gates)


def _rope_table_kernel(pos_ref, inv_ref, sign_ref, cos_ref, sin_ref):
    ang = pos_ref[...].astype(F32) * inv_ref[...]
    cos_ref[...] = jnp.cos(ang)
    sin_ref[...] = jnp.sin(ang) * sign_ref[...]


def _rope_tables(pos_col, inv_row, sign_row, tm=512):
    s = pos_col.shape[0]
    row = pl.BlockSpec((1, ATT_W), lambda i: (0, 0))
    out = pl.BlockSpec((tm, ATT_W), lambda i: (i, 0))
    return pl.pallas_call(
        _rope_table_kernel,
        grid=(s // tm,),
        in_specs=[pl.BlockSpec((tm, 1), lambda i: (i, 0)), row, row],
        out_specs=[out, out],
        out_shape=[jax.ShapeDtypeStruct((s, ATT_W), F32)] * 2,
        compiler_params=_cparams(("parallel",)),
        name="rope_tables",
    )(pos_col, inv_row, sign_row)


def _rope(u, cos, sin_signed):
    half = ROPE_DIM // 2
    lane = lax.broadcasted_iota(jnp.int32, u.shape, 1) % HEAD_DIM
    up = pltpu.roll(u, ATT_W - half, axis=1)
    down = pltpu.roll(u, half, axis=1)
    partner = jnp.where(lane < half, up, down)
    return u * cos + partner * sin_signed


def _moba_rope_kernel(q_ref, k_ref, cos_ref, sin_ref, qo_ref, ko_ref, km_ref):
    cos = cos_ref[...]
    sin = sin_ref[...]
    qo_ref[...] = _rope(q_ref[...], cos, sin)
    kr = _rope(k_ref[...], cos, sin)
    ko_ref[...] = kr
    km_ref[0] = jnp.mean(kr, axis=0, keepdims=True)


def _moba_rope(z, cos, sin):
    s = z.shape[0]
    tm = MOBA_BLOCK
    cb = COL_BQ // ATT_W
    blk = pl.BlockSpec((tm, ATT_W), lambda i: (i, 0))
    return pl.pallas_call(
        _moba_rope_kernel,
        grid=(s // tm,),
        in_specs=[pl.BlockSpec((tm, ATT_W), lambda i: (i, cb)),
                  pl.BlockSpec((tm, ATT_W), lambda i: (i, cb + 1)), blk, blk],
        out_specs=[blk, blk, pl.BlockSpec((1, 1, ATT_W), lambda i: (i, 0, 0))],
        out_shape=[jax.ShapeDtypeStruct((s, ATT_W), F32), jax.ShapeDtypeStruct((s, ATT_W), F32),
                   jax.ShapeDtypeStruct((s // tm, 1, ATT_W), F32)],
        compiler_params=_cparams(("parallel",)),
        name="moba_rope",
    )(z, z, cos, sin)


def _moba_select_kernel(q_ref, k_ref, v_ref, km_ref, qo_ref, ko_ref, vo_ref):
    own = pl.program_id(0)
    q = q_ref[...]
    gate_t = lax.dot_general(km_ref[...], q, (((1,), (1,)), ((), ())), preferred_element_type=F32,
                             precision=lax.Precision.HIGHEST)
    t = q.shape[0]
    blk = lax.broadcasted_iota(jnp.int32, (HEAD_DIM, t), 0)
    biases = []
    for h in range(N_ATT_HEADS):
        g = jnp.where(blk < own, gate_t[h * HEAD_DIM:(h + 1) * HEAD_DIM, :], -jnp.inf)
        bias = jnp.where(blk == own, 0.0, NEG_BIG)
        for r in range(MOBA_TOPK):
            mx = jnp.max(g, axis=0, keepdims=True)
            idx = jnp.min(jnp.where(g == mx, blk, HEAD_DIM), axis=0, keepdims=True)
            hit = blk == idx
            bias = jnp.where(hit, jnp.where(r < own, 0.0, bias), bias)
            g = jnp.where(hit, -jnp.inf, g)
        biases.append(bias)
    bias_all = jnp.concatenate(biases, axis=0).T
    lane = lax.broadcasted_iota(jnp.int32, (t, HEAD_DIM), 1)
    scale = HEAD_DIM ** -0.5 * LOG2E
    onehot_own = jnp.where(lane == own, 1.0, 0.0).astype(BF16)
    for h in range(N_ATT_HEADS):
        sl = slice(h * HEAD_DIM, (h + 1) * HEAD_DIM)
        qo_ref[h] = jnp.concatenate([(q[:, sl] * scale).astype(BF16), bias_all[:, sl].astype(BF16)], axis=1)
        ko_ref[h] = jnp.concatenate([k_ref[:, sl].astype(BF16), onehot_own], axis=1)
    _store_vt(v_ref, vo_ref)


def _moba_select(q_rope, k_rope, z, km_mat):
    s = z.shape[0]
    tm = MOBA_BLOCK
    cb = COL_BQ // ATT_W
    blk = pl.BlockSpec((tm, ATT_W), lambda i: (i, 0))
    head_spec = pl.BlockSpec((N_ATT_HEADS, tm, LANES), lambda i: (0, i, 0))
    return pl.pallas_call(
        _moba_select_kernel,
        grid=(s // tm,),
        in_specs=[blk, blk, pl.BlockSpec((tm, ATT_W), lambda i: (i, cb + 2)),
                  pl.BlockSpec((ATT_W, ATT_W), lambda i: (0, 0))],
        out_specs=[head_spec, head_spec, pl.BlockSpec((N_ATT_HEADS, VT_ROWS, tm), lambda i: (0, 0, i))],
        out_shape=[jax.ShapeDtypeStruct((N_ATT_HEADS, s, LANES), BF16),
                   jax.ShapeDtypeStruct((N_ATT_HEADS, s, LANES), BF16),
                   jax.ShapeDtypeStruct((N_ATT_HEADS, VT_ROWS, s), BF16)],
        compiler_params=_cparams(("parallel",)),
        name="moba_select",
    )(q_rope, k_rope, z, km_mat)


def _flash_kernel(qi_tab, ki_tab, *refs, tile, ngroups):
    groups = [refs[3 * g:3 * g + 3] for g in range(ngroups)]
    g_ref, o_ref, m_ref, acc_ref = refs[3 * ngroups:]
    step = pl.program_id(0)
    qi = qi_tab[step]
    ki = ki_tab[step]

    @pl.when(ki == 0)
    def _():
        m_ref[...] = jnp.full_like(m_ref, -jnp.inf)
        acc_ref[...] = jnp.zeros_like(acc_ref)

    units = [(g, h) for g in range(ngroups) for h in range(N_ATT_HEADS)]

    def update(masked):
        if masked:
            key = lax.broadcasted_iota(jnp.int32, (tile, tile), 0)
            qry = lax.broadcasted_iota(jnp.int32, (tile, tile), 1)
            causal = key <= qry
        scores = lambda g, h: _dot_nt(groups[g][1][h], groups[g][0][h])
        st_next = scores(*units[0])
        for u, (g, h) in enumerate(units):
            st = st_next
            if u + 1 < len(units):
                st_next = scores(*units[u + 1])
            if masked:
                st = jnp.where(causal, st, NEG_BIG)
            m_prev = m_ref[u]
            m_new = jnp.maximum(m_prev, jnp.max(st, axis=0, keepdims=True))
            alpha = jnp.exp2(m_prev - m_new)
            p = jnp.exp2(st - m_new).astype(BF16)
            m_ref[u] = m_new
            acc_ref[u] = acc_ref[u] * alpha + _dot(groups[g][2][h], p)

    @pl.when(ki < qi)
    def _():
        update(False)

    @pl.when(ki == qi)
    def _():
        update(True)
        outs = []
        for u in range(len(units)):
            acc = acc_ref[u]
            o = acc[0:HEAD_DIM, :] / acc[HEAD_DIM:HEAD_DIM + 1, :]
            outs.append(o * lax.rsqrt(jnp.mean(o * o, axis=0, keepdims=True) + EPS))
        o_ref[...] = (jnp.concatenate(outs, axis=0).T * g_ref[...]).astype(o_ref.dtype)


def _flash_attention(groups, gain_row, tile=1024):
    ngroups = len(groups)
    width = ATT_W * ngroups
    s = groups[0][2].shape[2]
    n = s // tile
    pairs = [(qi, ki) for qi in range(n) for ki in range(qi + 1)]
    qi_tab = jnp.asarray([p[0] for p in pairs], jnp.int32)
    ki_tab = jnp.asarray([p[1] for p in pairs], jnp.int32)
    group_specs = [pl.BlockSpec((N_ATT_HEADS, tile, LANES), lambda i, qt, kt: (0, qt[i], 0)),
                   pl.BlockSpec((N_ATT_HEADS, tile, LANES), lambda i, qt, kt: (0, kt[i], 0)),
                   pl.BlockSpec((N_ATT_HEADS, VT_ROWS, tile), lambda i, qt, kt: (0, 0, kt[i]))]
    grid_spec = pltpu.PrefetchScalarGridSpec(
        num_scalar_prefetch=2,
        grid=(len(pairs),),
        in_specs=group_specs * ngroups + [pl.BlockSpec((1, width), lambda i, qt, kt: (0, 0))],
        out_specs=pl.BlockSpec((tile, width), lambda i, qt, kt: (qt[i], 0)),
        scratch_shapes=[pltpu.VMEM((N_ATT_HEADS * ngroups, 1, tile), F32),
                        pltpu.VMEM((N_ATT_HEADS * ngroups, VT_ROWS, tile), F32)],
    )
    operands = [a for grp in groups for a in grp]
    return pl.pallas_call(
        functools.partial(_flash_kernel, tile=tile, ngroups=ngroups),
        grid_spec=grid_spec,
        out_shape=jax.ShapeDtypeStruct((s, width), BF16),
        compiler_params=_cparams(("arbitrary",)),
        name="flash_attention",
    )(qi_tab, ki_tab, *operands, gain_row)


def _shift_rows(u, tail, s):
    rolled = pltpu.roll(u, s, axis=0)
    rolled_tail = pltpu.roll(tail, s, axis=0)
    row8 = lax.broadcasted_iota(jnp.int32, tail.shape, 0)
    top = jnp.where(row8 < s, rolled_tail, rolled[0:8])
    return jnp.concatenate([top, rolled[8:]], axis=0)


def _mlstm_kernel(qk_ref, v_ref, o_ref, g_ref, cw_ref, gain_ref, y_ref, tail_ref, c_ref, n_ref, m_ref):
    @pl.when(pl.program_id(0) == 0)
    def _():
        tail_ref[...] = jnp.zeros_like(tail_ref)
        c_ref[...] = jnp.zeros_like(c_ref)
        n_ref[...] = jnp.zeros_like(n_ref)
        m_ref[...] = jnp.zeros_like(m_ref)

    L = ML_CHUNK
    rows = qk_ref.shape[0]
    nch = rows // L
    u = qk_ref[...]
    tail = tail_ref[...]
    cw = cw_ref[...]
    conv = u * cw[CONV_WIDTH - 1:CONV_WIDTH]
    for s in range(1, CONV_WIDTH):
        conv = conv + _shift_rows(u, tail, s) * cw[CONV_WIDTH - 1 - s:CONV_WIDTH - s]
    tail_ref[...] = u[rows - 8:rows]
    qk = conv * jax.nn.sigmoid(conv)

    r = lax.broadcasted_iota(jnp.int32, (L, L), 0)
    c = lax.broadcasted_iota(jnp.int32, (L, L), 1)
    tril = c <= r
    kscale = ML_DIM ** -0.5
    heads = range(ML_HEADS)
    st = []
    for ci in range(nch):
        rs = slice(ci * L, (ci + 1) * L)
        g = g_ref[rs, :]
        gt = g.T
        per_head = []
        for h in heads:
            sl = slice(h * ML_DIM, (h + 1) * ML_DIM)
            qh = qk[rs, sl]
            kh = qk[rs, ML_W + h * ML_DIM:ML_W + (h + 1) * ML_DIM] * kscale
            qb, kb, vb = qh.astype(BF16), kh.astype(BF16), v_ref[rs, sl].astype(BF16)
            b_col = g[:, 8 + h:9 + h]
            i_col = g[:, 4 + h:5 + h]
            b_row = gt[8 + h:9 + h, :]
            i_row = gt[4 + h:5 + h, :]
            dmat = jnp.where(tril, b_col - b_row + i_row, -jnp.inf)
            dmax = jnp.max(dmat, axis=1, keepdims=True)
            a1 = _dot_nt(qb, kb) * jnp.exp(dmat - dmax)
            b_last = b_row[:, L - 1:L]
            g_col = b_last - b_col + i_col
            gmax = jnp.max(g_col, axis=0, keepdims=True)
            kw1 = kh * jnp.exp(g_col - gmax)
            per_head.append(dict(qh=qh, qb=qb, b_col=b_col, dmax=dmax, b_last=b_last, gmax=gmax,
                                 av=_dot(a1.astype(BF16), vb), asum=jnp.sum(a1, axis=1, keepdims=True),
                                 u1=_dot(kw1.T.astype(BF16), vb), ksum=jnp.sum(kw1, axis=0, keepdims=True)))
        st.append(per_head)
    state = [(c_ref[h], n_ref[h], m_ref[h][:, 0:1]) for h in heads]
    for ci in range(nch):
        rs = slice(ci * L, (ci + 1) * L)
        outs = []
        for h in heads:
            s_ = st[ci][h]
            sl = slice(h * ML_DIM, (h + 1) * ML_DIM)
            cmat, nrow, m_prev = state[h]
            inter = s_["b_col"] + m_prev
            m_t = jnp.maximum(inter, s_["dmax"])
            s_intra = jnp.exp(s_["dmax"] - m_t)
            w_inter = jnp.exp(inter - m_t)
            num = s_intra * s_["av"] + w_inter * _dot(s_["qb"], cmat.astype(BF16))
            den = s_intra * s_["asum"] + w_inter * jnp.sum(s_["qh"] * nrow, axis=1, keepdims=True)
            hh = num / jnp.maximum(jnp.abs(den), jnp.exp(-m_t))
            y = jax.nn.sigmoid(o_ref[rs, sl]) * hh
            outs.append(y * lax.rsqrt(jnp.mean(y * y, axis=1, keepdims=True) + EPS))
        y_ref[rs, :] = (jnp.concatenate(outs, axis=1) * gain_ref[...]).astype(y_ref.dtype)
        for h in heads:
            s_ = st[ci][h]
            cmat, nrow, m_prev = state[h]
            m_new = jnp.maximum(s_["b_last"] + m_prev, s_["gmax"])
            decay = jnp.exp(s_["b_last"] + m_prev - m_new)
            scale = jnp.exp(s_["gmax"] - m_new)
            state[h] = (decay * cmat + scale * s_["u1"], decay * nrow + scale * s_["ksum"], m_new)
    for h in heads:
        c_ref[h], n_ref[h] = state[h][0], state[h][1]
        m_ref[h] = jnp.broadcast_to(state[h][2], (1, LANES))


ML_CHUNKS_PER_STEP = 4


def _mlstm(z, gates, conv_w, gain_row):
    s = z.shape[0]
    L = ML_CHUNK * ML_CHUNKS_PER_STEP
    return pl.pallas_call(
        _mlstm_kernel,
        grid=(s // L,),
        in_specs=[pl.BlockSpec((L, 2 * ML_W), lambda i: (i, COL_MQK // (2 * ML_W))),
                  pl.BlockSpec((L, ML_W), lambda i: (i, COL_MV // ML_W)),
                  pl.BlockSpec((L, ML_W), lambda i: (i, COL_MO // ML_W)),
                  pl.BlockSpec((L, LANES), lambda i: (i, 0)),
                  pl.BlockSpec((CONV_WIDTH, 2 * ML_W), lambda i: (0, 0)),
                  pl.BlockSpec((1, ML_W), lambda i: (0, 0))],
        out_specs=pl.BlockSpec((L, ML_W), lambda i: (i, 0)),
        out_shape=jax.ShapeDtypeStruct((s, ML_W), BF16),
        scratch_shapes=[pltpu.VMEM((8, 2 * ML_W), F32),
                        pltpu.VMEM((ML_HEADS, ML_DIM, ML_DIM), F32),
                        pltpu.VMEM((ML_HEADS, 1, ML_DIM), F32),
                        pltpu.VMEM((ML_HEADS, 1, LANES), F32)],
        compiler_params=_cparams(("arbitrary",)),
        name="mlstm",
    )(z, z, z, gates, conv_w, gain_row)


def _layer_norm(h, g, b):
    mu = jnp.mean(h, axis=1, keepdims=True)
    d = h - mu
    var = jnp.mean(d * d, axis=1, keepdims=True)
    return d * lax.rsqrt(var + EPS) * g + b


def _row_seg(j, rows):
    return pl.ds(j, rows, stride=SUBLANES)


def _outproj_kernel(yf_ref, ym_ref, yb_ref, w_ref, x_ref, g_ref, b_ref, wrh_ref, wrl_ref, br_ref, x1t_ref, lg_ref):
    mix = (_dot(yf_ref[...], w_ref[0:ATT_W, :]) + _dot(ym_ref[...], w_ref[ATT_W:ATT_W + ML_W, :])
           + _dot(yb_ref[...], w_ref[ATT_W + ML_W:, :]))
    x1 = _layer_norm(ALPHA * x_ref[...] + mix, g_ref[...], b_ref[...])
    tm = x1.shape[0]
    for j in range(D_MODEL // LANES):
        x1t_ref[_row_seg(j, tm), :] = x1[:, j * LANES:(j + 1) * LANES]
    hi = x1.astype(BF16)
    lo = (x1 - hi.astype(F32)).astype(BF16)
    wrh = wrh_ref[...]
    lg_ref[...] = _dot(hi, wrh) + _dot(lo, wrh) + _dot(hi, wrl_ref[...]) + br_ref[...]


def _outproj_ln_router(y_att, ym, w_out, x, ln_g, ln_b, w_router, b_router, tm=512):
    s = x.shape[0]
    const = lambda shape: pl.BlockSpec(shape, lambda i: (0, 0))
    rows = lambda w: pl.BlockSpec((tm, w), lambda i: (i, 0))
    yf, yb = y_att, y_att
    w_router_hi = w_router.astype(BF16)
    w_router_lo = (w_router - w_router_hi.astype(F32)).astype(BF16)
    return pl.pallas_call(
        _outproj_kernel,
        grid=(s // tm,),
        in_specs=[rows(ATT_W), rows(ML_W), pl.BlockSpec((tm, ATT_W), lambda i: (i, 1)),
                  const((D_MODEL, D_MODEL)), rows(D_MODEL),
                  const((1, D_MODEL)), const((1, D_MODEL)), const((D_MODEL, LANES)), const((D_MODEL, LANES)),
                  const((1, LANES))],
        out_specs=[pl.BlockSpec((SUBLANES * tm, LANES), lambda i: (i, 0)), rows(LANES)],
        out_shape=[jax.ShapeDtypeStruct((SUBLANES * s, LANES), F32), jax.ShapeDtypeStruct((s, LANES), F32)],
        compiler_params=_cparams(("parallel",)),
        name="outproj_ln_router",
    )(yf, ym, yb, w_out, x, ln_g, ln_b, w_router_hi, w_router_lo, b_router)


def _first_argmax(v, lane):
    mx = jnp.max(v, axis=1, keepdims=True)
    idx = jnp.min(jnp.where(v == mx, lane, LANES), axis=1, keepdims=True)
    return mx, idx


def _route_kernel(lg_ref, info_ref, cnt_ref, carry_ref):
    @pl.when(pl.program_id(0) == 0)
    def _():
        carry_ref[...] = jnp.zeros_like(carry_ref)

    lg = lg_ref[...]
    t = lg.shape[0]
    lane = lax.broadcasted_iota(jnp.int32, lg.shape, 1)
    is_grp = lane < N_GROUPS
    gmax, gsel = _first_argmax(jnp.where(is_grp, lg, -jnp.inf), lane)
    p_grp = 1.0 / jnp.sum(jnp.where(is_grp, jnp.exp(lg - gmax), 0.0), axis=1, keepdims=True)
    lo = N_GROUPS + EXPERTS_PER_GROUP * gsel
    el = jnp.where((lane >= lo) & (lane < lo + EXPERTS_PER_GROUP), lg, -jnp.inf)
    v0, i0 = _first_argmax(el, lane)
    v1, i1 = _first_argmax(jnp.where(lane == i0, -jnp.inf, el), lane)
    ex = jnp.exp(v1 - v0)
    w0 = p_grp / (1.0 + ex)
    w1 = p_grp * ex / (1.0 + ex)
    e0 = i0 - N_GROUPS
    e1 = i1 - N_GROUPS

    cnt = jnp.where((lane == e0) | (lane == e1), 1.0, 0.0)
    r = lax.broadcasted_iota(jnp.int32, (t, t), 0)
    c = lax.broadcasted_iota(jnp.int32, (t, t), 1)
    strict = jnp.where(c < r, 1.0, 0.0).astype(BF16)
    carry = carry_ref[...]
    before = _dot(strict, cnt.astype(BF16)) + carry
    rank0 = jnp.sum(jnp.where(lane == e0, before, 0.0), axis=1, keepdims=True)
    rank1 = jnp.sum(jnp.where(lane == e1, before, 0.0), axis=1, keepdims=True)
    carry = carry + jnp.sum(cnt, axis=0, keepdims=True)
    carry_ref[...] = carry
    cnt_ref[...] = carry
    vals = [e0.astype(F32), e1.astype(F32), w0, w1, rank0, rank1]
    info = jnp.zeros(lg.shape, F32)
    for j, val in enumerate(vals):
        info = jnp.where(lane == j, val, info)
    info_ref[...] = info


def _route(logits, tm=512):
    s = logits.shape[0]
    return pl.pallas_call(
        _route_kernel,
        grid=(s // tm,),
        in_specs=[pl.BlockSpec((tm, LANES), lambda i: (i, 0))],
        out_specs=[pl.BlockSpec((tm, LANES), lambda i: (i, 0)), pl.BlockSpec((1, LANES), lambda i: (0, 0))],
        out_shape=[jax.ShapeDtypeStruct((s, LANES), F32), jax.ShapeDtypeStruct((1, LANES), F32)],
        scratch_shapes=[pltpu.VMEM((1, LANES), F32)],
        compiler_params=_cparams(("arbitrary",)),
        name="route",
    )(logits)


MOE_ROWS = 256


def _moe_kernel(blk_e, nv, src0_ref, srcn_ref, slotp_ref, slotc_ref, x_hbm, wg_ref, wu_ref, wd_ref, out_hbm,
                xbuf0, xbuf1, ybuf0, ybuf1, wgb, wub, wdb, gsem, ssem, *, nblk):
    b = pl.program_id(0)
    xbufs = (xbuf0, xbuf1)
    ybufs = (ybuf0, ybuf1)
    nv_prev2, nv_prev, nv_cur, nv_next = nv[b], nv[b + 1], nv[b + 2], nv[b + 3]

    def row_in(tok8, slot, r):
        return pltpu.make_async_copy(x_hbm.at[pl.ds(pl.multiple_of(tok8, SUBLANES), SUBLANES), :],
                                     xbufs[slot].at[pl.ds(SUBLANES * r, SUBLANES), :], gsem.at[slot])

    def row_out(dst8, slot, r):
        return pltpu.make_async_copy(ybufs[slot].at[pl.ds(SUBLANES * r, SUBLANES), :],
                                     out_hbm.at[pl.ds(pl.multiple_of(dst8, SUBLANES), SUBLANES), :], ssem.at[slot])

    def start_rows(make, idx_ref, slot, n):
        for r in range(MOE_ROWS):
            @pl.when(r < n)
            def _():
                make(idx_ref[0, 0, r], slot, r).start(priority=r % 2)

    def wait_rows(make, slot, n):
        @pl.when(n > 0)
        def _():
            rows = pl.multiple_of(n * SUBLANES, SUBLANES)
            if make is row_in:
                pltpu.make_async_copy(x_hbm.at[pl.ds(0, rows), :], xbufs[slot].at[pl.ds(0, rows), :],
                                      gsem.at[slot]).wait()
            else:
                pltpu.make_async_copy(ybufs[slot].at[pl.ds(0, rows), :], out_hbm.at[pl.ds(0, rows), :],
                                      ssem.at[slot]).wait()

    @pl.when(b == 0)
    def _():
        for half in range(2):
            xbufs[half][...] = jnp.zeros_like(xbufs[half])
        start_rows(row_in, src0_ref, 0, nv_cur)

    @pl.when((b == 0) | (blk_e[b] != blk_e[jnp.maximum(b - 1, 0)]))
    def _():
        wgb[...] = wg_ref[0, 0].astype(BF16)
        wub[...] = wu_ref[0, 0].astype(BF16)
        wdb[...] = wd_ref[0, 0].astype(BF16)

    def step(cur):
        nxt = 1 - cur
        wait_rows(row_in, cur, nv_cur)
        start_rows(row_out, slotp_ref, nxt, nv_prev)
        start_rows(row_in, srcn_ref, nxt, nv_next)

        nseg = D_MODEL // LANES
        seg = lambda j: _row_seg(j, MOE_ROWS)
        xb = jnp.concatenate([xbufs[cur][seg(j), :] for j in range(nseg)], axis=1).astype(BF16)
        gate = _dot(xb, wgb[...])
        up = _dot(xb, wub[...])
        hid = (gate * jax.nn.sigmoid(gate) * up).astype(BF16)
        y = _dot(hid, wdb[...])

        wait_rows(row_out, cur, nv_prev2)

        for j in range(nseg):
            ybufs[cur][seg(j), :] = y[:, j * LANES:(j + 1) * LANES]

        @pl.when(b == nblk - 1)
        def _():
            start_rows(row_out, slotc_ref, cur, nv_cur)
            wait_rows(row_out, nxt, nv_prev)
            wait_rows(row_out, cur, nv_cur)

    def drain_step(cur):
        nxt = 1 - cur
        start_rows(row_out, slotp_ref, nxt, nv_prev)
        wait_rows(row_out, cur, nv_prev2)

        @pl.when(b == nblk - 1)
        def _():
            wait_rows(row_out, nxt, nv_prev)

    for parity in range(2):
        pl.when((b % 2 == parity) & (nv_cur > 0))(functools.partial(step, parity))
        pl.when((b % 2 == parity) & (nv_cur == 0))(functools.partial(drain_step, parity))


def _moe_ffn(x1, src_ext, slot_ext, blk_e, nv_ext, w_gate, w_up, w_down, layer):
    s = x1.shape[0] // SUBLANES
    nblk = src_ext.shape[0] - 1
    stage = (SUBLANES * MOE_ROWS, LANES)
    idx_blk = (1, 1, MOE_ROWS)
    smem = pltpu.SMEM
    grid_spec = pltpu.PrefetchScalarGridSpec(
        num_scalar_prefetch=2,
        grid=(nblk,),
        in_specs=[pl.BlockSpec(idx_blk, lambda b, be, nv: (0, 0, 0), memory_space=smem),
                  pl.BlockSpec(idx_blk, lambda b, be, nv: (b + 1, 0, 0), memory_space=smem),
                  pl.BlockSpec(idx_blk, lambda b, be, nv: (b, 0, 0), memory_space=smem),
                  pl.BlockSpec(idx_blk, lambda b, be, nv: (b + 1, 0, 0), memory_space=smem),
                  pl.BlockSpec(memory_space=pl.ANY),
                  pl.BlockSpec((1, 1, D_MODEL, D_EXPERT), lambda b, be, nv: (layer, be[b], 0, 0)),
                  pl.BlockSpec((1, 1, D_MODEL, D_EXPERT), lambda b, be, nv: (layer, be[b], 0, 0)),
                  pl.BlockSpec((1, 1, D_EXPERT, D_MODEL), lambda b, be, nv: (layer, be[b], 0, 0))],
        out_specs=pl.BlockSpec(memory_space=pl.ANY),
        scratch_shapes=[pltpu.VMEM(stage, F32),
                        pltpu.VMEM(stage, F32),
                        pltpu.VMEM(stage, F32),
                        pltpu.VMEM(stage, F32),
                        pltpu.VMEM((D_MODEL, D_EXPERT), BF16),
                        pltpu.VMEM((D_MODEL, D_EXPERT), BF16),
                        pltpu.VMEM((D_EXPERT, D_MODEL), BF16),
                        pltpu.SemaphoreType.DMA((2,)),
                        pltpu.SemaphoreType.DMA((2,))],
    )
    return pl.pallas_call(
        functools.partial(_moe_kernel, nblk=nblk),
        grid_spec=grid_spec,
        out_shape=jax.ShapeDtypeStruct((SUBLANES * 2 * s, LANES), F32),
        compiler_params=_cparams(("arbitrary",)),
        name="moe_ffn",
    )(blk_e, nv_ext, src_ext, src_ext, slot_ext, slot_ext, x1, w_gate, w_up, w_down)


def _dest_kernel(info_ref, pstart_ref, d_ref):
    info = info_ref[...]
    tm = info.shape[0]
    lane = lax.broadcasted_iota(jnp.int32, info.shape, 1)
    ps = pstart_ref[...]
    dests = []
    for k in range(2):
        e = info[:, k:k + 1].astype(jnp.int32)
        dests.append(jnp.sum(jnp.where(lane == e, ps, 0.0), axis=1, keepdims=True) + info[:, 4 + k:5 + k])
    packed = jnp.where(lane == 0, dests[0], jnp.where(lane == 1, dests[1], 0.0))
    for j in range(tm // LANES):
        rows = packed[j * LANES:(j + 1) * LANES, :].T
        for k in range(2):
            d_ref[k, j:j + 1, :] = rows[k:k + 1, :].astype(jnp.int32)


def _dest(info, pstart_row, tm=1024):
    s = info.shape[0]
    return pl.pallas_call(
        _dest_kernel,
        grid=(s // tm,),
        in_specs=[pl.BlockSpec((tm, LANES), lambda i: (i, 0)), pl.BlockSpec((1, LANES), lambda i: (0, 0))],
        out_specs=pl.BlockSpec((2, tm // LANES, LANES), lambda i: (0, i, 0)),
        out_shape=jax.ShapeDtypeStruct((2, s // LANES, LANES), jnp.int32),
        compiler_params=_cparams(("parallel",)),
        name="moe_dest",
    )(info, pstart_row)


def _moe_plan(info, counts_row, s):
    counts = counts_row[0, :N_EXPERTS].astype(jnp.int32)
    padded = ((counts + MOE_ROWS - 1) // MOE_ROWS) * MOE_ROWS
    pends = jnp.cumsum(padded)
    pstarts = pends - padded
    nblk = (2 * s) // MOE_ROWS + N_EXPERTS
    p = nblk * MOE_ROWS
    dest = _dest(info, _pad_lanes(pstarts.astype(F32))).reshape(2 * s)
    out_slot = jnp.zeros((p,), jnp.int32).at[dest].set(jnp.arange(2 * s, dtype=jnp.int32))
    src_tok = jnp.where(out_slot >= s, out_slot - s, out_slot)
    nused = (pends[-1] // MOE_ROWS).astype(jnp.int32)
    blk = jnp.arange(nblk, dtype=jnp.int32)
    blk_start = jnp.minimum(blk, nused - 1) * MOE_ROWS
    blk_e = jnp.minimum(jnp.sum((pends[None, :] <= blk_start[:, None]).astype(jnp.int32), axis=1), N_EXPERTS - 1)
    is_e = jnp.arange(N_EXPERTS)[None, :] == blk_e[:, None]
    seg_start = jnp.sum(jnp.where(is_e, pstarts[None, :], 0), axis=1)
    seg_count = jnp.sum(jnp.where(is_e, counts[None, :], 0), axis=1)
    nvalid = jnp.where(blk < nused, jnp.clip(seg_count - (blk * MOE_ROWS - seg_start), 0, MOE_ROWS), 0)
    nv_ext = jnp.concatenate([jnp.zeros((2,), jnp.int32), nvalid.astype(jnp.int32), jnp.zeros((2,), jnp.int32)])
    pad_blk = jnp.zeros((MOE_ROWS,), jnp.int32)
    src_ext = jnp.concatenate([src_tok, pad_blk]) * SUBLANES
    slot_ext = jnp.concatenate([pad_blk, out_slot]) * SUBLANES
    return src_ext.reshape(nblk + 1, 1, MOE_ROWS), slot_ext.reshape(nblk + 1, 1, MOE_ROWS), blk_e, nv_ext


def _combine_kernel(x1_ref, ya_ref, yb_ref, info_ref, g_ref, b_ref, o_ref):
    info = info_ref[...]
    tm = info.shape[0]
    wide = lambda ref: jnp.concatenate([ref[_row_seg(j, tm), :] for j in range(D_MODEL // LANES)], axis=1)
    ffn = info[:, 2:3] * wide(ya_ref) + info[:, 3:4] * wide(yb_ref)
    o_ref[...] = _layer_norm(ALPHA * wide(x1_ref) + ffn, g_ref[...], b_ref[...])


def _combine_ln(x1_tiles, y2, info, ln_g, ln_b, tm=512):
    s = x1_tiles.shape[0] // SUBLANES
    const = pl.BlockSpec((1, D_MODEL), lambda i: (0, 0))
    return pl.pallas_call(
        _combine_kernel,
        grid=(s // tm,),
        in_specs=[pl.BlockSpec((SUBLANES * tm, LANES), lambda i: (i, 0)),
                  pl.BlockSpec((SUBLANES * tm, LANES), lambda i: (i, 0)),
                  pl.BlockSpec((SUBLANES * tm, LANES), lambda i: (i + s // tm, 0)),
                  pl.BlockSpec((tm, LANES), lambda i: (i, 0)), const, const],
        out_specs=pl.BlockSpec((tm, D_MODEL), lambda i: (i, 0)),
        out_shape=jax.ShapeDtypeStruct((s, D_MODEL), F32),
        compiler_params=_cparams(("parallel",)),
        name="combine_ln",
    )(x1_tiles, y2, y2, info, ln_g, ln_b)


def _pad_lanes(v, width=LANES):
    return jnp.zeros((1, width), F32).at[0, :v.shape[0]].set(v)


def _layer(layer, x, cos, sin, w_in, b_fox_f, b_mlstm_i, b_mlstm_f, conv_w, g_fox, g_mlstm, g_moba, w_out,
           ln1_g, ln1_b, w_grp, b_grp, w_exp_router, b_exp_router, w_gate, w_up, w_down, ln2_g, ln2_b):
    s = x.shape[0]
    gate_bias = _pad_lanes(jnp.concatenate([b_fox_f, b_mlstm_i, b_mlstm_f]))

    z = _inproj(x, w_in, layer)
    gates = _gate_prep(z, gate_bias)

    fox_ops = _fox_prep(z, gates)

    q_rope, k_rope, kmean = _moba_rope(z, cos, sin)
    km = kmean[:, 0, :].reshape(s // MOBA_BLOCK, N_ATT_HEADS, HEAD_DIM)
    km_mat = jnp.zeros((N_ATT_HEADS, HEAD_DIM, N_ATT_HEADS, HEAD_DIM), F32)
    for h in range(N_ATT_HEADS):
        km_mat = km_mat.at[h, :s // MOBA_BLOCK, h, :].set(km[:, h, :])
    km_mat = km_mat.reshape(ATT_W, ATT_W)
    moba_ops = _moba_select(q_rope, k_rope, z, km_mat)
    y_att = _flash_attention([fox_ops, moba_ops], jnp.concatenate([g_fox, g_moba])[None, :])

    y_mlstm = _mlstm(z, gates, conv_w, g_mlstm[None, :])

    w_router = jnp.zeros((D_MODEL, LANES), F32)
    w_router = w_router.at[:, :N_GROUPS].set(w_grp)
    w_router = w_router.at[:, N_GROUPS:N_GROUPS + N_EXPERTS].set(w_exp_router.reshape(D_MODEL, N_EXPERTS))
    b_router = _pad_lanes(jnp.concatenate([b_grp, b_exp_router.reshape(N_EXPERTS)]))
    x1_tiles, logits = _outproj_ln_router(y_att, y_mlstm, w_out.astype(BF16), x, ln1_g[None, :],
                                          ln1_b[None, :], w_router, b_router)

    info, counts = _route(logits)
    src_ext, slot_ext, blk_e, nv_ext = _moe_plan(info, counts, s)
    y2 = _moe_ffn(x1_tiles, src_ext, slot_ext, blk_e, nv_ext, w_gate, w_up, w_down, layer)
    return _combine_ln(x1_tiles, y2, info, ln2_g[None, :], ln2_b[None, :])


def kernel(x, positions, w_in, b_fox_f, b_mlstm_i, b_mlstm_f, conv_w, g_fox, g_mlstm, g_moba, w_out, ln1_g, ln1_b, w_grp, b_grp, w_exp_router, b_exp_router, w_gate, w_up, w_down, ln2_g, ln2_b):
    assert x.shape[0] == 1
    xs = x[0]
    d = jnp.arange(ATT_W) % HEAD_DIM
    half = ROPE_DIM // 2
    inv = 1.0 / (ROPE_THETA ** (jnp.arange(0, ROPE_DIM, 2, dtype=F32) / ROPE_DIM))
    inv_row = jnp.where(d < ROPE_DIM, inv[d % half], 0.0)[None, :].astype(F32)
    sign_row = jnp.where(d < half, -1.0, 1.0)[None, :].astype(F32)
    cos, sin = _rope_tables(positions[0][:, None], inv_row, sign_row)
    for l in range(DEPTH):
        xs = _layer(l, xs, cos, sin, w_in, b_fox_f[l], b_mlstm_i[l], b_mlstm_f[l], conv_w[l], g_fox[l],
                    g_mlstm[l], g_moba[l], w_out[l], ln1_g[l], ln1_b[l], w_grp[l], b_grp[l],
                    w_exp_router[l], b_exp_router[l], w_gate, w_up, w_down, ln2_g[l], ln2_b[l])
    return xs[None]
```

```python
import functools

import jax
import jax.numpy as jnp
from jax import lax
from jax.experimental import pallas as pl
from jax.experimental.pallas import tpu as pltpu

D_MODEL = 1024
DEPTH = 2
HEAD_DIM = 64
N_ATT_HEADS = 4
ATT_W = N_ATT_HEADS * HEAD_DIM
ML_HEADS = 4
ML_DIM = 128
ML_W = ML_HEADS * ML_DIM
ML_CHUNK = 128
CONV_WIDTH = 4
ROPE_DIM = 16
ROPE_THETA = 500000.0
MOBA_BLOCK = 256
MOBA_TOPK = 3
N_GROUPS = 4
EXPERTS_PER_GROUP = 8
N_EXPERTS = N_GROUPS * EXPERTS_PER_GROUP
D_EXPERT = 512
ALPHA = (2 * DEPTH) ** 0.25
EPS = 1e-5

LANES = 128
SUBLANES = 8
NEG_BIG = -1e30
VMEM_LIMIT = 56 * 1024 * 1024

COL_MQK = 0
COL_MV = 1024
COL_MO = 1536
COL_FQ = 2048
COL_BQ = 2816
COL_GATE = 3584
Z_W = 3712

F32 = jnp.float32
BF16 = jnp.bfloat16


def _cparams(sem):
    return pltpu.CompilerParams(dimension_semantics=sem, vmem_limit_bytes=VMEM_LIMIT)


def _split3(c):
    hi = c.astype(BF16).astype(F32)
    r1 = c - hi
    mid = r1.astype(BF16).astype(F32)
    lo = (r1 - mid).astype(BF16).astype(F32)
    return hi, mid, lo


def _dot(a, b):
    return jnp.dot(a, b, preferred_element_type=F32)


def _dot_nt(a, b):
    return lax.dot_general(a, b, (((1,), (1,)), ((), ())), preferred_element_type=F32)


def _dot_tn(a, b):
    return lax.dot_general(a, b, (((0,), (0,)), ((), ())), preferred_element_type=F32)


LOG2E = 1.4426950408889634
VT_ROWS = 80


def _store_vt(v_ref, vo_ref):
    vt = v_ref[...].T
    t = vt.shape[1]
    row = lax.broadcasted_iota(jnp.int32, (VT_ROWS - HEAD_DIM, t), 0)
    tail = jnp.where(row == 0, 1.0, 0.0)
    for h in range(N_ATT_HEADS):
        vo_ref[h] = jnp.concatenate([vt[h * HEAD_DIM:(h + 1) * HEAD_DIM, :], tail], axis=0).astype(BF16)


IN_W = 3596
_W_RUNS = ((COL_MQK, 772, 2308),
           (COL_MO, 2316, 2828),
           (COL_FQ, 0, 768),
           (COL_BQ, 2828, 3596))
_W_GATE_RUNS = ((768, 772), (2308, 2316))


def _cols(w_ref, r0, r1, a, b):
    a0 = (a // LANES) * LANES
    b0 = min(-(-b // LANES) * LANES, IN_W)
    return w_ref[r0:r1, a0:b0][:, a - a0:b - a0]


def _inproj_kernel(x_ref, w_ref, o_ref, wb_ref):
    @pl.when(pl.program_id(0) == 0)
    def _():
        rows = 256
        for r0 in range(0, D_MODEL, rows):
            r1 = r0 + rows
            for dst, a, b in _W_RUNS:
                wb_ref[r0:r1, dst:dst + (b - a)] = _cols(w_ref, r0, r1, a, b).astype(BF16)
            gate = [_cols(w_ref, r0, r1, a, b) for a, b in _W_GATE_RUNS]
            used = sum(b - a for a, b in _W_GATE_RUNS)
            gate.append(jnp.zeros((rows, LANES - used), F32))
            wb_ref[r0:r1, COL_GATE:] = jnp.concatenate(gate, axis=1).astype(BF16)

    xb = x_ref[...].astype(BF16)
    n = o_ref.shape[1]
    step = 512
    for j in range(0, n, step):
        w = min(step, n - j)
        o_ref[:, j:j + w] = _dot(xb, wb_ref[:, j:j + w])


def _inproj(x, w_in, layer, tm=512):
    s = x.shape[0]
    return pl.pallas_call(
        _inproj_kernel,
        grid=(s // tm,),
        in_specs=[pl.BlockSpec((tm, D_MODEL), lambda i: (i, 0)),
                  pl.BlockSpec((D_MODEL, IN_W), lambda i: (layer, 0), pipeline_mode=pl.Buffered(1))],
        out_specs=pl.BlockSpec((tm, Z_W), lambda i: (i, 0)),
        out_shape=jax.ShapeDtypeStruct((s, Z_W), F32),
        scratch_shapes=[pltpu.VMEM((D_MODEL, Z_W), BF16)],
        compiler_params=_cparams(("arbitrary",)),
        name="inproj",
    )(x, w_in.reshape(-1, IN_W))


def _log_sigmoid(x):
    return jnp.minimum(x, 0.0) - jnp.log(1.0 + jnp.exp(-jnp.abs(x)))


def _gate_kernel(zg_ref, bias_ref, o_ref, carry_ref):
    @pl.when(pl.program_id(0) == 0)
    def _():
        carry_ref[...] = jnp.zeros_like(carry_ref)

    g = zg_ref[...] + bias_ref[...]
    ls = _log_sigmoid(g)
    t = g.shape[0]
    r = lax.broadcasted_iota(jnp.int32, (t, t), 0)
    c = lax.broadcasted_iota(jnp.int32, (t, t), 1)
    tri = c <= r
    tri_all = jnp.where(tri, 1.0, 0.0).astype(BF16)
    tri_chunk = jnp.where(tri & ((c // ML_CHUNK) == (r // ML_CHUNK)), 1.0, 0.0).astype(BF16)
    hi, mid, lo = _split3(ls)
    parts = [p.astype(BF16) for p in (hi, mid, lo)]
    cum_all = sum(_dot(tri_all, p) for p in parts)
    cum_chunk = sum(_dot(tri_chunk, p) for p in parts)
    carry = carry_ref[...]
    lane = lax.broadcasted_iota(jnp.int32, g.shape, 1)
    o_ref[...] = jnp.where(lane < 4, cum_all + carry, jnp.where(lane < 8, g, cum_chunk))
    carry_ref[...] = carry + cum_all[t - 1:t, :]


def _gate_prep(z, bias_row, tm=512):
    s = z.shape[0]
    return pl.pallas_call(
        _gate_kernel,
        grid=(s // tm,),
        in_specs=[pl.BlockSpec((tm, LANES), lambda i: (i, COL_GATE // LANES)),
                  pl.BlockSpec((1, LANES), lambda i: (0, 0))],
        out_specs=pl.BlockSpec((tm, LANES), lambda i: (i, 0)),
        out_shape=jax.ShapeDtypeStruct((s, LANES), F32),
        scratch_shapes=[pltpu.VMEM((1, LANES), F32)],
        compiler_params=_cparams(("arbitrary",)),
        name="gate_prep",
    )(z, bias_row)


def _fox_prep_kernel(q_ref, k_ref, v_ref, g_ref, qo_ref, ko_ref, vo_ref):
    g = g_ref[...]
    t = g.shape[0]
    lane = lax.broadcasted_iota(jnp.int32, (t, HEAD_DIM), 1)
    scale = HEAD_DIM ** -0.5 * LOG2E
    for h in range(N_ATT_HEADS):
        hi, mid, lo = _split3(g[:, h:h + 1] * LOG2E)
        aug_q = jnp.where(lane == 0, hi, jnp.where(lane == 1, mid, jnp.where(lane == 2, lo,
                          jnp.where(lane < 6, 1.0, 0.0))))
        aug_k = jnp.where(lane < 3, 1.0, jnp.where(lane == 3, -hi, jnp.where(lane == 4, -mid,
                          jnp.where(lane == 5, -lo, 0.0))))
        sl = slice(h * HEAD_DIM, (h + 1) * HEAD_DIM)
        qo_ref[h] = jnp.concatenate([(q_ref[:, sl] * scale).astype(BF16), aug_q.astype(BF16)], axis=1)
        ko_ref[h] = jnp.concatenate([k_ref[:, sl].astype(BF16), aug_k.astype(BF16)], axis=1)
    _store_vt(v_ref, vo_ref)


def _fox_prep(z, gates, tm=512):
    s = z.shape[0]
    cb = COL_FQ // ATT_W
    head_spec = pl.BlockSpec((N_ATT_HEADS, tm, LANES), lambda i: (0, i, 0))
    return pl.pallas_call(
        _fox_prep_kernel,
        grid=(s // tm,),
        in_specs=[pl.BlockSpec((tm, ATT_W), lambda i: (i, cb)),
                  pl.BlockSpec((tm, ATT_W), lambda i: (i, cb + 1)),
                  pl.BlockSpec((tm, ATT_W), lambda i: (i, cb + 2)),
                  pl.BlockSpec((tm, LANES), lambda i: (i, 0))],
        out_specs=[head_spec, head_spec, pl.BlockSpec((N_ATT_HEADS, VT_ROWS, tm), lambda i: (0, 0, i))],
        out_shape=[jax.ShapeDtypeStruct((N_ATT_HEADS, s, LANES), BF16),
                   jax.ShapeDtypeStruct((N_ATT_HEADS, s, LANES), BF16),
                   jax.ShapeDtypeStruct((N_ATT_HEADS, VT_ROWS, s), BF16)],
        compiler_params=_cparams(("parallel",)),
        name="fox_prep",
    )(z, z, z, gates)


def _rope_table_kernel(pos_ref, inv_ref, sign_ref, cos_ref, sin_ref):
    ang = pos_ref[...].astype(F32) * inv_ref[...]
    cos_ref[...] = jnp.cos(ang)
    sin_ref[...] = jnp.sin(ang) * sign_ref[...]


def _rope_tables(pos_col, inv_row, sign_row, tm=512):
    s = pos_col.shape[0]
    row = pl.BlockSpec((1, ATT_W), lambda i: (0, 0))
    out = pl.BlockSpec((tm, ATT_W), lambda i: (i, 0))
    return pl.pallas_call(
        _rope_table_kernel,
        grid=(s // tm,),
        in_specs=[pl.BlockSpec((tm, 1), lambda i: (i, 0)), row, row],
        out_specs=[out, out],
        out_shape=[jax.ShapeDtypeStruct((s, ATT_W), F32)] * 2,
        compiler_params=_cparams(("parallel",)),
        name="rope_tables",
    )(pos_col, inv_row, sign_row)


def _rope(u, cos, sin_signed):
    half = ROPE_DIM // 2
    lane = lax.broadcasted_iota(jnp.int32, u.shape, 1) % HEAD_DIM
    up = pltpu.roll(u, ATT_W - half, axis=1)
    down = pltpu.roll(u, half, axis=1)
    partner = jnp.where(lane < half, up, down)
    return u * cos + partner * sin_signed


def _moba_rope_kernel(q_ref, k_ref, cos_ref, sin_ref, qo_ref, ko_ref, km_ref):
    cos = cos_ref[...]
    sin = sin_ref[...]
    qo_ref[...] = _rope(q_ref[...], cos, sin)
    kr = _rope(k_ref[...], cos, sin)
    ko_ref[...] = kr
    km_ref[0] = jnp.mean(kr, axis=0, keepdims=True)


def _moba_rope(z, cos, sin):
    s = z.shape[0]
    tm = MOBA_BLOCK
    cb = COL_BQ // ATT_W
    blk = pl.BlockSpec((tm, ATT_W), lambda i: (i, 0))
    return pl.pallas_call(
        _moba_rope_kernel,
        grid=(s // tm,),
        in_specs=[pl.BlockSpec((tm, ATT_W), lambda i: (i, cb)),
                  pl.BlockSpec((tm, ATT_W), lambda i: (i, cb + 1)), blk, blk],
        out_specs=[blk, blk, pl.BlockSpec((1, 1, ATT_W), lambda i: (i, 0, 0))],
        out_shape=[jax.ShapeDtypeStruct((s, ATT_W), F32), jax.ShapeDtypeStruct((s, ATT_W), F32),
                   jax.ShapeDtypeStruct((s // tm, 1, ATT_W), F32)],
        compiler_params=_cparams(("parallel",)),
        name="moba_rope",
    )(z, z, cos, sin)


def _moba_select_kernel(q_ref, k_ref, v_ref, km_ref, qo_ref, ko_ref, vo_ref):
    own = pl.program_id(0)
    q = q_ref[...]
    gate_t = lax.dot_general(km_ref[...], q, (((1,), (1,)), ((), ())), preferred_element_type=F32,
                             precision=lax.Precision.HIGHEST)
    t = q.shape[0]
    blk = lax.broadcasted_iota(jnp.int32, (HEAD_DIM, t), 0)
    biases = []
    for h in range(N_ATT_HEADS):
        g = jnp.where(blk < own, gate_t[h * HEAD_DIM:(h + 1) * HEAD_DIM, :], -jnp.inf)
        bias = jnp.where(blk == own, 0.0, NEG_BIG)
        for r in range(MOBA_TOPK):
            mx = jnp.max(g, axis=0, keepdims=True)
            idx = jnp.min(jnp.where(g == mx, blk, HEAD_DIM), axis=0, keepdims=True)
            hit = blk == idx
            bias = jnp.where(hit, jnp.where(r < own, 0.0, bias), bias)
            g = jnp.where(hit, -jnp.inf, g)
        biases.append(bias)
    bias_all = jnp.concatenate(biases, axis=0).T
    lane = lax.broadcasted_iota(jnp.int32, (t, HEAD_DIM), 1)
    scale = HEAD_DIM ** -0.5 * LOG2E
    onehot_own = jnp.where(lane == own, 1.0, 0.0).astype(BF16)
    for h in range(N_ATT_HEADS):
        sl = slice(h * HEAD_DIM, (h + 1) * HEAD_DIM)
        qo_ref[h] = jnp.concatenate([(q[:, sl] * scale).astype(BF16), bias_all[:, sl].astype(BF16)], axis=1)
        ko_ref[h] = jnp.concatenate([k_ref[:, sl].astype(BF16), onehot_own], axis=1)
    _store_vt(v_ref, vo_ref)


def _moba_select(q_rope, k_rope, z, km_mat):
    s = z.shape[0]
    tm = MOBA_BLOCK
    cb = COL_BQ // ATT_W
    blk = pl.BlockSpec((tm, ATT_W), lambda i: (i, 0))
    head_spec = pl.BlockSpec((N_ATT_HEADS, tm, LANES), lambda i: (0, i, 0))
    return pl.pallas_call(
        _moba_select_kernel,
        grid=(s // tm,),
        in_specs=[blk, blk, pl.BlockSpec((tm, ATT_W), lambda i: (i, cb + 2)),
                  pl.BlockSpec((ATT_W, ATT_W), lambda i: (0, 0))],
        out_specs=[head_spec, head_spec, pl.BlockSpec((N_ATT_HEADS, VT_ROWS, tm), lambda i: (0, 0, i))],
        out_shape=[jax.ShapeDtypeStruct((N_ATT_HEADS, s, LANES), BF16),
                   jax.ShapeDtypeStruct((N_ATT_HEADS, s, LANES), BF16),
                   jax.ShapeDtypeStruct((N_ATT_HEADS, VT_ROWS, s), BF16)],
        compiler_params=_cparams(("parallel",)),
        name="moba_select",
    )(q_rope, k_rope, z, km_mat)


def _flash_kernel(qi_tab, ki_tab, *refs, tile, ngroups):
    groups = [refs[3 * g:3 * g + 3] for g in range(ngroups)]
    g_ref, o_ref, m_ref, acc_ref = refs[3 * ngroups:]
    step = pl.program_id(0)
    qi = qi_tab[step]
    ki = ki_tab[step]

    @pl.when(ki == 0)
    def _():
        m_ref[...] = jnp.full_like(m_ref, -jnp.inf)
        acc_ref[...] = jnp.zeros_like(acc_ref)

    units = [(g, h) for g in range(ngroups) for h in range(N_ATT_HEADS)]

    def update(masked):
        if masked:
            key = lax.broadcasted_iota(jnp.int32, (tile, tile), 0)
            qry = lax.broadcasted_iota(jnp.int32, (tile, tile), 1)
            causal = key <= qry
        scores = lambda g, h: _dot_nt(groups[g][1][h], groups[g][0][h])
        st_next = scores(*units[0])
        for u, (g, h) in enumerate(units):
            st = st_next
            if u + 1 < len(units):
                st_next = scores(*units[u + 1])
            if masked:
                st = jnp.where(causal, st, NEG_BIG)
            m_prev = m_ref[u]
            m_new = jnp.maximum(m_prev, jnp.max(st, axis=0, keepdims=True))
            alpha = jnp.exp2(m_prev - m_new)
            p = jnp.exp2(st - m_new).astype(BF16)
            m_ref[u] = m_new
            acc_ref[u] = acc_ref[u] * alpha + _dot(groups[g][2][h], p)

    @pl.when(ki < qi)
    def _():
        update(False)

    @pl.when(ki == qi)
    def _():
        update(True)
        outs = []
        for u in range(len(units)):
            acc = acc_ref[u]
            o = acc[0:HEAD_DIM, :] / acc[HEAD_DIM:HEAD_DIM + 1, :]
            outs.append(o * lax.rsqrt(jnp.mean(o * o, axis=0, keepdims=True) + EPS))
        o_ref[...] = (jnp.concatenate(outs, axis=0).T * g_ref[...]).astype(o_ref.dtype)


def _flash_attention(groups, gain_row, tile=1024):
    ngroups = len(groups)
    width = ATT_W * ngroups
    s = groups[0][2].shape[2]
    n = s // tile
    pairs = [(qi, ki) for qi in range(n) for ki in range(qi + 1)]
    qi_tab = jnp.asarray([p[0] for p in pairs], jnp.int32)
    ki_tab = jnp.asarray([p[1] for p in pairs], jnp.int32)
    group_specs = [pl.BlockSpec((N_ATT_HEADS, tile, LANES), lambda i, qt, kt: (0, qt[i], 0)),
                   pl.BlockSpec((N_ATT_HEADS, tile, LANES), lambda i, qt, kt: (0, kt[i], 0)),
                   pl.BlockSpec((N_ATT_HEADS, VT_ROWS, tile), lambda i, qt, kt: (0, 0, kt[i]))]
    grid_spec = pltpu.PrefetchScalarGridSpec(
        num_scalar_prefetch=2,
        grid=(len(pairs),),
        in_specs=group_specs * ngroups + [pl.BlockSpec((1, width), lambda i, qt, kt: (0, 0))],
        out_specs=pl.BlockSpec((tile, width), lambda i, qt, kt: (qt[i], 0)),
        scratch_shapes=[pltpu.VMEM((N_ATT_HEADS * ngroups, 1, tile), F32),
                        pltpu.VMEM((N_ATT_HEADS * ngroups, VT_ROWS, tile), F32)],
    )
    operands = [a for grp in groups for a in grp]
    return pl.pallas_call(
        functools.partial(_flash_kernel, tile=tile, ngroups=ngroups),
        grid_spec=grid_spec,
        out_shape=jax.ShapeDtypeStruct((s, width), BF16),
        compiler_params=_cparams(("arbitrary",)),
        name="flash_attention",
    )(qi_tab, ki_tab, *operands, gain_row)


def _shift_rows(u, tail, s):
    rolled = pltpu.roll(u, s, axis=0)
    rolled_tail = pltpu.roll(tail, s, axis=0)
    row8 = lax.broadcasted_iota(jnp.int32, tail.shape, 0)
    top = jnp.where(row8 < s, rolled_tail, rolled[0:8])
    return jnp.concatenate([top, rolled[8:]], axis=0)


def _mlstm_kernel(qk_ref, v_ref, o_ref, g_ref, cw_ref, gain_ref, y_ref, tail_ref, c_ref, n_ref, m_ref):
    @pl.when(pl.program_id(0) == 0)
    def _():
        tail_ref[...] = jnp.zeros_like(tail_ref)
        c_ref[...] = jnp.zeros_like(c_ref)
        n_ref[...] = jnp.zeros_like(n_ref)
        m_ref[...] = jnp.zeros_like(m_ref)

    L = ML_CHUNK
    rows = qk_ref.shape[0]
    nch = rows // L
    u = qk_ref[...]
    tail = tail_ref[...]
    cw = cw_ref[...]
    conv = u * cw[CONV_WIDTH - 1:CONV_WIDTH]
    for s in range(1, CONV_WIDTH):
        conv = conv + _shift_rows(u, tail, s) * cw[CONV_WIDTH - 1 - s:CONV_WIDTH - s]
    tail_ref[...] = u[rows - 8:rows]
    qk = conv * jax.nn.sigmoid(conv)

    r = lax.broadcasted_iota(jnp.int32, (L, L), 0)
    c = lax.broadcasted_iota(jnp.int32, (L, L), 1)
    tril = c <= r
    kscale = ML_DIM ** -0.5
    heads = range(ML_HEADS)
    st = []
    for ci in range(nch):
        rs = slice(ci * L, (ci + 1) * L)
        g = g_ref[rs, :]
        gt = g.T
        per_head = []
        for h in heads:
            sl = slice(h * ML_DIM, (h + 1) * ML_DIM)
            qh = qk[rs, sl]
            kh = qk[rs, ML_W + h * ML_DIM:ML_W + (h + 1) * ML_DIM] * kscale
            qb, kb, vb = qh.astype(BF16), kh.astype(BF16), v_ref[rs, sl].astype(BF16)
            b_col = g[:, 8 + h:9 + h]
            i_col = g[:, 4 + h:5 + h]
            b_row = gt[8 + h:9 + h, :]
            i_row = gt[4 + h:5 + h, :]
            dmat = jnp.where(tril, b_col - b_row + i_row, -jnp.inf)
            dmax = jnp.max(dmat, axis=1, keepdims=True)
            a1 = _dot_nt(qb, kb) * jnp.exp(dmat - dmax)
            b_last = b_row[:, L - 1:L]
            g_col = b_last - b_col + i_col
            gmax = jnp.max(g_col, axis=0, keepdims=True)
            kw1 = kh * jnp.exp(g_col - gmax)
            per_head.append(dict(qh=qh, qb=qb, b_col=b_col, dmax=dmax, b_last=b_last, gmax=gmax,
                                 av=_dot(a1.astype(BF16), vb), asum=jnp.sum(a1, axis=1, keepdims=True),
                                 u1=_dot_tn(kw1.astype(BF16), vb), ksum=jnp.sum(kw1, axis=0, keepdims=True)))
        st.append(per_head)
    state = [(c_ref[h], n_ref[h], m_ref[h][:, 0:1]) for h in heads]
    for ci in range(nch):
        rs = slice(ci * L, (ci + 1) * L)
        outs = []
        for h in heads:
            s_ = st[ci][h]
            sl = slice(h * ML_DIM, (h + 1) * ML_DIM)
            cmat, nrow, m_prev = state[h]
            inter = s_["b_col"] + m_prev
            m_t = jnp.maximum(inter, s_["dmax"])
            s_intra = jnp.exp(s_["dmax"] - m_t)
            w_inter = jnp.exp(inter - m_t)
            num = s_intra * s_["av"] + w_inter * _dot(s_["qb"], cmat.astype(BF16))
            den = s_intra * s_["asum"] + w_inter * jnp.sum(s_["qh"] * nrow, axis=1, keepdims=True)
            hh = num / jnp.maximum(jnp.abs(den), jnp.exp(-m_t))
            y = jax.nn.sigmoid(o_ref[rs, sl]) * hh
            outs.append(y * lax.rsqrt(jnp.mean(y * y, axis=1, keepdims=True) + EPS))
        y_ref[rs, :] = (jnp.concatenate(outs, axis=1) * gain_ref[...]).astype(y_ref.dtype)
        for h in heads:
            s_ = st[ci][h]
            cmat, nrow, m_prev = state[h]
            m_new = jnp.maximum(s_["b_last"] + m_prev, s_["gmax"])
            decay = jnp.exp(s_["b_last"] + m_prev - m_new)
            scale = jnp.exp(s_["gmax"] - m_new)
            state[h] = (decay * cmat + scale * s_["u1"], decay * nrow + scale * s_["ksum"], m_new)
    for h in heads:
        c_ref[h], n_ref[h] = state[h][0], state[h][1]
        m_ref[h] = jnp.broadcast_to(state[h][2], (1, LANES))


ML_CHUNKS_PER_STEP = 4


def _mlstm(z, gates, conv_w, gain_row):
    s = z.shape[0]
    L = ML_CHUNK * ML_CHUNKS_PER_STEP
    return pl.pallas_call(
        _mlstm_kernel,
        grid=(s // L,),
        in_specs=[pl.BlockSpec((L, 2 * ML_W), lambda i: (i, COL_MQK // (2 * ML_W))),
                  pl.BlockSpec((L, ML_W), lambda i: (i, COL_MV // ML_W)),
                  pl.BlockSpec((L, ML_W), lambda i: (i, COL_MO // ML_W)),
                  pl.BlockSpec((L, LANES), lambda i: (i, 0)),
                  pl.BlockSpec((CONV_WIDTH, 2 * ML_W), lambda i: (0, 0)),
                  pl.BlockSpec((1, ML_W), lambda i: (0, 0))],
        out_specs=pl.BlockSpec((L, ML_W), lambda i: (i, 0)),
        out_shape=jax.ShapeDtypeStruct((s, ML_W), BF16),
        scratch_shapes=[pltpu.VMEM((8, 2 * ML_W), F32),
                        pltpu.VMEM((ML_HEADS, ML_DIM, ML_DIM), F32),
                        pltpu.VMEM((ML_HEADS, 1, ML_DIM), F32),
                        pltpu.VMEM((ML_HEADS, 1, LANES), F32)],
        compiler_params=_cparams(("arbitrary",)),
        name="mlstm",
    )(z, z, z, gates, conv_w, gain_row)


def _layer_norm(h, g, b):
    mu = jnp.mean(h, axis=1, keepdims=True)
    d = h - mu
    var = jnp.mean(d * d, axis=1, keepdims=True)
    return d * lax.rsqrt(var + EPS) * g + b


def _row_seg(j, rows):
    return pl.ds(j, rows, stride=SUBLANES)


def _outproj_kernel(yf_ref, ym_ref, yb_ref, w_ref, x_ref, g_ref, b_ref, wrh_ref, wrl_ref, br_ref, x1t_ref, lg_ref):
    mix = (_dot(yf_ref[...], w_ref[0:ATT_W, :]) + _dot(ym_ref[...], w_ref[ATT_W:ATT_W + ML_W, :])
           + _dot(yb_ref[...], w_ref[ATT_W + ML_W:, :]))
    x1 = _layer_norm(ALPHA * x_ref[...] + mix, g_ref[...], b_ref[...])
    tm = x1.shape[0]
    for j in range(D_MODEL // LANES):
        x1t_ref[_row_seg(j, tm), :] = x1[:, j * LANES:(j + 1) * LANES]
    hi = x1.astype(BF16)
    lo = (x1 - hi.astype(F32)).astype(BF16)
    wrh = wrh_ref[...]
    lg_ref[...] = _dot(hi, wrh) + _dot(lo, wrh) + _dot(hi, wrl_ref[...]) + br_ref[...]


def _outproj_ln_router(y_att, ym, w_out, x, ln_g, ln_b, w_router, b_router, tm=512):
    s = x.shape[0]
    const = lambda shape: pl.BlockSpec(shape, lambda i: (0, 0))
    rows = lambda w: pl.BlockSpec((tm, w), lambda i: (i, 0))
    yf, yb = y_att, y_att
    w_router_hi = w_router.astype(BF16)
    w_router_lo = (w_router - w_router_hi.astype(F32)).astype(BF16)
    return pl.pallas_call(
        _outproj_kernel,
        grid=(s // tm,),
        in_specs=[rows(ATT_W), rows(ML_W), pl.BlockSpec((tm, ATT_W), lambda i: (i, 1)),
                  const((D_MODEL, D_MODEL)), rows(D_MODEL),
                  const((1, D_MODEL)), const((1, D_MODEL)), const((D_MODEL, LANES)), const((D_MODEL, LANES)),
                  const((1, LANES))],
        out_specs=[pl.BlockSpec((SUBLANES * tm, LANES), lambda i: (i, 0)), rows(LANES)],
        out_shape=[jax.ShapeDtypeStruct((SUBLANES * s, LANES), F32), jax.ShapeDtypeStruct((s, LANES), F32)],
        compiler_params=_cparams(("parallel",)),
        name="outproj_ln_router",
    )(yf, ym, yb, w_out, x, ln_g, ln_b, w_router_hi, w_router_lo, b_router)


def _first_argmax(v, lane):
    mx = jnp.max(v, axis=1, keepdims=True)
    idx = jnp.min(jnp.where(v == mx, lane, LANES), axis=1, keepdims=True)
    return mx, idx


def _route_kernel(lg_ref, info_ref, cnt_ref, carry_ref):
    @pl.when(pl.program_id(0) == 0)
    def _():
        carry_ref[...] = jnp.zeros_like(carry_ref)

    lg = lg_ref[...]
    t = lg.shape[0]
    lane = lax.broadcasted_iota(jnp.int32, lg.shape, 1)
    is_grp = lane < N_GROUPS
    gmax, gsel = _first_argmax(jnp.where(is_grp, lg, -jnp.inf), lane)
    p_grp = 1.0 / jnp.sum(jnp.where(is_grp, jnp.exp(lg - gmax), 0.0), axis=1, keepdims=True)
    lo = N_GROUPS + EXPERTS_PER_GROUP * gsel
    el = jnp.where((lane >= lo) & (lane < lo + EXPERTS_PER_GROUP), lg, -jnp.inf)
    v0, i0 = _first_argmax(el, lane)
    v1, i1 = _first_argmax(jnp.where(lane == i0, -jnp.inf, el), lane)
    ex = jnp.exp(v1 - v0)
    w0 = p_grp / (1.0 + ex)
    w1 = p_grp * ex / (1.0 + ex)
    e0 = i0 - N_GROUPS
    e1 = i1 - N_GROUPS

    cnt = jnp.where((lane == e0) | (lane == e1), 1.0, 0.0)
    r = lax.broadcasted_iota(jnp.int32, (t, t), 0)
    c = lax.broadcasted_iota(jnp.int32, (t, t), 1)
    strict = jnp.where(c < r, 1.0, 0.0).astype(BF16)
    carry = carry_ref[...]
    before = _dot(strict, cnt.astype(BF16)) + carry
    rank0 = jnp.sum(jnp.where(lane == e0, before, 0.0), axis=1, keepdims=True)
    rank1 = jnp.sum(jnp.where(lane == e1, before, 0.0), axis=1, keepdims=True)
    carry = carry + jnp.sum(cnt, axis=0, keepdims=True)
    carry_ref[...] = carry
    cnt_ref[...] = carry
    vals = [e0.astype(F32), e1.astype(F32), w0, w1, rank0, rank1]
    info = jnp.zeros(lg.shape, F32)
    for j, val in enumerate(vals):
        info = jnp.where(lane == j, val, info)
    info_ref[...] = info


def _route(logits, tm=512):
    s = logits.shape[0]
    return pl.pallas_call(
        _route_kernel,
        grid=(s // tm,),
        in_specs=[pl.BlockSpec((tm, LANES), lambda i: (i, 0))],
        out_specs=[pl.BlockSpec((tm, LANES), lambda i: (i, 0)), pl.BlockSpec((1, LANES), lambda i: (0, 0))],
        out_shape=[jax.ShapeDtypeStruct((s, LANES), F32), jax.ShapeDtypeStruct((1, LANES), F32)],
        scratch_shapes=[pltpu.VMEM((1, LANES), F32)],
        compiler_params=_cparams(("arbitrary",)),
        name="route",
    )(logits)


MOE_ROWS = 256


def _moe_kernel(blk_e, nv, src0_ref, srcn_ref, slotp_ref, slotc_ref, x_hbm, wg_ref, wu_ref, wd_ref, out_hbm,
                xbuf0, xbuf1, ybuf0, ybuf1, wgb, wub, wdb, gsem, ssem, *, nblk):
    b = pl.program_id(0)
    xbufs = (xbuf0, xbuf1)
    ybufs = (ybuf0, ybuf1)
    nv_prev2, nv_prev, nv_cur, nv_next = nv[b], nv[b + 1], nv[b + 2], nv[b + 3]

    def row_in(tok8, slot, r):
        return pltpu.make_async_copy(x_hbm.at[pl.ds(pl.multiple_of(tok8, SUBLANES), SUBLANES), :],
                                     xbufs[slot].at[pl.ds(SUBLANES * r, SUBLANES), :], gsem.at[slot])

    def row_out(dst8, slot, r):
        return pltpu.make_async_copy(ybufs[slot].at[pl.ds(SUBLANES * r, SUBLANES), :],
                                     out_hbm.at[pl.ds(pl.multiple_of(dst8, SUBLANES), SUBLANES), :], ssem.at[slot])

    def start_rows(make, idx_ref, slot, n):
        for r in range(MOE_ROWS):
            idx = idx_ref[0, 0, r]

            @pl.when(r < n)
            def _():
                make(idx, slot, r).start(priority=r % 2)

    def wait_rows(make, slot, n):
        @pl.when(n > 0)
        def _():
            rows = pl.multiple_of(n * SUBLANES, SUBLANES)
            if make is row_in:
                pltpu.make_async_copy(x_hbm.at[pl.ds(0, rows), :], xbufs[slot].at[pl.ds(0, rows), :],
                                      gsem.at[slot]).wait()
            else:
                pltpu.make_async_copy(ybufs[slot].at[pl.ds(0, rows), :], out_hbm.at[pl.ds(0, rows), :],
                                      ssem.at[slot]).wait()

    @pl.when(b == 0)
    def _():
        for half in range(2):
            xbufs[half][...] = jnp.zeros_like(xbufs[half])
        start_rows(row_in, src0_ref, 0, nv_cur)

    @pl.when((b == 0) | (blk_e[b] != blk_e[jnp.maximum(b - 1, 0)]))
    def _():
        wgb[...] = wg_ref[0, 0].astype(BF16)
        wub[...] = wu_ref[0, 0].astype(BF16)
        wdb[...] = wd_ref[0, 0].astype(BF16)

    def step(cur):
        nxt = 1 - cur
        wait_rows(row_in, cur, nv_cur)
        start_rows(row_out, slotp_ref, nxt, nv_prev)
        start_rows(row_in, srcn_ref, nxt, nv_next)

        nseg = D_MODEL // LANES
        seg = lambda j: _row_seg(j, MOE_ROWS)
        xb = jnp.concatenate([xbufs[cur][seg(j), :] for j in range(nseg)], axis=1).astype(BF16)
        gate = _dot(xb, wgb[...])
        up = _dot(xb, wub[...])
        hid = (gate * jax.nn.sigmoid(gate) * up).astype(BF16)
        y = _dot(hid, wdb[...])

        wait_rows(row_out, cur, nv_prev2)

        for j in range(nseg):
            ybufs[cur][seg(j), :] = y[:, j * LANES:(j + 1) * LANES]

        @pl.when(b == nblk - 1)
        def _():
            start_rows(row_out, slotc_ref, cur, nv_cur)
            wait_rows(row_out, nxt, nv_prev)
            wait_rows(row_out, cur, nv_cur)

    def drain_step(cur):
        nxt = 1 - cur
        start_rows(row_out, slotp_ref, nxt, nv_prev)
        wait_rows(row_out, cur, nv_prev2)

        @pl.when(b == nblk - 1)
        def _():
            wait_rows(row_out, nxt, nv_prev)

    for parity in range(2):
        pl.when((b % 2 == parity) & (nv_cur > 0))(functools.partial(step, parity))
        pl.when((b % 2 == parity) & (nv_cur == 0))(functools.partial(drain_step, parity))


def _moe_ffn(x1, src_ext, slot_ext, blk_e, nv_ext, w_gate, w_up, w_down, layer):
    s = x1.shape[0] // SUBLANES
    nblk = src_ext.shape[0] - 1
    stage = (SUBLANES * MOE_ROWS, LANES)
    idx_blk = (1, 1, MOE_ROWS)
    smem = pltpu.SMEM
    grid_spec = pltpu.PrefetchScalarGridSpec(
        num_scalar_prefetch=2,
        grid=(nblk,),
        in_specs=[pl.BlockSpec(idx_blk, lambda b, be, nv: (0, 0, 0), memory_space=smem),
                  pl.BlockSpec(idx_blk, lambda b, be, nv: (b + 1, 0, 0), memory_space=smem),
                  pl.BlockSpec(idx_blk, lambda b, be, nv: (b, 0, 0), memory_space=smem),
                  pl.BlockSpec(idx_blk, lambda b, be, nv: (b + 1, 0, 0), memory_space=smem),
                  pl.BlockSpec(memory_space=pl.ANY),
                  pl.BlockSpec((1, 1, D_MODEL, D_EXPERT), lambda b, be, nv: (layer, be[b], 0, 0)),
                  pl.BlockSpec((1, 1, D_MODEL, D_EXPERT), lambda b, be, nv: (layer, be[b], 0, 0)),
                  pl.BlockSpec((1, 1, D_EXPERT, D_MODEL), lambda b, be, nv: (layer, be[b], 0, 0))],
        out_specs=pl.BlockSpec(memory_space=pl.ANY),
        scratch_shapes=[pltpu.VMEM(stage, F32),
                        pltpu.VMEM(stage, F32),
                        pltpu.VMEM(stage, F32),
                        pltpu.VMEM(stage, F32),
                        pltpu.VMEM((D_MODEL, D_EXPERT), BF16),
                        pltpu.VMEM((D_MODEL, D_EXPERT), BF16),
                        pltpu.VMEM((D_EXPERT, D_MODEL), BF16),
                        pltpu.SemaphoreType.DMA((2,)),
                        pltpu.SemaphoreType.DMA((2,))],
    )
    return pl.pallas_call(
        functools.partial(_moe_kernel, nblk=nblk),
        grid_spec=grid_spec,
        out_shape=jax.ShapeDtypeStruct((SUBLANES * 2 * s, LANES), F32),
        compiler_params=_cparams(("arbitrary",)),
        name="moe_ffn",
    )(blk_e, nv_ext, src_ext, src_ext, slot_ext, slot_ext, x1, w_gate, w_up, w_down)


def _dest_kernel(info_ref, pstart_ref, d_ref):
    info = info_ref[...]
    tm = info.shape[0]
    lane = lax.broadcasted_iota(jnp.int32, info.shape, 1)
    ps = pstart_ref[...]
    dests = []
    for k in range(2):
        e = info[:, k:k + 1].astype(jnp.int32)
        dests.append(jnp.sum(jnp.where(lane == e, ps, 0.0), axis=1, keepdims=True) + info[:, 4 + k:5 + k])
    packed = jnp.where(lane == 0, dests[0], jnp.where(lane == 1, dests[1], 0.0))
    for j in range(tm // LANES):
        rows = packed[j * LANES:(j + 1) * LANES, :].T
        for k in range(2):
            d_ref[k, j:j + 1, :] = rows[k:k + 1, :].astype(jnp.int32)


def _dest(info, pstart_row, tm=1024):
    s = info.shape[0]
    return pl.pallas_call(
        _dest_kernel,
        grid=(s // tm,),
        in_specs=[pl.BlockSpec((tm, LANES), lambda i: (i, 0)), pl.BlockSpec((1, LANES), lambda i: (0, 0))],
        out_specs=pl.BlockSpec((2, tm // LANES, LANES), lambda i: (0, i, 0)),
        out_shape=jax.ShapeDtypeStruct((2, s // LANES, LANES), jnp.int32),
        compiler_params=_cparams(("parallel",)),
        name="moe_dest",
    )(info, pstart_row)


def _moe_plan(info, counts_row, s):
    counts = counts_row[0, :N_EXPERTS].astype(jnp.int32)
    padded = ((counts + MOE_ROWS - 1) // MOE_ROWS) * MOE_ROWS
    pends = jnp.cumsum(padded)
    pstarts = pends - padded
    nblk = (2 * s) // MOE_ROWS + N_EXPERTS
    p = nblk * MOE_ROWS
    dest = _dest(info, _pad_lanes(pstarts.astype(F32))).reshape(2 * s)
    out_slot = jnp.zeros((p,), jnp.int32).at[dest].set(jnp.arange(2 * s, dtype=jnp.int32))
    src_tok = jnp.where(out_slot >= s, out_slot - s, out_slot)
    nused = (pends[-1] // MOE_ROWS).astype(jnp.int32)
    blk = jnp.arange(nblk, dtype=jnp.int32)
    blk_start = jnp.minimum(blk, nused - 1) * MOE_ROWS
    blk_e = jnp.minimum(jnp.sum((pends[None, :] <= blk_start[:, None]).astype(jnp.int32), axis=1), N_EXPERTS - 1)
    is_e = jnp.arange(N_EXPERTS)[None, :] == blk_e[:, None]
    seg_start = jnp.sum(jnp.where(is_e, pstarts[None, :], 0), axis=1)
    seg_count = jnp.sum(jnp.where(is_e, counts[None, :], 0), axis=1)
    nvalid = jnp.where(blk < nused, jnp.clip(seg_count - (blk * MOE_ROWS - seg_start), 0, MOE_ROWS), 0)
    nv_ext = jnp.concatenate([jnp.zeros((2,), jnp.int32), nvalid.astype(jnp.int32), jnp.zeros((2,), jnp.int32)])
    pad_blk = jnp.zeros((MOE_ROWS,), jnp.int32)
    src_ext = jnp.concatenate([src_tok, pad_blk]) * SUBLANES
    slot_ext = jnp.concatenate([pad_blk, out_slot]) * SUBLANES
    return src_ext.reshape(nblk + 1, 1, MOE_ROWS), slot_ext.reshape(nblk + 1, 1, MOE_ROWS), blk_e, nv_ext


def _combine_kernel(x1_ref, ya_ref, yb_ref, info_ref, g_ref, b_ref, o_ref):
    info = info_ref[...]
    tm = info.shape[0]
    wide = lambda ref: jnp.concatenate([ref[_row_seg(j, tm), :] for j in range(D_MODEL // LANES)], axis=1)
    ffn = info[:, 2:3] * wide(ya_ref) + info[:, 3:4] * wide(yb_ref)
    o_ref[...] = _layer_norm(ALPHA * wide(x1_ref) + ffn, g_ref[...], b_ref[...])


def _combine_ln(x1_tiles, y2, info, ln_g, ln_b, tm=512):
    s = x1_tiles.shape[0] // SUBLANES
    const = pl.BlockSpec((1, D_MODEL), lambda i: (0, 0))
    return pl.pallas_call(
        _combine_kernel,
        grid=(s // tm,),
        in_specs=[pl.BlockSpec((SUBLANES * tm, LANES), lambda i: (i, 0)),
                  pl.BlockSpec((SUBLANES * tm, LANES), lambda i: (i, 0)),
                  pl.BlockSpec((SUBLANES * tm, LANES), lambda i: (i + s // tm, 0)),
                  pl.BlockSpec((tm, LANES), lambda i: (i, 0)), const, const],
        out_specs=pl.BlockSpec((tm, D_MODEL), lambda i: (i, 0)),
        out_shape=jax.ShapeDtypeStruct((s, D_MODEL), F32),
        compiler_params=_cparams(("parallel",)),
        name="combine_ln",
    )(x1_tiles, y2, y2, info, ln_g, ln_b)


def _pad_lanes(v, width=LANES):
    return jnp.zeros((1, width), F32).at[0, :v.shape[0]].set(v)


def _layer(layer, x, cos, sin, w_in, b_fox_f, b_mlstm_i, b_mlstm_f, conv_w, g_fox, g_mlstm, g_moba, w_out,
           ln1_g, ln1_b, w_grp, b_grp, w_exp_router, b_exp_router, w_gate, w_up, w_down, ln2_g, ln2_b):
    s = x.shape[0]
    gate_bias = _pad_lanes(jnp.concatenate([b_fox_f, b_mlstm_i, b_mlstm_f]))

    z = _inproj(x, w_in, layer)
    gates = _gate_prep(z, gate_bias)

    fox_ops = _fox_prep(z, gates)

    q_rope, k_rope, kmean = _moba_rope(z, cos, sin)
    km = kmean[:, 0, :].reshape(s // MOBA_BLOCK, N_ATT_HEADS, HEAD_DIM)
    km_mat = jnp.zeros((N_ATT_HEADS, HEAD_DIM, N_ATT_HEADS, HEAD_DIM), F32)
    for h in range(N_ATT_HEADS):
        km_mat = km_mat.at[h, :s // MOBA_BLOCK, h, :].set(km[:, h, :])
    km_mat = km_mat.reshape(ATT_W, ATT_W)
    moba_ops = _moba_select(q_rope, k_rope, z, km_mat)
    y_att = _flash_attention([fox_ops, moba_ops], jnp.concatenate([g_fox, g_moba])[None, :])

    y_mlstm = _mlstm(z, gates, conv_w, g_mlstm[None, :])

    w_router = jnp.zeros((D_MODEL, LANES), F32)
    w_router = w_router.at[:, :N_GROUPS].set(w_grp)
    w_router = w_router.at[:, N_GROUPS:N_GROUPS + N_EXPERTS].set(w_exp_router.reshape(D_MODEL, N_EXPERTS))
    b_router = _pad_lanes(jnp.concatenate([b_grp, b_exp_router.reshape(N_EXPERTS)]))
    x1_tiles, logits = _outproj_ln_router(y_att, y_mlstm, w_out.astype(BF16), x, ln1_g[None, :],
                                          ln1_b[None, :], w_router, b_router)

    info, counts = _route(logits)
    src_ext, slot_ext, blk_e, nv_ext = _moe_plan(info, counts, s)
    y2 = _moe_ffn(x1_tiles, src_ext, slot_ext, blk_e, nv_ext, w_gate, w_up, w_down, layer)
    return _combine_ln(x1_tiles, y2, info, ln2_g[None, :], ln2_b[None, :])


def kernel(x, positions, w_in, b_fox_f, b_mlstm_i, b_mlstm_f, conv_w, g_fox, g_mlstm, g_moba, w_out, ln1_g, ln1_b, w_grp, b_grp, w_exp_router, b_exp_router, w_gate, w_up, w_down, ln2_g, ln2_b):
    assert x.shape[0] == 1
    xs = x[0]
    d = jnp.arange(ATT_W) % HEAD_DIM
    half = ROPE_DIM // 2
    inv = 1.0 / (ROPE_THETA ** (jnp.arange(0, ROPE_DIM, 2, dtype=F32) / ROPE_DIM))
    inv_row = jnp.where(d < ROPE_DIM, inv[d % half], 0.0)[None, :].astype(F32)
    sign_row = jnp.where(d < half, -1.0, 1.0)[None, :].astype(F32)
    cos, sin = _rope_tables(positions[0][:, None], inv_row, sign_row)
    for l in range(DEPTH):
        xs = _layer(l, xs, cos, sin, w_in, b_fox_f[l], b_mlstm_i[l], b_mlstm_f[l], conv_w[l], g_fox[l],
                    g_mlstm[l], g_moba[l], w_out[l], ln1_g[l], ln1_b[l], w_grp[l], b_grp[l],
                    w_exp_router[l], b_exp_router[l], w_gate, w_up, w_down, ln2_g[l], ln2_b[l])
    return xs[None]
```

```python
import functools

import jax
import jax.numpy as jnp
from jax import lax
from jax.experimental import pallas as pl
from jax.experimental.pallas import tpu as pltpu

D_MODEL = 1024
DEPTH = 2
HEAD_DIM = 64
N_ATT_HEADS = 4
ATT_W = N_ATT_HEADS * HEAD_DIM
ML_HEADS = 4
ML_DIM = 128
ML_W = ML_HEADS * ML_DIM
ML_CHUNK = 128
CONV_WIDTH = 4
ROPE_DIM = 16
ROPE_THETA = 500000.0
MOBA_BLOCK = 256
MOBA_TOPK = 3
N_GROUPS = 4
EXPERTS_PER_GROUP = 8
N_EXPERTS = N_GROUPS * EXPERTS_PER_GROUP
D_EXPERT = 512
ALPHA = (2 * DEPTH) ** 0.25
EPS = 1e-5

LANES = 128
SUBLANES = 8
NEG_BIG = -1e30
VMEM_LIMIT = 56 * 1024 * 1024

COL_MQK = 0
COL_MV = 1024
COL_MO = 1536
COL_FQ = 2048
COL_BQ = 2816
COL_GATE = 3584
Z_W = 3712

F32 = jnp.float32
BF16 = jnp.bfloat16


def _cparams(sem):
    return pltpu.CompilerParams(dimension_semantics=sem, vmem_limit_bytes=VMEM_LIMIT)


def _split3(c):
    hi = c.astype(BF16).astype(F32)
    r1 = c - hi
    mid = r1.astype(BF16).astype(F32)
    lo = (r1 - mid).astype(BF16).astype(F32)
    return hi, mid, lo


def _dot(a, b):
    return jnp.dot(a, b, preferred_element_type=F32)


def _dot_nt(a, b):
    return lax.dot_general(a, b, (((1,), (1,)), ((), ())), preferred_element_type=F32)


def _dot_tn(a, b):
    return lax.dot_general(a, b, (((0,), (0,)), ((), ())), preferred_element_type=F32)


LOG2E = 1.4426950408889634
VT_ROWS = 80


def _store_vt(v_ref, vo_ref):
    vt = v_ref[...].T
    t = vt.shape[1]
    row = lax.broadcasted_iota(jnp.int32, (VT_ROWS - HEAD_DIM, t), 0)
    tail = jnp.where(row == 0, 1.0, 0.0)
    for h in range(N_ATT_HEADS):
        vo_ref[h] = jnp.concatenate([vt[h * HEAD_DIM:(h + 1) * HEAD_DIM, :], tail], axis=0).astype(BF16)


IN_W = 3596
_W_RUNS = ((COL_MQK, 772, 2308),
           (COL_MO, 2316, 2828),
           (COL_FQ, 0, 768),
           (COL_BQ, 2828, 3596))
_W_GATE_RUNS = ((768, 772), (2308, 2316))


def _cols(w_ref, r0, r1, a, b):
    a0 = (a // LANES) * LANES
    b0 = min(-(-b // LANES) * LANES, IN_W)
    return w_ref[r0:r1, a0:b0][:, a - a0:b - a0]


def _inproj_kernel(x_ref, w_ref, o_ref, wb_ref):
    @pl.when(pl.program_id(0) == 0)
    def _():
        rows = 256
        for r0 in range(0, D_MODEL, rows):
            r1 = r0 + rows
            for dst, a, b in _W_RUNS:
                wb_ref[r0:r1, dst:dst + (b - a)] = _cols(w_ref, r0, r1, a, b).astype(BF16)
            gate = [_cols(w_ref, r0, r1, a, b) for a, b in _W_GATE_RUNS]
            used = sum(b - a for a, b in _W_GATE_RUNS)
            gate.append(jnp.zeros((rows, LANES - used), F32))
            wb_ref[r0:r1, COL_GATE:] = jnp.concatenate(gate, axis=1).astype(BF16)

    xb = x_ref[...].astype(BF16)
    n = o_ref.shape[1]
    step = 512
    for j in range(0, n, step):
        w = min(step, n - j)
        o_ref[:, j:j + w] = _dot(xb, wb_ref[:, j:j + w])


def _inproj(x, w_in, layer, tm=512):
    s = x.shape[0]
    return pl.pallas_call(
        _inproj_kernel,
        grid=(s // tm,),
        in_specs=[pl.BlockSpec((tm, D_MODEL), lambda i: (i, 0)),
                  pl.BlockSpec((D_MODEL, IN_W), lambda i: (layer, 0), pipeline_mode=pl.Buffered(1))],
        out_specs=pl.BlockSpec((tm, Z_W), lambda i: (i, 0)),
        out_shape=jax.ShapeDtypeStruct((s, Z_W), F32),
        scratch_shapes=[pltpu.VMEM((D_MODEL, Z_W), BF16)],
        compiler_params=_cparams(("arbitrary",)),
        name="inproj",
    )(x, w_in.reshape(-1, IN_W))


def _log_sigmoid(x):
    return jnp.minimum(x, 0.0) - jnp.log(1.0 + jnp.exp(-jnp.abs(x)))


def _gate_kernel(zg_ref, bias_ref, o_ref, carry_ref):
    @pl.when(pl.program_id(0) == 0)
    def _():
        carry_ref[...] = jnp.zeros_like(carry_ref)

    g = zg_ref[...] + bias_ref[...]
    ls = _log_sigmoid(g)
    t = g.shape[0]
    r = lax.broadcasted_iota(jnp.int32, (t, t), 0)
    c = lax.broadcasted_iota(jnp.int32, (t, t), 1)
    tri = c <= r
    tri_all = jnp.where(tri, 1.0, 0.0).astype(BF16)
    tri_chunk = jnp.where(tri & ((c // ML_CHUNK) == (r // ML_CHUNK)), 1.0, 0.0).astype(BF16)
    hi, mid, lo = _split3(ls)
    parts = [p.astype(BF16) for p in (hi, mid, lo)]
    cum_all = sum(_dot(tri_all, p) for p in parts)
    cum_chunk = sum(_dot(tri_chunk, p) for p in parts)
    carry = carry_ref[...]
    lane = lax.broadcasted_iota(jnp.int32, g.shape, 1)
    o_ref[...] = jnp.where(lane < 4, cum_all + carry, jnp.where(lane < 8, g, cum_chunk))
    carry_ref[...] = carry + cum_all[t - 1:t, :]


def _gate_prep(z, bias_row, tm=512):
    s = z.shape[0]
    return pl.pallas_call(
        _gate_kernel,
        grid=(s // tm,),
        in_specs=[pl.BlockSpec((tm, LANES), lambda i: (i, COL_GATE // LANES)),
                  pl.BlockSpec((1, LANES), lambda i: (0, 0))],
        out_specs=pl.BlockSpec((tm, LANES), lambda i: (i, 0)),
        out_shape=jax.ShapeDtypeStruct((s, LANES), F32),
        scratch_shapes=[pltpu.VMEM((1, LANES), F32)],
        compiler_params=_cparams(("arbitrary",)),
        name="gate_prep",
    )(z, bias_row)


def _fox_prep_kernel(q_ref, k_ref, v_ref, g_ref, qo_ref, ko_ref, vo_ref):
    g = g_ref[...]
    t = g.shape[0]
    lane = lax.broadcasted_iota(jnp.int32, (t, HEAD_DIM), 1)
    scale = HEAD_DIM ** -0.5 * LOG2E
    for h in range(N_ATT_HEADS):
        hi, mid, lo = _split3(g[:, h:h + 1] * LOG2E)
        aug_q = jnp.where(lane == 0, hi, jnp.where(lane == 1, mid, jnp.where(lane == 2, lo,
                          jnp.where(lane < 6, 1.0, 0.0))))
        aug_k = jnp.where(lane < 3, 1.0, jnp.where(lane == 3, -hi, jnp.where(lane == 4, -mid,
                          jnp.where(lane == 5, -lo, 0.0))))
        sl = slice(h * HEAD_DIM, (h + 1) * HEAD_DIM)
        qo_ref[h] = jnp.concatenate([(q_ref[:, sl] * scale).astype(BF16), aug_q.astype(BF16)], axis=1)
        ko_ref[h] = jnp.concatenate([k_ref[:, sl].astype(BF16), aug_k.astype(BF16)], axis=1)
    _store_vt(v_ref, vo_ref)


def _fox_prep(z, gates, tm=512):
    s = z.shape[0]
    cb = COL_FQ // ATT_W
    head_spec = pl.BlockSpec((N_ATT_HEADS, tm, LANES), lambda i: (0, i, 0))
    return pl.pallas_call(
        _fox_prep_kernel,
        grid=(s // tm,),
        in_specs=[pl.BlockSpec((tm, ATT_W), lambda i: (i, cb)),
                  pl.BlockSpec((tm, ATT_W), lambda i: (i, cb + 1)),
                  pl.BlockSpec((tm, ATT_W), lambda i: (i, cb + 2)),
                  pl.BlockSpec((tm, LANES), lambda i: (i, 0))],
        out_specs=[head_spec, head_spec, pl.BlockSpec((N_ATT_HEADS, VT_ROWS, tm), lambda i: (0, 0, i))],
        out_shape=[jax.ShapeDtypeStruct((N_ATT_HEADS, s, LANES), BF16),
                   jax.ShapeDtypeStruct((N_ATT_HEADS, s, LANES), BF16),
                   jax.ShapeDtypeStruct((N_ATT_HEADS, VT_ROWS, s), BF16)],
        compiler_params=_cparams(("parallel",)),
        name="fox_prep",
    )(z, z, z, gates)


def _rope_table_kernel(pos_ref, inv_ref, sign_ref, cos_ref, sin_ref):
    reps = ATT_W // LANES
    ang = pos_ref[...].astype(F32) * inv_ref[:, 0:LANES]
    cos_ref[...] = jnp.concatenate([jnp.cos(ang)] * reps, axis=1)
    sin_ref[...] = jnp.concatenate([jnp.sin(ang) * sign_ref[:, 0:LANES]] * reps, axis=1)


def _rope_tables(pos_col, inv_row, sign_row, tm=512):
    s = pos_col.shape[0]
    row = pl.BlockSpec((1, ATT_W), lambda i: (0, 0))
    out = pl.BlockSpec((tm, ATT_W), lambda i: (i, 0))
    return pl.pallas_call(
        _rope_table_kernel,
        grid=(s // tm,),
        in_specs=[pl.BlockSpec((tm, 1), lambda i: (i, 0)), row, row],
        out_specs=[out, out],
        out_shape=[jax.ShapeDtypeStruct((s, ATT_W), F32)] * 2,
        compiler_params=_cparams(("parallel",)),
        name="rope_tables",
    )(pos_col, inv_row, sign_row)


def _rope(u, cos, sin_signed):
    half = ROPE_DIM // 2
    lane = lax.broadcasted_iota(jnp.int32, u.shape, 1) % HEAD_DIM
    up = pltpu.roll(u, ATT_W - half, axis=1)
    down = pltpu.roll(u, half, axis=1)
    partner = jnp.where(lane < half, up, down)
    return u * cos + partner * sin_signed


def _moba_rope_kernel(q_ref, k_ref, cos_ref, sin_ref, qo_ref, ko_ref, km_ref):
    cos = cos_ref[...]
    sin = sin_ref[...]
    qo_ref[...] = _rope(q_ref[...], cos, sin)
    kr = _rope(k_ref[...], cos, sin)
    ko_ref[...] = kr
    km_ref[0] = jnp.mean(kr, axis=0, keepdims=True)


def _moba_rope(z, cos, sin):
    s = z.shape[0]
    tm = MOBA_BLOCK
    cb = COL_BQ // ATT_W
    blk = pl.BlockSpec((tm, ATT_W), lambda i: (i, 0))
    return pl.pallas_call(
        _moba_rope_kernel,
        grid=(s // tm,),
        in_specs=[pl.BlockSpec((tm, ATT_W), lambda i: (i, cb)),
                  pl.BlockSpec((tm, ATT_W), lambda i: (i, cb + 1)), blk, blk],
        out_specs=[blk, blk, pl.BlockSpec((1, 1, ATT_W), lambda i: (i, 0, 0))],
        out_shape=[jax.ShapeDtypeStruct((s, ATT_W), F32), jax.ShapeDtypeStruct((s, ATT_W), F32),
                   jax.ShapeDtypeStruct((s // tm, 1, ATT_W), F32)],
        compiler_params=_cparams(("parallel",)),
        name="moba_rope",
    )(z, z, cos, sin)


def _moba_select_kernel(q_ref, k_ref, v_ref, km_ref, qo_ref, ko_ref, vo_ref):
    own = pl.program_id(0)
    q = q_ref[...]
    km = km_ref[...]
    km_hi, q_hi = km.astype(BF16), q.astype(BF16)
    km_lo, q_lo = (km - km_hi.astype(F32)).astype(BF16), (q - q_hi.astype(F32)).astype(BF16)
    gate_t = _dot_nt(km_hi, q_hi) + _dot_nt(km_lo, q_hi) + _dot_nt(km_hi, q_lo)
    t = q.shape[0]
    blk = lax.broadcasted_iota(jnp.int32, (HEAD_DIM, t), 0)
    biases = []
    for h in range(N_ATT_HEADS):
        g = jnp.where(blk < own, gate_t[h * HEAD_DIM:(h + 1) * HEAD_DIM, :], -jnp.inf)
        bias = jnp.where(blk == own, 0.0, NEG_BIG)
        for r in range(MOBA_TOPK):
            mx = jnp.max(g, axis=0, keepdims=True)
            idx = jnp.min(jnp.where(g == mx, blk, HEAD_DIM), axis=0, keepdims=True)
            hit = blk == idx
            bias = jnp.where(hit, jnp.where(r < own, 0.0, bias), bias)
            g = jnp.where(hit, -jnp.inf, g)
        biases.append(bias)
    bias_all = jnp.concatenate(biases, axis=0).T
    lane = lax.broadcasted_iota(jnp.int32, (t, HEAD_DIM), 1)
    scale = HEAD_DIM ** -0.5 * LOG2E
    onehot_own = jnp.where(lane == own, 1.0, 0.0).astype(BF16)
    for h in range(N_ATT_HEADS):
        sl = slice(h * HEAD_DIM, (h + 1) * HEAD_DIM)
        qo_ref[h] = jnp.concatenate([(q[:, sl] * scale).astype(BF16), bias_all[:, sl].astype(BF16)], axis=1)
        ko_ref[h] = jnp.concatenate([k_ref[:, sl].astype(BF16), onehot_own], axis=1)
    _store_vt(v_ref, vo_ref)


def _moba_select(q_rope, k_rope, z, km_mat):
    s = z.shape[0]
    tm = MOBA_BLOCK
    cb = COL_BQ // ATT_W
    blk = pl.BlockSpec((tm, ATT_W), lambda i: (i, 0))
    head_spec = pl.BlockSpec((N_ATT_HEADS, tm, LANES), lambda i: (0, i, 0))
    return pl.pallas_call(
        _moba_select_kernel,
        grid=(s // tm,),
        in_specs=[blk, blk, pl.BlockSpec((tm, ATT_W), lambda i: (i, cb + 2)),
                  pl.BlockSpec((ATT_W, ATT_W), lambda i: (0, 0))],
        out_specs=[head_spec, head_spec, pl.BlockSpec((N_ATT_HEADS, VT_ROWS, tm), lambda i: (0, 0, i))],
        out_shape=[jax.ShapeDtypeStruct((N_ATT_HEADS, s, LANES), BF16),
                   jax.ShapeDtypeStruct((N_ATT_HEADS, s, LANES), BF16),
                   jax.ShapeDtypeStruct((N_ATT_HEADS, VT_ROWS, s), BF16)],
        compiler_params=_cparams(("parallel",)),
        name="moba_select",
    )(q_rope, k_rope, z, km_mat)


def _flash_kernel(qi_tab, ki_tab, *refs, tile, ngroups):
    groups = [refs[3 * g:3 * g + 3] for g in range(ngroups)]
    g_ref, o_ref, m_ref, acc_ref = refs[3 * ngroups:]
    step = pl.program_id(0)
    qi = qi_tab[step]
    ki = ki_tab[step]

    @pl.when(ki == 0)
    def _():
        m_ref[...] = jnp.full_like(m_ref, -jnp.inf)
        acc_ref[...] = jnp.zeros_like(acc_ref)

    units = [(g, h) for g in range(ngroups) for h in range(N_ATT_HEADS)]

    def update(masked):
        if masked:
            key = lax.broadcasted_iota(jnp.int32, (tile, tile), 0)
            qry = lax.broadcasted_iota(jnp.int32, (tile, tile), 1)
            causal = key <= qry
        scores = lambda g, h: _dot_nt(groups[g][1][h], groups[g][0][h])
        st_next = scores(*units[0])
        for u, (g, h) in enumerate(units):
            st = st_next
            if u + 1 < len(units):
                st_next = scores(*units[u + 1])
            if masked:
                st = jnp.where(causal, st, NEG_BIG)
            m_prev = m_ref[u]
            m_new = jnp.maximum(m_prev, jnp.max(st, axis=0, keepdims=True))
            alpha = jnp.exp2(m_prev - m_new)
            p = jnp.exp2(st - m_new).astype(BF16)
            m_ref[u] = m_new
            acc_ref[u] = acc_ref[u] * alpha + _dot(groups[g][2][h], p)

    @pl.when(ki < qi)
    def _():
        update(False)

    @pl.when(ki == qi)
    def _():
        update(True)
        outs = []
        for u in range(len(units)):
            acc = acc_ref[u]
            o = acc[0:HEAD_DIM, :] / acc[HEAD_DIM:HEAD_DIM + 1, :]
            outs.append(o * lax.rsqrt(jnp.mean(o * o, axis=0, keepdims=True) + EPS))
        o_ref[...] = (jnp.concatenate(outs, axis=0).T * g_ref[...]).astype(o_ref.dtype)


def _flash_attention(groups, gain_row, tile=1024):
    ngroups = len(groups)
    width = ATT_W * ngroups
    s = groups[0][2].shape[2]
    n = s // tile
    pairs = [(qi, ki) for qi in range(n) for ki in range(qi + 1)]
    qi_tab = jnp.asarray([p[0] for p in pairs], jnp.int32)
    ki_tab = jnp.asarray([p[1] for p in pairs], jnp.int32)
    group_specs = [pl.BlockSpec((N_ATT_HEADS, tile, LANES), lambda i, qt, kt: (0, qt[i], 0)),
                   pl.BlockSpec((N_ATT_HEADS, tile, LANES), lambda i, qt, kt: (0, kt[i], 0)),
                   pl.BlockSpec((N_ATT_HEADS, VT_ROWS, tile), lambda i, qt, kt: (0, 0, kt[i]))]
    grid_spec = pltpu.PrefetchScalarGridSpec(
        num_scalar_prefetch=2,
        grid=(len(pairs),),
        in_specs=group_specs * ngroups + [pl.BlockSpec((1, width), lambda i, qt, kt: (0, 0))],
        out_specs=pl.BlockSpec((tile, width), lambda i, qt, kt: (qt[i], 0)),
        scratch_shapes=[pltpu.VMEM((N_ATT_HEADS * ngroups, 1, tile), F32),
                        pltpu.VMEM((N_ATT_HEADS * ngroups, VT_ROWS, tile), F32)],
    )
    operands = [a for grp in groups for a in grp]
    return pl.pallas_call(
        functools.partial(_flash_kernel, tile=tile, ngroups=ngroups),
        grid_spec=grid_spec,
        out_shape=jax.ShapeDtypeStruct((s, width), BF16),
        compiler_params=_cparams(("arbitrary",)),
        name="flash_attention",
    )(qi_tab, ki_tab, *operands, gain_row)


def _shift_rows(u, tail, s):
    rolled = pltpu.roll(u, s, axis=0)
    rolled_tail = pltpu.roll(tail, s, axis=0)
    row8 = lax.broadcasted_iota(jnp.int32, tail.shape, 0)
    top = jnp.where(row8 < s, rolled_tail, rolled[0:8])
    return jnp.concatenate([top, rolled[8:]], axis=0)


def _mlstm_kernel(qk_ref, v_ref, o_ref, g_ref, cw_ref, gain_ref, y_ref, tail_ref, c_ref, n_ref, m_ref):
    @pl.when(pl.program_id(0) == 0)
    def _():
        tail_ref[...] = jnp.zeros_like(tail_ref)
        c_ref[...] = jnp.zeros_like(c_ref)
        n_ref[...] = jnp.zeros_like(n_ref)
        m_ref[...] = jnp.zeros_like(m_ref)

    L = ML_CHUNK
    rows = qk_ref.shape[0]
    nch = rows // L
    u = qk_ref[...]
    tail = tail_ref[...]
    cw = cw_ref[...]
    conv = u * cw[CONV_WIDTH - 1:CONV_WIDTH]
    for s in range(1, CONV_WIDTH):
        conv = conv + _shift_rows(u, tail, s) * cw[CONV_WIDTH - 1 - s:CONV_WIDTH - s]
    tail_ref[...] = u[rows - 8:rows]
    qk = conv * jax.nn.sigmoid(conv)

    r = lax.broadcasted_iota(jnp.int32, (L, L), 0)
    c = lax.broadcasted_iota(jnp.int32, (L, L), 1)
    tril = c <= r
    kscale = ML_DIM ** -0.5
    heads = range(ML_HEADS)
    st = []
    for ci in range(nch):
        rs = slice(ci * L, (ci + 1) * L)
        g = g_ref[rs, :]
        gt = g.T
        per_head = []
        for h in heads:
            sl = slice(h * ML_DIM, (h + 1) * ML_DIM)
            qh = qk[rs, sl]
            kh = qk[rs, ML_W + h * ML_DIM:ML_W + (h + 1) * ML_DIM] * kscale
            qb, kb, vb = qh.astype(BF16), kh.astype(BF16), v_ref[rs, sl].astype(BF16)
            b_col = g[:, 8 + h:9 + h]
            i_col = g[:, 4 + h:5 + h]
            b_row = gt[8 + h:9 + h, :]
            i_row = gt[4 + h:5 + h, :]
            dmat = jnp.where(tril, b_col - b_row + i_row, -jnp.inf)
            dmax = jnp.max(dmat, axis=1, keepdims=True)
            a1 = _dot_nt(qb, kb) * jnp.exp(dmat - dmax)
            b_last = b_row[:, L - 1:L]
            g_col = b_last - b_col + i_col
            gmax = jnp.max(g_col, axis=0, keepdims=True)
            kw1 = kh * jnp.exp(g_col - gmax)
            per_head.append(dict(qh=qh, qb=qb, b_col=b_col, dmax=dmax, b_last=b_last, gmax=gmax,
                                 av=_dot(a1.astype(BF16), vb), asum=jnp.sum(a1, axis=1, keepdims=True),
                                 u1=_dot_tn(kw1.astype(BF16), vb), ksum=jnp.sum(kw1, axis=0, keepdims=True)))
        st.append(per_head)
    state = [(c_ref[h], n_ref[h], m_ref[h][:, 0:1]) for h in heads]
    for ci in range(nch):
        rs = slice(ci * L, (ci + 1) * L)
        outs = []
        for h in heads:
            s_ = st[ci][h]
            sl = slice(h * ML_DIM, (h + 1) * ML_DIM)
            cmat, nrow, m_prev = state[h]
            inter = s_["b_col"] + m_prev
            m_t = jnp.maximum(inter, s_["dmax"])
            s_intra = jnp.exp(s_["dmax"] - m_t)
            w_inter = jnp.exp(inter - m_t)
            num = s_intra * s_["av"] + w_inter * _dot(s_["qb"], cmat.astype(BF16))
            den = s_intra * s_["asum"] + w_inter * jnp.sum(s_["qh"] * nrow, axis=1, keepdims=True)
            hh = num / jnp.maximum(jnp.abs(den), jnp.exp(-m_t))
            y = jax.nn.sigmoid(o_ref[rs, sl]) * hh
            outs.append(y * lax.rsqrt(jnp.mean(y * y, axis=1, keepdims=True) + EPS))
        y_ref[rs, :] = (jnp.concatenate(outs, axis=1) * gain_ref[...]).astype(y_ref.dtype)
        for h in heads:
            s_ = st[ci][h]
            cmat, nrow, m_prev = state[h]
            m_new = jnp.maximum(s_["b_last"] + m_prev, s_["gmax"])
            decay = jnp.exp(s_["b_last"] + m_prev - m_new)
            scale = jnp.exp(s_["gmax"] - m_new)
            state[h] = (decay * cmat + scale * s_["u1"], decay * nrow + scale * s_["ksum"], m_new)
    for h in heads:
        c_ref[h], n_ref[h] = state[h][0], state[h][1]
        m_ref[h] = jnp.broadcast_to(state[h][2], (1, LANES))


ML_CHUNKS_PER_STEP = 4


def _mlstm(z, gates, conv_w, gain_row):
    s = z.shape[0]
    L = ML_CHUNK * ML_CHUNKS_PER_STEP
    return pl.pallas_call(
        _mlstm_kernel,
        grid=(s // L,),
        in_specs=[pl.BlockSpec((L, 2 * ML_W), lambda i: (i, COL_MQK // (2 * ML_W))),
                  pl.BlockSpec((L, ML_W), lambda i: (i, COL_MV // ML_W)),
                  pl.BlockSpec((L, ML_W), lambda i: (i, COL_MO // ML_W)),
                  pl.BlockSpec((L, LANES), lambda i: (i, 0)),
                  pl.BlockSpec((CONV_WIDTH, 2 * ML_W), lambda i: (0, 0)),
                  pl.BlockSpec((1, ML_W), lambda i: (0, 0))],
        out_specs=pl.BlockSpec((L, ML_W), lambda i: (i, 0)),
        out_shape=jax.ShapeDtypeStruct((s, ML_W), BF16),
        scratch_shapes=[pltpu.VMEM((8, 2 * ML_W), F32),
                        pltpu.VMEM((ML_HEADS, ML_DIM, ML_DIM), F32),
                        pltpu.VMEM((ML_HEADS, 1, ML_DIM), F32),
                        pltpu.VMEM((ML_HEADS, 1, LANES), F32)],
        compiler_params=_cparams(("arbitrary",)),
        name="mlstm",
    )(z, z, z, gates, conv_w, gain_row)


def _layer_norm(h, g, b):
    mu = jnp.mean(h, axis=1, keepdims=True)
    d = h - mu
    var = jnp.mean(d * d, axis=1, keepdims=True)
    return d * lax.rsqrt(var + EPS) * g + b


def _row_seg(j, rows):
    return pl.ds(j, rows, stride=SUBLANES)


def _outproj_kernel(yf_ref, ym_ref, yb_ref, w_ref, x_ref, g_ref, b_ref, wrh_ref, wrl_ref, br_ref, x1t_ref, lg_ref):
    mix = (_dot(yf_ref[...], w_ref[0:ATT_W, :]) + _dot(ym_ref[...], w_ref[ATT_W:ATT_W + ML_W, :])
           + _dot(yb_ref[...], w_ref[ATT_W + ML_W:, :]))
    x1 = _layer_norm(ALPHA * x_ref[...] + mix, g_ref[...], b_ref[...])
    tm = x1.shape[0]
    for j in range(D_MODEL // LANES):
        x1t_ref[_row_seg(j, tm), :] = x1[:, j * LANES:(j + 1) * LANES]
    hi = x1.astype(BF16)
    lo = (x1 - hi.astype(F32)).astype(BF16)
    wrh = wrh_ref[...]
    lg_ref[...] = _dot(hi, wrh) + _dot(lo, wrh) + _dot(hi, wrl_ref[...]) + br_ref[...]


def _outproj_ln_router(y_att, ym, w_out, x, ln_g, ln_b, w_router, b_router, tm=512):
    s = x.shape[0]
    const = lambda shape: pl.BlockSpec(shape, lambda i: (0, 0))
    rows = lambda w: pl.BlockSpec((tm, w), lambda i: (i, 0))
    yf, yb = y_att, y_att
    w_router_hi = w_router.astype(BF16)
    w_router_lo = (w_router - w_router_hi.astype(F32)).astype(BF16)
    return pl.pallas_call(
        _outproj_kernel,
        grid=(s // tm,),
        in_specs=[rows(ATT_W), rows(ML_W), pl.BlockSpec((tm, ATT_W), lambda i: (i, 1)),
                  const((D_MODEL, D_MODEL)), rows(D_MODEL),
                  const((1, D_MODEL)), const((1, D_MODEL)), const((D_MODEL, LANES)), const((D_MODEL, LANES)),
                  const((1, LANES))],
        out_specs=[pl.BlockSpec((SUBLANES * tm, LANES), lambda i: (i, 0)), rows(LANES)],
        out_shape=[jax.ShapeDtypeStruct((SUBLANES * s, LANES), F32), jax.ShapeDtypeStruct((s, LANES), F32)],
        compiler_params=_cparams(("parallel",)),
        name="outproj_ln_router",
    )(yf, ym, yb, w_out, x, ln_g, ln_b, w_router_hi, w_router_lo, b_router)


def _first_argmax(v, lane):
    mx = jnp.max(v, axis=1, keepdims=True)
    idx = jnp.min(jnp.where(v == mx, lane, LANES), axis=1, keepdims=True)
    return mx, idx


def _route_kernel(lg_ref, info_ref, cnt_ref, carry_ref):
    @pl.when(pl.program_id(0) == 0)
    def _():
        carry_ref[...] = jnp.zeros_like(carry_ref)

    lg = lg_ref[...]
    t = lg.shape[0]
    lane = lax.broadcasted_iota(jnp.int32, lg.shape, 1)
    is_grp = lane < N_GROUPS
    gmax, gsel = _first_argmax(jnp.where(is_grp, lg, -jnp.inf), lane)
    p_grp = 1.0 / jnp.sum(jnp.where(is_grp, jnp.exp(lg - gmax), 0.0), axis=1, keepdims=True)
    lo = N_GROUPS + EXPERTS_PER_GROUP * gsel
    el = jnp.where((lane >= lo) & (lane < lo + EXPERTS_PER_GROUP), lg, -jnp.inf)
    v0, i0 = _first_argmax(el, lane)
    v1, i1 = _first_argmax(jnp.where(lane == i0, -jnp.inf, el), lane)
    ex = jnp.exp(v1 - v0)
    w0 = p_grp / (1.0 + ex)
    w1 = p_grp * ex / (1.0 + ex)
    e0 = i0 - N_GROUPS
    e1 = i1 - N_GROUPS

    cnt = jnp.where((lane == e0) | (lane == e1), 1.0, 0.0)
    r = lax.broadcasted_iota(jnp.int32, (t, t), 0)
    c = lax.broadcasted_iota(jnp.int32, (t, t), 1)
    strict = jnp.where(c < r, 1.0, 0.0).astype(BF16)
    carry = carry_ref[...]
    before = _dot(strict, cnt.astype(BF16)) + carry
    rank0 = jnp.sum(jnp.where(lane == e0, before, 0.0), axis=1, keepdims=True)
    rank1 = jnp.sum(jnp.where(lane == e1, before, 0.0), axis=1, keepdims=True)
    carry = carry + jnp.sum(cnt, axis=0, keepdims=True)
    carry_ref[...] = carry
    cnt_ref[...] = carry
    vals = [e0.astype(F32), e1.astype(F32), w0, w1, rank0, rank1]
    info = jnp.zeros(lg.shape, F32)
    for j, val in enumerate(vals):
        info = jnp.where(lane == j, val, info)
    info_ref[...] = info


def _route(logits, tm=512):
    s = logits.shape[0]
    return pl.pallas_call(
        _route_kernel,
        grid=(s // tm,),
        in_specs=[pl.BlockSpec((tm, LANES), lambda i: (i, 0))],
        out_specs=[pl.BlockSpec((tm, LANES), lambda i: (i, 0)), pl.BlockSpec((1, LANES), lambda i: (0, 0))],
        out_shape=[jax.ShapeDtypeStruct((s, LANES), F32), jax.ShapeDtypeStruct((1, LANES), F32)],
        scratch_shapes=[pltpu.VMEM((1, LANES), F32)],
        compiler_params=_cparams(("arbitrary",)),
        name="route",
    )(logits)


MOE_ROWS = 256


def _moe_kernel(blk_e, nv, src0_ref, srcn_ref, slotp_ref, slotc_ref, x_hbm, wg_ref, wu_ref, wd_ref, out_hbm,
                xbuf0, xbuf1, ybuf0, ybuf1, wgb, wub, wdb, gsem, ssem, *, nblk):
    b = pl.program_id(0)
    xbufs = (xbuf0, xbuf1)
    ybufs = (ybuf0, ybuf1)
    nv_prev2, nv_prev, nv_cur, nv_next = nv[b], nv[b + 1], nv[b + 2], nv[b + 3]

    def row_in(tok8, slot, r):
        return pltpu.make_async_copy(x_hbm.at[pl.ds(pl.multiple_of(tok8, SUBLANES), SUBLANES), :],
                                     xbufs[slot].at[pl.ds(SUBLANES * r, SUBLANES), :], gsem.at[slot])

    def row_out(dst8, slot, r):
        return pltpu.make_async_copy(ybufs[slot].at[pl.ds(SUBLANES * r, SUBLANES), :],
                                     out_hbm.at[pl.ds(pl.multiple_of(dst8, SUBLANES), SUBLANES), :], ssem.at[slot])

    def start_rows(make, idx_ref, slot, n):
        for r in range(MOE_ROWS):
            idx = idx_ref[0, 0, r]

            @pl.when(r < n)
            def _():
                make(idx, slot, r).start(priority=r % 2)

    def wait_rows(make, slot, n):
        @pl.when(n > 0)
        def _():
            rows = pl.multiple_of(n * SUBLANES, SUBLANES)
            if make is row_in:
                pltpu.make_async_copy(x_hbm.at[pl.ds(0, rows), :], xbufs[slot].at[pl.ds(0, rows), :],
                                      gsem.at[slot]).wait()
            else:
                pltpu.make_async_copy(ybufs[slot].at[pl.ds(0, rows), :], out_hbm.at[pl.ds(0, rows), :],
                                      ssem.at[slot]).wait()

    @pl.when(b == 0)
    def _():
        for half in range(2):
            xbufs[half][...] = jnp.zeros_like(xbufs[half])
        start_rows(row_in, src0_ref, 0, nv_cur)

    @pl.when((b == 0) | (blk_e[b] != blk_e[jnp.maximum(b - 1, 0)]))
    def _():
        wgb[...] = wg_ref[0, 0].astype(BF16)
        wub[...] = wu_ref[0, 0].astype(BF16)
        wdb[...] = wd_ref[0, 0].astype(BF16)

    def step(cur):
        nxt = 1 - cur
        wait_rows(row_in, cur, nv_cur)
        start_rows(row_out, slotp_ref, nxt, nv_prev)
        start_rows(row_in, srcn_ref, nxt, nv_next)

        nseg = D_MODEL // LANES
        seg = lambda j: _row_seg(j, MOE_ROWS)
        xb = jnp.concatenate([xbufs[cur][seg(j), :] for j in range(nseg)], axis=1).astype(BF16)
        gate = _dot(xb, wgb[...])
        up = _dot(xb, wub[...])
        hid = (gate * jax.nn.sigmoid(gate) * up).astype(BF16)
        y = _dot(hid, wdb[...])

        wait_rows(row_out, cur, nv_prev2)

        for j in range(nseg):
            ybufs[cur][seg(j), :] = y[:, j * LANES:(j + 1) * LANES]

        @pl.when(b == nblk - 1)
        def _():
            start_rows(row_out, slotc_ref, cur, nv_cur)
            wait_rows(row_out, nxt, nv_prev)
            wait_rows(row_out, cur, nv_cur)

    def drain_step(cur):
        nxt = 1 - cur
        start_rows(row_out, slotp_ref, nxt, nv_prev)
        wait_rows(row_out, cur, nv_prev2)

        @pl.when(b == nblk - 1)
        def _():
            wait_rows(row_out, nxt, nv_prev)

    for parity in range(2):
        pl.when((b % 2 == parity) & (nv_cur > 0))(functools.partial(step, parity))
        pl.when((b % 2 == parity) & (nv_cur == 0))(functools.partial(drain_step, parity))


def _moe_ffn(x1, src_ext, slot_ext, blk_e, nv_ext, w_gate, w_up, w_down, layer):
    s = x1.shape[0] // SUBLANES
    nblk = src_ext.shape[0] - 1
    stage = (SUBLANES * MOE_ROWS, LANES)
    idx_blk = (1, 1, MOE_ROWS)
    smem = pltpu.SMEM
    grid_spec = pltpu.PrefetchScalarGridSpec(
        num_scalar_prefetch=2,
        grid=(nblk,),
        in_specs=[pl.BlockSpec(idx_blk, lambda b, be, nv: (0, 0, 0), memory_space=smem),
                  pl.BlockSpec(idx_blk, lambda b, be, nv: (b + 1, 0, 0), memory_space=smem),
                  pl.BlockSpec(idx_blk, lambda b, be, nv: (b, 0, 0), memory_space=smem),
                  pl.BlockSpec(idx_blk, lambda b, be, nv: (b + 1, 0, 0), memory_space=smem),
                  pl.BlockSpec(memory_space=pl.ANY),
                  pl.BlockSpec((1, 1, D_MODEL, D_EXPERT), lambda b, be, nv: (layer, be[b], 0, 0)),
                  pl.BlockSpec((1, 1, D_MODEL, D_EXPERT), lambda b, be, nv: (layer, be[b], 0, 0)),
                  pl.BlockSpec((1, 1, D_EXPERT, D_MODEL), lambda b, be, nv: (layer, be[b], 0, 0))],
        out_specs=pl.BlockSpec(memory_space=pl.ANY),
        scratch_shapes=[pltpu.VMEM(stage, F32),
                        pltpu.VMEM(stage, F32),
                        pltpu.VMEM(stage, F32),
                        pltpu.VMEM(stage, F32),
                        pltpu.VMEM((D_MODEL, D_EXPERT), BF16),
                        pltpu.VMEM((D_MODEL, D_EXPERT), BF16),
                        pltpu.VMEM((D_EXPERT, D_MODEL), BF16),
                        pltpu.SemaphoreType.DMA((2,)),
                        pltpu.SemaphoreType.DMA((2,))],
    )
    return pl.pallas_call(
        functools.partial(_moe_kernel, nblk=nblk),
        grid_spec=grid_spec,
        out_shape=jax.ShapeDtypeStruct((SUBLANES * 2 * s, LANES), F32),
        compiler_params=_cparams(("arbitrary",)),
        name="moe_ffn",
    )(blk_e, nv_ext, src_ext, src_ext, slot_ext, slot_ext, x1, w_gate, w_up, w_down)


def _dest_kernel(info_ref, pstart_ref, d_ref):
    info = info_ref[...]
    tm = info.shape[0]
    lane = lax.broadcasted_iota(jnp.int32, info.shape, 1)
    ps = pstart_ref[...]
    dests = []
    for k in range(2):
        e = info[:, k:k + 1].astype(jnp.int32)
        dests.append(jnp.sum(jnp.where(lane == e, ps, 0.0), axis=1, keepdims=True) + info[:, 4 + k:5 + k])
    packed = jnp.where(lane == 0, dests[0], jnp.where(lane == 1, dests[1], 0.0))
    for j in range(tm // LANES):
        rows = packed[j * LANES:(j + 1) * LANES, :].T
        for k in range(2):
            d_ref[k, j:j + 1, :] = rows[k:k + 1, :].astype(jnp.int32)


def _dest(info, pstart_row, tm=1024):
    s = info.shape[0]
    return pl.pallas_call(
        _dest_kernel,
        grid=(s // tm,),
        in_specs=[pl.BlockSpec((tm, LANES), lambda i: (i, 0)), pl.BlockSpec((1, LANES), lambda i: (0, 0))],
        out_specs=pl.BlockSpec((2, tm // LANES, LANES), lambda i: (0, i, 0)),
        out_shape=jax.ShapeDtypeStruct((2, s // LANES, LANES), jnp.int32),
        compiler_params=_cparams(("parallel",)),
        name="moe_dest",
    )(info, pstart_row)


def _moe_plan(info, counts_row, s):
    counts = counts_row[0, :N_EXPERTS].astype(jnp.int32)
    padded = ((counts + MOE_ROWS - 1) // MOE_ROWS) * MOE_ROWS
    pends = jnp.cumsum(padded)
    pstarts = pends - padded
    nblk = (2 * s) // MOE_ROWS + N_EXPERTS
    p = nblk * MOE_ROWS
    dest = _dest(info, _pad_lanes(pstarts.astype(F32))).reshape(2 * s)
    out_slot = jnp.zeros((p,), jnp.int32).at[dest].set(jnp.arange(2 * s, dtype=jnp.int32))
    src_tok = jnp.where(out_slot >= s, out_slot - s, out_slot)
    nused = (pends[-1] // MOE_ROWS).astype(jnp.int32)
    blk = jnp.arange(nblk, dtype=jnp.int32)
    blk_start = jnp.minimum(blk, nused - 1) * MOE_ROWS
    blk_e = jnp.minimum(jnp.sum((pends[None, :] <= blk_start[:, None]).astype(jnp.int32), axis=1), N_EXPERTS - 1)
    is_e = jnp.arange(N_EXPERTS)[None, :] == blk_e[:, None]
    seg_start = jnp.sum(jnp.where(is_e, pstarts[None, :], 0), axis=1)
    seg_count = jnp.sum(jnp.where(is_e, counts[None, :], 0), axis=1)
    nvalid = jnp.where(blk < nused, jnp.clip(seg_count - (blk * MOE_ROWS - seg_start), 0, MOE_ROWS), 0)
    nv_ext = jnp.concatenate([jnp.zeros((2,), jnp.int32), nvalid.astype(jnp.int32), jnp.zeros((2,), jnp.int32)])
    pad_blk = jnp.zeros((MOE_ROWS,), jnp.int32)
    src_ext = jnp.concatenate([src_tok, pad_blk]) * SUBLANES
    slot_ext = jnp.concatenate([pad_blk, out_slot]) * SUBLANES
    return src_ext.reshape(nblk + 1, 1, MOE_ROWS), slot_ext.reshape(nblk + 1, 1, MOE_ROWS), blk_e, nv_ext


def _combine_kernel(x1_ref, ya_ref, yb_ref, info_ref, g_ref, b_ref, o_ref):
    info = info_ref[...]
    tm = info.shape[0]
    wide = lambda ref: jnp.concatenate([ref[_row_seg(j, tm), :] for j in range(D_MODEL // LANES)], axis=1)
    ffn = info[:, 2:3] * wide(ya_ref) + info[:, 3:4] * wide(yb_ref)
    o_ref[...] = _layer_norm(ALPHA * wide(x1_ref) + ffn, g_ref[...], b_ref[...])


def _combine_ln(x1_tiles, y2, info, ln_g, ln_b, tm=512):
    s = x1_tiles.shape[0] // SUBLANES
    const = pl.BlockSpec((1, D_MODEL), lambda i: (0, 0))
    return pl.pallas_call(
        _combine_kernel,
        grid=(s // tm,),
        in_specs=[pl.BlockSpec((SUBLANES * tm, LANES), lambda i: (i, 0)),
                  pl.BlockSpec((SUBLANES * tm, LANES), lambda i: (i, 0)),
                  pl.BlockSpec((SUBLANES * tm, LANES), lambda i: (i + s // tm, 0)),
                  pl.BlockSpec((tm, LANES), lambda i: (i, 0)), const, const],
        out_specs=pl.BlockSpec((tm, D_MODEL), lambda i: (i, 0)),
        out_shape=jax.ShapeDtypeStruct((s, D_MODEL), F32),
        compiler_params=_cparams(("parallel",)),
        name="combine_ln",
    )(x1_tiles, y2, y2, info, ln_g, ln_b)


def _pad_lanes(v, width=LANES):
    return jnp.zeros((1, width), F32).at[0, :v.shape[0]].set(v)


def _layer(layer, x, cos, sin, w_in, b_fox_f, b_mlstm_i, b_mlstm_f, conv_w, g_fox, g_mlstm, g_moba, w_out,
           ln1_g, ln1_b, w_grp, b_grp, w_exp_router, b_exp_router, w_gate, w_up, w_down, ln2_g, ln2_b):
    s = x.shape[0]
    gate_bias = _pad_lanes(jnp.concatenate([b_fox_f, b_mlstm_i, b_mlstm_f]))

    z = _inproj(x, w_in, layer)
    gates = _gate_prep(z, gate_bias)

    fox_ops = _fox_prep(z, gates)

    q_rope, k_rope, kmean = _moba_rope(z, cos, sin)
    km = kmean[:, 0, :].reshape(s // MOBA_BLOCK, N_ATT_HEADS, HEAD_DIM)
    km_mat = jnp.zeros((N_ATT_HEADS, HEAD_DIM, N_ATT_HEADS, HEAD_DIM), F32)
    for h in range(N_ATT_HEADS):
        km_mat = km_mat.at[h, :s // MOBA_BLOCK, h, :].set(km[:, h, :])
    km_mat = km_mat.reshape(ATT_W, ATT_W)
    moba_ops = _moba_select(q_rope, k_rope, z, km_mat)
    y_att = _flash_attention([fox_ops, moba_ops], jnp.concatenate([g_fox, g_moba])[None, :])

    y_mlstm = _mlstm(z, gates, conv_w, g_mlstm[None, :])

    w_router = jnp.zeros((D_MODEL, LANES), F32)
    w_router = w_router.at[:, :N_GROUPS].set(w_grp)
    w_router = w_router.at[:, N_GROUPS:N_GROUPS + N_EXPERTS].set(w_exp_router.reshape(D_MODEL, N_EXPERTS))
    b_router = _pad_lanes(jnp.concatenate([b_grp, b_exp_router.reshape(N_EXPERTS)]))
    x1_tiles, logits = _outproj_ln_router(y_att, y_mlstm, w_out.astype(BF16), x, ln1_g[None, :],
                                          ln1_b[None, :], w_router, b_router)

    info, counts = _route(logits)
    src_ext, slot_ext, blk_e, nv_ext = _moe_plan(info, counts, s)
    y2 = _moe_ffn(x1_tiles, src_ext, slot_ext, blk_e, nv_ext, w_gate, w_up, w_down, layer)
    return _combine_ln(x1_tiles, y2, info, ln2_g[None, :], ln2_b[None, :])


def kernel(x, positions, w_in, b_fox_f, b_mlstm_i, b_mlstm_f, conv_w, g_fox, g_mlstm, g_moba, w_out, ln1_g, ln1_b, w_grp, b_grp, w_exp_router, b_exp_router, w_gate, w_up, w_down, ln2_g, ln2_b):
    assert x.shape[0] == 1
    xs = x[0]
    d = jnp.arange(ATT_W) % HEAD_DIM
    half = ROPE_DIM // 2
    inv = 1.0 / (ROPE_THETA ** (jnp.arange(0, ROPE_DIM, 2, dtype=F32) / ROPE_DIM))
    inv_row = jnp.where(d < ROPE_DIM, inv[d % half], 0.0)[None, :].astype(F32)
    sign_row = jnp.where(d < half, -1.0, 1.0)[None, :].astype(F32)
    cos, sin = _rope_tables(positions[0][:, None], inv_row, sign_row)
    for l in range(DEPTH):
        xs = _layer(l, xs, cos, sin, w_in, b_fox_f[l], b_mlstm_i[l], b_mlstm_f[l], conv_w[l], g_fox[l],
                    g_mlstm[l], g_moba[l], w_out[l], ln1_g[l], ln1_b[l], w_grp[l], b_grp[l],
                    w_exp_router[l], b_exp_router[l], w_gate, w_up, w_down, ln2_g[l], ln2_b[l])
    return xs[None]
```

```python
import functools

import jax
import jax.numpy as jnp
from jax import lax
from jax.experimental import pallas as pl
from jax.experimental.pallas import tpu as pltpu

D_MODEL = 1024
DEPTH = 2
HEAD_DIM = 64
N_ATT_HEADS = 4
ATT_W = N_ATT_HEADS * HEAD_DIM
ML_HEADS = 4
ML_DIM = 128
ML_W = ML_HEADS * ML_DIM
ML_CHUNK = 128
CONV_WIDTH = 4
ROPE_DIM = 16
ROPE_THETA = 500000.0
MOBA_BLOCK = 256
MOBA_TOPK = 3
N_GROUPS = 4
EXPERTS_PER_GROUP = 8
N_EXPERTS = N_GROUPS * EXPERTS_PER_GROUP
D_EXPERT = 512
ALPHA = (2 * DEPTH) ** 0.25
EPS = 1e-5

LANES = 128
SUBLANES = 8
NEG_BIG = -1e30
VMEM_LIMIT = 56 * 1024 * 1024

COL_MQK = 0
COL_MV = 1024
COL_MO = 1536
COL_FQ = 2048
COL_BQ = 2816
COL_GATE = 3584
Z_W = 3712

F32 = jnp.float32
BF16 = jnp.bfloat16


def _cparams(sem):
    return pltpu.CompilerParams(dimension_semantics=sem, vmem_limit_bytes=VMEM_LIMIT)


def _split3(c):
    hi = c.astype(BF16).astype(F32)
    r1 = c - hi
    mid = r1.astype(BF16).astype(F32)
    lo = (r1 - mid).astype(BF16).astype(F32)
    return hi, mid, lo


def _dot(a, b):
    return jnp.dot(a, b, preferred_element_type=F32)


def _dot_nt(a, b):
    return lax.dot_general(a, b, (((1,), (1,)), ((), ())), preferred_element_type=F32)


def _dot_tn(a, b):
    return lax.dot_general(a, b, (((0,), (0,)), ((), ())), preferred_element_type=F32)


LOG2E = 1.4426950408889634
VT_ROWS = 80


def _store_vt(v_ref, vo_ref):
    vt = v_ref[...].T
    t = vt.shape[1]
    row = lax.broadcasted_iota(jnp.int32, (VT_ROWS - HEAD_DIM, t), 0)
    tail = jnp.where(row == 0, 1.0, 0.0)
    for h in range(N_ATT_HEADS):
        vo_ref[h] = jnp.concatenate([vt[h * HEAD_DIM:(h + 1) * HEAD_DIM, :], tail], axis=0).astype(BF16)


IN_W = 3596
_W_RUNS = ((COL_MQK, 772, 2308),
           (COL_MO, 2316, 2828),
           (COL_FQ, 0, 768),
           (COL_BQ, 2828, 3596))
_W_GATE_RUNS = ((768, 772), (2308, 2316))


def _cols(w_ref, r0, r1, a, b):
    a0 = (a // LANES) * LANES
    b0 = min(-(-b // LANES) * LANES, IN_W)
    return w_ref[r0:r1, a0:b0][:, a - a0:b - a0]


def _inproj_kernel(x_ref, w_ref, o_ref, wb_ref):
    @pl.when(pl.program_id(0) == 0)
    def _():
        rows = 256
        for r0 in range(0, D_MODEL, rows):
            r1 = r0 + rows
            for dst, a, b in _W_RUNS:
                wb_ref[r0:r1, dst:dst + (b - a)] = _cols(w_ref, r0, r1, a, b).astype(BF16)
            gate = [_cols(w_ref, r0, r1, a, b) for a, b in _W_GATE_RUNS]
            used = sum(b - a for a, b in _W_GATE_RUNS)
            gate.append(jnp.zeros((rows, LANES - used), F32))
            wb_ref[r0:r1, COL_GATE:] = jnp.concatenate(gate, axis=1).astype(BF16)

    xb = x_ref[...].astype(BF16)
    n = o_ref.shape[1]
    step = 512
    for j in range(0, n, step):
        w = min(step, n - j)
        o_ref[:, j:j + w] = _dot(xb, wb_ref[:, j:j + w])


def _inproj(x, w_in, layer, tm=512):
    s = x.shape[0]
    return pl.pallas_call(
        _inproj_kernel,
        grid=(s // tm,),
        in_specs=[pl.BlockSpec((tm, D_MODEL), lambda i: (i, 0)),
                  pl.BlockSpec((D_MODEL, IN_W), lambda i: (layer, 0), pipeline_mode=pl.Buffered(1))],
        out_specs=pl.BlockSpec((tm, Z_W), lambda i: (i, 0)),
        out_shape=jax.ShapeDtypeStruct((s, Z_W), F32),
        scratch_shapes=[pltpu.VMEM((D_MODEL, Z_W), BF16)],
        compiler_params=_cparams(("arbitrary",)),
        name="inproj",
    )(x, w_in.reshape(-1, IN_W))


def _log_sigmoid(x):
    return jnp.minimum(x, 0.0) - jnp.log(1.0 + jnp.exp(-jnp.abs(x)))


def _gate_kernel(zg_ref, bias_ref, o_ref, carry_ref, tri_all_ref, tri_chunk_ref):
    t = zg_ref.shape[0]

    @pl.when(pl.program_id(0) == 0)
    def _():
        carry_ref[...] = jnp.zeros_like(carry_ref)
        r = lax.broadcasted_iota(jnp.int32, (t, t), 0)
        c = lax.broadcasted_iota(jnp.int32, (t, t), 1)
        tri = c <= r
        tri_all_ref[...] = jnp.where(tri, 1.0, 0.0).astype(BF16)
        tri_chunk_ref[...] = jnp.where(tri & ((c // ML_CHUNK) == (r // ML_CHUNK)), 1.0, 0.0).astype(BF16)

    g = zg_ref[...] + bias_ref[...]
    ls = _log_sigmoid(g)
    tri_all = tri_all_ref[...]
    tri_chunk = tri_chunk_ref[...]
    hi, mid, lo = _split3(ls)
    parts = [p.astype(BF16) for p in (hi, mid, lo)]
    cum_all = sum(_dot(tri_all, p) for p in parts)
    cum_chunk = sum(_dot(tri_chunk, p) for p in parts)
    carry = carry_ref[...]
    lane = lax.broadcasted_iota(jnp.int32, g.shape, 1)
    o_ref[...] = jnp.where(lane < 4, cum_all + carry, jnp.where(lane < 8, g, cum_chunk))
    carry_ref[...] = carry + cum_all[t - 1:t, :]


def _gate_prep(z, bias_row, tm=512):
    s = z.shape[0]
    return pl.pallas_call(
        _gate_kernel,
        grid=(s // tm,),
        in_specs=[pl.BlockSpec((tm, LANES), lambda i: (i, COL_GATE // LANES)),
                  pl.BlockSpec((1, LANES), lambda i: (0, 0))],
        out_specs=pl.BlockSpec((tm, LANES), lambda i: (i, 0)),
        out_shape=jax.ShapeDtypeStruct((s, LANES), F32),
        scratch_shapes=[pltpu.VMEM((1, LANES), F32), pltpu.VMEM((tm, tm), BF16), pltpu.VMEM((tm, tm), BF16)],
        compiler_params=_cparams(("arbitrary",)),
        name="gate_prep",
    )(z, bias_row)


def _fox_prep_kernel(q_ref, k_ref, v_ref, g_ref, qo_ref, ko_ref, vo_ref):
    g = g_ref[...]
    t = g.shape[0]
    lane = lax.broadcasted_iota(jnp.int32, (t, HEAD_DIM), 1)
    scale = HEAD_DIM ** -0.5 * LOG2E
    for h in range(N_ATT_HEADS):
        hi, mid, lo = _split3(g[:, h:h + 1] * LOG2E)
        aug_q = jnp.where(lane == 0, hi, jnp.where(lane == 1, mid, jnp.where(lane == 2, lo,
                          jnp.where(lane < 6, 1.0, 0.0))))
        aug_k = jnp.where(lane < 3, 1.0, jnp.where(lane == 3, -hi, jnp.where(lane == 4, -mid,
                          jnp.where(lane == 5, -lo, 0.0))))
        sl = slice(h * HEAD_DIM, (h + 1) * HEAD_DIM)
        qo_ref[h] = jnp.concatenate([(q_ref[:, sl] * scale).astype(BF16), aug_q.astype(BF16)], axis=1)
        ko_ref[h] = jnp.concatenate([k_ref[:, sl].astype(BF16), aug_k.astype(BF16)], axis=1)
    _store_vt(v_ref, vo_ref)


def _fox_prep(z, gates, tm=512):
    s = z.shape[0]
    cb = COL_FQ // ATT_W
    head_spec = pl.BlockSpec((N_ATT_HEADS, tm, LANES), lambda i: (0, i, 0))
    return pl.pallas_call(
        _fox_prep_kernel,
        grid=(s // tm,),
        in_specs=[pl.BlockSpec((tm, ATT_W), lambda i: (i, cb)),
                  pl.BlockSpec((tm, ATT_W), lambda i: (i, cb + 1)),
                  pl.BlockSpec((tm, ATT_W), lambda i: (i, cb + 2)),
                  pl.BlockSpec((tm, LANES), lambda i: (i, 0))],
        out_specs=[head_spec, head_spec, pl.BlockSpec((N_ATT_HEADS, VT_ROWS, tm), lambda i: (0, 0, i))],
        out_shape=[jax.ShapeDtypeStruct((N_ATT_HEADS, s, LANES), BF16),
                   jax.ShapeDtypeStruct((N_ATT_HEADS, s, LANES), BF16),
                   jax.ShapeDtypeStruct((N_ATT_HEADS, VT_ROWS, s), BF16)],
        compiler_params=_cparams(("parallel",)),
        name="fox_prep",
    )(z, z, z, gates)


def _rope_table_kernel(pos_ref, inv_ref, sign_ref, cos_ref, sin_ref):
    reps = ATT_W // LANES
    ang = pos_ref[...].astype(F32) * inv_ref[:, 0:LANES]
    cos_ref[...] = jnp.concatenate([jnp.cos(ang)] * reps, axis=1)
    sin_ref[...] = jnp.concatenate([jnp.sin(ang) * sign_ref[:, 0:LANES]] * reps, axis=1)


def _rope_tables(pos_col, inv_row, sign_row, tm=512):
    s = pos_col.shape[0]
    row = pl.BlockSpec((1, ATT_W), lambda i: (0, 0))
    out = pl.BlockSpec((tm, ATT_W), lambda i: (i, 0))
    return pl.pallas_call(
        _rope_table_kernel,
        grid=(s // tm,),
        in_specs=[pl.BlockSpec((tm, 1), lambda i: (i, 0)), row, row],
        out_specs=[out, out],
        out_shape=[jax.ShapeDtypeStruct((s, ATT_W), F32)] * 2,
        compiler_params=_cparams(("parallel",)),
        name="rope_tables",
    )(pos_col, inv_row, sign_row)


def _rope(u, cos, sin_signed):
    half = ROPE_DIM // 2
    lane = lax.broadcasted_iota(jnp.int32, u.shape, 1) % HEAD_DIM
    up = pltpu.roll(u, ATT_W - half, axis=1)
    down = pltpu.roll(u, half, axis=1)
    partner = jnp.where(lane < half, up, down)
    return u * cos + partner * sin_signed


def _moba_rope_kernel(q_ref, k_ref, cos_ref, sin_ref, qo_ref, ko_ref, km_ref):
    cos = cos_ref[...]
    sin = sin_ref[...]
    qo_ref[...] = _rope(q_ref[...], cos, sin)
    kr = _rope(k_ref[...], cos, sin)
    ko_ref[...] = kr
    km_ref[0] = jnp.mean(kr, axis=0, keepdims=True)


def _moba_rope(z, cos, sin):
    s = z.shape[0]
    tm = MOBA_BLOCK
    cb = COL_BQ // ATT_W
    blk = pl.BlockSpec((tm, ATT_W), lambda i: (i, 0))
    return pl.pallas_call(
        _moba_rope_kernel,
        grid=(s // tm,),
        in_specs=[pl.BlockSpec((tm, ATT_W), lambda i: (i, cb)),
                  pl.BlockSpec((tm, ATT_W), lambda i: (i, cb + 1)), blk, blk],
        out_specs=[blk, blk, pl.BlockSpec((1, 1, ATT_W), lambda i: (i, 0, 0))],
        out_shape=[jax.ShapeDtypeStruct((s, ATT_W), F32), jax.ShapeDtypeStruct((s, ATT_W), F32),
                   jax.ShapeDtypeStruct((s // tm, 1, ATT_W), F32)],
        compiler_params=_cparams(("parallel",)),
        name="moba_rope",
    )(z, z, cos, sin)


def _moba_select_kernel(q_ref, k_ref, v_ref, km_ref, qo_ref, ko_ref, vo_ref):
    own = pl.program_id(0)
    q = q_ref[...]
    km = km_ref[...]
    km_hi, q_hi = km.astype(BF16), q.astype(BF16)
    km_lo, q_lo = (km - km_hi.astype(F32)).astype(BF16), (q - q_hi.astype(F32)).astype(BF16)
    gate_t = _dot_nt(km_hi, q_hi) + _dot_nt(km_lo, q_hi) + _dot_nt(km_hi, q_lo)
    t = q.shape[0]
    blk = lax.broadcasted_iota(jnp.int32, (HEAD_DIM, t), 0)
    biases = []
    for h in range(N_ATT_HEADS):
        g = jnp.where(blk < own, gate_t[h * HEAD_DIM:(h + 1) * HEAD_DIM, :], -jnp.inf)
        bias = jnp.where(blk == own, 0.0, NEG_BIG)
        for r in range(MOBA_TOPK):
            mx = jnp.max(g, axis=0, keepdims=True)
            idx = jnp.min(jnp.where(g == mx, blk, HEAD_DIM), axis=0, keepdims=True)
            hit = blk == idx
            bias = jnp.where(hit, jnp.where(r < own, 0.0, bias), bias)
            g = jnp.where(hit, -jnp.inf, g)
        biases.append(bias)
    bias_all = jnp.concatenate(biases, axis=0).T
    lane = lax.broadcasted_iota(jnp.int32, (t, HEAD_DIM), 1)
    scale = HEAD_DIM ** -0.5 * LOG2E
    onehot_own = jnp.where(lane == own, 1.0, 0.0).astype(BF16)
    for h in range(N_ATT_HEADS):
        sl = slice(h * HEAD_DIM, (h + 1) * HEAD_DIM)
        qo_ref[h] = jnp.concatenate([(q[:, sl] * scale).astype(BF16), bias_all[:, sl].astype(BF16)], axis=1)
        ko_ref[h] = jnp.concatenate([k_ref[:, sl].astype(BF16), onehot_own], axis=1)
    _store_vt(v_ref, vo_ref)


def _moba_select(q_rope, k_rope, z, km_mat):
    s = z.shape[0]
    tm = MOBA_BLOCK
    cb = COL_BQ // ATT_W
    blk = pl.BlockSpec((tm, ATT_W), lambda i: (i, 0))
    head_spec = pl.BlockSpec((N_ATT_HEADS, tm, LANES), lambda i: (0, i, 0))
    return pl.pallas_call(
        _moba_select_kernel,
        grid=(s // tm,),
        in_specs=[blk, blk, pl.BlockSpec((tm, ATT_W), lambda i: (i, cb + 2)),
                  pl.BlockSpec((ATT_W, ATT_W), lambda i: (0, 0))],
        out_specs=[head_spec, head_spec, pl.BlockSpec((N_ATT_HEADS, VT_ROWS, tm), lambda i: (0, 0, i))],
        out_shape=[jax.ShapeDtypeStruct((N_ATT_HEADS, s, LANES), BF16),
                   jax.ShapeDtypeStruct((N_ATT_HEADS, s, LANES), BF16),
                   jax.ShapeDtypeStruct((N_ATT_HEADS, VT_ROWS, s), BF16)],
        compiler_params=_cparams(("parallel",)),
        name="moba_select",
    )(q_rope, k_rope, z, km_mat)


def _flash_kernel(qi_tab, ki_tab, *refs, tile, ngroups):
    groups = [refs[3 * g:3 * g + 3] for g in range(ngroups)]
    g_ref, o_ref, m_ref, acc_ref = refs[3 * ngroups:]
    step = pl.program_id(0)
    qi = qi_tab[step]
    ki = ki_tab[step]

    @pl.when(ki == 0)
    def _():
        m_ref[...] = jnp.full_like(m_ref, -jnp.inf)
        acc_ref[...] = jnp.zeros_like(acc_ref)

    units = [(g, h) for g in range(ngroups) for h in range(N_ATT_HEADS)]

    def update(masked):
        if masked:
            key = lax.broadcasted_iota(jnp.int32, (tile, tile), 0)
            qry = lax.broadcasted_iota(jnp.int32, (tile, tile), 1)
            causal = key <= qry
        scores = lambda g, h: _dot_nt(groups[g][1][h], groups[g][0][h])
        st_next = scores(*units[0])
        for u, (g, h) in enumerate(units):
            st = st_next
            if u + 1 < len(units):
                st_next = scores(*units[u + 1])
            if masked:
                st = jnp.where(causal, st, NEG_BIG)
            m_prev = m_ref[u]
            m_new = jnp.maximum(m_prev, jnp.max(st, axis=0, keepdims=True))
            alpha = jnp.exp2(m_prev - m_new)
            p = jnp.exp2(st - m_new).astype(BF16)
            m_ref[u] = m_new
            acc_ref[u] = acc_ref[u] * alpha + _dot(groups[g][2][h], p)

    @pl.when(ki < qi)
    def _():
        update(False)

    @pl.when(ki == qi)
    def _():
        update(True)
        outs = []
        for u in range(len(units)):
            acc = acc_ref[u]
            o = acc[0:HEAD_DIM, :] / acc[HEAD_DIM:HEAD_DIM + 1, :]
            outs.append(o * lax.rsqrt(jnp.mean(o * o, axis=0, keepdims=True) + EPS))
        o_ref[...] = (jnp.concatenate(outs, axis=0).T * g_ref[...]).astype(o_ref.dtype)


def _flash_attention(groups, gain_row, tile=1024):
    ngroups = len(groups)
    width = ATT_W * ngroups
    s = groups[0][2].shape[2]
    n = s // tile
    pairs = [(qi, ki) for qi in range(n) for ki in range(qi + 1)]
    qi_tab = jnp.asarray([p[0] for p in pairs], jnp.int32)
    ki_tab = jnp.asarray([p[1] for p in pairs], jnp.int32)
    group_specs = [pl.BlockSpec((N_ATT_HEADS, tile, LANES), lambda i, qt, kt: (0, qt[i], 0)),
                   pl.BlockSpec((N_ATT_HEADS, tile, LANES), lambda i, qt, kt: (0, kt[i], 0)),
                   pl.BlockSpec((N_ATT_HEADS, VT_ROWS, tile), lambda i, qt, kt: (0, 0, kt[i]))]
    grid_spec = pltpu.PrefetchScalarGridSpec(
        num_scalar_prefetch=2,
        grid=(len(pairs),),
        in_specs=group_specs * ngroups + [pl.BlockSpec((1, width), lambda i, qt, kt: (0, 0))],
        out_specs=pl.BlockSpec((tile, width), lambda i, qt, kt: (qt[i], 0)),
        scratch_shapes=[pltpu.VMEM((N_ATT_HEADS * ngroups, 1, tile), F32),
                        pltpu.VMEM((N_ATT_HEADS * ngroups, VT_ROWS, tile), F32)],
    )
    operands = [a for grp in groups for a in grp]
    return pl.pallas_call(
        functools.partial(_flash_kernel, tile=tile, ngroups=ngroups),
        grid_spec=grid_spec,
        out_shape=jax.ShapeDtypeStruct((s, width), BF16),
        compiler_params=_cparams(("arbitrary",)),
        name="flash_attention",
    )(qi_tab, ki_tab, *operands, gain_row)


def _shift_rows(u, tail, s):
    rolled = pltpu.roll(u, s, axis=0)
    rolled_tail = pltpu.roll(tail, s, axis=0)
    row8 = lax.broadcasted_iota(jnp.int32, tail.shape, 0)
    top = jnp.where(row8 < s, rolled_tail, rolled[0:8])
    return jnp.concatenate([top, rolled[8:]], axis=0)


def _mlstm_kernel(qk_ref, v_ref, o_ref, g_ref, cw_ref, gain_ref, y_ref, tail_ref, c_ref, n_ref, m_ref):
    @pl.when(pl.program_id(0) == 0)
    def _():
        tail_ref[...] = jnp.zeros_like(tail_ref)
        c_ref[...] = jnp.zeros_like(c_ref)
        n_ref[...] = jnp.zeros_like(n_ref)
        m_ref[...] = jnp.zeros_like(m_ref)

    L = ML_CHUNK
    rows = qk_ref.shape[0]
    nch = rows // L
    u = qk_ref[...]
    tail = tail_ref[...]
    cw = cw_ref[...]
    conv = u * cw[CONV_WIDTH - 1:CONV_WIDTH]
    for s in range(1, CONV_WIDTH):
        conv = conv + _shift_rows(u, tail, s) * cw[CONV_WIDTH - 1 - s:CONV_WIDTH - s]
    tail_ref[...] = u[rows - 8:rows]
    qk = conv * jax.nn.sigmoid(conv)

    r = lax.broadcasted_iota(jnp.int32, (L, L), 0)
    c = lax.broadcasted_iota(jnp.int32, (L, L), 1)
    tril = c <= r
    kscale = ML_DIM ** -0.5
    heads = range(ML_HEADS)
    st = []
    for ci in range(nch):
        rs = slice(ci * L, (ci + 1) * L)
        g = g_ref[rs, :]
        gt = g.T
        per_head = []
        for h in heads:
            sl = slice(h * ML_DIM, (h + 1) * ML_DIM)
            qh = qk[rs, sl]
            kh = qk[rs, ML_W + h * ML_DIM:ML_W + (h + 1) * ML_DIM] * kscale
            qb, kb, vb = qh.astype(BF16), kh.astype(BF16), v_ref[rs, sl].astype(BF16)
            b_col = g[:, 8 + h:9 + h]
            i_col = g[:, 4 + h:5 + h]
            b_row = gt[8 + h:9 + h, :]
            i_row = gt[4 + h:5 + h, :]
            dmat = jnp.where(tril, b_col - b_row + i_row, -jnp.inf)
            dmax = jnp.max(dmat, axis=1, keepdims=True)
            a1 = _dot_nt(qb, kb) * jnp.exp(dmat - dmax)
            b_last = b_row[:, L - 1:L]
            g_col = b_last - b_col + i_col
            gmax = jnp.max(g_col, axis=0, keepdims=True)
            kw1 = kh * jnp.exp(g_col - gmax)
            per_head.append(dict(qh=qh, qb=qb, b_col=b_col, dmax=dmax, b_last=b_last, gmax=gmax,
                                 av=_dot(a1.astype(BF16), vb), asum=jnp.sum(a1, axis=1, keepdims=True),
                                 u1=_dot_tn(kw1.astype(BF16), vb), ksum=jnp.sum(kw1, axis=0, keepdims=True)))
        st.append(per_head)
    state = [(c_ref[h], n_ref[h], m_ref[h][:, 0:1]) for h in heads]
    for ci in range(nch):
        rs = slice(ci * L, (ci + 1) * L)
        outs = []
        for h in heads:
            s_ = st[ci][h]
            sl = slice(h * ML_DIM, (h + 1) * ML_DIM)
            cmat, nrow, m_prev = state[h]
            inter = s_["b_col"] + m_prev
            m_t = jnp.maximum(inter, s_["dmax"])
            s_intra = jnp.exp(s_["dmax"] - m_t)
            w_inter = jnp.exp(inter - m_t)
            num = s_intra * s_["av"] + w_inter * _dot(s_["qb"], cmat.astype(BF16))
            den = s_intra * s_["asum"] + w_inter * jnp.sum(s_["qh"] * nrow, axis=1, keepdims=True)
            hh = num / jnp.maximum(jnp.abs(den), jnp.exp(-m_t))
            y = jax.nn.sigmoid(o_ref[rs, sl]) * hh
            outs.append(y * lax.rsqrt(jnp.mean(y * y, axis=1, keepdims=True) + EPS))
        y_ref[rs, :] = (jnp.concatenate(outs, axis=1) * gain_ref[...]).astype(y_ref.dtype)
        for h in heads:
            s_ = st[ci][h]
            cmat, nrow, m_prev = state[h]
            m_new = jnp.maximum(s_["b_last"] + m_prev, s_["gmax"])
            decay = jnp.exp(s_["b_last"] + m_prev - m_new)
            scale = jnp.exp(s_["gmax"] - m_new)
            state[h] = (decay * cmat + scale * s_["u1"], decay * nrow + scale * s_["ksum"], m_new)
    for h in heads:
        c_ref[h], n_ref[h] = state[h][0], state[h][1]
        m_ref[h] = jnp.broadcast_to(state[h][2], (1, LANES))


ML_CHUNKS_PER_STEP = 4


def _mlstm(z, gates, conv_w, gain_row):
    s = z.shape[0]
    L = ML_CHUNK * ML_CHUNKS_PER_STEP
    return pl.pallas_call(
        _mlstm_kernel,
        grid=(s // L,),
        in_specs=[pl.BlockSpec((L, 2 * ML_W), lambda i: (i, COL_MQK // (2 * ML_W))),
                  pl.BlockSpec((L, ML_W), lambda i: (i, COL_MV // ML_W)),
                  pl.BlockSpec((L, ML_W), lambda i: (i, COL_MO // ML_W)),
                  pl.BlockSpec((L, LANES), lambda i: (i, 0)),
                  pl.BlockSpec((CONV_WIDTH, 2 * ML_W), lambda i: (0, 0)),
                  pl.BlockSpec((1, ML_W), lambda i: (0, 0))],
        out_specs=pl.BlockSpec((L, ML_W), lambda i: (i, 0)),
        out_shape=jax.ShapeDtypeStruct((s, ML_W), BF16),
        scratch_shapes=[pltpu.VMEM((8, 2 * ML_W), F32),
                        pltpu.VMEM((ML_HEADS, ML_DIM, ML_DIM), F32),
                        pltpu.VMEM((ML_HEADS, 1, ML_DIM), F32),
                        pltpu.VMEM((ML_HEADS, 1, LANES), F32)],
        compiler_params=_cparams(("arbitrary",)),
        name="mlstm",
    )(z, z, z, gates, conv_w, gain_row)


def _layer_norm(h, g, b):
    mu = jnp.mean(h, axis=1, keepdims=True)
    d = h - mu
    var = jnp.mean(d * d, axis=1, keepdims=True)
    return d * lax.rsqrt(var + EPS) * g + b


def _row_seg(j, rows):
    return pl.ds(j, rows, stride=SUBLANES)


def _outproj_kernel(yf_ref, ym_ref, yb_ref, w_ref, x_ref, g_ref, b_ref, wrh_ref, wrl_ref, br_ref, x1t_ref, lg_ref):
    mix = (_dot(yf_ref[...], w_ref[0:ATT_W, :]) + _dot(ym_ref[...], w_ref[ATT_W:ATT_W + ML_W, :])
           + _dot(yb_ref[...], w_ref[ATT_W + ML_W:, :]))
    x1 = _layer_norm(ALPHA * x_ref[...] + mix, g_ref[...], b_ref[...])
    tm = x1.shape[0]
    for j in range(D_MODEL // LANES):
        x1t_ref[_row_seg(j, tm), :] = x1[:, j * LANES:(j + 1) * LANES]
    hi = x1.astype(BF16)
    lo = (x1 - hi.astype(F32)).astype(BF16)
    wrh = wrh_ref[...]
    lg_ref[...] = _dot(hi, wrh) + _dot(lo, wrh) + _dot(hi, wrl_ref[...]) + br_ref[...]


def _outproj_ln_router(y_att, ym, w_out, x, ln_g, ln_b, w_router, b_router, tm=512):
    s = x.shape[0]
    const = lambda shape: pl.BlockSpec(shape, lambda i: (0, 0))
    rows = lambda w: pl.BlockSpec((tm, w), lambda i: (i, 0))
    yf, yb = y_att, y_att
    w_router_hi = w_router.astype(BF16)
    w_router_lo = (w_router - w_router_hi.astype(F32)).astype(BF16)
    return pl.pallas_call(
        _outproj_kernel,
        grid=(s // tm,),
        in_specs=[rows(ATT_W), rows(ML_W), pl.BlockSpec((tm, ATT_W), lambda i: (i, 1)),
                  const((D_MODEL, D_MODEL)), rows(D_MODEL),
                  const((1, D_MODEL)), const((1, D_MODEL)), const((D_MODEL, LANES)), const((D_MODEL, LANES)),
                  const((1, LANES))],
        out_specs=[pl.BlockSpec((SUBLANES * tm, LANES), lambda i: (i, 0)), rows(LANES)],
        out_shape=[jax.ShapeDtypeStruct((SUBLANES * s, LANES), F32), jax.ShapeDtypeStruct((s, LANES), F32)],
        compiler_params=_cparams(("parallel",)),
        name="outproj_ln_router",
    )(yf, ym, yb, w_out, x, ln_g, ln_b, w_router_hi, w_router_lo, b_router)


def _first_argmax(v, lane):
    mx = jnp.max(v, axis=1, keepdims=True)
    idx = jnp.min(jnp.where(v == mx, lane, LANES), axis=1, keepdims=True)
    return mx, idx


def _route_kernel(lg_ref, info_ref, cnt_ref, carry_ref):
    @pl.when(pl.program_id(0) == 0)
    def _():
        carry_ref[...] = jnp.zeros_like(carry_ref)

    lg = lg_ref[...]
    t = lg.shape[0]
    lane = lax.broadcasted_iota(jnp.int32, lg.shape, 1)
    is_grp = lane < N_GROUPS
    gmax, gsel = _first_argmax(jnp.where(is_grp, lg, -jnp.inf), lane)
    p_grp = 1.0 / jnp.sum(jnp.where(is_grp, jnp.exp(lg - gmax), 0.0), axis=1, keepdims=True)
    lo = N_GROUPS + EXPERTS_PER_GROUP * gsel
    el = jnp.where((lane >= lo) & (lane < lo + EXPERTS_PER_GROUP), lg, -jnp.inf)
    v0, i0 = _first_argmax(el, lane)
    v1, i1 = _first_argmax(jnp.where(lane == i0, -jnp.inf, el), lane)
    ex = jnp.exp(v1 - v0)
    w0 = p_grp / (1.0 + ex)
    w1 = p_grp * ex / (1.0 + ex)
    e0 = i0 - N_GROUPS
    e1 = i1 - N_GROUPS

    cnt = jnp.where((lane == e0) | (lane == e1), 1.0, 0.0)
    r = lax.broadcasted_iota(jnp.int32, (t, t), 0)
    c = lax.broadcasted_iota(jnp.int32, (t, t), 1)
    strict = jnp.where(c < r, 1.0, 0.0).astype(BF16)
    carry = carry_ref[...]
    before = _dot(strict, cnt.astype(BF16)) + carry
    rank0 = jnp.sum(jnp.where(lane == e0, before, 0.0), axis=1, keepdims=True)
    rank1 = jnp.sum(jnp.where(lane == e1, before, 0.0), axis=1, keepdims=True)
    carry = carry + jnp.sum(cnt, axis=0, keepdims=True)
    carry_ref[...] = carry
    cnt_ref[...] = carry
    vals = [e0.astype(F32), e1.astype(F32), w0, w1, rank0, rank1]
    info = jnp.zeros(lg.shape, F32)
    for j, val in enumerate(vals):
        info = jnp.where(lane == j, val, info)
    info_ref[...] = info


def _route(logits, tm=512):
    s = logits.shape[0]
    return pl.pallas_call(
        _route_kernel,
        grid=(s // tm,),
        in_specs=[pl.BlockSpec((tm, LANES), lambda i: (i, 0))],
        out_specs=[pl.BlockSpec((tm, LANES), lambda i: (i, 0)), pl.BlockSpec((1, LANES), lambda i: (0, 0))],
        out_shape=[jax.ShapeDtypeStruct((s, LANES), F32), jax.ShapeDtypeStruct((1, LANES), F32)],
        scratch_shapes=[pltpu.VMEM((1, LANES), F32)],
        compiler_params=_cparams(("arbitrary",)),
        name="route",
    )(logits)


MOE_ROWS = 256


def _moe_kernel(blk_e, nv, src0_ref, srcn_ref, slotp_ref, slotc_ref, x_hbm, wg_ref, wu_ref, wd_ref, out_hbm,
                xbuf0, xbuf1, ybuf0, ybuf1, wgb, wub, wdb, gsem, ssem, *, nblk):
    b = pl.program_id(0)
    xbufs = (xbuf0, xbuf1)
    ybufs = (ybuf0, ybuf1)
    nv_prev2, nv_prev, nv_cur, nv_next = nv[b], nv[b + 1], nv[b + 2], nv[b + 3]

    def row_in(tok8, slot, r):
        return pltpu.make_async_copy(x_hbm.at[pl.ds(pl.multiple_of(tok8, SUBLANES), SUBLANES), :],
                                     xbufs[slot].at[pl.ds(SUBLANES * r, SUBLANES), :], gsem.at[slot])

    def row_out(dst8, slot, r):
        return pltpu.make_async_copy(ybufs[slot].at[pl.ds(SUBLANES * r, SUBLANES), :],
                                     out_hbm.at[pl.ds(pl.multiple_of(dst8, SUBLANES), SUBLANES), :], ssem.at[slot])

    def start_rows(make, idx_ref, slot, n):
        for r in range(MOE_ROWS):
            idx = idx_ref[0, 0, r]

            @pl.when(r < n)
            def _():
                make(idx, slot, r).start(priority=r % 2)

    def wait_rows(make, slot, n):
        @pl.when(n > 0)
        def _():
            rows = pl.multiple_of(n * SUBLANES, SUBLANES)
            if make is row_in:
                pltpu.make_async_copy(x_hbm.at[pl.ds(0, rows), :], xbufs[slot].at[pl.ds(0, rows), :],
                                      gsem.at[slot]).wait()
            else:
                pltpu.make_async_copy(ybufs[slot].at[pl.ds(0, rows), :], out_hbm.at[pl.ds(0, rows), :],
                                      ssem.at[slot]).wait()

    @pl.when(b == 0)
    def _():
        for half in range(2):
            xbufs[half][...] = jnp.zeros_like(xbufs[half])
        start_rows(row_in, src0_ref, 0, nv_cur)

    @pl.when((b == 0) | (blk_e[b] != blk_e[jnp.maximum(b - 1, 0)]))
    def _():
        wgb[...] = wg_ref[0, 0].astype(BF16)
        wub[...] = wu_ref[0, 0].astype(BF16)
        wdb[...] = wd_ref[0, 0].astype(BF16)

    def step(cur):
        nxt = 1 - cur
        wait_rows(row_in, cur, nv_cur)
        start_rows(row_out, slotp_ref, nxt, nv_prev)
        start_rows(row_in, srcn_ref, nxt, nv_next)

        nseg = D_MODEL // LANES
        seg = lambda j: _row_seg(j, MOE_ROWS)
        xb = jnp.concatenate([xbufs[cur][seg(j), :] for j in range(nseg)], axis=1).astype(BF16)
        gate = _dot(xb, wgb[...])
        up = _dot(xb, wub[...])
        hid = (gate * jax.nn.sigmoid(gate) * up).astype(BF16)
        y = _dot(hid, wdb[...])

        wait_rows(row_out, cur, nv_prev2)

        for j in range(nseg):
            ybufs[cur][seg(j), :] = y[:, j * LANES:(j + 1) * LANES]

        @pl.when(b == nblk - 1)
        def _():
            start_rows(row_out, slotc_ref, cur, nv_cur)
            wait_rows(row_out, nxt, nv_prev)
            wait_rows(row_out, cur, nv_cur)

    def drain_step(cur):
        nxt = 1 - cur
        start_rows(row_out, slotp_ref, nxt, nv_prev)
        wait_rows(row_out, cur, nv_prev2)

        @pl.when(b == nblk - 1)
        def _():
            wait_rows(row_out, nxt, nv_prev)

    for parity in range(2):
        pl.when((b % 2 == parity) & (nv_cur > 0))(functools.partial(step, parity))
        pl.when((b % 2 == parity) & (nv_cur == 0))(functools.partial(drain_step, parity))


def _moe_ffn(x1, src_ext, slot_ext, blk_e, nv_ext, w_gate, w_up, w_down, layer):
    s = x1.shape[0] // SUBLANES
    nblk = src_ext.shape[0] - 1
    stage = (SUBLANES * MOE_ROWS, LANES)
    idx_blk = (1, 1, MOE_ROWS)
    smem = pltpu.SMEM
    grid_spec = pltpu.PrefetchScalarGridSpec(
        num_scalar_prefetch=2,
        grid=(nblk,),
        in_specs=[pl.BlockSpec(idx_blk, lambda b, be, nv: (0, 0, 0), memory_space=smem),
                  pl.BlockSpec(idx_blk, lambda b, be, nv: (b + 1, 0, 0), memory_space=smem),
                  pl.BlockSpec(idx_blk, lambda b, be, nv: (b, 0, 0), memory_space=smem),
                  pl.BlockSpec(idx_blk, lambda b, be, nv: (b + 1, 0, 0), memory_space=smem),
                  pl.BlockSpec(memory_space=pl.ANY),
                  pl.BlockSpec((1, 1, D_MODEL, D_EXPERT), lambda b, be, nv: (layer, be[b], 0, 0)),
                  pl.BlockSpec((1, 1, D_MODEL, D_EXPERT), lambda b, be, nv: (layer, be[b], 0, 0)),
                  pl.BlockSpec((1, 1, D_EXPERT, D_MODEL), lambda b, be, nv: (layer, be[b], 0, 0))],
        out_specs=pl.BlockSpec(memory_space=pl.ANY),
        scratch_shapes=[pltpu.VMEM(stage, F32),
                        pltpu.VMEM(stage, F32),
                        pltpu.VMEM(stage, F32),
                        pltpu.VMEM(stage, F32),
                        pltpu.VMEM((D_MODEL, D_EXPERT), BF16),
                        pltpu.VMEM((D_MODEL, D_EXPERT), BF16),
                        pltpu.VMEM((D_EXPERT, D_MODEL), BF16),
                        pltpu.SemaphoreType.DMA((2,)),
                        pltpu.SemaphoreType.DMA((2,))],
    )
    return pl.pallas_call(
        functools.partial(_moe_kernel, nblk=nblk),
        grid_spec=grid_spec,
        out_shape=jax.ShapeDtypeStruct((SUBLANES * 2 * s, LANES), F32),
        compiler_params=_cparams(("arbitrary",)),
        name="moe_ffn",
    )(blk_e, nv_ext, src_ext, src_ext, slot_ext, slot_ext, x1, w_gate, w_up, w_down)


def _dest_kernel(info_ref, pstart_ref, d_ref):
    info = info_ref[...]
    tm = info.shape[0]
    lane = lax.broadcasted_iota(jnp.int32, info.shape, 1)
    ps = pstart_ref[...]
    dests = []
    for k in range(2):
        e = info[:, k:k + 1].astype(jnp.int32)
        dests.append(jnp.sum(jnp.where(lane == e, ps, 0.0), axis=1, keepdims=True) + info[:, 4 + k:5 + k])
    packed = jnp.where(lane == 0, dests[0], jnp.where(lane == 1, dests[1], 0.0))
    for j in range(tm // LANES):
        rows = packed[j * LANES:(j + 1) * LANES, :].T
        for k in range(2):
            d_ref[k, j:j + 1, :] = rows[k:k + 1, :].astype(jnp.int32)


def _dest(info, pstart_row, tm=1024):
    s = info.shape[0]
    return pl.pallas_call(
        _dest_kernel,
        grid=(s // tm,),
        in_specs=[pl.BlockSpec((tm, LANES), lambda i: (i, 0)), pl.BlockSpec((1, LANES), lambda i: (0, 0))],
        out_specs=pl.BlockSpec((2, tm // LANES, LANES), lambda i: (0, i, 0)),
        out_shape=jax.ShapeDtypeStruct((2, s // LANES, LANES), jnp.int32),
        compiler_params=_cparams(("parallel",)),
        name="moe_dest",
    )(info, pstart_row)


def _moe_plan(info, counts_row, s):
    counts = counts_row[0, :N_EXPERTS].astype(jnp.int32)
    padded = ((counts + MOE_ROWS - 1) // MOE_ROWS) * MOE_ROWS
    pends = jnp.cumsum(padded)
    pstarts = pends - padded
    nblk = (2 * s) // MOE_ROWS + N_EXPERTS
    p = nblk * MOE_ROWS
    dest = _dest(info, _pad_lanes(pstarts.astype(F32))).reshape(2 * s)
    out_slot = jnp.zeros((p,), jnp.int32).at[dest].set(jnp.arange(2 * s, dtype=jnp.int32))
    src_tok = jnp.where(out_slot >= s, out_slot - s, out_slot)
    nused = (pends[-1] // MOE_ROWS).astype(jnp.int32)
    blk = jnp.arange(nblk, dtype=jnp.int32)
    blk_start = jnp.minimum(blk, nused - 1) * MOE_ROWS
    blk_e = jnp.minimum(jnp.sum((pends[None, :] <= blk_start[:, None]).astype(jnp.int32), axis=1), N_EXPERTS - 1)
    is_e = jnp.arange(N_EXPERTS)[None, :] == blk_e[:, None]
    seg_start = jnp.sum(jnp.where(is_e, pstarts[None, :], 0), axis=1)
    seg_count = jnp.sum(jnp.where(is_e, counts[None, :], 0), axis=1)
    nvalid = jnp.where(blk < nused, jnp.clip(seg_count - (blk * MOE_ROWS - seg_start), 0, MOE_ROWS), 0)
    nv_ext = jnp.concatenate([jnp.zeros((2,), jnp.int32), nvalid.astype(jnp.int32), jnp.zeros((2,), jnp.int32)])
    pad_blk = jnp.zeros((MOE_ROWS,), jnp.int32)
    src_ext = jnp.concatenate([src_tok, pad_blk]) * SUBLANES
    slot_ext = jnp.concatenate([pad_blk, out_slot]) * SUBLANES
    return src_ext.reshape(nblk + 1, 1, MOE_ROWS), slot_ext.reshape(nblk + 1, 1, MOE_ROWS), blk_e, nv_ext


def _combine_kernel(x1_ref, ya_ref, yb_ref, info_ref, g_ref, b_ref, o_ref):
    info = info_ref[...]
    tm = info.shape[0]
    wide = lambda ref: jnp.concatenate([ref[_row_seg(j, tm), :] for j in range(D_MODEL // LANES)], axis=1)
    ffn = info[:, 2:3] * wide(ya_ref) + info[:, 3:4] * wide(yb_ref)
    o_ref[...] = _layer_norm(ALPHA * wide(x1_ref) + ffn, g_ref[...], b_ref[...])


def _combine_ln(x1_tiles, y2, info, ln_g, ln_b, tm=512):
    s = x1_tiles.shape[0] // SUBLANES
    const = pl.BlockSpec((1, D_MODEL), lambda i: (0, 0))
    return pl.pallas_call(
        _combine_kernel,
        grid=(s // tm,),
        in_specs=[pl.BlockSpec((SUBLANES * tm, LANES), lambda i: (i, 0)),
                  pl.BlockSpec((SUBLANES * tm, LANES), lambda i: (i, 0)),
                  pl.BlockSpec((SUBLANES * tm, LANES), lambda i: (i + s // tm, 0)),
                  pl.BlockSpec((tm, LANES), lambda i: (i, 0)), const, const],
        out_specs=pl.BlockSpec((tm, D_MODEL), lambda i: (i, 0)),
        out_shape=jax.ShapeDtypeStruct((s, D_MODEL), F32),
        compiler_params=_cparams(("parallel",)),
        name="combine_ln",
    )(x1_tiles, y2, y2, info, ln_g, ln_b)


def _pad_lanes(v, width=LANES):
    return jnp.zeros((1, width), F32).at[0, :v.shape[0]].set(v)


def _layer(layer, x, cos, sin, w_in, b_fox_f, b_mlstm_i, b_mlstm_f, conv_w, g_fox, g_mlstm, g_moba, w_out,
           ln1_g, ln1_b, w_grp, b_grp, w_exp_router, b_exp_router, w_gate, w_up, w_down, ln2_g, ln2_b):
    s = x.shape[0]
    gate_bias = _pad_lanes(jnp.concatenate([b_fox_f, b_mlstm_i, b_mlstm_f]))

    z = _inproj(x, w_in, layer)
    gates = _gate_prep(z, gate_bias)

    fox_ops = _fox_prep(z, gates)

    q_rope, k_rope, kmean = _moba_rope(z, cos, sin)
    km = kmean[:, 0, :].reshape(s // MOBA_BLOCK, N_ATT_HEADS, HEAD_DIM)
    km_mat = jnp.zeros((N_ATT_HEADS, HEAD_DIM, N_ATT_HEADS, HEAD_DIM), F32)
    for h in range(N_ATT_HEADS):
        km_mat = km_mat.at[h, :s // MOBA_BLOCK, h, :].set(km[:, h, :])
    km_mat = km_mat.reshape(ATT_W, ATT_W)
    moba_ops = _moba_select(q_rope, k_rope, z, km_mat)
    y_att = _flash_attention([fox_ops, moba_ops], jnp.concatenate([g_fox, g_moba])[None, :])

    y_mlstm = _mlstm(z, gates, conv_w, g_mlstm[None, :])

    w_router = jnp.zeros((D_MODEL, LANES), F32)
    w_router = w_router.at[:, :N_GROUPS].set(w_grp)
    w_router = w_router.at[:, N_GROUPS:N_GROUPS + N_EXPERTS].set(w_exp_router.reshape(D_MODEL, N_EXPERTS))
    b_router = _pad_lanes(jnp.concatenate([b_grp, b_exp_router.reshape(N_EXPERTS)]))
    x1_tiles, logits = _outproj_ln_router(y_att, y_mlstm, w_out.astype(BF16), x, ln1_g[None, :],
                                          ln1_b[None, :], w_router, b_router)

    info, counts = _route(logits)
    src_ext, slot_ext, blk_e, nv_ext = _moe_plan(info, counts, s)
    y2 = _moe_ffn(x1_tiles, src_ext, slot_ext, blk_e, nv_ext, w_gate, w_up, w_down, layer)
    return _combine_ln(x1_tiles, y2, info, ln2_g[None, :], ln2_b[None, :])


def kernel(x, positions, w_in, b_fox_f, b_mlstm_i, b_mlstm_f, conv_w, g_fox, g_mlstm, g_moba, w_out, ln1_g, ln1_b, w_grp, b_grp, w_exp_router, b_exp_router, w_gate, w_up, w_down, ln2_g, ln2_b):
    assert x.shape[0] == 1
    xs = x[0]
    d = jnp.arange(ATT_W) % HEAD_DIM
    half = ROPE_DIM // 2
    inv = 1.0 / (ROPE_THETA ** (jnp.arange(0, ROPE_DIM, 2, dtype=F32) / ROPE_DIM))
    inv_row = jnp.where(d < ROPE_DIM, inv[d % half], 0.0)[None, :].astype(F32)
    sign_row = jnp.where(d < half, -1.0, 1.0)[None, :].astype(F32)
    cos, sin = _rope_tables(positions[0][:, None], inv_row, sign_row)
    for l in range(DEPTH):
        xs = _layer(l, xs, cos, sin, w_in, b_fox_f[l], b_mlstm_i[l], b_mlstm_f[l], conv_w[l], g_fox[l],
                    g_mlstm[l], g_moba[l], w_out[l], ln1_g[l], ln1_b[l], w_grp[l], b_grp[l],
                    w_exp_router[l], b_exp_router[l], w_gate, w_up, w_down, ln2_g[l], ln2_b[l])
    return xs[None]
```

```python
import functools

import jax
import jax.numpy as jnp
from jax import lax
from jax.experimental import pallas as pl
from jax.experimental.pallas import tpu as pltpu

D_MODEL = 1024
DEPTH = 2
HEAD_DIM = 64
N_ATT_HEADS = 4
ATT_W = N_ATT_HEADS * HEAD_DIM
ML_HEADS = 4
ML_DIM = 128
ML_W = ML_HEADS * ML_DIM
ML_CHUNK = 128
CONV_WIDTH = 4
ROPE_DIM = 16
ROPE_THETA = 500000.0
MOBA_BLOCK = 256
MOBA_TOPK = 3
N_GROUPS = 4
EXPERTS_PER_GROUP = 8
N_EXPERTS = N_GROUPS * EXPERTS_PER_GROUP
D_EXPERT = 512
ALPHA = (2 * DEPTH) ** 0.25
EPS = 1e-5

LANES = 128
SUBLANES = 8
NEG_BIG = -1e30
VMEM_LIMIT = 56 * 1024 * 1024

COL_MQK = 0
COL_MV = 1024
COL_MO = 1536
COL_FQ = 2048
COL_BQ = 2816
COL_GATE = 3584
Z_W = 3712

F32 = jnp.float32
BF16 = jnp.bfloat16


def _cparams(sem):
    return pltpu.CompilerParams(dimension_semantics=sem, vmem_limit_bytes=VMEM_LIMIT)


def _split3(c):
    hi = c.astype(BF16).astype(F32)
    r1 = c - hi
    mid = r1.astype(BF16).astype(F32)
    lo = (r1 - mid).astype(BF16).astype(F32)
    return hi, mid, lo


def _dot(a, b):
    return jnp.dot(a, b, preferred_element_type=F32)


def _dot_nt(a, b):
    return lax.dot_general(a, b, (((1,), (1,)), ((), ())), preferred_element_type=F32)


def _dot_tn(a, b):
    return lax.dot_general(a, b, (((0,), (0,)), ((), ())), preferred_element_type=F32)


LOG2E = 1.4426950408889634
VT_ROWS = 80


def _store_vt(v_ref, vo_ref):
    vt = v_ref[...].T
    t = vt.shape[1]
    row = lax.broadcasted_iota(jnp.int32, (VT_ROWS - HEAD_DIM, t), 0)
    tail = jnp.where(row == 0, 1.0, 0.0)
    for h in range(N_ATT_HEADS):
        vo_ref[h] = jnp.concatenate([vt[h * HEAD_DIM:(h + 1) * HEAD_DIM, :], tail], axis=0).astype(BF16)


IN_W = 3596
_W_RUNS = ((COL_MQK, 772, 2308),
           (COL_MO, 2316, 2828),
           (COL_FQ, 0, 768),
           (COL_BQ, 2828, 3596))
_W_GATE_RUNS = ((768, 772), (2308, 2316))


def _cols(w_ref, r0, r1, a, b):
    a0 = (a // LANES) * LANES
    b0 = min(-(-b // LANES) * LANES, IN_W)
    return w_ref[r0:r1, a0:b0][:, a - a0:b - a0]


def _inproj_kernel(x_ref, w_ref, o_ref, wb_ref):
    @pl.when(pl.program_id(0) == 0)
    def _():
        rows = 256
        for r0 in range(0, D_MODEL, rows):
            r1 = r0 + rows
            for dst, a, b in _W_RUNS:
                wb_ref[r0:r1, dst:dst + (b - a)] = _cols(w_ref, r0, r1, a, b).astype(BF16)
            gate = [_cols(w_ref, r0, r1, a, b) for a, b in _W_GATE_RUNS]
            used = sum(b - a for a, b in _W_GATE_RUNS)
            gate.append(jnp.zeros((rows, LANES - used), F32))
            wb_ref[r0:r1, COL_GATE:] = jnp.concatenate(gate, axis=1).astype(BF16)

    xb = x_ref[...].astype(BF16)
    n = o_ref.shape[1]
    step = 512
    for j in range(0, n, step):
        w = min(step, n - j)
        o_ref[:, j:j + w] = _dot(xb, wb_ref[:, j:j + w])


def _inproj(x, w_in, layer, tm=512):
    s = x.shape[0]
    return pl.pallas_call(
        _inproj_kernel,
        grid=(s // tm,),
        in_specs=[pl.BlockSpec((tm, D_MODEL), lambda i: (i, 0)),
                  pl.BlockSpec((D_MODEL, IN_W), lambda i: (layer, 0), pipeline_mode=pl.Buffered(1))],
        out_specs=pl.BlockSpec((tm, Z_W), lambda i: (i, 0)),
        out_shape=jax.ShapeDtypeStruct((s, Z_W), F32),
        scratch_shapes=[pltpu.VMEM((D_MODEL, Z_W), BF16)],
        compiler_params=_cparams(("arbitrary",)),
        name="inproj",
    )(x, w_in.reshape(-1, IN_W))


def _log_sigmoid(x):
    return jnp.minimum(x, 0.0) - jnp.log(1.0 + jnp.exp(-jnp.abs(x)))


def _gate_kernel(zg_ref, bias_ref, o_ref, carry_ref, tri_all_ref, tri_chunk_ref):
    t = zg_ref.shape[0]

    @pl.when(pl.program_id(0) == 0)
    def _():
        carry_ref[...] = jnp.zeros_like(carry_ref)
        r = lax.broadcasted_iota(jnp.int32, (t, t), 0)
        c = lax.broadcasted_iota(jnp.int32, (t, t), 1)
        tri = c <= r
        tri_all_ref[...] = jnp.where(tri, 1.0, 0.0).astype(BF16)
        tri_chunk_ref[...] = jnp.where(tri & ((c // ML_CHUNK) == (r // ML_CHUNK)), 1.0, 0.0).astype(BF16)

    g = zg_ref[...] + bias_ref[...]
    ls = _log_sigmoid(g)
    tri_all = tri_all_ref[...]
    tri_chunk = tri_chunk_ref[...]
    hi, mid, lo = _split3(ls)
    parts = [p.astype(BF16) for p in (hi, mid, lo)]
    cum_all = sum(_dot(tri_all, p) for p in parts)
    cum_chunk = sum(_dot(tri_chunk, p) for p in parts)
    carry = carry_ref[...]
    lane = lax.broadcasted_iota(jnp.int32, g.shape, 1)
    o_ref[...] = jnp.where(lane < 4, cum_all + carry, jnp.where(lane < 8, g, cum_chunk))
    carry_ref[...] = carry + cum_all[t - 1:t, :]


def _gate_prep(z, bias_row, tm=512):
    s = z.shape[0]
    return pl.pallas_call(
        _gate_kernel,
        grid=(s // tm,),
        in_specs=[pl.BlockSpec((tm, LANES), lambda i: (i, COL_GATE // LANES)),
                  pl.BlockSpec((1, LANES), lambda i: (0, 0))],
        out_specs=pl.BlockSpec((tm, LANES), lambda i: (i, 0)),
        out_shape=jax.ShapeDtypeStruct((s, LANES), F32),
        scratch_shapes=[pltpu.VMEM((1, LANES), F32), pltpu.VMEM((tm, tm), BF16), pltpu.VMEM((tm, tm), BF16)],
        compiler_params=_cparams(("arbitrary",)),
        name="gate_prep",
    )(z, bias_row)


def _fox_prep_kernel(q_ref, k_ref, v_ref, g_ref, qo_ref, ko_ref, vo_ref):
    g = g_ref[...]
    t = g.shape[0]
    lane = lax.broadcasted_iota(jnp.int32, (t, LANES), 1)
    scale = HEAD_DIM ** -0.5 * LOG2E
    for h in range(N_ATT_HEADS):
        hi, mid, lo = _split3(g[:, h:h + 1] * LOG2E)
        data_low = h % 2 == 0
        e = lane - (HEAD_DIM if data_low else 0)
        is_data = (lane < HEAD_DIM) if data_low else (lane >= HEAD_DIM)
        aug_q = jnp.where(e == 0, hi, jnp.where(e == 1, mid, jnp.where(e == 2, lo,
                          jnp.where((e >= 3) & (e < 6), 1.0, 0.0))))
        aug_k = jnp.where((e >= 0) & (e < 3), 1.0, jnp.where(e == 3, -hi, jnp.where(e == 4, -mid,
                          jnp.where(e == 5, -lo, 0.0))))
        grp = slice((h // 2) * LANES, (h // 2 + 1) * LANES)
        qo_ref[h] = jnp.where(is_data, q_ref[:, grp] * scale, aug_q).astype(BF16)
        ko_ref[h] = jnp.where(is_data, k_ref[:, grp], aug_k).astype(BF16)
    _store_vt(v_ref, vo_ref)


def _fox_prep(z, gates, tm=512):
    s = z.shape[0]
    cb = COL_FQ // ATT_W
    head_spec = pl.BlockSpec((N_ATT_HEADS, tm, LANES), lambda i: (0, i, 0))
    return pl.pallas_call(
        _fox_prep_kernel,
        grid=(s // tm,),
        in_specs=[pl.BlockSpec((tm, ATT_W), lambda i: (i, cb)),
                  pl.BlockSpec((tm, ATT_W), lambda i: (i, cb + 1)),
                  pl.BlockSpec((tm, ATT_W), lambda i: (i, cb + 2)),
                  pl.BlockSpec((tm, LANES), lambda i: (i, 0))],
        out_specs=[head_spec, head_spec, pl.BlockSpec((N_ATT_HEADS, VT_ROWS, tm), lambda i: (0, 0, i))],
        out_shape=[jax.ShapeDtypeStruct((N_ATT_HEADS, s, LANES), BF16),
                   jax.ShapeDtypeStruct((N_ATT_HEADS, s, LANES), BF16),
                   jax.ShapeDtypeStruct((N_ATT_HEADS, VT_ROWS, s), BF16)],
        compiler_params=_cparams(("parallel",)),
        name="fox_prep",
    )(z, z, z, gates)


def _rope_table_kernel(pos_ref, inv_ref, sign_ref, cos_ref, sin_ref):
    reps = ATT_W // LANES
    ang = pos_ref[...].astype(F32) * inv_ref[:, 0:LANES]
    cos_ref[...] = jnp.concatenate([jnp.cos(ang)] * reps, axis=1)
    sin_ref[...] = jnp.concatenate([jnp.sin(ang) * sign_ref[:, 0:LANES]] * reps, axis=1)


def _rope_tables(pos_col, inv_row, sign_row, tm=512):
    s = pos_col.shape[0]
    row = pl.BlockSpec((1, ATT_W), lambda i: (0, 0))
    out = pl.BlockSpec((tm, ATT_W), lambda i: (i, 0))
    return pl.pallas_call(
        _rope_table_kernel,
        grid=(s // tm,),
        in_specs=[pl.BlockSpec((tm, 1), lambda i: (i, 0)), row, row],
        out_specs=[out, out],
        out_shape=[jax.ShapeDtypeStruct((s, ATT_W), F32)] * 2,
        compiler_params=_cparams(("parallel",)),
        name="rope_tables",
    )(pos_col, inv_row, sign_row)


def _rope(u, cos, sin_signed):
    half = ROPE_DIM // 2
    lane = lax.broadcasted_iota(jnp.int32, u.shape, 1) % HEAD_DIM
    up = pltpu.roll(u, ATT_W - half, axis=1)
    down = pltpu.roll(u, half, axis=1)
    partner = jnp.where(lane < half, up, down)
    return u * cos + partner * sin_signed


def _moba_rope_kernel(q_ref, k_ref, cos_ref, sin_ref, qo_ref, ko_ref, km_ref):
    cos = cos_ref[...]
    sin = sin_ref[...]
    qo_ref[...] = _rope(q_ref[...], cos, sin)
    kr = _rope(k_ref[...], cos, sin)
    ko_ref[...] = kr
    km_ref[0] = jnp.mean(kr, axis=0, keepdims=True)


def _moba_rope(z, cos, sin):
    s = z.shape[0]
    tm = MOBA_BLOCK
    cb = COL_BQ // ATT_W
    blk = pl.BlockSpec((tm, ATT_W), lambda i: (i, 0))
    return pl.pallas_call(
        _moba_rope_kernel,
        grid=(s // tm,),
        in_specs=[pl.BlockSpec((tm, ATT_W), lambda i: (i, cb)),
                  pl.BlockSpec((tm, ATT_W), lambda i: (i, cb + 1)), blk, blk],
        out_specs=[blk, blk, pl.BlockSpec((1, 1, ATT_W), lambda i: (i, 0, 0))],
        out_shape=[jax.ShapeDtypeStruct((s, ATT_W), F32), jax.ShapeDtypeStruct((s, ATT_W), F32),
                   jax.ShapeDtypeStruct((s // tm, 1, ATT_W), F32)],
        compiler_params=_cparams(("parallel",)),
        name="moba_rope",
    )(z, z, cos, sin)


def _moba_select_kernel(q_ref, k_ref, v_ref, km_ref, qo_ref, ko_ref, vo_ref):
    own = pl.program_id(0)
    q = q_ref[...]
    km = km_ref[...]
    km_hi, q_hi = km.astype(BF16), q.astype(BF16)
    km_lo, q_lo = (km - km_hi.astype(F32)).astype(BF16), (q - q_hi.astype(F32)).astype(BF16)
    gate_t = _dot_nt(km_hi, q_hi) + _dot_nt(km_lo, q_hi) + _dot_nt(km_hi, q_lo)
    t = q.shape[0]
    blk = lax.broadcasted_iota(jnp.int32, (HEAD_DIM, t), 0)
    biases = []
    for h in range(N_ATT_HEADS):
        g = jnp.where(blk < own, gate_t[h * HEAD_DIM:(h + 1) * HEAD_DIM, :], -jnp.inf)
        bias = jnp.where(blk == own, 0.0, NEG_BIG)
        for r in range(MOBA_TOPK):
            mx = jnp.max(g, axis=0, keepdims=True)
            idx = jnp.min(jnp.where(g == mx, blk, HEAD_DIM), axis=0, keepdims=True)
            hit = blk == idx
            bias = jnp.where(hit, jnp.where(r < own, 0.0, bias), bias)
            g = jnp.where(hit, -jnp.inf, g)
        biases.append(bias)
    swapped = [biases[h ^ 1] for h in range(N_ATT_HEADS)]
    bias_all = jnp.concatenate(swapped, axis=0).T
    lane = lax.broadcasted_iota(jnp.int32, (t, LANES), 1)
    scale = HEAD_DIM ** -0.5 * LOG2E
    for h in range(N_ATT_HEADS):
        data_low = h % 2 == 0
        is_data = (lane < HEAD_DIM) if data_low else (lane >= HEAD_DIM)
        own_lane = own + (HEAD_DIM if data_low else 0)
        grp = slice((h // 2) * LANES, (h // 2 + 1) * LANES)
        qo_ref[h] = jnp.where(is_data, q[:, grp] * scale, bias_all[:, grp]).astype(BF16)
        ko_ref[h] = jnp.where(is_data, k_ref[:, grp], jnp.where(lane == own_lane, 1.0, 0.0)).astype(BF16)
    _store_vt(v_ref, vo_ref)


def _moba_select(q_rope, k_rope, z, km_mat):
    s = z.shape[0]
    tm = MOBA_BLOCK
    cb = COL_BQ // ATT_W
    blk = pl.BlockSpec((tm, ATT_W), lambda i: (i, 0))
    head_spec = pl.BlockSpec((N_ATT_HEADS, tm, LANES), lambda i: (0, i, 0))
    return pl.pallas_call(
        _moba_select_kernel,
        grid=(s // tm,),
        in_specs=[blk, blk, pl.BlockSpec((tm, ATT_W), lambda i: (i, cb + 2)),
                  pl.BlockSpec((ATT_W, ATT_W), lambda i: (0, 0))],
        out_specs=[head_spec, head_spec, pl.BlockSpec((N_ATT_HEADS, VT_ROWS, tm), lambda i: (0, 0, i))],
        out_shape=[jax.ShapeDtypeStruct((N_ATT_HEADS, s, LANES), BF16),
                   jax.ShapeDtypeStruct((N_ATT_HEADS, s, LANES), BF16),
                   jax.ShapeDtypeStruct((N_ATT_HEADS, VT_ROWS, s), BF16)],
        compiler_params=_cparams(("parallel",)),
        name="moba_select",
    )(q_rope, k_rope, z, km_mat)


def _flash_kernel(qi_tab, ki_tab, *refs, tile, ngroups):
    groups = [refs[3 * g:3 * g + 3] for g in range(ngroups)]
    g_ref, o_ref, m_ref, acc_ref = refs[3 * ngroups:]
    step = pl.program_id(0)
    qi = qi_tab[step]
    ki = ki_tab[step]

    @pl.when(ki == 0)
    def _():
        m_ref[...] = jnp.full_like(m_ref, -jnp.inf)
        acc_ref[...] = jnp.zeros_like(acc_ref)

    units = [(g, h) for g in range(ngroups) for h in range(N_ATT_HEADS)]

    def update(masked):
        if masked:
            key = lax.broadcasted_iota(jnp.int32, (tile, tile), 0)
            qry = lax.broadcasted_iota(jnp.int32, (tile, tile), 1)
            causal = key <= qry
        scores = lambda g, h: _dot_nt(groups[g][1][h], groups[g][0][h])
        st_next = scores(*units[0])
        for u, (g, h) in enumerate(units):
            st = st_next
            if u + 1 < len(units):
                st_next = scores(*units[u + 1])
            if masked:
                st = jnp.where(causal, st, NEG_BIG)
            m_prev = m_ref[u]
            m_new = jnp.maximum(m_prev, jnp.max(st, axis=0, keepdims=True))
            alpha = jnp.exp2(m_prev - m_new)
            p = jnp.exp2(st - m_new).astype(BF16)
            m_ref[u] = m_new
            acc_ref[u] = acc_ref[u] * alpha + _dot(groups[g][2][h], p)

    @pl.when(ki < qi)
    def _():
        update(False)

    @pl.when(ki == qi)
    def _():
        update(True)
        outs = []
        for u in range(len(units)):
            acc = acc_ref[u]
            o = acc[0:HEAD_DIM, :] / acc[HEAD_DIM:HEAD_DIM + 1, :]
            outs.append(o * lax.rsqrt(jnp.mean(o * o, axis=0, keepdims=True) + EPS))
        o_ref[...] = (jnp.concatenate(outs, axis=0).T * g_ref[...]).astype(o_ref.dtype)


def _flash_attention(groups, gain_row, tile=1024):
    ngroups = len(groups)
    width = ATT_W * ngroups
    s = groups[0][2].shape[2]
    n = s // tile
    pairs = [(qi, ki) for qi in range(n) for ki in range(qi + 1)]
    qi_tab = jnp.asarray([p[0] for p in pairs], jnp.int32)
    ki_tab = jnp.asarray([p[1] for p in pairs], jnp.int32)
    group_specs = [pl.BlockSpec((N_ATT_HEADS, tile, LANES), lambda i, qt, kt: (0, qt[i], 0)),
                   pl.BlockSpec((N_ATT_HEADS, tile, LANES), lambda i, qt, kt: (0, kt[i], 0)),
                   pl.BlockSpec((N_ATT_HEADS, VT_ROWS, tile), lambda i, qt, kt: (0, 0, kt[i]))]
    grid_spec = pltpu.PrefetchScalarGridSpec(
        num_scalar_prefetch=2,
        grid=(len(pairs),),
        in_specs=group_specs * ngroups + [pl.BlockSpec((1, width), lambda i, qt, kt: (0, 0))],
        out_specs=pl.BlockSpec((tile, width), lambda i, qt, kt: (qt[i], 0)),
        scratch_shapes=[pltpu.VMEM((N_ATT_HEADS * ngroups, 1, tile), F32),
                        pltpu.VMEM((N_ATT_HEADS * ngroups, VT_ROWS, tile), F32)],
    )
    operands = [a for grp in groups for a in grp]
    return pl.pallas_call(
        functools.partial(_flash_kernel, tile=tile, ngroups=ngroups),
        grid_spec=grid_spec,
        out_shape=jax.ShapeDtypeStruct((s, width), BF16),
        compiler_params=_cparams(("arbitrary",)),
        name="flash_attention",
    )(qi_tab, ki_tab, *operands, gain_row)


def _shift_rows(u, tail, s):
    rolled = pltpu.roll(u, s, axis=0)
    rolled_tail = pltpu.roll(tail, s, axis=0)
    row8 = lax.broadcasted_iota(jnp.int32, tail.shape, 0)
    top = jnp.where(row8 < s, rolled_tail, rolled[0:8])
    return jnp.concatenate([top, rolled[8:]], axis=0)


def _mlstm_kernel(qk_ref, v_ref, o_ref, g_ref, cw_ref, gain_ref, y_ref, tail_ref, c_ref, n_ref, m_ref):
    @pl.when(pl.program_id(0) == 0)
    def _():
        tail_ref[...] = jnp.zeros_like(tail_ref)
        c_ref[...] = jnp.zeros_like(c_ref)
        n_ref[...] = jnp.zeros_like(n_ref)
        m_ref[...] = jnp.zeros_like(m_ref)

    L = ML_CHUNK
    rows = qk_ref.shape[0]
    nch = rows // L
    u = qk_ref[...]
    tail = tail_ref[...]
    cw = cw_ref[...]
    conv = u * cw[CONV_WIDTH - 1:CONV_WIDTH]
    for s in range(1, CONV_WIDTH):
        conv = conv + _shift_rows(u, tail, s) * cw[CONV_WIDTH - 1 - s:CONV_WIDTH - s]
    tail_ref[...] = u[rows - 8:rows]
    qk = conv * jax.nn.sigmoid(conv)

    r = lax.broadcasted_iota(jnp.int32, (L, L), 0)
    c = lax.broadcasted_iota(jnp.int32, (L, L), 1)
    tril = c <= r
    kscale = ML_DIM ** -0.5
    heads = range(ML_HEADS)
    st = []
    for ci in range(nch):
        rs = slice(ci * L, (ci + 1) * L)
        g = g_ref[rs, :]
        gt = g.T
        per_head = []
        for h in heads:
            sl = slice(h * ML_DIM, (h + 1) * ML_DIM)
            qh = qk[rs, sl]
            kh = qk[rs, ML_W + h * ML_DIM:ML_W + (h + 1) * ML_DIM] * kscale
            qb, kb, vb = qh.astype(BF16), kh.astype(BF16), v_ref[rs, sl].astype(BF16)
            b_col = g[:, 8 + h:9 + h]
            i_col = g[:, 4 + h:5 + h]
            b_row = gt[8 + h:9 + h, :]
            i_row = gt[4 + h:5 + h, :]
            dmat = jnp.where(tril, b_col - b_row + i_row, -jnp.inf)
            dmax = jnp.max(dmat, axis=1, keepdims=True)
            a1 = _dot_nt(qb, kb) * jnp.exp(dmat - dmax)
            b_last = b_row[:, L - 1:L]
            g_col = b_last - b_col + i_col
            gmax = jnp.max(g_col, axis=0, keepdims=True)
            kw1 = kh * jnp.exp(g_col - gmax)
            per_head.append(dict(qh=qh, qb=qb, b_col=b_col, dmax=dmax, b_last=b_last, gmax=gmax,
                                 av=_dot(a1.astype(BF16), vb), asum=jnp.sum(a1, axis=1, keepdims=True),
                                 u1=_dot_tn(kw1.astype(BF16), vb), ksum=jnp.sum(kw1, axis=0, keepdims=True)))
        st.append(per_head)
    state = [(c_ref[h], n_ref[h], m_ref[h][:, 0:1]) for h in heads]
    for ci in range(nch):
        rs = slice(ci * L, (ci + 1) * L)
        outs = []
        for h in heads:
            s_ = st[ci][h]
            sl = slice(h * ML_DIM, (h + 1) * ML_DIM)
            cmat, nrow, m_prev = state[h]
            inter = s_["b_col"] + m_prev
            m_t = jnp.maximum(inter, s_["dmax"])
            s_intra = jnp.exp(s_["dmax"] - m_t)
            w_inter = jnp.exp(inter - m_t)
            num = s_intra * s_["av"] + w_inter * _dot(s_["qb"], cmat.astype(BF16))
            den = s_intra * s_["asum"] + w_inter * jnp.sum(s_["qh"] * nrow, axis=1, keepdims=True)
            hh = num / jnp.maximum(jnp.abs(den), jnp.exp(-m_t))
            y = jax.nn.sigmoid(o_ref[rs, sl]) * hh
            outs.append(y * lax.rsqrt(jnp.mean(y * y, axis=1, keepdims=True) + EPS))
        y_ref[rs, :] = (jnp.concatenate(outs, axis=1) * gain_ref[...]).astype(y_ref.dtype)
        for h in heads:
            s_ = st[ci][h]
            cmat, nrow, m_prev = state[h]
            m_new = jnp.maximum(s_["b_last"] + m_prev, s_["gmax"])
            decay = jnp.exp(s_["b_last"] + m_prev - m_new)
            scale = jnp.exp(s_["gmax"] - m_new)
            state[h] = (decay * cmat + scale * s_["u1"], decay * nrow + scale * s_["ksum"], m_new)
    for h in heads:
        c_ref[h], n_ref[h] = state[h][0], state[h][1]
        m_ref[h] = jnp.broadcast_to(state[h][2], (1, LANES))


ML_CHUNKS_PER_STEP = 4


def _mlstm(z, gates, conv_w, gain_row):
    s = z.shape[0]
    L = ML_CHUNK * ML_CHUNKS_PER_STEP
    return pl.pallas_call(
        _mlstm_kernel,
        grid=(s // L,),
        in_specs=[pl.BlockSpec((L, 2 * ML_W), lambda i: (i, COL_MQK // (2 * ML_W))),
                  pl.BlockSpec((L, ML_W), lambda i: (i, COL_MV // ML_W)),
                  pl.BlockSpec((L, ML_W), lambda i: (i, COL_MO // ML_W)),
                  pl.BlockSpec((L, LANES), lambda i: (i, 0)),
                  pl.BlockSpec((CONV_WIDTH, 2 * ML_W), lambda i: (0, 0)),
                  pl.BlockSpec((1, ML_W), lambda i: (0, 0))],
        out_specs=pl.BlockSpec((L, ML_W), lambda i: (i, 0)),
        out_shape=jax.ShapeDtypeStruct((s, ML_W), BF16),
        scratch_shapes=[pltpu.VMEM((8, 2 * ML_W), F32),
                        pltpu.VMEM((ML_HEADS, ML_DIM, ML_DIM), F32),
                        pltpu.VMEM((ML_HEADS, 1, ML_DIM), F32),
                        pltpu.VMEM((ML_HEADS, 1, LANES), F32)],
        compiler_params=_cparams(("arbitrary",)),
        name="mlstm",
    )(z, z, z, gates, conv_w, gain_row)


def _layer_norm(h, g, b):
    mu = jnp.mean(h, axis=1, keepdims=True)
    d = h - mu
    var = jnp.mean(d * d, axis=1, keepdims=True)
    return d * lax.rsqrt(var + EPS) * g + b


def _row_seg(j, rows):
    return pl.ds(j, rows, stride=SUBLANES)


def _outproj_kernel(yf_ref, ym_ref, yb_ref, w_ref, x_ref, g_ref, b_ref, wrh_ref, wrl_ref, br_ref, x1t_ref, lg_ref):
    mix = (_dot(yf_ref[...], w_ref[0:ATT_W, :]) + _dot(ym_ref[...], w_ref[ATT_W:ATT_W + ML_W, :])
           + _dot(yb_ref[...], w_ref[ATT_W + ML_W:, :]))
    x1 = _layer_norm(ALPHA * x_ref[...] + mix, g_ref[...], b_ref[...])
    tm = x1.shape[0]
    for j in range(D_MODEL // LANES):
        x1t_ref[_row_seg(j, tm), :] = x1[:, j * LANES:(j + 1) * LANES]
    hi = x1.astype(BF16)
    lo = (x1 - hi.astype(F32)).astype(BF16)
    wrh = wrh_ref[...]
    lg_ref[...] = _dot(hi, wrh) + _dot(lo, wrh) + _dot(hi, wrl_ref[...]) + br_ref[...]


def _outproj_ln_router(y_att, ym, w_out, x, ln_g, ln_b, w_router, b_router, tm=512):
    s = x.shape[0]
    const = lambda shape: pl.BlockSpec(shape, lambda i: (0, 0))
    rows = lambda w: pl.BlockSpec((tm, w), lambda i: (i, 0))
    yf, yb = y_att, y_att
    w_router_hi = w_router.astype(BF16)
    w_router_lo = (w_router - w_router_hi.astype(F32)).astype(BF16)
    return pl.pallas_call(
        _outproj_kernel,
        grid=(s // tm,),
        in_specs=[rows(ATT_W), rows(ML_W), pl.BlockSpec((tm, ATT_W), lambda i: (i, 1)),
                  const((D_MODEL, D_MODEL)), rows(D_MODEL),
                  const((1, D_MODEL)), const((1, D_MODEL)), const((D_MODEL, LANES)), const((D_MODEL, LANES)),
                  const((1, LANES))],
        out_specs=[pl.BlockSpec((SUBLANES * tm, LANES), lambda i: (i, 0)), rows(LANES)],
        out_shape=[jax.ShapeDtypeStruct((SUBLANES * s, LANES), F32), jax.ShapeDtypeStruct((s, LANES), F32)],
        compiler_params=_cparams(("parallel",)),
        name="outproj_ln_router",
    )(yf, ym, yb, w_out, x, ln_g, ln_b, w_router_hi, w_router_lo, b_router)


def _first_argmax(v, lane):
    mx = jnp.max(v, axis=1, keepdims=True)
    idx = jnp.min(jnp.where(v == mx, lane, LANES), axis=1, keepdims=True)
    return mx, idx


def _route_kernel(lg_ref, info_ref, cnt_ref, carry_ref):
    @pl.when(pl.program_id(0) == 0)
    def _():
        carry_ref[...] = jnp.zeros_like(carry_ref)

    lg = lg_ref[...]
    t = lg.shape[0]
    lane = lax.broadcasted_iota(jnp.int32, lg.shape, 1)
    is_grp = lane < N_GROUPS
    gmax, gsel = _first_argmax(jnp.where(is_grp, lg, -jnp.inf), lane)
    p_grp = 1.0 / jnp.sum(jnp.where(is_grp, jnp.exp(lg - gmax), 0.0), axis=1, keepdims=True)
    lo = N_GROUPS + EXPERTS_PER_GROUP * gsel
    el = jnp.where((lane >= lo) & (lane < lo + EXPERTS_PER_GROUP), lg, -jnp.inf)
    v0, i0 = _first_argmax(el, lane)
    v1, i1 = _first_argmax(jnp.where(lane == i0, -jnp.inf, el), lane)
    ex = jnp.exp(v1 - v0)
    w0 = p_grp / (1.0 + ex)
    w1 = p_grp * ex / (1.0 + ex)
    e0 = i0 - N_GROUPS
    e1 = i1 - N_GROUPS

    cnt = jnp.where((lane == e0) | (lane == e1), 1.0, 0.0)
    r = lax.broadcasted_iota(jnp.int32, (t, t), 0)
    c = lax.broadcasted_iota(jnp.int32, (t, t), 1)
    strict = jnp.where(c < r, 1.0, 0.0).astype(BF16)
    carry = carry_ref[...]
    before = _dot(strict, cnt.astype(BF16)) + carry
    rank0 = jnp.sum(jnp.where(lane == e0, before, 0.0), axis=1, keepdims=True)
    rank1 = jnp.sum(jnp.where(lane == e1, before, 0.0), axis=1, keepdims=True)
    carry = carry + jnp.sum(cnt, axis=0, keepdims=True)
    carry_ref[...] = carry
    cnt_ref[...] = carry
    vals = [e0.astype(F32), e1.astype(F32), w0, w1, rank0, rank1]
    info = jnp.zeros(lg.shape, F32)
    for j, val in enumerate(vals):
        info = jnp.where(lane == j, val, info)
    info_ref[...] = info


def _route(logits, tm=512):
    s = logits.shape[0]
    return pl.pallas_call(
        _route_kernel,
        grid=(s // tm,),
        in_specs=[pl.BlockSpec((tm, LANES), lambda i: (i, 0))],
        out_specs=[pl.BlockSpec((tm, LANES), lambda i: (i, 0)), pl.BlockSpec((1, LANES), lambda i: (0, 0))],
        out_shape=[jax.ShapeDtypeStruct((s, LANES), F32), jax.ShapeDtypeStruct((1, LANES), F32)],
        scratch_shapes=[pltpu.VMEM((1, LANES), F32)],
        compiler_params=_cparams(("arbitrary",)),
        name="route",
    )(logits)


MOE_ROWS = 256


def _moe_kernel(blk_e, nv, src0_ref, srcn_ref, slotp_ref, slotc_ref, x_hbm, wg_ref, wu_ref, wd_ref, out_hbm,
                xbuf0, xbuf1, ybuf0, ybuf1, wgb, wub, wdb, gsem, ssem, *, nblk):
    b = pl.program_id(0)
    xbufs = (xbuf0, xbuf1)
    ybufs = (ybuf0, ybuf1)
    nv_prev2, nv_prev, nv_cur, nv_next = nv[b], nv[b + 1], nv[b + 2], nv[b + 3]

    def row_in(tok8, slot, r):
        return pltpu.make_async_copy(x_hbm.at[pl.ds(pl.multiple_of(tok8, SUBLANES), SUBLANES), :],
                                     xbufs[slot].at[pl.ds(SUBLANES * r, SUBLANES), :], gsem.at[slot])

    def row_out(dst8, slot, r):
        return pltpu.make_async_copy(ybufs[slot].at[pl.ds(SUBLANES * r, SUBLANES), :],
                                     out_hbm.at[pl.ds(pl.multiple_of(dst8, SUBLANES), SUBLANES), :], ssem.at[slot])

    def start_rows(make, idx_ref, slot, n):
        for r in range(MOE_ROWS):
            idx = idx_ref[0, 0, r]

            @pl.when(r < n)
            def _():
                make(idx, slot, r).start(priority=r % 2)

    def wait_rows(make, slot, n):
        @pl.when(n > 0)
        def _():
            rows = pl.multiple_of(n * SUBLANES, SUBLANES)
            if make is row_in:
                pltpu.make_async_copy(x_hbm.at[pl.ds(0, rows), :], xbufs[slot].at[pl.ds(0, rows), :],
                                      gsem.at[slot]).wait()
            else:
                pltpu.make_async_copy(ybufs[slot].at[pl.ds(0, rows), :], out_hbm.at[pl.ds(0, rows), :],
                                      ssem.at[slot]).wait()

    @pl.when(b == 0)
    def _():
        for half in range(2):
            xbufs[half][...] = jnp.zeros_like(xbufs[half])
        start_rows(row_in, src0_ref, 0, nv_cur)

    @pl.when((b == 0) | (blk_e[b] != blk_e[jnp.maximum(b - 1, 0)]))
    def _():
        wgb[...] = wg_ref[0, 0].astype(BF16)
        wub[...] = wu_ref[0, 0].astype(BF16)
        wdb[...] = wd_ref[0, 0].astype(BF16)

    def step(cur):
        nxt = 1 - cur
        wait_rows(row_in, cur, nv_cur)
        start_rows(row_out, slotp_ref, nxt, nv_prev)
        start_rows(row_in, srcn_ref, nxt, nv_next)

        nseg = D_MODEL // LANES
        seg = lambda j: _row_seg(j, MOE_ROWS)
        xb = jnp.concatenate([xbufs[cur][seg(j), :] for j in range(nseg)], axis=1).astype(BF16)
        gate = _dot(xb, wgb[...])
        up = _dot(xb, wub[...])
        hid = (gate * jax.nn.sigmoid(gate) * up).astype(BF16)
        y = _dot(hid, wdb[...])

        wait_rows(row_out, cur, nv_prev2)

        for j in range(nseg):
            ybufs[cur][seg(j), :] = y[:, j * LANES:(j + 1) * LANES]

        @pl.when(b == nblk - 1)
        def _():
            start_rows(row_out, slotc_ref, cur, nv_cur)
            wait_rows(row_out, nxt, nv_prev)
            wait_rows(row_out, cur, nv_cur)

    def drain_step(cur):
        nxt = 1 - cur
        start_rows(row_out, slotp_ref, nxt, nv_prev)
        wait_rows(row_out, cur, nv_prev2)

        @pl.when(b == nblk - 1)
        def _():
            wait_rows(row_out, nxt, nv_prev)

    for parity in range(2):
        pl.when((b % 2 == parity) & (nv_cur > 0))(functools.partial(step, parity))
        pl.when((b % 2 == parity) & (nv_cur == 0))(functools.partial(drain_step, parity))


def _moe_ffn(x1, src_ext, slot_ext, blk_e, nv_ext, w_gate, w_up, w_down, layer):
    s = x1.shape[0] // SUBLANES
    nblk = src_ext.shape[0] - 1
    stage = (SUBLANES * MOE_ROWS, LANES)
    idx_blk = (1, 1, MOE_ROWS)
    smem = pltpu.SMEM
    grid_spec = pltpu.PrefetchScalarGridSpec(
        num_scalar_prefetch=2,
        grid=(nblk,),
        in_specs=[pl.BlockSpec(idx_blk, lambda b, be, nv: (0, 0, 0), memory_space=smem),
                  pl.BlockSpec(idx_blk, lambda b, be, nv: (b + 1, 0, 0), memory_space=smem),
                  pl.BlockSpec(idx_blk, lambda b, be, nv: (b, 0, 0), memory_space=smem),
                  pl.BlockSpec(idx_blk, lambda b, be, nv: (b + 1, 0, 0), memory_space=smem),
                  pl.BlockSpec(memory_space=pl.ANY),
                  pl.BlockSpec((1, 1, D_MODEL, D_EXPERT), lambda b, be, nv: (layer, be[b], 0, 0)),
                  pl.BlockSpec((1, 1, D_MODEL, D_EXPERT), lambda b, be, nv: (layer, be[b], 0, 0)),
                  pl.BlockSpec((1, 1, D_EXPERT, D_MODEL), lambda b, be, nv: (layer, be[b], 0, 0))],
        out_specs=pl.BlockSpec(memory_space=pl.ANY),
        scratch_shapes=[pltpu.VMEM(stage, F32),
                        pltpu.VMEM(stage, F32),
                        pltpu.VMEM(stage, F32),
                        pltpu.VMEM(stage, F32),
                        pltpu.VMEM((D_MODEL, D_EXPERT), BF16),
                        pltpu.VMEM((D_MODEL, D_EXPERT), BF16),
                        pltpu.VMEM((D_EXPERT, D_MODEL), BF16),
                        pltpu.SemaphoreType.DMA((2,)),
                        pltpu.SemaphoreType.DMA((2,))],
    )
    return pl.pallas_call(
        functools.partial(_moe_kernel, nblk=nblk),
        grid_spec=grid_spec,
        out_shape=jax.ShapeDtypeStruct((SUBLANES * 2 * s, LANES), F32),
        compiler_params=_cparams(("arbitrary",)),
        name="moe_ffn",
    )(blk_e, nv_ext, src_ext, src_ext, slot_ext, slot_ext, x1, w_gate, w_up, w_down)


def _dest_kernel(info_ref, pstart_ref, d_ref):
    info = info_ref[...]
    tm = info.shape[0]
    lane = lax.broadcasted_iota(jnp.int32, info.shape, 1)
    ps = pstart_ref[...]
    dests = []
    for k in range(2):
        e = info[:, k:k + 1].astype(jnp.int32)
        dests.append(jnp.sum(jnp.where(lane == e, ps, 0.0), axis=1, keepdims=True) + info[:, 4 + k:5 + k])
    packed = jnp.where(lane == 0, dests[0], jnp.where(lane == 1, dests[1], 0.0))
    for j in range(tm // LANES):
        rows = packed[j * LANES:(j + 1) * LANES, :].T
        for k in range(2):
            d_ref[k, j:j + 1, :] = rows[k:k + 1, :].astype(jnp.int32)


def _dest(info, pstart_row, tm=1024):
    s = info.shape[0]
    return pl.pallas_call(
        _dest_kernel,
        grid=(s // tm,),
        in_specs=[pl.BlockSpec((tm, LANES), lambda i: (i, 0)), pl.BlockSpec((1, LANES), lambda i: (0, 0))],
        out_specs=pl.BlockSpec((2, tm // LANES, LANES), lambda i: (0, i, 0)),
        out_shape=jax.ShapeDtypeStruct((2, s // LANES, LANES), jnp.int32),
        compiler_params=_cparams(("parallel",)),
        name="moe_dest",
    )(info, pstart_row)


def _moe_plan(info, counts_row, s):
    counts = counts_row[0, :N_EXPERTS].astype(jnp.int32)
    padded = ((counts + MOE_ROWS - 1) // MOE_ROWS) * MOE_ROWS
    pends = jnp.cumsum(padded)
    pstarts = pends - padded
    nblk = (2 * s) // MOE_ROWS + N_EXPERTS
    p = nblk * MOE_ROWS
    dest = _dest(info, _pad_lanes(pstarts.astype(F32))).reshape(2 * s)
    out_slot = jnp.zeros((p,), jnp.int32).at[dest].set(jnp.arange(2 * s, dtype=jnp.int32))
    src_tok = jnp.where(out_slot >= s, out_slot - s, out_slot)
    nused = (pends[-1] // MOE_ROWS).astype(jnp.int32)
    blk = jnp.arange(nblk, dtype=jnp.int32)
    blk_start = jnp.minimum(blk, nused - 1) * MOE_ROWS
    blk_e = jnp.minimum(jnp.sum((pends[None, :] <= blk_start[:, None]).astype(jnp.int32), axis=1), N_EXPERTS - 1)
    is_e = jnp.arange(N_EXPERTS)[None, :] == blk_e[:, None]
    seg_start = jnp.sum(jnp.where(is_e, pstarts[None, :], 0), axis=1)
    seg_count = jnp.sum(jnp.where(is_e, counts[None, :], 0), axis=1)
    nvalid = jnp.where(blk < nused, jnp.clip(seg_count - (blk * MOE_ROWS - seg_start), 0, MOE_ROWS), 0)
    nv_ext = jnp.concatenate([jnp.zeros((2,), jnp.int32), nvalid.astype(jnp.int32), jnp.zeros((2,), jnp.int32)])
    pad_blk = jnp.zeros((MOE_ROWS,), jnp.int32)
    src_ext = jnp.concatenate([src_tok, pad_blk]) * SUBLANES
    slot_ext = jnp.concatenate([pad_blk, out_slot]) * SUBLANES
    return src_ext.reshape(nblk + 1, 1, MOE_ROWS), slot_ext.reshape(nblk + 1, 1, MOE_ROWS), blk_e, nv_ext


def _combine_kernel(x1_ref, ya_ref, yb_ref, info_ref, g_ref, b_ref, o_ref):
    info = info_ref[...]
    tm = info.shape[0]
    wide = lambda ref: jnp.concatenate([ref[_row_seg(j, tm), :] for j in range(D_MODEL // LANES)], axis=1)
    ffn = info[:, 2:3] * wide(ya_ref) + info[:, 3:4] * wide(yb_ref)
    o_ref[...] = _layer_norm(ALPHA * wide(x1_ref) + ffn, g_ref[...], b_ref[...])


def _combine_ln(x1_tiles, y2, info, ln_g, ln_b, tm=512):
    s = x1_tiles.shape[0] // SUBLANES
    const = pl.BlockSpec((1, D_MODEL), lambda i: (0, 0))
    return pl.pallas_call(
        _combine_kernel,
        grid=(s // tm,),
        in_specs=[pl.BlockSpec((SUBLANES * tm, LANES), lambda i: (i, 0)),
                  pl.BlockSpec((SUBLANES * tm, LANES), lambda i: (i, 0)),
                  pl.BlockSpec((SUBLANES * tm, LANES), lambda i: (i + s // tm, 0)),
                  pl.BlockSpec((tm, LANES), lambda i: (i, 0)), const, const],
        out_specs=pl.BlockSpec((tm, D_MODEL), lambda i: (i, 0)),
        out_shape=jax.ShapeDtypeStruct((s, D_MODEL), F32),
        compiler_params=_cparams(("parallel",)),
        name="combine_ln",
    )(x1_tiles, y2, y2, info, ln_g, ln_b)


def _pad_lanes(v, width=LANES):
    return jnp.zeros((1, width), F32).at[0, :v.shape[0]].set(v)


def _layer(layer, x, cos, sin, w_in, b_fox_f, b_mlstm_i, b_mlstm_f, conv_w, g_fox, g_mlstm, g_moba, w_out,
           ln1_g, ln1_b, w_grp, b_grp, w_exp_router, b_exp_router, w_gate, w_up, w_down, ln2_g, ln2_b):
    s = x.shape[0]
    gate_bias = _pad_lanes(jnp.concatenate([b_fox_f, b_mlstm_i, b_mlstm_f]))

    z = _inproj(x, w_in, layer)
    gates = _gate_prep(z, gate_bias)

    fox_ops = _fox_prep(z, gates)

    q_rope, k_rope, kmean = _moba_rope(z, cos, sin)
    km = kmean[:, 0, :].reshape(s // MOBA_BLOCK, N_ATT_HEADS, HEAD_DIM)
    km_mat = jnp.zeros((N_ATT_HEADS, HEAD_DIM, N_ATT_HEADS, HEAD_DIM), F32)
    for h in range(N_ATT_HEADS):
        km_mat = km_mat.at[h, :s // MOBA_BLOCK, h, :].set(km[:, h, :])
    km_mat = km_mat.reshape(ATT_W, ATT_W)
    moba_ops = _moba_select(q_rope, k_rope, z, km_mat)
    y_att = _flash_attention([fox_ops, moba_ops], jnp.concatenate([g_fox, g_moba])[None, :])

    y_mlstm = _mlstm(z, gates, conv_w, g_mlstm[None, :])

    w_router = jnp.zeros((D_MODEL, LANES), F32)
    w_router = w_router.at[:, :N_GROUPS].set(w_grp)
    w_router = w_router.at[:, N_GROUPS:N_GROUPS + N_EXPERTS].set(w_exp_router.reshape(D_MODEL, N_EXPERTS))
    b_router = _pad_lanes(jnp.concatenate([b_grp, b_exp_router.reshape(N_EXPERTS)]))
    x1_tiles, logits = _outproj_ln_router(y_att, y_mlstm, w_out.astype(BF16), x, ln1_g[None, :],
                                          ln1_b[None, :], w_router, b_router)

    info, counts = _route(logits)
    src_ext, slot_ext, blk_e, nv_ext = _moe_plan(info, counts, s)
    y2 = _moe_ffn(x1_tiles, src_ext, slot_ext, blk_e, nv_ext, w_gate, w_up, w_down, layer)
    return _combine_ln(x1_tiles, y2, info, ln2_g[None, :], ln2_b[None, :])


def kernel(x, positions, w_in, b_fox_f, b_mlstm_i, b_mlstm_f, conv_w, g_fox, g_mlstm, g_moba, w_out, ln1_g, ln1_b, w_grp, b_grp, w_exp_router, b_exp_router, w_gate, w_up, w_down, ln2_g, ln2_b):
    assert x.shape[0] == 1
    xs = x[0]
    d = jnp.arange(ATT_W) % HEAD_DIM
    half = ROPE_DIM // 2
    inv = 1.0 / (ROPE_THETA ** (jnp.arange(0, ROPE_DIM, 2, dtype=F32) / ROPE_DIM))
    inv_row = jnp.where(d < ROPE_DIM, inv[d % half], 0.0)[None, :].astype(F32)
    sign_row = jnp.where(d < half, -1.0, 1.0)[None, :].astype(F32)
    cos, sin = _rope_tables(positions[0][:, None], inv_row, sign_row)
    for l in range(DEPTH):
        xs = _layer(l, xs, cos, sin, w_in, b_fox_f[l], b_mlstm_i[l], b_mlstm_f[l], conv_w[l], g_fox[l],
                    g_mlstm[l], g_moba[l], w_out[l], ln1_g[l], ln1_b[l], w_grp[l], b_grp[l],
                    w_exp_router[l], b_exp_router[l], w_gate, w_up, w_down, ln2_g[l], ln2_b[l])
    return xs[None]
```

```python
import functools

import jax
import jax.numpy as jnp
from jax import lax
from jax.experimental import pallas as pl
from jax.experimental.pallas import tpu as pltpu

D_MODEL = 1024
DEPTH = 2
HEAD_DIM = 64
N_ATT_HEADS = 4
ATT_W = N_ATT_HEADS * HEAD_DIM
ML_HEADS = 4
ML_DIM = 128
ML_W = ML_HEADS * ML_DIM
ML_CHUNK = 128
CONV_WIDTH = 4
ROPE_DIM = 16
ROPE_THETA = 500000.0
MOBA_BLOCK = 256
MOBA_TOPK = 3
N_GROUPS = 4
EXPERTS_PER_GROUP = 8
N_EXPERTS = N_GROUPS * EXPERTS_PER_GROUP
D_EXPERT = 512
ALPHA = (2 * DEPTH) ** 0.25
EPS = 1e-5

LANES = 128
SUBLANES = 8
NEG_BIG = -1e30
VMEM_LIMIT = 56 * 1024 * 1024

COL_MQK = 0
COL_MV = 1024
COL_MO = 1536
COL_FQ = 2048
COL_BQ = 2816
COL_GATE = 3584
Z_W = 3712

F32 = jnp.float32
BF16 = jnp.bfloat16


def _cparams(sem):
    return pltpu.CompilerParams(dimension_semantics=sem, vmem_limit_bytes=VMEM_LIMIT)


def _split3(c):
    hi = c.astype(BF16).astype(F32)
    r1 = c - hi
    mid = r1.astype(BF16).astype(F32)
    lo = (r1 - mid).astype(BF16).astype(F32)
    return hi, mid, lo


def _dot(a, b):
    return jnp.dot(a, b, preferred_element_type=F32)


def _dot_nt(a, b):
    return lax.dot_general(a, b, (((1,), (1,)), ((), ())), preferred_element_type=F32)


def _dot_tn(a, b):
    return lax.dot_general(a, b, (((0,), (0,)), ((), ())), preferred_element_type=F32)


LOG2E = 1.4426950408889634
VT_ROWS = 80


def _store_vt(v_ref, vo_ref):
    vt = v_ref[...].T
    t = vt.shape[1]
    row = lax.broadcasted_iota(jnp.int32, (VT_ROWS - HEAD_DIM, t), 0)
    tail = jnp.where(row == 0, 1.0, 0.0)
    for h in range(N_ATT_HEADS):
        vo_ref[h] = jnp.concatenate([vt[h * HEAD_DIM:(h + 1) * HEAD_DIM, :], tail], axis=0).astype(BF16)


IN_W = 3596
_W_RUNS = ((COL_MQK, 772, 2308),
           (COL_MO, 2316, 2828),
           (COL_FQ, 0, 768),
           (COL_BQ, 2828, 3596))
_W_GATE_RUNS = ((768, 772), (2308, 2316))


def _cols(w_ref, r0, r1, a, b):
    a0 = (a // LANES) * LANES
    b0 = min(-(-b // LANES) * LANES, IN_W)
    return w_ref[r0:r1, a0:b0][:, a - a0:b - a0]


def _inproj_kernel(x_ref, w_ref, o_ref, wb_ref):
    @pl.when(pl.program_id(0) == 0)
    def _():
        rows = 256
        for r0 in range(0, D_MODEL, rows):
            r1 = r0 + rows
            for dst, a, b in _W_RUNS:
                wb_ref[r0:r1, dst:dst + (b - a)] = _cols(w_ref, r0, r1, a, b).astype(BF16)
            gate = [_cols(w_ref, r0, r1, a, b) for a, b in _W_GATE_RUNS]
            used = sum(b - a for a, b in _W_GATE_RUNS)
            gate.append(jnp.zeros((rows, LANES - used), F32))
            wb_ref[r0:r1, COL_GATE:] = jnp.concatenate(gate, axis=1).astype(BF16)

    xb = x_ref[...].astype(BF16)
    n = o_ref.shape[1]
    step = 512
    for j in range(0, n, step):
        w = min(step, n - j)
        o_ref[:, j:j + w] = _dot(xb, wb_ref[:, j:j + w])


def _inproj(x, w_in, layer, tm=512):
    s = x.shape[0]
    return pl.pallas_call(
        _inproj_kernel,
        grid=(s // tm,),
        in_specs=[pl.BlockSpec((tm, D_MODEL), lambda i: (i, 0)),
                  pl.BlockSpec((D_MODEL, IN_W), lambda i: (layer, 0), pipeline_mode=pl.Buffered(1))],
        out_specs=pl.BlockSpec((tm, Z_W), lambda i: (i, 0)),
        out_shape=jax.ShapeDtypeStruct((s, Z_W), F32),
        scratch_shapes=[pltpu.VMEM((D_MODEL, Z_W), BF16)],
        compiler_params=_cparams(("arbitrary",)),
        name="inproj",
    )(x, w_in.reshape(-1, IN_W))


def _log_sigmoid(x):
    return jnp.minimum(x, 0.0) - jnp.log(1.0 + jnp.exp(-jnp.abs(x)))


def _gate_kernel(zg_ref, bias_ref, o_ref, carry_ref, tri_all_ref, tri_chunk_ref):
    t = zg_ref.shape[0]

    @pl.when(pl.program_id(0) == 0)
    def _():
        carry_ref[...] = jnp.zeros_like(carry_ref)
        r = lax.broadcasted_iota(jnp.int32, (t, t), 0)
        c = lax.broadcasted_iota(jnp.int32, (t, t), 1)
        tri = c <= r
        tri_all_ref[...] = jnp.where(tri, 1.0, 0.0).astype(BF16)
        tri_chunk_ref[...] = jnp.where(tri & ((c // ML_CHUNK) == (r // ML_CHUNK)), 1.0, 0.0).astype(BF16)

    g = zg_ref[...] + bias_ref[...]
    ls = _log_sigmoid(g)
    tri_all = tri_all_ref[...]
    tri_chunk = tri_chunk_ref[...]
    hi, mid, lo = _split3(ls)
    parts = [p.astype(BF16) for p in (hi, mid, lo)]
    cum_all = sum(_dot(tri_all, p) for p in parts)
    cum_chunk = sum(_dot(tri_chunk, p) for p in parts)
    carry = carry_ref[...]
    lane = lax.broadcasted_iota(jnp.int32, g.shape, 1)
    o_ref[...] = jnp.where(lane < 4, cum_all + carry, jnp.where(lane < 8, g, cum_chunk))
    carry_ref[...] = carry + cum_all[t - 1:t, :]


def _gate_prep(z, bias_row, tm=512):
    s = z.shape[0]
    return pl.pallas_call(
        _gate_kernel,
        grid=(s // tm,),
        in_specs=[pl.BlockSpec((tm, LANES), lambda i: (i, COL_GATE // LANES)),
                  pl.BlockSpec((1, LANES), lambda i: (0, 0))],
        out_specs=pl.BlockSpec((tm, LANES), lambda i: (i, 0)),
        out_shape=jax.ShapeDtypeStruct((s, LANES), F32),
        scratch_shapes=[pltpu.VMEM((1, LANES), F32), pltpu.VMEM((tm, tm), BF16), pltpu.VMEM((tm, tm), BF16)],
        compiler_params=_cparams(("arbitrary",)),
        name="gate_prep",
    )(z, bias_row)


def _fox_prep_kernel(q_ref, k_ref, v_ref, g_ref, qo_ref, ko_ref, vo_ref):
    g = g_ref[...]
    t = g.shape[0]
    lane = lax.broadcasted_iota(jnp.int32, (t, LANES), 1)
    scale = HEAD_DIM ** -0.5 * LOG2E
    for h in range(N_ATT_HEADS):
        hi, mid, lo = _split3(g[:, h:h + 1] * LOG2E)
        data_low = h % 2 == 0
        e = lane - (HEAD_DIM if data_low else 0)
        is_data = (lane < HEAD_DIM) if data_low else (lane >= HEAD_DIM)
        aug_q = jnp.where(e == 0, hi, jnp.where(e == 1, mid, jnp.where(e == 2, lo,
                          jnp.where((e >= 3) & (e < 6), 1.0, 0.0))))
        aug_k = jnp.where((e >= 0) & (e < 3), 1.0, jnp.where(e == 3, -hi, jnp.where(e == 4, -mid,
                          jnp.where(e == 5, -lo, 0.0))))
        grp = slice((h // 2) * LANES, (h // 2 + 1) * LANES)
        qo_ref[h] = jnp.where(is_data, q_ref[:, grp] * scale, aug_q).astype(BF16)
        ko_ref[h] = jnp.where(is_data, k_ref[:, grp], aug_k).astype(BF16)
    _store_vt(v_ref, vo_ref)


def _fox_prep(z, gates, tm=512):
    s = z.shape[0]
    cb = COL_FQ // ATT_W
    head_spec = pl.BlockSpec((N_ATT_HEADS, tm, LANES), lambda i: (0, i, 0))
    return pl.pallas_call(
        _fox_prep_kernel,
        grid=(s // tm,),
        in_specs=[pl.BlockSpec((tm, ATT_W), lambda i: (i, cb)),
                  pl.BlockSpec((tm, ATT_W), lambda i: (i, cb + 1)),
                  pl.BlockSpec((tm, ATT_W), lambda i: (i, cb + 2)),
                  pl.BlockSpec((tm, LANES), lambda i: (i, 0))],
        out_specs=[head_spec, head_spec, pl.BlockSpec((N_ATT_HEADS, VT_ROWS, tm), lambda i: (0, 0, i))],
        out_shape=[jax.ShapeDtypeStruct((N_ATT_HEADS, s, LANES), BF16),
                   jax.ShapeDtypeStruct((N_ATT_HEADS, s, LANES), BF16),
                   jax.ShapeDtypeStruct((N_ATT_HEADS, VT_ROWS, s), BF16)],
        compiler_params=_cparams(("parallel",)),
        name="fox_prep",
    )(z, z, z, gates)


def _rope_table_kernel(pos_ref, inv_ref, sign_ref, cos_ref, sin_ref):
    reps = ATT_W // LANES
    ang = pos_ref[...].astype(F32) * inv_ref[:, 0:LANES]
    cos_ref[...] = jnp.concatenate([jnp.cos(ang)] * reps, axis=1)
    sin_ref[...] = jnp.concatenate([jnp.sin(ang) * sign_ref[:, 0:LANES]] * reps, axis=1)


def _rope_tables(pos_col, inv_row, sign_row, tm=512):
    s = pos_col.shape[0]
    row = pl.BlockSpec((1, ATT_W), lambda i: (0, 0))
    out = pl.BlockSpec((tm, ATT_W), lambda i: (i, 0))
    return pl.pallas_call(
        _rope_table_kernel,
        grid=(s // tm,),
        in_specs=[pl.BlockSpec((tm, 1), lambda i: (i, 0)), row, row],
        out_specs=[out, out],
        out_shape=[jax.ShapeDtypeStruct((s, ATT_W), F32)] * 2,
        compiler_params=_cparams(("parallel",)),
        name="rope_tables",
    )(pos_col, inv_row, sign_row)


def _rope(u, cos, sin_signed, perm):
    partner = sum(_dot(t.astype(BF16), perm) for t in _split3(u))
    return u * cos + partner * sin_signed


def _moba_rope_kernel(q_ref, k_ref, cos_ref, sin_ref, perm_ref, qo_ref, ko_ref, km_ref):
    cos = cos_ref[...]
    sin = sin_ref[...]
    perm = perm_ref[...]
    qo_ref[...] = _rope(q_ref[...], cos, sin, perm)
    kr = _rope(k_ref[...], cos, sin, perm)
    ko_ref[...] = kr
    km_ref[0] = jnp.mean(kr, axis=0, keepdims=True)


def _moba_rope(z, cos, sin):
    s = z.shape[0]
    tm = MOBA_BLOCK
    cb = COL_BQ // ATT_W
    blk = pl.BlockSpec((tm, ATT_W), lambda i: (i, 0))
    half = ROPE_DIM // 2
    dst = jnp.arange(ATT_W)
    d = dst % HEAD_DIM
    src = jnp.where(d < half, dst + half, dst - half)
    perm = ((jnp.arange(ATT_W)[:, None] == src[None, :]) & (d < ROPE_DIM)[None, :]).astype(BF16)
    return pl.pallas_call(
        _moba_rope_kernel,
        grid=(s // tm,),
        in_specs=[pl.BlockSpec((tm, ATT_W), lambda i: (i, cb)),
                  pl.BlockSpec((tm, ATT_W), lambda i: (i, cb + 1)), blk, blk,
                  pl.BlockSpec((ATT_W, ATT_W), lambda i: (0, 0))],
        out_specs=[blk, blk, pl.BlockSpec((1, 1, ATT_W), lambda i: (i, 0, 0))],
        out_shape=[jax.ShapeDtypeStruct((s, ATT_W), F32), jax.ShapeDtypeStruct((s, ATT_W), F32),
                   jax.ShapeDtypeStruct((s // tm, 1, ATT_W), F32)],
        compiler_params=_cparams(("parallel",)),
        name="moba_rope",
    )(z, z, cos, sin, perm)


def _moba_select_kernel(q_ref, k_ref, v_ref, km_ref, qo_ref, ko_ref, vo_ref):
    own = pl.program_id(0)
    q = q_ref[...]
    km = km_ref[...]
    km_hi, q_hi = km.astype(BF16), q.astype(BF16)
    km_lo, q_lo = (km - km_hi.astype(F32)).astype(BF16), (q - q_hi.astype(F32)).astype(BF16)
    gate_t = _dot_nt(km_hi, q_hi) + _dot_nt(km_lo, q_hi) + _dot_nt(km_hi, q_lo)
    t = q.shape[0]
    blk = lax.broadcasted_iota(jnp.int32, (HEAD_DIM, t), 0)
    biases = []
    for h in range(N_ATT_HEADS):
        g = jnp.where(blk < own, gate_t[h * HEAD_DIM:(h + 1) * HEAD_DIM, :], -jnp.inf)
        bias = jnp.where(blk == own, 0.0, NEG_BIG)
        for r in range(MOBA_TOPK):
            mx = jnp.max(g, axis=0, keepdims=True)
            idx = jnp.min(jnp.where(g == mx, blk, HEAD_DIM), axis=0, keepdims=True)
            hit = blk == idx
            bias = jnp.where(hit, jnp.where(r < own, 0.0, bias), bias)
            g = jnp.where(hit, -jnp.inf, g)
        biases.append(bias)
    swapped = [biases[h ^ 1] for h in range(N_ATT_HEADS)]
    bias_all = jnp.concatenate(swapped, axis=0).T
    lane = lax.broadcasted_iota(jnp.int32, (t, LANES), 1)
    scale = HEAD_DIM ** -0.5 * LOG2E
    for h in range(N_ATT_HEADS):
        data_low = h % 2 == 0
        is_data = (lane < HEAD_DIM) if data_low else (lane >= HEAD_DIM)
        own_lane = own + (HEAD_DIM if data_low else 0)
        grp = slice((h // 2) * LANES, (h // 2 + 1) * LANES)
        qo_ref[h] = jnp.where(is_data, q[:, grp] * scale, bias_all[:, grp]).astype(BF16)
        ko_ref[h] = jnp.where(is_data, k_ref[:, grp], jnp.where(lane == own_lane, 1.0, 0.0)).astype(BF16)
    _store_vt(v_ref, vo_ref)


def _moba_select(q_rope, k_rope, z, km_mat):
    s = z.shape[0]
    tm = MOBA_BLOCK
    cb = COL_BQ // ATT_W
    blk = pl.BlockSpec((tm, ATT_W), lambda i: (i, 0))
    head_spec = pl.BlockSpec((N_ATT_HEADS, tm, LANES), lambda i: (0, i, 0))
    return pl.pallas_call(
        _moba_select_kernel,
        grid=(s // tm,),
        in_specs=[blk, blk, pl.BlockSpec((tm, ATT_W), lambda i: (i, cb + 2)),
                  pl.BlockSpec((ATT_W, ATT_W), lambda i: (0, 0))],
        out_specs=[head_spec, head_spec, pl.BlockSpec((N_ATT_HEADS, VT_ROWS, tm), lambda i: (0, 0, i))],
        out_shape=[jax.ShapeDtypeStruct((N_ATT_HEADS, s, LANES), BF16),
                   jax.ShapeDtypeStruct((N_ATT_HEADS, s, LANES), BF16),
                   jax.ShapeDtypeStruct((N_ATT_HEADS, VT_ROWS, s), BF16)],
        compiler_params=_cparams(("parallel",)),
        name="moba_select",
    )(q_rope, k_rope, z, km_mat)


def _flash_kernel(qi_tab, ki_tab, *refs, tile, ngroups):
    groups = [refs[3 * g:3 * g + 3] for g in range(ngroups)]
    g_ref, o_ref, m_ref, acc_ref = refs[3 * ngroups:]
    step = pl.program_id(0)
    qi = qi_tab[step]
    ki = ki_tab[step]

    @pl.when(ki == 0)
    def _():
        m_ref[...] = jnp.full_like(m_ref, -jnp.inf)
        acc_ref[...] = jnp.zeros_like(acc_ref)

    units = [(g, h) for g in range(ngroups) for h in range(N_ATT_HEADS)]

    def update(masked):
        if masked:
            key = lax.broadcasted_iota(jnp.int32, (tile, tile), 0)
            qry = lax.broadcasted_iota(jnp.int32, (tile, tile), 1)
            causal = key <= qry
        scores = lambda g, h: _dot_nt(groups[g][1][h], groups[g][0][h])
        st_next = scores(*units[0])
        for u, (g, h) in enumerate(units):
            st = st_next
            if u + 1 < len(units):
                st_next = scores(*units[u + 1])
            if masked:
                st = jnp.where(causal, st, NEG_BIG)
            m_prev = m_ref[u]
            m_new = jnp.maximum(m_prev, jnp.max(st, axis=0, keepdims=True))
            alpha = jnp.exp2(m_prev - m_new)
            p = jnp.exp2(st - m_new).astype(BF16)
            m_ref[u] = m_new
            acc_ref[u] = acc_ref[u] * alpha + _dot(groups[g][2][h], p)

    @pl.when(ki < qi)
    def _():
        update(False)

    @pl.when(ki == qi)
    def _():
        update(True)
        outs = []
        for u in range(len(units)):
            acc = acc_ref[u]
            o = acc[0:HEAD_DIM, :] / acc[HEAD_DIM:HEAD_DIM + 1, :]
            outs.append(o * lax.rsqrt(jnp.mean(o * o, axis=0, keepdims=True) + EPS))
        o_ref[...] = (jnp.concatenate(outs, axis=0).T * g_ref[...]).astype(o_ref.dtype)


def _flash_attention(groups, gain_row, tile=1024):
    ngroups = len(groups)
    width = ATT_W * ngroups
    s = groups[0][2].shape[2]
    n = s // tile
    pairs = [(qi, ki) for qi in range(n) for ki in range(qi + 1)]
    qi_tab = jnp.asarray([p[0] for p in pairs], jnp.int32)
    ki_tab = jnp.asarray([p[1] for p in pairs], jnp.int32)
    group_specs = [pl.BlockSpec((N_ATT_HEADS, tile, LANES), lambda i, qt, kt: (0, qt[i], 0)),
                   pl.BlockSpec((N_ATT_HEADS, tile, LANES), lambda i, qt, kt: (0, kt[i], 0)),
                   pl.BlockSpec((N_ATT_HEADS, VT_ROWS, tile), lambda i, qt, kt: (0, 0, kt[i]))]
    grid_spec = pltpu.PrefetchScalarGridSpec(
        num_scalar_prefetch=2,
        grid=(len(pairs),),
        in_specs=group_specs * ngroups + [pl.BlockSpec((1, width), lambda i, qt, kt: (0, 0))],
        out_specs=pl.BlockSpec((tile, width), lambda i, qt, kt: (qt[i], 0)),
        scratch_shapes=[pltpu.VMEM((N_ATT_HEADS * ngroups, 1, tile), F32),
                        pltpu.VMEM((N_ATT_HEADS * ngroups, VT_ROWS, tile), F32)],
    )
    operands = [a for grp in groups for a in grp]
    return pl.pallas_call(
        functools.partial(_flash_kernel, tile=tile, ngroups=ngroups),
        grid_spec=grid_spec,
        out_shape=jax.ShapeDtypeStruct((s, width), BF16),
        compiler_params=_cparams(("arbitrary",)),
        name="flash_attention",
    )(qi_tab, ki_tab, *operands, gain_row)


def _shift_rows(u, tail, s):
    rolled = pltpu.roll(u, s, axis=0)
    rolled_tail = pltpu.roll(tail, s, axis=0)
    row8 = lax.broadcasted_iota(jnp.int32, tail.shape, 0)
    top = jnp.where(row8 < s, rolled_tail, rolled[0:8])
    return jnp.concatenate([top, rolled[8:]], axis=0)


def _mlstm_kernel(qk_ref, v_ref, o_ref, g_ref, cw_ref, gain_ref, y_ref, tail_ref, c_ref, n_ref, m_ref):
    @pl.when(pl.program_id(0) == 0)
    def _():
        tail_ref[...] = jnp.zeros_like(tail_ref)
        c_ref[...] = jnp.zeros_like(c_ref)
        n_ref[...] = jnp.zeros_like(n_ref)
        m_ref[...] = jnp.zeros_like(m_ref)

    L = ML_CHUNK
    rows = qk_ref.shape[0]
    nch = rows // L
    u = qk_ref[...]
    tail = tail_ref[...]
    cw = cw_ref[...]
    conv = u * cw[CONV_WIDTH - 1:CONV_WIDTH]
    for s in range(1, CONV_WIDTH):
        conv = conv + _shift_rows(u, tail, s) * cw[CONV_WIDTH - 1 - s:CONV_WIDTH - s]
    tail_ref[...] = u[rows - 8:rows]
    qk = conv * jax.nn.sigmoid(conv)

    r = lax.broadcasted_iota(jnp.int32, (L, L), 0)
    c = lax.broadcasted_iota(jnp.int32, (L, L), 1)
    tril = c <= r
    kscale = ML_DIM ** -0.5
    heads = range(ML_HEADS)
    st = []
    for ci in range(nch):
        rs = slice(ci * L, (ci + 1) * L)
        g = g_ref[rs, :]
        gt = g.T
        per_head = []
        for h in heads:
            sl = slice(h * ML_DIM, (h + 1) * ML_DIM)
            qh = qk[rs, sl]
            kh = qk[rs, ML_W + h * ML_DIM:ML_W + (h + 1) * ML_DIM] * kscale
            qb, kb, vb = qh.astype(BF16), kh.astype(BF16), v_ref[rs, sl].astype(BF16)
            b_col = g[:, 8 + h:9 + h]
            i_col = g[:, 4 + h:5 + h]
            b_row = gt[8 + h:9 + h, :]
            i_row = gt[4 + h:5 + h, :]
            dmat = jnp.where(tril, b_col - b_row + i_row, -jnp.inf)
            dmax = jnp.max(dmat, axis=1, keepdims=True)
            a1 = _dot_nt(qb, kb) * jnp.exp(dmat - dmax)
            b_last = b_row[:, L - 1:L]
            g_col = b_last - b_col + i_col
            gmax = jnp.max(g_col, axis=0, keepdims=True)
            kw1 = kh * jnp.exp(g_col - gmax)
            per_head.append(dict(qh=qh, qb=qb, b_col=b_col, dmax=dmax, b_last=b_last, gmax=gmax,
                                 av=_dot(a1.astype(BF16), vb), asum=jnp.sum(a1, axis=1, keepdims=True),
                                 u1=_dot_tn(kw1.astype(BF16), vb), ksum=jnp.sum(kw1, axis=0, keepdims=True)))
        st.append(per_head)
    state = [(c_ref[h], n_ref[h], m_ref[h][:, 0:1]) for h in heads]
    for ci in range(nch):
        rs = slice(ci * L, (ci + 1) * L)
        outs = []
        for h in heads:
            s_ = st[ci][h]
            sl = slice(h * ML_DIM, (h + 1) * ML_DIM)
            cmat, nrow, m_prev = state[h]
            inter = s_["b_col"] + m_prev
            m_t = jnp.maximum(inter, s_["dmax"])
            s_intra = jnp.exp(s_["dmax"] - m_t)
            w_inter = jnp.exp(inter - m_t)
            num = s_intra * s_["av"] + w_inter * _dot(s_["qb"], cmat.astype(BF16))
            den = s_intra * s_["asum"] + w_inter * jnp.sum(s_["qh"] * nrow, axis=1, keepdims=True)
            hh = num / jnp.maximum(jnp.abs(den), jnp.exp(-m_t))
            y = jax.nn.sigmoid(o_ref[rs, sl]) * hh
            outs.append(y * lax.rsqrt(jnp.mean(y * y, axis=1, keepdims=True) + EPS))
        y_ref[rs, :] = (jnp.concatenate(outs, axis=1) * gain_ref[...]).astype(y_ref.dtype)
        for h in heads:
            s_ = st[ci][h]
            cmat, nrow, m_prev = state[h]
            m_new = jnp.maximum(s_["b_last"] + m_prev, s_["gmax"])
            decay = jnp.exp(s_["b_last"] + m_prev - m_new)
            scale = jnp.exp(s_["gmax"] - m_new)
            state[h] = (decay * cmat + scale * s_["u1"], decay * nrow + scale * s_["ksum"], m_new)
    for h in heads:
        c_ref[h], n_ref[h] = state[h][0], state[h][1]
        m_ref[h] = jnp.broadcast_to(state[h][2], (1, LANES))


ML_CHUNKS_PER_STEP = 4


def _mlstm(z, gates, conv_w, gain_row):
    s = z.shape[0]
    L = ML_CHUNK * ML_CHUNKS_PER_STEP
    return pl.pallas_call(
        _mlstm_kernel,
        grid=(s // L,),
        in_specs=[pl.BlockSpec((L, 2 * ML_W), lambda i: (i, COL_MQK // (2 * ML_W))),
                  pl.BlockSpec((L, ML_W), lambda i: (i, COL_MV // ML_W)),
                  pl.BlockSpec((L, ML_W), lambda i: (i, COL_MO // ML_W)),
                  pl.BlockSpec((L, LANES), lambda i: (i, 0)),
                  pl.BlockSpec((CONV_WIDTH, 2 * ML_W), lambda i: (0, 0)),
                  pl.BlockSpec((1, ML_W), lambda i: (0, 0))],
        out_specs=pl.BlockSpec((L, ML_W), lambda i: (i, 0)),
        out_shape=jax.ShapeDtypeStruct((s, ML_W), BF16),
        scratch_shapes=[pltpu.VMEM((8, 2 * ML_W), F32),
                        pltpu.VMEM((ML_HEADS, ML_DIM, ML_DIM), F32),
                        pltpu.VMEM((ML_HEADS, 1, ML_DIM), F32),
                        pltpu.VMEM((ML_HEADS, 1, LANES), F32)],
        compiler_params=_cparams(("arbitrary",)),
        name="mlstm",
    )(z, z, z, gates, conv_w, gain_row)


def _layer_norm(h, g, b):
    mu = jnp.mean(h, axis=1, keepdims=True)
    d = h - mu
    var = jnp.mean(d * d, axis=1, keepdims=True)
    return d * lax.rsqrt(var + EPS) * g + b


def _row_seg(j, rows):
    return pl.ds(j, rows, stride=SUBLANES)


def _outproj_kernel(yf_ref, ym_ref, yb_ref, w_ref, x_ref, g_ref, b_ref, wrh_ref, wrl_ref, br_ref, x1t_ref, lg_ref):
    mix = (_dot(yf_ref[...], w_ref[0:ATT_W, :]) + _dot(ym_ref[...], w_ref[ATT_W:ATT_W + ML_W, :])
           + _dot(yb_ref[...], w_ref[ATT_W + ML_W:, :]))
    x1 = _layer_norm(ALPHA * x_ref[...] + mix, g_ref[...], b_ref[...])
    tm = x1.shape[0]
    for j in range(D_MODEL // LANES):
        x1t_ref[_row_seg(j, tm), :] = x1[:, j * LANES:(j + 1) * LANES]
    hi = x1.astype(BF16)
    lo = (x1 - hi.astype(F32)).astype(BF16)
    wrh = wrh_ref[...]
    lg_ref[...] = _dot(hi, wrh) + _dot(lo, wrh) + _dot(hi, wrl_ref[...]) + br_ref[...]


def _outproj_ln_router(y_att, ym, w_out, x, ln_g, ln_b, w_router, b_router, tm=512):
    s = x.shape[0]
    const = lambda shape: pl.BlockSpec(shape, lambda i: (0, 0))
    rows = lambda w: pl.BlockSpec((tm, w), lambda i: (i, 0))
    yf, yb = y_att, y_att
    w_router_hi = w_router.astype(BF16)
    w_router_lo = (w_router - w_router_hi.astype(F32)).astype(BF16)
    return pl.pallas_call(
        _outproj_kernel,
        grid=(s // tm,),
        in_specs=[rows(ATT_W), rows(ML_W), pl.BlockSpec((tm, ATT_W), lambda i: (i, 1)),
                  const((D_MODEL, D_MODEL)), rows(D_MODEL),
                  const((1, D_MODEL)), const((1, D_MODEL)), const((D_MODEL, LANES)), const((D_MODEL, LANES)),
                  const((1, LANES))],
        out_specs=[pl.BlockSpec((SUBLANES * tm, LANES), lambda i: (i, 0)), rows(LANES)],
        out_shape=[jax.ShapeDtypeStruct((SUBLANES * s, LANES), F32), jax.ShapeDtypeStruct((s, LANES), F32)],
        compiler_params=_cparams(("parallel",)),
        name="outproj_ln_router",
    )(yf, ym, yb, w_out, x, ln_g, ln_b, w_router_hi, w_router_lo, b_router)


def _first_argmax(v, lane):
    mx = jnp.max(v, axis=1, keepdims=True)
    idx = jnp.min(jnp.where(v == mx, lane, LANES), axis=1, keepdims=True)
    return mx, idx


def _route_kernel(lg_ref, info_ref, cnt_ref, carry_ref):
    @pl.when(pl.program_id(0) == 0)
    def _():
        carry_ref[...] = jnp.zeros_like(carry_ref)

    lg = lg_ref[...]
    t = lg.shape[0]
    lane = lax.broadcasted_iota(jnp.int32, lg.shape, 1)
    is_grp = lane < N_GROUPS
    gmax, gsel = _first_argmax(jnp.where(is_grp, lg, -jnp.inf), lane)
    p_grp = 1.0 / jnp.sum(jnp.where(is_grp, jnp.exp(lg - gmax), 0.0), axis=1, keepdims=True)
    lo = N_GROUPS + EXPERTS_PER_GROUP * gsel
    el = jnp.where((lane >= lo) & (lane < lo + EXPERTS_PER_GROUP), lg, -jnp.inf)
    v0, i0 = _first_argmax(el, lane)
    v1, i1 = _first_argmax(jnp.where(lane == i0, -jnp.inf, el), lane)
    ex = jnp.exp(v1 - v0)
    w0 = p_grp / (1.0 + ex)
    w1 = p_grp * ex / (1.0 + ex)
    e0 = i0 - N_GROUPS
    e1 = i1 - N_GROUPS

    cnt = jnp.where((lane == e0) | (lane == e1), 1.0, 0.0)
    r = lax.broadcasted_iota(jnp.int32, (t, t), 0)
    c = lax.broadcasted_iota(jnp.int32, (t, t), 1)
    strict = jnp.where(c < r, 1.0, 0.0).astype(BF16)
    carry = carry_ref[...]
    before = _dot(strict, cnt.astype(BF16)) + carry
    rank0 = jnp.sum(jnp.where(lane == e0, before, 0.0), axis=1, keepdims=True)
    rank1 = jnp.sum(jnp.where(lane == e1, before, 0.0), axis=1, keepdims=True)
    carry = carry + jnp.sum(cnt, axis=0, keepdims=True)
    carry_ref[...] = carry
    cnt_ref[...] = carry
    vals = [e0.astype(F32), e1.astype(F32), w0, w1, rank0, rank1]
    info = jnp.zeros(lg.shape, F32)
    for j, val in enumerate(vals):
        info = jnp.where(lane == j, val, info)
    info_ref[...] = info


def _route(logits, tm=512):
    s = logits.shape[0]
    return pl.pallas_call(
        _route_kernel,
        grid=(s // tm,),
        in_specs=[pl.BlockSpec((tm, LANES), lambda i: (i, 0))],
        out_specs=[pl.BlockSpec((tm, LANES), lambda i: (i, 0)), pl.BlockSpec((1, LANES), lambda i: (0, 0))],
        out_shape=[jax.ShapeDtypeStruct((s, LANES), F32), jax.ShapeDtypeStruct((1, LANES), F32)],
        scratch_shapes=[pltpu.VMEM((1, LANES), F32)],
        compiler_params=_cparams(("arbitrary",)),
        name="route",
    )(logits)


MOE_ROWS = 256


def _moe_kernel(blk_e, nv, src0_ref, srcn_ref, slotp_ref, slotc_ref, x_hbm, wg_ref, wu_ref, wd_ref, out_hbm,
                xbuf0, xbuf1, ybuf0, ybuf1, wgb, wub, wdb, gsem, ssem, *, nblk):
    b = pl.program_id(0)
    xbufs = (xbuf0, xbuf1)
    ybufs = (ybuf0, ybuf1)
    nv_prev2, nv_prev, nv_cur, nv_next = nv[b], nv[b + 1], nv[b + 2], nv[b + 3]

    def row_in(tok8, slot, r):
        return pltpu.make_async_copy(x_hbm.at[pl.ds(pl.multiple_of(tok8, SUBLANES), SUBLANES), :],
                                     xbufs[slot].at[pl.ds(SUBLANES * r, SUBLANES), :], gsem.at[slot])

    def row_out(dst8, slot, r):
        return pltpu.make_async_copy(ybufs[slot].at[pl.ds(SUBLANES * r, SUBLANES), :],
                                     out_hbm.at[pl.ds(pl.multiple_of(dst8, SUBLANES), SUBLANES), :], ssem.at[slot])

    def start_rows(make, idx_ref, slot, n):
        for r in range(MOE_ROWS):
            idx = idx_ref[0, 0, r]

            @pl.when(r < n)
            def _():
                make(idx, slot, r).start(priority=r % 2)

    def wait_rows(make, slot, n):
        @pl.when(n > 0)
        def _():
            rows = pl.multiple_of(n * SUBLANES, SUBLANES)
            if make is row_in:
                pltpu.make_async_copy(x_hbm.at[pl.ds(0, rows), :], xbufs[slot].at[pl.ds(0, rows), :],
                                      gsem.at[slot]).wait()
            else:
                pltpu.make_async_copy(ybufs[slot].at[pl.ds(0, rows), :], out_hbm.at[pl.ds(0, rows), :],
                                      ssem.at[slot]).wait()

    @pl.when(b == 0)
    def _():
        for half in range(2):
            xbufs[half][...] = jnp.zeros_like(xbufs[half])
        start_rows(row_in, src0_ref, 0, nv_cur)

    @pl.when((b == 0) | (blk_e[b] != blk_e[jnp.maximum(b - 1, 0)]))
    def _():
        wgb[...] = wg_ref[0, 0].astype(BF16)
        wub[...] = wu_ref[0, 0].astype(BF16)
        wdb[...] = wd_ref[0, 0].astype(BF16)

    def step(cur):
        nxt = 1 - cur
        wait_rows(row_in, cur, nv_cur)
        start_rows(row_out, slotp_ref, nxt, nv_prev)
        start_rows(row_in, srcn_ref, nxt, nv_next)

        nseg = D_MODEL // LANES
        seg = lambda j: _row_seg(j, MOE_ROWS)
        xb = jnp.concatenate([xbufs[cur][seg(j), :] for j in range(nseg)], axis=1).astype(BF16)
        gate = _dot(xb, wgb[...])
        up = _dot(xb, wub[...])
        hid = (gate * jax.nn.sigmoid(gate) * up).astype(BF16)
        y = _dot(hid, wdb[...])

        wait_rows(row_out, cur, nv_prev2)

        for j in range(nseg):
            ybufs[cur][seg(j), :] = y[:, j * LANES:(j + 1) * LANES]

        @pl.when(b == nblk - 1)
        def _():
            start_rows(row_out, slotc_ref, cur, nv_cur)
            wait_rows(row_out, nxt, nv_prev)
            wait_rows(row_out, cur, nv_cur)

    def drain_step(cur):
        nxt = 1 - cur
        start_rows(row_out, slotp_ref, nxt, nv_prev)
        wait_rows(row_out, cur, nv_prev2)

        @pl.when(b == nblk - 1)
        def _():
            wait_rows(row_out, nxt, nv_prev)

    for parity in range(2):
        pl.when((b % 2 == parity) & (nv_cur > 0))(functools.partial(step, parity))
        pl.when((b % 2 == parity) & (nv_cur == 0))(functools.partial(drain_step, parity))


def _moe_ffn(x1, src_ext, slot_ext, blk_e, nv_ext, w_gate, w_up, w_down, layer):
    s = x1.shape[0] // SUBLANES
    nblk = src_ext.shape[0] - 1
    stage = (SUBLANES * MOE_ROWS, LANES)
    idx_blk = (1, 1, MOE_ROWS)
    smem = pltpu.SMEM
    grid_spec = pltpu.PrefetchScalarGridSpec(
        num_scalar_prefetch=2,
        grid=(nblk,),
        in_specs=[pl.BlockSpec(idx_blk, lambda b, be, nv: (0, 0, 0), memory_space=smem),
                  pl.BlockSpec(idx_blk, lambda b, be, nv: (b + 1, 0, 0), memory_space=smem),
                  pl.BlockSpec(idx_blk, lambda b, be, nv: (b, 0, 0), memory_space=smem),
                  pl.BlockSpec(idx_blk, lambda b, be, nv: (b + 1, 0, 0), memory_space=smem),
                  pl.BlockSpec(memory_space=pl.ANY),
                  pl.BlockSpec((1, 1, D_MODEL, D_EXPERT), lambda b, be, nv: (layer, be[b], 0, 0)),
                  pl.BlockSpec((1, 1, D_MODEL, D_EXPERT), lambda b, be, nv: (layer, be[b], 0, 0)),
                  pl.BlockSpec((1, 1, D_EXPERT, D_MODEL), lambda b, be, nv: (layer, be[b], 0, 0))],
        out_specs=pl.BlockSpec(memory_space=pl.ANY),
        scratch_shapes=[pltpu.VMEM(stage, F32),
                        pltpu.VMEM(stage, F32),
                        pltpu.VMEM(stage, F32),
                        pltpu.VMEM(stage, F32),
                        pltpu.VMEM((D_MODEL, D_EXPERT), BF16),
                        pltpu.VMEM((D_MODEL, D_EXPERT), BF16),
                        pltpu.VMEM((D_EXPERT, D_MODEL), BF16),
                        pltpu.SemaphoreType.DMA((2,)),
                        pltpu.SemaphoreType.DMA((2,))],
    )
    return pl.pallas_call(
        functools.partial(_moe_kernel, nblk=nblk),
        grid_spec=grid_spec,
        out_shape=jax.ShapeDtypeStruct((SUBLANES * 2 * s, LANES), F32),
        compiler_params=_cparams(("arbitrary",)),
        name="moe_ffn",
    )(blk_e, nv_ext, src_ext, src_ext, slot_ext, slot_ext, x1, w_gate, w_up, w_down)


def _dest_kernel(info_ref, pstart_ref, d_ref):
    info = info_ref[...]
    tm = info.shape[0]
    lane = lax.broadcasted_iota(jnp.int32, info.shape, 1)
    ps = pstart_ref[...]
    dests = []
    for k in range(2):
        e = info[:, k:k + 1].astype(jnp.int32)
        dests.append(jnp.sum(jnp.where(lane == e, ps, 0.0), axis=1, keepdims=True) + info[:, 4 + k:5 + k])
    packed = jnp.where(lane == 0, dests[0], jnp.where(lane == 1, dests[1], 0.0))
    for j in range(tm // LANES):
        rows = packed[j * LANES:(j + 1) * LANES, :].T
        for k in range(2):
            d_ref[k, j:j + 1, :] = rows[k:k + 1, :].astype(jnp.int32)


def _dest(info, pstart_row, tm=1024):
    s = info.shape[0]
    return pl.pallas_call(
        _dest_kernel,
        grid=(s // tm,),
        in_specs=[pl.BlockSpec((tm, LANES), lambda i: (i, 0)), pl.BlockSpec((1, LANES), lambda i: (0, 0))],
        out_specs=pl.BlockSpec((2, tm // LANES, LANES), lambda i: (0, i, 0)),
        out_shape=jax.ShapeDtypeStruct((2, s // LANES, LANES), jnp.int32),
        compiler_params=_cparams(("parallel",)),
        name="moe_dest",
    )(info, pstart_row)


def _moe_plan(info, counts_row, s):
    counts = counts_row[0, :N_EXPERTS].astype(jnp.int32)
    padded = ((counts + MOE_ROWS - 1) // MOE_ROWS) * MOE_ROWS
    pends = jnp.cumsum(padded)
    pstarts = pends - padded
    nblk = (2 * s) // MOE_ROWS + N_EXPERTS
    p = nblk * MOE_ROWS
    dest = _dest(info, _pad_lanes(pstarts.astype(F32))).reshape(2 * s)
    out_slot = jnp.zeros((p,), jnp.int32).at[dest].set(jnp.arange(2 * s, dtype=jnp.int32))
    src_tok = jnp.where(out_slot >= s, out_slot - s, out_slot)
    nused = (pends[-1] // MOE_ROWS).astype(jnp.int32)
    blk = jnp.arange(nblk, dtype=jnp.int32)
    blk_start = jnp.minimum(blk, nused - 1) * MOE_ROWS
    blk_e = jnp.minimum(jnp.sum((pends[None, :] <= blk_start[:, None]).astype(jnp.int32), axis=1), N_EXPERTS - 1)
    is_e = jnp.arange(N_EXPERTS)[None, :] == blk_e[:, None]
    seg_start = jnp.sum(jnp.where(is_e, pstarts[None, :], 0), axis=1)
    seg_count = jnp.sum(jnp.where(is_e, counts[None, :], 0), axis=1)
    nvalid = jnp.where(blk < nused, jnp.clip(seg_count - (blk * MOE_ROWS - seg_start), 0, MOE_ROWS), 0)
    nv_ext = jnp.concatenate([jnp.zeros((2,), jnp.int32), nvalid.astype(jnp.int32), jnp.zeros((2,), jnp.int32)])
    pad_blk = jnp.zeros((MOE_ROWS,), jnp.int32)
    src_ext = jnp.concatenate([src_tok, pad_blk]) * SUBLANES
    slot_ext = jnp.concatenate([pad_blk, out_slot]) * SUBLANES
    return src_ext.reshape(nblk + 1, 1, MOE_ROWS), slot_ext.reshape(nblk + 1, 1, MOE_ROWS), blk_e, nv_ext


def _combine_kernel(x1_ref, ya_ref, yb_ref, info_ref, g_ref, b_ref, o_ref):
    info = info_ref[...]
    tm = info.shape[0]
    wide = lambda ref: jnp.concatenate([ref[_row_seg(j, tm), :] for j in range(D_MODEL // LANES)], axis=1)
    ffn = info[:, 2:3] * wide(ya_ref) + info[:, 3:4] * wide(yb_ref)
    o_ref[...] = _layer_norm(ALPHA * wide(x1_ref) + ffn, g_ref[...], b_ref[...])


def _combine_ln(x1_tiles, y2, info, ln_g, ln_b, tm=512):
    s = x1_tiles.shape[0] // SUBLANES
    const = pl.BlockSpec((1, D_MODEL), lambda i: (0, 0))
    return pl.pallas_call(
        _combine_kernel,
        grid=(s // tm,),
        in_specs=[pl.BlockSpec((SUBLANES * tm, LANES), lambda i: (i, 0)),
                  pl.BlockSpec((SUBLANES * tm, LANES), lambda i: (i, 0)),
                  pl.BlockSpec((SUBLANES * tm, LANES), lambda i: (i + s // tm, 0)),
                  pl.BlockSpec((tm, LANES), lambda i: (i, 0)), const, const],
        out_specs=pl.BlockSpec((tm, D_MODEL), lambda i: (i, 0)),
        out_shape=jax.ShapeDtypeStruct((s, D_MODEL), F32),
        compiler_params=_cparams(("parallel",)),
        name="combine_ln",
    )(x1_tiles, y2, y2, info, ln_g, ln_b)


def _pad_lanes(v, width=LANES):
    return jnp.zeros((1, width), F32).at[0, :v.shape[0]].set(v)


def _layer(layer, x, cos, sin, w_in, b_fox_f, b_mlstm_i, b_mlstm_f, conv_w, g_fox, g_mlstm, g_moba, w_out,
           ln1_g, ln1_b, w_grp, b_grp, w_exp_router, b_exp_router, w_gate, w_up, w_down, ln2_g, ln2_b):
    s = x.shape[0]
    gate_bias = _pad_lanes(jnp.concatenate([b_fox_f, b_mlstm_i, b_mlstm_f]))

    z = _inproj(x, w_in, layer)
    gates = _gate_prep(z, gate_bias)

    fox_ops = _fox_prep(z, gates)

    q_rope, k_rope, kmean = _moba_rope(z, cos, sin)
    km = kmean[:, 0, :].reshape(s // MOBA_BLOCK, N_ATT_HEADS, HEAD_DIM)
    km_mat = jnp.zeros((N_ATT_HEADS, HEAD_DIM, N_ATT_HEADS, HEAD_DIM), F32)
    for h in range(N_ATT_HEADS):
        km_mat = km_mat.at[h, :s // MOBA_BLOCK, h, :].set(km[:, h, :])
    km_mat = km_mat.reshape(ATT_W, ATT_W)
    moba_ops = _moba_select(q_rope, k_rope, z, km_mat)
    y_att = _flash_attention([fox_ops, moba_ops], jnp.concatenate([g_fox, g_moba])[None, :])

    y_mlstm = _mlstm(z, gates, conv_w, g_mlstm[None, :])

    w_router = jnp.zeros((D_MODEL, LANES), F32)
    w_router = w_router.at[:, :N_GROUPS].set(w_grp)
    w_router = w_router.at[:, N_GROUPS:N_GROUPS + N_EXPERTS].set(w_exp_router.reshape(D_MODEL, N_EXPERTS))
    b_router = _pad_lanes(jnp.concatenate([b_grp, b_exp_router.reshape(N_EXPERTS)]))
    x1_tiles, logits = _outproj_ln_router(y_att, y_mlstm, w_out.astype(BF16), x, ln1_g[None, :],
                                          ln1_b[None, :], w_router, b_router)

    info, counts = _route(logits)
    src_ext, slot_ext, blk_e, nv_ext = _moe_plan(info, counts, s)
    y2 = _moe_ffn(x1_tiles, src_ext, slot_ext, blk_e, nv_ext, w_gate, w_up, w_down, layer)
    return _combine_ln(x1_tiles, y2, info, ln2_g[None, :], ln2_b[None, :])


def kernel(x, positions, w_in, b_fox_f, b_mlstm_i, b_mlstm_f, conv_w, g_fox, g_mlstm, g_moba, w_out, ln1_g, ln1_b, w_grp, b_grp, w_exp_router, b_exp_router, w_gate, w_up, w_down, ln2_g, ln2_b):
    assert x.shape[0] == 1
    xs = x[0]
    d = jnp.arange(ATT_W) % HEAD_DIM
    half = ROPE_DIM // 2
    inv = 1.0 / (ROPE_THETA ** (jnp.arange(0, ROPE_DIM, 2, dtype=F32) / ROPE_DIM))
    inv_row = jnp.where(d < ROPE_DIM, inv[d % half], 0.0)[None, :].astype(F32)
    sign_row = jnp.where(d < half, -1.0, 1.0)[None, :].astype(F32)
    cos, sin = _rope_tables(positions[0][:, None], inv_row, sign_row)
    for l in range(DEPTH):
        xs = _layer(l, xs, cos, sin, w_in, b_fox_f[l], b_mlstm_i[l], b_mlstm_f[l], conv_w[l], g_fox[l],
                    g_mlstm[l], g_moba[l], w_out[l], ln1_g[l], ln1_b[l], w_grp[l], b_grp[l],
                    w_exp_router[l], b_exp_router[l], w_gate, w_up, w_down, ln2_g[l], ln2_b[l])
    return xs[None]
```
